```python
import jax, jax.numpy as jnp
from jax import lax
import numpy as np

D_MODEL = 1024
BATCH = 8
SEQ = 2048
DEPTH = 2
DEC_BATCH = 128
DEC_SEQ = 1
PAST_LEN = 16384
PAGE_SIZE = 128

N_BRANCH = 4
W_BR = D_MODEL // N_BRANCH
HEAD_A = 64
H_A = W_BR // HEAD_A
R_DECAY = 32
R_ICL = 32
GN_EPS_A = 64e-5
H_B = 4
DK_B = W_BR // (2 * H_B)
DV_B = W_BR // H_B
R_GATE_B = 16
GLA_TAU = 16.0
GLA_CHUNK = 64
H_C = 4
KV_C = 2
HD_C = W_BR // H_C
G_C = H_C // KV_C
WINDOW = 128
NB_D = 4
BS_D = W_BR // NB_D
CONV_W = 4
C_RG = 8.0
NORM_EPS = 1e-6

A_COLS = 3 * W_BR + R_DECAY + R_ICL
B_COLS = 2 * H_B * DK_B + W_BR + R_GATE_B
C_COLS = H_C * HD_C + 2 * KV_C * HD_C
D_COLS = W_BR
Z_COLS = N_BRANCH * W_BR
G_COLS = N_BRANCH * D_MODEL
IN_COLS = A_COLS + B_COLS + C_COLS + D_COLS + Z_COLS + G_COLS

kernel_name = 'hybrid_rwkv7_gla_swa_rglru_decode_step'

f32 = jnp.float32


def _split(x, sizes):
    idx = np.cumsum(sizes)[:-1].tolist()
    return jnp.split(x, idx, axis=-1)


def _rmsnorm(x, g):
    xf = x.astype(f32)
    y = xf * lax.rsqrt(jnp.mean(xf * xf, -1, keepdims=True) + NORM_EPS)
    return (y * g.astype(f32)).astype(x.dtype)


def _alibi_slopes(n):
    return 2.0 ** (-8.0 * jnp.arange(1, n + 1, dtype=f32) / n)


def _rwkv7_mix(u, shift0, S0, mu, w0, w_up, a0, a_up, k_k, k_a, r_k, ln_g, ln_b):
    B, L, _ = u.shape
    u = u.astype(f32)
    u_prev = jnp.concatenate([shift0[:, None].astype(f32), u[:, :-1]], axis=1)
    us = u + (u_prev - u) * mu.astype(f32)
    r, k, v, wl, al = _split(us, [W_BR, W_BR, W_BR, R_DECAY, R_ICL])
    w = -jax.nn.softplus(-(w0 + jnp.tanh(wl) @ w_up)) - 0.5
    decay = jnp.exp(-jnp.exp(w))
    a = jax.nn.sigmoid(a0 + al @ a_up)
    heads = lambda t: t.reshape(B, L, H_A, HEAD_A)
    kk = heads(k * k_k)
    kk = kk / jnp.maximum(jnp.sqrt(jnp.sum(kk * kk, -1, keepdims=True)), 1e-12)
    k = k * (1.0 + (a - 1.0) * k_a)
    r_h, k_h, v_h, w_h, a_h = heads(r), heads(k), heads(v), heads(decay), heads(a)

    def step(S, inp):
        r_t, k_t, v_t, w_t, kk_t, a_t = inp
        sa = jnp.einsum('bhvk,bhk->bhv', S, -kk_t)
        S = (S * w_t[:, :, None, :] + sa[..., None] * (kk_t * a_t)[:, :, None, :]
             + v_t[..., None] * k_t[:, :, None, :])
        return S, jnp.einsum('bhvk,bhk->bhv', S, r_t)

    xs = tuple(jnp.moveaxis(t, 1, 0) for t in (r_h, k_h, v_h, w_h, kk, a_h))
    S_fin, o = lax.scan(step, S0.astype(f32), xs)
    o = jnp.moveaxis(o, 0, 1)
    mean = jnp.mean(o, -1, keepdims=True)
    var = jnp.mean(jnp.square(o - mean), -1, keepdims=True)
    o = ((o - mean) * lax.rsqrt(var + GN_EPS_A)).reshape(B, L, W_BR) * ln_g + ln_b
    bonus = jnp.sum(r_h * k_h * r_k, -1, keepdims=True) * v_h
    return o + bonus.reshape(B, L, W_BR), u[:, -1], S_fin


def _gla_mix(f, S0, up, bias, norm_g):
    B, L, _ = f.shape
    f = f.astype(f32)
    q, k, v, gl = _split(f, [H_B * DK_B, H_B * DK_B, W_BR, R_GATE_B])
    g = jax.nn.log_sigmoid(gl @ up + bias) / GLA_TAU
    q = q.reshape(B, L, H_B, DK_B) * DK_B ** -0.5
    k = k.reshape(B, L, H_B, DK_B)
    g = g.reshape(B, L, H_B, DK_B)
    v = v.reshape(B, L, H_B, DV_B)
    pad = (-L) % GLA_CHUNK
    nc = (L + pad) // GLA_CHUNK

    def chunks(t):
        t = jnp.pad(t, ((0, 0), (0, pad), (0, 0), (0, 0)))
        return t.reshape(B, nc, GLA_CHUNK, H_B, t.shape[-1]).transpose(1, 0, 3, 2, 4)

    causal = jnp.tril(jnp.ones((GLA_CHUNK, GLA_CHUNK), bool))

    def step(S, inp):
        q_c, k_c, v_c, g_c = inp
        b = jnp.cumsum(g_c, axis=2)
        qe = q_c * jnp.exp(b)
        ke = k_c * jnp.exp(-b)
        att = jnp.where(causal, jnp.einsum('bhtd,bhsd->bhts', qe, ke), 0.0)
        o = jnp.einsum('bhts,bhse->bhte', att, v_c) + jnp.einsum('bhtd,bhde->bhte', qe, S)
        b_last = b[:, :, -1:, :]
        S = (jnp.exp(b_last[:, :, 0, :])[..., None] * S
             + jnp.einsum('bhsd,bhse->bhde', k_c * jnp.exp(b_last - b), v_c))
        return S, o

    S_fin, o = lax.scan(step, S0.astype(f32), (chunks(q), chunks(k), chunks(v), chunks(g)))
    o = o.transpose(1, 0, 3, 2, 4).reshape(B, nc * GLA_CHUNK, H_B, DV_B)[:, :L]
    o = o * lax.rsqrt(jnp.mean(o * o, -1, keepdims=True) + NORM_EPS) * norm_g
    return o.reshape(B, L, W_BR), S_fin


def _sink_attend(q, k, v, dist, valid, sinks):
    slopes = _alibi_slopes(H_C).reshape(KV_C, G_C)
    s = jnp.einsum('bnqkgd,bnskd->bnkgqs', q, k).astype(f32) * HD_C ** -0.5
    s = s - slopes[None, None, :, :, None, None] * dist[None, :, None, None].astype(f32)
    s = jnp.where(valid[None, :, None, None], s, -jnp.inf)
    sink = jnp.broadcast_to(sinks.astype(f32).reshape(KV_C, G_C)[None, None, :, :, None, None],
                            s.shape[:-1] + (1,))
    p = jax.nn.softmax(jnp.concatenate([s, sink], axis=-1), axis=-1)[..., :-1]
    return jnp.einsum('bnkgqs,bnskd->bnqkgd', p.astype(v.dtype), v)


def _swa_prompt(q, k, v, sinks):
    B, L = q.shape[:2]
    nb = L // WINDOW
    qb = q.reshape(B, nb, WINDOW, KV_C, G_C, HD_C)

    def band(t):
        tp = jnp.pad(t, ((0, 0), (WINDOW, 0), (0, 0), (0, 0))).reshape(B, nb + 1, WINDOW, KV_C, HD_C)
        return jnp.concatenate([tp[:, :-1], tp[:, 1:]], axis=2)

    t_pos = jnp.arange(WINDOW)[:, None] + WINDOW
    s_pos = jnp.arange(2 * WINDOW)[None, :]
    dist = t_pos - s_pos
    blk = jnp.arange(nb)[:, None, None]
    valid = (dist >= 0) & (dist <= WINDOW) & (blk * WINDOW - WINDOW + s_pos >= 0)
    dist = jnp.broadcast_to(dist, valid.shape)
    out = _sink_attend(qb, band(k), band(v), dist, valid, sinks)
    return out.reshape(B, L, H_C * HD_C), k[:, L - WINDOW:], v[:, L - WINDOW:]


def _swa_sample(q, k, v, kbuf, vbuf, sinks):
    B, T = q.shape[:2]
    WB = kbuf.shape[1]
    kc = jnp.concatenate([kbuf.astype(k.dtype), k], axis=1)
    vc = jnp.concatenate([vbuf.astype(v.dtype), v], axis=1)
    dist = (WB + jnp.arange(T)[:, None] - jnp.arange(WB + T)[None, :])[None]
    valid = (dist >= 0) & (dist <= WINDOW)
    out = _sink_attend(q[:, None], kc[:, None], vc[:, None], dist, valid, sinks)
    return out.reshape(B, T, H_C * HD_C), kc[:, -WB:], vc[:, -WB:]


def _rglru_mix(xd, conv0, h0, conv_w, conv_b, wa, ba, wx, bx, lam):
    B, L, _ = xd.shape
    xp = jnp.concatenate([conv0.astype(f32), xd.astype(f32)], axis=1)
    xc = conv_b.astype(f32)
    for j in range(CONV_W):
        xc = xc + xp[:, j:j + L] * conv_w[j]
    xb = xc.reshape(B, L, NB_D, BS_D)
    r = jax.nn.sigmoid(jnp.einsum('blnc,ncd->blnd', xb, wa).reshape(B, L, W_BR) + ba)
    i = jax.nn.sigmoid(jnp.einsum('blnc,ncd->blnd', xb, wx).reshape(B, L, W_BR) + bx)
    log_a = C_RG * r * jax.nn.log_sigmoid(lam)
    a = jnp.exp(log_a)
    bterm = jnp.sqrt(-jnp.expm1(2.0 * log_a)) * (i * xc)
    bterm = bterm.at[:, 0].add(a[:, 0] * h0.astype(f32))

    def comb(lhs, rhs):
        a1, b1 = lhs
        a2, b2 = rhs
        return a1 * a2, a2 * b1 + b2

    _, h = lax.associative_scan(comb, (a, bterm), axis=1)
    return h, xp[:, -(CONV_W - 1):], h[:, -1]


def _layer(x, lp, wkv0, shift0, gla0, conv0, h0, kbuf, vbuf):
    B, L, _ = x.shape
    hn = _rmsnorm(x, lp['norm_g'])
    proj = hn @ lp['w_in']
    fa, fb, fc, fd, z, gates = _split(proj, [A_COLS, B_COLS, C_COLS, D_COLS, Z_COLS, G_COLS])
    y_a, shift1, wkv1 = _rwkv7_mix(fa, shift0, wkv0, lp['mu_shift'], lp['w0'], lp['w_decay_up'],
                                   lp['a0'], lp['a_icl_up'], lp['k_k'], lp['k_a'], lp['r_k'],
                                   lp['ln_x_g'], lp['ln_x_b'])
    y_b, gla1 = _gla_mix(fb, gla0, lp['gla_gate_up'], lp['gla_gate_b'], lp['gla_norm_g'])
    q, k, v = _split(fc, [H_C * HD_C, KV_C * HD_C, KV_C * HD_C])
    q = q.reshape(B, L, KV_C, G_C, HD_C)
    k = k.reshape(B, L, KV_C, HD_C)
    v = v.reshape(B, L, KV_C, HD_C)
    if kbuf is None:
        y_c, k1, v1 = _swa_prompt(q, k, v, lp['swa_sinks'])
    else:
        y_c, k1, v1 = _swa_sample(q, k, v, kbuf, vbuf, lp['swa_sinks'])
    y_d, conv1, h1 = _rglru_mix(fd, conv0, h0, lp['lru_conv_w'], lp['lru_conv_b'], lp['lru_wa'],
                                lp['lru_ba'], lp['lru_wx'], lp['lru_bx'], lp['lru_lambda'])
    ys = jnp.stack([y_a, y_b, y_c.astype(f32), y_d], axis=2)
    ys = ys * jax.nn.silu(z.reshape(B, L, N_BRANCH, W_BR).astype(f32))
    br = jnp.einsum('blnw,nwd->blnd', ys.astype(x.dtype), lp['w_branch'])
    gate = jax.nn.sigmoid(gates.reshape(B, L, N_BRANCH, D_MODEL))
    merged = jnp.sum(gate * br, axis=2)
    out = x + merged @ lp['w_out']
    return out, (wkv1, shift1, gla1, k1, v1, conv1, h1)


def setup_inputs(seed: int = 0) -> dict:
    key = jax.random.key(seed)
    ks = iter(jax.random.split(key, 48))
    nrm = lambda shape, scale: scale * jax.random.normal(next(ks), shape, f32)
    uni = lambda shape, lo, hi: jax.random.uniform(next(ks), shape, f32, lo, hi)
    win_buf = min(WINDOW, PAST_LEN)
    u = uni((DEPTH, W_BR), 0.9, 0.999)
    s = u ** (1.0 / C_RG)
    lam = jnp.log(s) - jnp.log1p(-s)
    return {
        'x_prompt': nrm((BATCH, SEQ, D_MODEL), 1.0),
        'x_sample': nrm((DEC_BATCH, DEC_SEQ, D_MODEL), 1.0),
        'state_wkv': nrm((DEPTH, DEC_BATCH, H_A, HEAD_A, HEAD_A), 1.0),
        'state_shift': nrm((DEPTH, DEC_BATCH, A_COLS), 1.0),
        'state_gla': nrm((DEPTH, DEC_BATCH, H_B, DK_B, DV_B), 1.0),
        'cache_swa_k': nrm((DEPTH, DEC_BATCH, win_buf, KV_C, HD_C), 1.0),
        'cache_swa_v': nrm((DEPTH, DEC_BATCH, win_buf, KV_C, HD_C), 1.0),
        'state_lru_conv': nrm((DEPTH, DEC_BATCH, CONV_W - 1, W_BR), 1.0),
        'state_lru_h': nrm((DEPTH, DEC_BATCH, W_BR), 0.5),
        'norm_g': 1.0 + nrm((DEPTH, D_MODEL), 0.02),
        'w_in': nrm((DEPTH, D_MODEL, IN_COLS), D_MODEL ** -0.5),
        'mu_shift': uni((DEPTH, A_COLS), 0.0, 1.0),
        'w0': uni((DEPTH, W_BR), -6.0, 1.0),
        'w_decay_up': nrm((DEPTH, R_DECAY, W_BR), 0.1 * R_DECAY ** -0.5),
        'a0': nrm((DEPTH, W_BR), 0.1),
        'a_icl_up': nrm((DEPTH, R_ICL, W_BR), R_ICL ** -0.5),
        'k_k': 0.85 + nrm((DEPTH, W_BR), 0.02),
        'k_a': 1.0 + nrm((DEPTH, W_BR), 0.02),
        'r_k': nrm((DEPTH, H_A, HEAD_A), 0.1),
        'ln_x_g': 1.0 + nrm((DEPTH, W_BR), 0.02),
        'ln_x_b': nrm((DEPTH, W_BR), 0.02),
        'gla_gate_up': nrm((DEPTH, R_GATE_B, H_B * DK_B), R_GATE_B ** -0.5),
        'gla_gate_b': uni((DEPTH, H_B * DK_B), 0.0, 4.0),
        'gla_norm_g': 1.0 + nrm((DEPTH, DV_B), 0.02),
        'swa_sinks': nrm((DEPTH, H_C), 0.5),
        'lru_conv_w': nrm((DEPTH, CONV_W, W_BR), CONV_W ** -0.5),
        'lru_conv_b': nrm((DEPTH, W_BR), 0.01),
        'lru_wa': nrm((DEPTH, NB_D, BS_D, BS_D), BS_D ** -0.5),
        'lru_ba': nrm((DEPTH, W_BR), 0.01),
        'lru_wx': nrm((DEPTH, NB_D, BS_D, BS_D), BS_D ** -0.5),
        'lru_bx': nrm((DEPTH, W_BR), 0.01),
        'lru_lambda': lam,
        'w_branch': nrm((DEPTH, N_BRANCH, W_BR, D_MODEL), W_BR ** -0.5),
        'w_out': nrm((DEPTH, D_MODEL, D_MODEL), D_MODEL ** -0.5),
        'final_norm_g': 1.0 + nrm((D_MODEL,), 0.02),
    }


def reference(x_prompt, x_sample, state_wkv, state_shift, state_gla, cache_swa_k, cache_swa_v,
              state_lru_conv, state_lru_h, norm_g, w_in, mu_shift, w0, w_decay_up, a0, a_icl_up,
              k_k, k_a, r_k, ln_x_g, ln_x_b, gla_gate_up, gla_gate_b, gla_norm_g, swa_sinks,
              lru_conv_w, lru_conv_b, lru_wa, lru_ba, lru_wx, lru_bx, lru_lambda, w_branch, w_out,
              final_norm_g):
    bp = x_prompt.shape[0]
    y_p, y_s = x_prompt, x_sample
    new_p = [[] for _ in range(7)]
    new_s = [[] for _ in range(7)]
    for l in range(DEPTH):
        lp = dict(norm_g=norm_g[l], w_in=w_in[l], mu_shift=mu_shift[l], w0=w0[l],
                  w_decay_up=w_decay_up[l], a0=a0[l], a_icl_up=a_icl_up[l], k_k=k_k[l], k_a=k_a[l],
                  r_k=r_k[l], ln_x_g=ln_x_g[l], ln_x_b=ln_x_b[l], gla_gate_up=gla_gate_up[l],
                  gla_gate_b=gla_gate_b[l], gla_norm_g=gla_norm_g[l], swa_sinks=swa_sinks[l],
                  lru_conv_w=lru_conv_w[l], lru_conv_b=lru_conv_b[l], lru_wa=lru_wa[l],
                  lru_ba=lru_ba[l], lru_wx=lru_wx[l], lru_bx=lru_bx[l], lru_lambda=lru_lambda[l],
                  w_branch=w_branch[l], w_out=w_out[l])
        y_p, st_p = _layer(y_p, lp,
                           jnp.zeros((bp, H_A, HEAD_A, HEAD_A), f32),
                           jnp.zeros((bp, A_COLS), f32),
                           jnp.zeros((bp, H_B, DK_B, DV_B), f32),
                           jnp.zeros((bp, CONV_W - 1, W_BR), f32),
                           jnp.zeros((bp, W_BR), f32),
                           None, None)
        y_s, st_s = _layer(y_s, lp, state_wkv[l], state_shift[l], state_gla[l],
                           state_lru_conv[l], state_lru_h[l], cache_swa_k[l], cache_swa_v[l])
        for i in range(7):
            new_p[i].append(st_p[i])
            new_s[i].append(st_s[i])
    y_prompt = _rmsnorm(y_p, final_norm_g)
    y_sample = _rmsnorm(y_s, final_norm_g)
    wkv_p, shift_p, gla_p, k_p, v_p, conv_p, h_p = [jnp.stack(t) for t in new_p]
    wkv_s, shift_s, gla_s, k_s, v_s, conv_s, h_s = [jnp.stack(t) for t in new_s]
    return (y_prompt, y_sample, wkv_p, wkv_s, shift_p, shift_s, gla_p, gla_s, k_p, k_s, v_p, v_s,
            conv_p, conv_s, h_p, h_s)
```

```python
import functools

import jax
import jax.numpy as jnp
from jax import lax
from jax.experimental import pallas as pl
from jax.experimental.pallas import tpu as pltpu

f32 = jnp.float32
bf16 = jnp.bfloat16
HI = lax.Precision.HIGHEST

D_MODEL = 1024
N_BRANCH = 4
W_BR = 256
HEAD_A = 64
H_A = 4
R_DECAY = 32
R_ICL = 32
GN_EPS_A = 64e-5
H_B = 4
DK_B = 32
DV_B = 64
R_GATE_B = 16
GLA_TAU = 16.0
GLA_CHUNK = 64
H_C = 4
KV_C = 2
HD_C = 64
G_C = 2
WINDOW = 128
CONV_W = 4
C_RG = 8.0
NORM_EPS = 1e-6

A_COLS = 3 * W_BR + R_DECAY + R_ICL
B_COLS = 2 * H_B * DK_B + W_BR + R_GATE_B
C_COLS = H_C * HD_C + 2 * KV_C * HD_C
D_COLS = W_BR
Z_COLS = N_BRANCH * W_BR
G_COLS = N_BRANCH * D_MODEL

LANE = 128
A_PAD = 7 * LANE
B_PAD = 5 * LANE
RWKV_CHUNK = 64
LRU_CHUNK = 256
VMEM_LIMIT = 56 * 1024 * 1024
NEG_BIG = -1e30

ALIBI_SLOPES = tuple(2.0 ** (-8.0 * (h + 1) / H_C) for h in range(H_C))


def _mm(a, b):
    return jnp.dot(a, b, precision=HI, preferred_element_type=f32)


def _mm_nt(a, b):
    return lax.dot_general(a, b, (((1,), (1,)), ((), ())), precision=HI, preferred_element_type=f32)


def _mm_tn(a, b):
    return lax.dot_general(a, b, (((0,), (0,)), ((), ())), precision=HI, preferred_element_type=f32)


def _bdot(a, b):
    return jnp.dot(a.astype(bf16), b.astype(bf16), preferred_element_type=f32)


def _bdot_nt(a, b):
    return lax.dot_general(a.astype(bf16), b.astype(bf16), (((1,), (1,)), ((), ())),
                           preferred_element_type=f32)


def _iota(shape, dim):
    return lax.broadcasted_iota(jnp.int32, shape, dim)


def _eye(n):
    return (_iota((n, n), 0) == _iota((n, n), 1)).astype(f32)


def _block_ones(n, blk):
    return ((_iota((n, n), 0) // blk) == (_iota((n, n), 1) // blk)).astype(f32)


def _tril_ones(n):
    return (_iota((n, n), 0) >= _iota((n, n), 1)).astype(f32)


def _softplus(x):
    return jnp.maximum(x, 0.0) + jnp.log(1.0 + jnp.exp(-jnp.abs(x)))


def _log_sigmoid(x):
    return -_softplus(-x)


def _sigmoid(x):
    return 1.0 / (1.0 + jnp.exp(-x))


def _rms(x, g):
    return x * lax.rsqrt(jnp.mean(x * x, -1, keepdims=True) + NORM_EPS) * g


def _inproj_kernel(x_ref, g_ref, wa_ref, wb_ref, wc_ref, wd_ref, oa_ref, ob_ref, oc_ref, od_ref):
    hn = _rms(x_ref[...], g_ref[...]).astype(bf16)
    oa_ref[...] = jnp.dot(hn, wa_ref[...], preferred_element_type=f32)
    ob_ref[...] = jnp.dot(hn, wb_ref[...], preferred_element_type=f32)
    oc_ref[...] = jnp.dot(hn, wc_ref[...], preferred_element_type=f32)
    od_ref[...] = jnp.dot(hn, wd_ref[...], preferred_element_type=f32)


def _const_spec(shape):
    return pl.BlockSpec(shape, lambda *_: (0,) * len(shape))


def _inproj(x, g, wa, wb, wc, wd, tm):
    m = x.shape[0]
    widths = (A_PAD, B_PAD, C_COLS, D_COLS)
    return pl.pallas_call(
        _inproj_kernel,
        grid=(m // tm,),
        in_specs=[pl.BlockSpec((tm, D_MODEL), lambda i: (i, 0)), _const_spec((1, D_MODEL))]
        + [_const_spec((D_MODEL, w)) for w in widths],
        out_specs=[pl.BlockSpec((tm, w), lambda i: (i, 0)) for w in widths],
        out_shape=[jax.ShapeDtypeStruct((m, w), f32) for w in widths],
        compiler_params=pltpu.CompilerParams(dimension_semantics=("arbitrary",),
                                             vmem_limit_bytes=VMEM_LIMIT),
        name="inproj",
    )(x, g, wa, wb, wc, wd)


def _rwkv_features(us, w0, wup, a0, aup, kk_w, ka_w):
    r = us[:, 0:W_BR]
    k = us[:, W_BR:2 * W_BR]
    v = us[:, 2 * W_BR:3 * W_BR]
    lora = us[:, 3 * W_BR:A_PAD]
    w = -_softplus(-(w0 + _mm(jnp.tanh(lora), wup))) - 0.5
    logdecay = -jnp.exp(w)
    a = _sigmoid(a0 + _mm(lora, aup))
    kk = k * kk_w
    ss = _mm(kk * kk, _block_ones(W_BR, HEAD_A))
    kk = kk / jnp.maximum(jnp.sqrt(ss), 1e-12)
    kmod = k * (1.0 + (a - 1.0) * ka_w)
    return r, kmod, v, logdecay, kk, a


def _rwkv_finish(o, r, kmod, v, rk, lng, lnb):
    ones = _block_ones(W_BR, HEAD_A)
    mean = _mm(o, ones) * (1.0 / HEAD_A)
    cen = o - mean
    var = _mm(cen * cen, ones) * (1.0 / HEAD_A)
    o = cen * lax.rsqrt(var + GN_EPS_A) * lng + lnb
    bonus = _mm(r * kmod * rk, ones) * v
    return o + bonus


def _rwkv_prompt_kernel(u_ref, mu_ref, w0_ref, wup_ref, a0_ref, aup_ref, kkw_ref, kaw_ref, rk_ref,
                        lng_ref, lnb_ref, y_ref, s_ref, st_scr, prev_scr, o_scr):
    c = pl.program_id(1)
    T = RWKV_CHUNK

    @pl.when(c == 0)
    def _():
        st_scr[...] = jnp.zeros_like(st_scr)
        prev_scr[...] = jnp.zeros_like(prev_scr)

    u = u_ref[0]
    row = _iota(u.shape, 0)
    u_prev = jnp.where(row == 0, prev_scr[...], pltpu.roll(u, 1, 0))
    prev_scr[...] = u[T - 1:T, :]
    us = u + (u_prev - u) * mu_ref[...]
    r, kmod, v, ld, kk, a = _rwkv_features(us, w0_ref[...], wup_ref[...], a0_ref[...], aup_ref[...],
                                           kkw_ref[...], kaw_ref[...])
    cum = _mm(_tril_ones(T), ld)
    cum_last = cum[T - 1:T, :]
    g_inc = jnp.exp(cum)
    g_inv = jnp.exp(-cum)
    g_tail = jnp.exp(cum_last - cum)
    at = -kk * jnp.exp(cum - ld)
    kka = kk * a
    bt = kka * g_inv
    kt = kmod * g_inv
    rt = r * g_inc
    btg = kka * g_tail
    ktg = kmod * g_tail
    g_last = jnp.exp(cum_last)

    ti = _iota((T, T), 0)
    si = _iota((T, T), 1)
    strict = ti > si
    incl = ti >= si
    eye_t = (ti == si).astype(f32)
    eye_h = _eye(HEAD_A)

    for h in range(H_A):
        sl = slice(h * HEAD_A, (h + 1) * HEAD_A)
        at_h, bt_h, kt_h, rt_h, v_h = at[:, sl], bt[:, sl], kt[:, sl], rt[:, sl], v[:, sl]
        lab = jnp.where(strict, _mm_nt(at_h, bt_h), 0.0)
        lak = jnp.where(strict, _mm_nt(at_h, kt_h), 0.0)
        mrb = jnp.where(incl, _mm_nt(rt_h, bt_h), 0.0)
        mrk = jnp.where(incl, _mm_nt(rt_h, kt_h), 0.0)
        x = eye_t + lab
        pw = lab
        for _ in range(5):
            pw = _mm(pw, pw)
            x = x + _mm(x, pw)
        st = st_scr[h]
        uu = _mm(x, _mm(at_h, st) + _mm(lak, v_h))
        o_scr[:, sl] = _mm(rt_h, st) + _mm(mrb, uu) + _mm(mrk, v_h)
        st_scr[h] = (_mm(eye_h * g_last[:, sl], st) + _mm_tn(btg[:, sl], uu) + _mm_tn(ktg[:, sl], v_h))

    y_ref[0] = _rwkv_finish(o_scr[...], r, kmod, v, rk_ref[...], lng_ref[...], lnb_ref[...])

    @pl.when(c == pl.num_programs(1) - 1)
    def _():
        s_ref[0] = st_scr[...]


def _rwkv_prompt(pa, prm):
    b, l, _ = pa.shape
    T = RWKV_CHUNK
    row = lambda w: _const_spec((1, w))
    return pl.pallas_call(
        _rwkv_prompt_kernel,
        grid=(b, l // T),
        in_specs=[pl.BlockSpec((1, T, A_PAD), lambda i, c: (i, c, 0)), row(A_PAD), row(W_BR),
                  _const_spec((LANE, W_BR)), row(W_BR), _const_spec((LANE, W_BR)),
                  row(W_BR), row(W_BR), row(W_BR), row(W_BR), row(W_BR)],
        out_specs=[pl.BlockSpec((1, T, W_BR), lambda i, c: (i, c, 0)),
                   pl.BlockSpec((1, H_A, HEAD_A, HEAD_A), lambda i, c: (i, 0, 0, 0))],
        out_shape=[jax.ShapeDtypeStruct((b, l, W_BR), f32),
                   jax.ShapeDtypeStruct((b, H_A, HEAD_A, HEAD_A), f32)],
        scratch_shapes=[pltpu.VMEM((H_A, HEAD_A, HEAD_A), f32), pltpu.VMEM((1, A_PAD), f32),
                        pltpu.VMEM((T, W_BR), f32)],
        compiler_params=pltpu.CompilerParams(dimension_semantics=("arbitrary", "arbitrary"),
                                             vmem_limit_bytes=VMEM_LIMIT),
        name="rwkv_prompt",
    )(pa, prm["mu"], prm["w0"], prm["wup"], prm["a0"], prm["aup"], prm["kkw"], prm["kaw"], prm["rk"],
      prm["lng"], prm["lnb"])


def _gla_prompt_kernel(f_ref, up_ref, bias_ref, ng_ref, y_ref, s_ref, s_scr, o_scr):
    c = pl.program_id(1)
    T = GLA_CHUNK

    @pl.when(c == 0)
    def _():
        s_scr[...] = jnp.zeros_like(s_scr)

    f = f_ref[0]
    hk = H_B * DK_B
    q = f[:, 0:hk] * (DK_B ** -0.5)
    k = f[:, hk:2 * hk]
    v = f[:, 2 * hk:2 * hk + W_BR]
    gl = f[:, 2 * hk + W_BR:B_PAD]
    g = _log_sigmoid(_mm(gl, up_ref[...]) + bias_ref[...]) * (1.0 / GLA_TAU)
    bcum = _mm(_tril_ones(T), g)
    b_last = bcum[T - 1:T, :]
    qe = q * jnp.exp(bcum)
    ke = k * jnp.exp(-bcum)
    kl = k * jnp.exp(b_last - bcum)
    e_last = jnp.exp(b_last)
    causal = _iota((T, T), 0) >= _iota((T, T), 1)
    eye_k = _eye(DK_B)
    for h in range(H_B):
        ks = slice(h * DK_B, (h + 1) * DK_B)
        vs = slice(h * DV_B, (h + 1) * DV_B)
        s0 = s_scr[h]
        att = jnp.where(causal, _mm_nt(qe[:, ks], ke[:, ks]), 0.0)
        o_scr[:, vs] = _mm(att, v[:, vs]) + _mm(qe[:, ks], s0)
        s_scr[h] = _mm(eye_k * e_last[:, ks], s0) + _mm_tn(kl[:, ks], v[:, vs])
    o = o_scr[...]
    ms = _mm(o * o, _block_ones(W_BR, DV_B)) * (1.0 / DV_B)
    y_ref[0] = o * lax.rsqrt(ms + NORM_EPS) * ng_ref[...]

    @pl.when(c == pl.num_programs(1) - 1)
    def _():
        s_ref[0] = s_scr[...]


def _gla_prompt(pb, prm):
    b, l, _ = pb.shape
    T = GLA_CHUNK
    return pl.pallas_call(
        _gla_prompt_kernel,
        grid=(b, l // T),
        in_specs=[pl.BlockSpec((1, T, B_PAD), lambda i, c: (i, c, 0)),
                  _const_spec((LANE, H_B * DK_B)), _const_spec((1, H_B * DK_B)), _const_spec((1, W_BR))],
        out_specs=[pl.BlockSpec((1, T, W_BR), lambda i, c: (i, c, 0)),
                   pl.BlockSpec((1, H_B, DK_B, DV_B), lambda i, c: (i, 0, 0, 0))],
        out_shape=[jax.ShapeDtypeStruct((b, l, W_BR), f32),
                   jax.ShapeDtypeStruct((b, H_B, DK_B, DV_B), f32)],
        scratch_shapes=[pltpu.VMEM((H_B, DK_B, DV_B), f32), pltpu.VMEM((T, W_BR), f32)],
        compiler_params=pltpu.CompilerParams(dimension_semantics=("arbitrary", "arbitrary"),
                                             vmem_limit_bytes=VMEM_LIMIT),
        name="gla_prompt",
    )(pb, prm["gup"], prm["gbias"], prm["gng"])


def _swa_prompt_kernel(cur_ref, prev_ref, sink_ref, y_ref):
    i = pl.program_id(1)
    W = WINDOW
    cur = cur_ref[0]
    prev = prev_ref[0]
    qo, ko, vo = 0, H_C * HD_C, H_C * HD_C + KV_C * HD_C
    t = _iota((W, W), 0)
    s = _iota((W, W), 1)
    dist_prev = (W + t - s).astype(f32)
    dist_cur = (t - s).astype(f32)
    ok_prev = (s >= t) & (i > 0)
    ok_cur = s <= t
    scale = HD_C ** -0.5
    for g in range(KV_C):
        k_cur = cur[:, ko + g * HD_C:ko + (g + 1) * HD_C]
        k_prev = prev[:, ko + g * HD_C:ko + (g + 1) * HD_C]
        v_cur = cur[:, vo + g * HD_C:vo + (g + 1) * HD_C]
        v_prev = prev[:, vo + g * HD_C:vo + (g + 1) * HD_C]
        for j in range(G_C):
            h = g * G_C + j
            qh = cur[:, qo + h * HD_C:qo + (h + 1) * HD_C]
            sp = jnp.where(ok_prev, _bdot_nt(qh, k_prev) * scale - ALIBI_SLOPES[h] * dist_prev, NEG_BIG)
            sc = jnp.where(ok_cur, _bdot_nt(qh, k_cur) * scale - ALIBI_SLOPES[h] * dist_cur, NEG_BIG)
            sink = sink_ref[:, h:h + 1]
            m = jnp.maximum(jnp.maximum(jnp.max(sp, -1, keepdims=True), jnp.max(sc, -1, keepdims=True)), sink)
            pp = jnp.exp(sp - m)
            pc = jnp.exp(sc - m)
            den = jnp.sum(pp, -1, keepdims=True) + jnp.sum(pc, -1, keepdims=True) + jnp.exp(sink - m)
            out = _bdot(pp, v_prev) + _bdot(pc, v_cur)
            y_ref[0, :, h * HD_C:(h + 1) * HD_C] = out / den


def _swa_prompt(pc, sinks):
    b, l, _ = pc.shape
    W = WINDOW
    return pl.pallas_call(
        _swa_prompt_kernel,
        grid=(b, l // W),
        in_specs=[pl.BlockSpec((1, W, C_COLS), lambda i, c: (i, c, 0)),
                  pl.BlockSpec((1, W, C_COLS), lambda i, c: (i, jnp.maximum(c - 1, 0), 0)),
                  _const_spec((1, H_C))],
        out_specs=pl.BlockSpec((1, W, W_BR), lambda i, c: (i, c, 0)),
        out_shape=jax.ShapeDtypeStruct((b, l, W_BR), f32),
        compiler_params=pltpu.CompilerParams(dimension_semantics=("arbitrary", "arbitrary"),
                                             vmem_limit_bytes=VMEM_LIMIT),
        name="swa_prompt",
    )(pc, pc, sinks)


def _lru_gates(xc, wa, ba, wx, bx, lam):
    r = _sigmoid(_bdot(xc, wa) + ba)
    i = _sigmoid(_bdot(xc, wx) + bx)
    log_a = C_RG * r * _log_sigmoid(lam)
    a = jnp.exp(log_a)
    bterm = jnp.sqrt(1.0 - jnp.exp(2.0 * log_a)) * (i * xc)
    return a, bterm


def _lru_prompt_kernel(x_ref, cw_ref, cb_ref, wa_ref, ba_ref, wx_ref, bx_ref, lam_ref, y_ref,
                       xbuf_scr, h_scr):
    c = pl.program_id(1)
    T = LRU_CHUNK
    PADR = 8

    @pl.when(c == 0)
    def _():
        xbuf_scr[0:PADR, :] = jnp.zeros((PADR, W_BR), f32)
        h_scr[...] = jnp.zeros_like(h_scr)

    x = x_ref[0]
    xbuf_scr[PADR:PADR + T, :] = x
    xc = cb_ref[...] + x * cw_ref[CONV_W - 1:CONV_W, :]
    for j in range(1, CONV_W):
        xc = xc + xbuf_scr[PADR - j:PADR - j + T, :] * cw_ref[CONV_W - 1 - j:CONV_W - j, :]
    xbuf_scr[0:PADR, :] = x[T - PADR:T, :]
    a, bv = _lru_gates(xc, wa_ref[...], ba_ref[...], wx_ref[...], bx_ref[...], lam_ref[...])
    row = _iota((T, W_BR), 0)
    d = 1
    while d < T:
        keep = row >= d
        a_sh = jnp.where(keep, pltpu.roll(a, d, 0), 1.0)
        b_sh = jnp.where(keep, pltpu.roll(bv, d, 0), 0.0)
        bv = a * b_sh + bv
        a = a * a_sh
        d *= 2
    h = a * h_scr[...] + bv
    y_ref[0] = h
    h_scr[...] = h[T - 1:T, :]


def _lru_prompt(pd, prm):
    b, l, _ = pd.shape
    T = LRU_CHUNK
    row = lambda: _const_spec((1, W_BR))
    return pl.pallas_call(
        _lru_prompt_kernel,
        grid=(b, l // T),
        in_specs=[pl.BlockSpec((1, T, W_BR), lambda i, c: (i, c, 0)), _const_spec((CONV_W, W_BR)), row(),
                  _const_spec((W_BR, W_BR)), row(), _const_spec((W_BR, W_BR)), row(), row()],
        out_specs=pl.BlockSpec((1, T, W_BR), lambda i, c: (i, c, 0)),
        out_shape=jax.ShapeDtypeStruct((b, l, W_BR), f32),
        scratch_shapes=[pltpu.VMEM((T + 8, W_BR), f32), pltpu.VMEM((1, W_BR), f32)],
        compiler_params=pltpu.CompilerParams(dimension_semantics=("arbitrary", "arbitrary"),
                                             vmem_limit_bytes=VMEM_LIMIT),
        name="lru_prompt",
    )(pd, prm["cw"], prm["cb"], prm["lwa"], prm["lba"], prm["lwx"], prm["lbx"], prm["lam"])


def _decode_kernel(pa_ref, pb_ref, pc_ref, pd_ref, shift_ref, swkv_ref, sgla_ref, kbuf_ref, vbuf_ref,
                   c0_ref, c1_ref, c2_ref, h0_ref,
                   mu_ref, w0_ref, wup_ref, a0_ref, aup_ref, kkw_ref, kaw_ref, rk_ref, lng_ref, lnb_ref,
                   gup_ref, gbias_ref, gng_ref, sink_ref,
                   cw_ref, cb_ref, wa_ref, ba_ref, wx_ref, bx_ref, lam_ref,
                   ya_ref, yb_ref, yc_ref, yd_ref, swkv_out, sgla_out,
                   oa_scr, ob_scr):
    bt = pa_ref.shape[0]
    u = pa_ref[...]
    us = u + (shift_ref[...] - u) * mu_ref[...]
    r, kmod, v, ld, kk, a = _rwkv_features(us, w0_ref[...], wup_ref[...], a0_ref[...], aup_ref[...],
                                           kkw_ref[...], kaw_ref[...])
    decay = jnp.exp(ld)
    kka = kk * a
    fb = pb_ref[...]
    hk = H_B * DK_B
    qb = fb[:, 0:hk] * (DK_B ** -0.5)
    kb = fb[:, hk:2 * hk]
    vb = fb[:, 2 * hk:2 * hk + W_BR]
    gb = _log_sigmoid(_mm(fb[:, 2 * hk + W_BR:B_PAD], gup_ref[...]) + gbias_ref[...]) * (1.0 / GLA_TAU)
    eg = jnp.exp(gb)
    qeg = qb * eg
    fc = pc_ref[...]

    eye_a = _eye(HEAD_A)
    eye_b = _eye(DK_B)
    wdist = (WINDOW - _iota((WINDOW, 1), 0)).astype(f32)
    scale = HD_C ** -0.5

    for b in range(bt):
        rw = lambda t: t[b:b + 1, :]
        r_r, km_r, v_r, w_r, kk_r, kka_r = rw(r), rw(kmod), rw(v), rw(decay), rw(kk), rw(kka)
        for h in range(H_A):
            sl = slice(h * HEAD_A, (h + 1) * HEAD_A)
            s = swkv_ref[b, h]
            v_c = jnp.sum(eye_a * v_r[:, sl], axis=1, keepdims=True)
            sa_c = -jnp.sum(s * kk_r[:, sl], axis=1, keepdims=True)
            s_new = s * w_r[:, sl] + sa_c * kka_r[:, sl] + v_c * km_r[:, sl]
            swkv_out[b, h] = s_new
            o_c = jnp.sum(s_new * r_r[:, sl], axis=1, keepdims=True)
            oa_scr[b:b + 1, sl] = jnp.sum(eye_a * o_c, axis=0, keepdims=True)
        q_all, k_all, eg_all, qe_all, vb_r = rw(qb), rw(kb), rw(eg), rw(qeg), rw(vb)
        for h in range(H_B):
            ks = slice(h * DK_B, (h + 1) * DK_B)
            vs = slice(h * DV_B, (h + 1) * DV_B)
            s = sgla_ref[b, h]
            q_r, k_r, eg_r, qe_r = q_all[:, ks], k_all[:, ks], eg_all[:, ks], qe_all[:, ks]
            k_c = jnp.sum(eye_b * k_r, axis=1, keepdims=True)
            eg_c = jnp.sum(eye_b * eg_r, axis=1, keepdims=True)
            qe_c = jnp.sum(eye_b * qe_r, axis=1, keepdims=True)
            qk = jnp.sum(q_r * k_r, axis=1, keepdims=True)
            ob_scr[b:b + 1, vs] = qk * vb_r[:, vs] + jnp.sum(qe_c * s, axis=0, keepdims=True)
            sgla_out[b, h] = eg_c * s + k_c * vb_r[:, vs]
        q_row, kv_new = rw(fc[:, 0:W_BR]), rw(fc[:, W_BR:2 * W_BR])
        kc = kbuf_ref[b]
        vc = vbuf_ref[b]
        for g in range(KV_C):
            gs = slice(g * HD_C, (g + 1) * HD_C)
            kg, vg = kc[:, gs], vc[:, gs]
            kn = kv_new[:, gs]
            vn = kv_new[:, KV_C * HD_C + g * HD_C:KV_C * HD_C + (g + 1) * HD_C]
            for j in range(G_C):
                h = g * G_C + j
                qh = q_row[:, h * HD_C:(h + 1) * HD_C]
                sc = jnp.sum(kg * qh, axis=1, keepdims=True) * scale - ALIBI_SLOPES[h] * wdist
                sn = jnp.sum(kn * qh, axis=1, keepdims=True) * scale
                sink = sink_ref[:, h:h + 1]
                m = jnp.maximum(jnp.maximum(jnp.max(sc, axis=0, keepdims=True), sn), sink)
                p = jnp.exp(sc - m)
                pn = jnp.exp(sn - m)
                den = jnp.sum(p, axis=0, keepdims=True) + pn + jnp.exp(sink - m)
                out = (jnp.sum(p * vg, axis=0, keepdims=True) + pn * vn) / den
                yc_ref[b:b + 1, h * HD_C:(h + 1) * HD_C] = out

    ya_ref[...] = _rwkv_finish(oa_scr[...], r, kmod, v, rk_ref[...], lng_ref[...], lnb_ref[...])
    ob = ob_scr[...]
    ms = _mm(ob * ob, _block_ones(W_BR, DV_B)) * (1.0 / DV_B)
    yb_ref[...] = ob * lax.rsqrt(ms + NORM_EPS) * gng_ref[...]
    xd = pd_ref[...]
    xc = (cb_ref[...] + c0_ref[...] * cw_ref[0:1, :] + c1_ref[...] * cw_ref[1:2, :]
          + c2_ref[...] * cw_ref[2:3, :] + xd * cw_ref[3:4, :])
    al, bterm = _lru_gates(xc, wa_ref[...], ba_ref[...], wx_ref[...], bx_ref[...], lam_ref[...])
    yd_ref[...] = al * h0_ref[...] + bterm


def _decode(pa, pb, pc, pd, shift0, swkv, sgla, kbuf, vbuf, c0, c1, c2, h0, prm, bt):
    n = pa.shape[0]
    rows = lambda w: pl.BlockSpec((bt, w), lambda i: (i, 0))
    crow = lambda w: _const_spec((1, w))
    in_specs = [rows(A_PAD), rows(B_PAD), rows(C_COLS), rows(D_COLS), rows(A_PAD),
                pl.BlockSpec((bt, H_A, HEAD_A, HEAD_A), lambda i: (i, 0, 0, 0)),
                pl.BlockSpec((bt, H_B, DK_B, DV_B), lambda i: (i, 0, 0, 0)),
                pl.BlockSpec((bt, WINDOW, KV_C * HD_C), lambda i: (i, 0, 0)),
                pl.BlockSpec((bt, WINDOW, KV_C * HD_C), lambda i: (i, 0, 0)),
                rows(W_BR), rows(W_BR), rows(W_BR), rows(W_BR),
                crow(A_PAD), crow(W_BR), _const_spec((LANE, W_BR)), crow(W_BR), _const_spec((LANE, W_BR)),
                crow(W_BR), crow(W_BR), crow(W_BR), crow(W_BR), crow(W_BR),
                _const_spec((LANE, H_B * DK_B)), crow(H_B * DK_B), crow(W_BR), crow(H_C),
                _const_spec((CONV_W, W_BR)), crow(W_BR), _const_spec((W_BR, W_BR)), crow(W_BR),
                _const_spec((W_BR, W_BR)), crow(W_BR), crow(W_BR)]
    out_specs = [rows(W_BR), rows(W_BR), rows(W_BR), rows(W_BR),
                 pl.BlockSpec((bt, H_A, HEAD_A, HEAD_A), lambda i: (i, 0, 0, 0)),
                 pl.BlockSpec((bt, H_B, DK_B, DV_B), lambda i: (i, 0, 0, 0))]
    out_shape = [jax.ShapeDtypeStruct((n, W_BR), f32)] * 4 + [
        jax.ShapeDtypeStruct((n, H_A, HEAD_A, HEAD_A), f32),
        jax.ShapeDtypeStruct((n, H_B, DK_B, DV_B), f32)]
    return pl.pallas_call(
        _decode_kernel,
        grid=(n // bt,),
        in_specs=in_specs,
        out_specs=out_specs,
        out_shape=out_shape,
        scratch_shapes=[pltpu.VMEM((bt, W_BR), f32), pltpu.VMEM((bt, W_BR), f32)],
        compiler_params=pltpu.CompilerParams(dimension_semantics=("arbitrary",),
                                             vmem_limit_bytes=VMEM_LIMIT),
        name="decode_mixers",
    )(pa, pb, pc, pd, shift0, swkv, sgla, kbuf, vbuf, c0, c1, c2, h0,
      prm["mu"], prm["w0"], prm["wup"], prm["a0"], prm["aup"], prm["kkw"], prm["kaw"], prm["rk"],
      prm["lng"], prm["lnb"], prm["gup"], prm["gbias"], prm["gng"], prm["sinks"],
      prm["cw"], prm["cb"], prm["lwa"], prm["lba"], prm["lwx"], prm["lbx"], prm["lam"])


def _merge_kernel(x_ref, ya_ref, yb_ref, yc_ref, yd_ref, g_ref, wz_ref, wg_ref, wbr_ref, wout_ref,
                  fg_ref, o_ref, *, final):
    x = x_ref[...]
    hn = _rms(x, g_ref[...]).astype(bf16)
    ys = (ya_ref, yb_ref, yc_ref, yd_ref)
    merged = None
    for n in range(N_BRANCH):
        z = jnp.dot(hn, wz_ref[:, n * W_BR:(n + 1) * W_BR], preferred_element_type=f32)
        yz = ys[n][...] * (z * _sigmoid(z))
        br = jnp.dot(yz.astype(bf16), wbr_ref[n], preferred_element_type=f32)
        gate = _sigmoid(jnp.dot(hn, wg_ref[:, n * D_MODEL:(n + 1) * D_MODEL], preferred_element_type=f32))
        merged = gate * br if merged is None else merged + gate * br
    out = x + jnp.dot(merged.astype(bf16), wout_ref[...], preferred_element_type=f32)
    if final:
        out = _rms(out, fg_ref[...])
    o_ref[...] = out


def _merge(x, ya, yb, yc, yd, g, wz, wg, wbr, wout, fg, tm, final):
    m = x.shape[0]
    tile = lambda w: pl.BlockSpec((tm, w), lambda i: (i, 0))
    single = lambda shape: pl.BlockSpec(shape, lambda *_: (0,) * len(shape), pipeline_mode=pl.Buffered(1))
    return pl.pallas_call(
        functools.partial(_merge_kernel, final=final),
        grid=(m // tm,),
        in_specs=[tile(D_MODEL), tile(W_BR), tile(W_BR), tile(W_BR), tile(W_BR), _const_spec((1, D_MODEL)),
                  single((D_MODEL, Z_COLS)), single((D_MODEL, G_COLS)),
                  single((N_BRANCH, W_BR, D_MODEL)), single((D_MODEL, D_MODEL)), _const_spec((1, D_MODEL))],
        out_specs=tile(D_MODEL),
        out_shape=jax.ShapeDtypeStruct((m, D_MODEL), f32),
        compiler_params=pltpu.CompilerParams(dimension_semantics=("arbitrary",),
                                             vmem_limit_bytes=VMEM_LIMIT),
        name="merge_final" if final else "merge",
    )(x, ya, yb, yc, yd, g, wz, wg, wbr, wout, fg)


def _pad_cols(w, width):
    return jnp.pad(w, ((0, 0), (0, width - w.shape[1])))


def _pad_rows_at(w, start, total):
    return jnp.pad(w, ((start, total - start - w.shape[0]), (0, 0)))


def _block_diag(w):
    nb, bs, _ = w.shape
    out = jnp.zeros((nb * bs, nb * bs), w.dtype)
    for n in range(nb):
        out = out.at[n * bs:(n + 1) * bs, n * bs:(n + 1) * bs].set(w[n])
    return out


def _layer_params(l, norm_g, w_in, mu_shift, w0, w_decay_up, a0, a_icl_up, k_k, k_a, r_k, ln_x_g, ln_x_b,
                  gla_gate_up, gla_gate_b, gla_norm_g, swa_sinks, lru_conv_w, lru_conv_b, lru_wa, lru_ba,
                  lru_wx, lru_bx, lru_lambda, w_branch, w_out):
    w = w_in[l]
    o_b = A_COLS
    o_c = o_b + B_COLS
    o_d = o_c + C_COLS
    o_z = o_d + D_COLS
    o_g = o_z + Z_COLS
    row = lambda t: t.reshape(1, -1)
    return dict(
        g=row(norm_g[l]),
        wa=_pad_cols(w[:, 0:o_b], A_PAD).astype(bf16),
        wb=_pad_cols(w[:, o_b:o_c], B_PAD).astype(bf16),
        wc=w[:, o_c:o_d].astype(bf16),
        wd=w[:, o_d:o_z].astype(bf16),
        wz=w[:, o_z:o_g].astype(bf16),
        wg=w[:, o_g:].astype(bf16),
        wbr=w_branch[l].astype(bf16),
        wout=w_out[l].astype(bf16),
        mu=_pad_cols(row(mu_shift[l]), A_PAD),
        w0=row(w0[l]),
        wup=_pad_rows_at(w_decay_up[l], 0, LANE),
        a0=row(a0[l]),
        aup=_pad_rows_at(a_icl_up[l], R_DECAY, LANE),
        kkw=row(k_k[l]), kaw=row(k_a[l]), rk=row(r_k[l]), lng=row(ln_x_g[l]), lnb=row(ln_x_b[l]),
        gup=_pad_rows_at(gla_gate_up[l], 0, LANE),
        gbias=row(gla_gate_b[l]),
        gng=row(jnp.tile(gla_norm_g[l], H_B)),
        sinks=row(swa_sinks[l]),
        cw=lru_conv_w[l], cb=row(lru_conv_b[l]),
        lwa=_block_diag(lru_wa[l]).astype(bf16), lba=row(lru_ba[l]),
        lwx=_block_diag(lru_wx[l]).astype(bf16), lbx=row(lru_bx[l]),
        lam=row(lru_lambda[l]),
    )


def kernel(x_prompt, x_sample, state_wkv, state_shift, state_gla, cache_swa_k, cache_swa_v, state_lru_conv, state_lru_h, norm_g, w_in, mu_shift, w0, w_decay_up, a0, a_icl_up, k_k, k_a, r_k, ln_x_g, ln_x_b, gla_gate_up, gla_gate_b, gla_norm_g, swa_sinks, lru_conv_w, lru_conv_b, lru_wa, lru_ba, lru_wx, lru_bx, lru_lambda, w_branch, w_out, final_norm_g):
    bp, lp, _ = x_prompt.shape
    bs = x_sample.shape[0]
    depth = w_in.shape[0]
    fg = final_norm_g.reshape(1, -1)
    xp = x_prompt.reshape(bp * lp, D_MODEL)
    xs = x_sample.reshape(bs, D_MODEL)
    outs_p = [[] for _ in range(7)]
    outs_s = [[] for _ in range(7)]
    for l in range(depth):
        prm = _layer_params(l, norm_g, w_in, mu_shift, w0, w_decay_up, a0, a_icl_up, k_k, k_a, r_k, ln_x_g,
                            ln_x_b, gla_gate_up, gla_gate_b, gla_norm_g, swa_sinks, lru_conv_w, lru_conv_b,
                            lru_wa, lru_ba, lru_wx, lru_bx, lru_lambda, w_branch, w_out)
        final = l == depth - 1
        pa, pb, pc, pd = _inproj(xp, prm["g"], prm["wa"], prm["wb"], prm["wc"], prm["wd"], tm=512)
        pa3, pb3 = pa.reshape(bp, lp, A_PAD), pb.reshape(bp, lp, B_PAD)
        pc3, pd3 = pc.reshape(bp, lp, C_COLS), pd.reshape(bp, lp, D_COLS)
        ya, st_t = _rwkv_prompt(pa3, prm)
        yb, sgla = _gla_prompt(pb3, prm)
        yc = _swa_prompt(pc3, prm["sinks"])
        yd = _lru_prompt(pd3, prm)
        flat = lambda t: t.reshape(bp * lp, W_BR)
        xp = _merge(xp, flat(ya), flat(yb), flat(yc), flat(yd), prm["g"], prm["wz"], prm["wg"], prm["wbr"],
                    prm["wout"], fg, tm=256, final=final)
        kv = pc3[:, lp - WINDOW:, H_C * HD_C:]
        outs_p[0].append(jnp.swapaxes(st_t, -1, -2))
        outs_p[1].append(pa3[:, lp - 1, :A_COLS])
        outs_p[2].append(sgla)
        outs_p[3].append(kv[:, :, :KV_C * HD_C].reshape(bp, WINDOW, KV_C, HD_C))
        outs_p[4].append(kv[:, :, KV_C * HD_C:].reshape(bp, WINDOW, KV_C, HD_C))
        outs_p[5].append(pd3[:, lp - (CONV_W - 1):, :])
        outs_p[6].append(yd[:, lp - 1, :])
        sa, sb, sc, sd = _inproj(xs, prm["g"], prm["wa"], prm["wb"], prm["wc"], prm["wd"], tm=bs)
        kbuf = cache_swa_k[l].reshape(bs, WINDOW, KV_C * HD_C)
        vbuf = cache_swa_v[l].reshape(bs, WINDOW, KV_C * HD_C)
        conv0 = state_lru_conv[l]
        ya_s, yb_s, yc_s, yd_s, wkv1, gla1 = _decode(
            sa, sb, sc, sd, _pad_cols(state_shift[l], A_PAD), state_wkv[l], state_gla[l], kbuf, vbuf,
            conv0[:, 0], conv0[:, 1], conv0[:, 2], state_lru_h[l], prm, bt=8)
        xs = _merge(xs, ya_s, yb_s, yc_s, yd_s, prm["g"], prm["wz"], prm["wg"], prm["wbr"], prm["wout"], fg,
                    tm=bs, final=final)
        k_new = sc[:, W_BR:W_BR + KV_C * HD_C]
        v_new = sc[:, W_BR + KV_C * HD_C:]
        outs_s[0].append(wkv1)
        outs_s[1].append(sa[:, :A_COLS])
        outs_s[2].append(gla1)
        outs_s[3].append(jnp.concatenate([kbuf[:, 1:], k_new[:, None]], axis=1).reshape(bs, WINDOW, KV_C, HD_C))
        outs_s[4].append(jnp.concatenate([vbuf[:, 1:], v_new[:, None]], axis=1).reshape(bs, WINDOW, KV_C, HD_C))
        outs_s[5].append(jnp.stack([conv0[:, 1], conv0[:, 2], sd], axis=1))
        outs_s[6].append(yd_s)
    y_prompt = xp.reshape(bp, lp, D_MODEL)
    y_sample = xs.reshape(bs, 1, D_MODEL)
    sp = [jnp.stack(t) for t in outs_p]
    ss = [jnp.stack(t) for t in outs_s]
    return (y_prompt, y_sample, sp[0], ss[0], sp[1], ss[1], sp[2], ss[2], sp[3], ss[3], sp[4], ss[4],
            sp[5], ss[5], sp[6], ss[6])
```

```python
import functools

import jax
import jax.numpy as jnp
from jax import lax
from jax.experimental import pallas as pl
from jax.experimental.pallas import tpu as pltpu

f32 = jnp.float32
bf16 = jnp.bfloat16
HI = lax.Precision.HIGHEST

D_MODEL = 1024
N_BRANCH = 4
W_BR = 256
HEAD_A = 64
H_A = 4
R_DECAY = 32
R_ICL = 32
GN_EPS_A = 64e-5
H_B = 4
DK_B = 32
DV_B = 64
R_GATE_B = 16
GLA_TAU = 16.0
GLA_CHUNK = 64
GLA_STEP_ROWS = 256
H_C = 4
KV_C = 2
HD_C = 64
G_C = 2
WINDOW = 128
CONV_W = 4
C_RG = 8.0
NORM_EPS = 1e-6

A_COLS = 3 * W_BR + R_DECAY + R_ICL
B_COLS = 2 * H_B * DK_B + W_BR + R_GATE_B
C_COLS = H_C * HD_C + 2 * KV_C * HD_C
D_COLS = W_BR
Z_COLS = N_BRANCH * W_BR
G_COLS = N_BRANCH * D_MODEL

LANE = 128
A_PAD = 7 * LANE
B_PAD = 5 * LANE
RWKV_CHUNK = 64
RWKV_STEP_ROWS = 256
RWKV_PASSES = 1
GLA_PASSES = 1
LRU_CHUNK = 256
VMEM_LIMIT = 56 * 1024 * 1024
NEG_BIG = -1e30

ALIBI_SLOPES = tuple(2.0 ** (-8.0 * (h + 1) / H_C) for h in range(H_C))


def _mm(a, b):
    return jnp.dot(a, b, precision=HI, preferred_element_type=f32)


def _mm_nt(a, b):
    return lax.dot_general(a, b, (((1,), (1,)), ((), ())), precision=HI, preferred_element_type=f32)


def _mm_tn(a, b):
    return lax.dot_general(a, b, (((0,), (0,)), ((), ())), precision=HI, preferred_element_type=f32)


def _bdot(a, b):
    return jnp.dot(a.astype(bf16), b.astype(bf16), preferred_element_type=f32)


def _bdot_nt(a, b):
    return lax.dot_general(a.astype(bf16), b.astype(bf16), (((1,), (1,)), ((), ())),
                           preferred_element_type=f32)


NN = ((1,), (0,))
NT = ((1,), (1,))
TN = ((0,), (0,))


def _dg(a, b, dims):
    return lax.dot_general(a, b, (dims, ((), ())), preferred_element_type=f32)


def _split_bf16(a):
    hi = a.astype(bf16)
    return hi, (a - hi.astype(f32)).astype(bf16)


def _dotp(a, b, dims, passes):
    if passes == 1:
        return _dg(a.astype(bf16), b.astype(bf16), dims)
    ah, al = _split_bf16(a)
    bh, bl = _split_bf16(b)
    return _dg(ah, bh, dims) + (_dg(ah, bl, dims) + _dg(al, bh, dims))


def _ones_dot(a, ones_bf16):
    ah, al = _split_bf16(a)
    return _dg(ah, ones_bf16, NN) + _dg(al, ones_bf16, NN)


def _cumsum_rows(x, seg):
    pos = _iota(x.shape, 0) % seg
    d = 1
    while d < seg:
        x = x + jnp.where(pos >= d, pltpu.roll(x, d, 0), 0.0)
        d *= 2
    return x


def _iota(shape, dim):
    return lax.broadcasted_iota(jnp.int32, shape, dim)


def _eye(n):
    return (_iota((n, n), 0) == _iota((n, n), 1)).astype(f32)


def _block_ones(n, blk):
    return ((_iota((n, n), 0) // blk) == (_iota((n, n), 1) // blk)).astype(f32)


def _tril_ones(n):
    return (_iota((n, n), 0) >= _iota((n, n), 1)).astype(f32)


def _softplus(x):
    return jnp.maximum(x, 0.0) + jnp.log(1.0 + jnp.exp(-jnp.abs(x)))


def _log_sigmoid(x):
    return -_softplus(-x)


def _sigmoid(x):
    return 1.0 / (1.0 + jnp.exp(-x))


def _rms(x, g):
    return x * lax.rsqrt(jnp.mean(x * x, -1, keepdims=True) + NORM_EPS) * g


def _inproj_kernel(x_ref, g_ref, wa_ref, wb_ref, wc_ref, wd_ref, oa_ref, ob_ref, oc_ref, od_ref):
    hn = _rms(x_ref[...], g_ref[...]).astype(bf16)
    oa_ref[...] = jnp.dot(hn, wa_ref[...], preferred_element_type=f32)
    ob_ref[...] = jnp.dot(hn, wb_ref[...], preferred_element_type=f32)
    oc_ref[...] = jnp.dot(hn, wc_ref[...], preferred_element_type=f32)
    od_ref[...] = jnp.dot(hn, wd_ref[...], preferred_element_type=f32)


def _const_spec(shape):
    return pl.BlockSpec(shape, lambda *_: (0,) * len(shape))


def _inproj(x, g, wa, wb, wc, wd, tm):
    m = x.shape[0]
    widths = (A_PAD, B_PAD, C_COLS, D_COLS)
    return pl.pallas_call(
        _inproj_kernel,
        grid=(m // tm,),
        in_specs=[pl.BlockSpec((tm, D_MODEL), lambda i: (i, 0)), _const_spec((1, D_MODEL))]
        + [_const_spec((D_MODEL, w)) for w in widths],
        out_specs=[pl.BlockSpec((tm, w), lambda i: (i, 0)) for w in widths],
        out_shape=[jax.ShapeDtypeStruct((m, w), f32) for w in widths],
        compiler_params=pltpu.CompilerParams(dimension_semantics=("arbitrary",),
                                             vmem_limit_bytes=VMEM_LIMIT),
        name="inproj",
    )(x, g, wa, wb, wc, wd)


def _rwkv_features(us, w0, wup, a0, aup, kk_w, ka_w):
    r = us[:, 0:W_BR]
    k = us[:, W_BR:2 * W_BR]
    v = us[:, 2 * W_BR:3 * W_BR]
    lora = us[:, 3 * W_BR:A_PAD]
    w = -_softplus(-(w0 + _bdot(jnp.tanh(lora), wup))) - 0.5
    logdecay = -jnp.exp(w)
    a = _sigmoid(a0 + _bdot(lora, aup))
    kk = k * kk_w
    ss = _ones_dot(kk * kk, _block_ones(W_BR, HEAD_A).astype(bf16))
    kk = kk / jnp.maximum(jnp.sqrt(ss), 1e-12)
    kmod = k * (1.0 + (a - 1.0) * ka_w)
    return r, kmod, v, logdecay, kk, a


def _rwkv_finish(o, r, kmod, v, rk, lng, lnb):
    ones = _block_ones(W_BR, HEAD_A).astype(bf16)
    mean = _ones_dot(o, ones) * (1.0 / HEAD_A)
    cen = o - mean
    var = _ones_dot(cen * cen, ones) * (1.0 / HEAD_A)
    o = cen * lax.rsqrt(var + GN_EPS_A) * lng + lnb
    bonus = _ones_dot(r * kmod * rk, ones) * v
    return o + bonus


def _rwkv_prompt_kernel(u_ref, mu_ref, w0_ref, wup_ref, a0_ref, aup_ref, kkw_ref, kaw_ref, rk_ref,
                        lng_ref, lnb_ref, y_ref, s_ref, st_scr, prev_scr, o_scr):
    step = pl.program_id(1)
    T = RWKV_CHUNK
    TT = RWKV_STEP_ROWS

    @pl.when(step == 0)
    def _():
        st_scr[...] = jnp.zeros_like(st_scr)
        prev_scr[...] = jnp.zeros_like(prev_scr)

    u = u_ref[0]
    row = _iota(u.shape, 0)
    u_prev = jnp.where(row == 0, prev_scr[...], pltpu.roll(u, 1, 0))
    prev_scr[...] = u[TT - 1:TT, :]
    us = u + (u_prev - u) * mu_ref[...]
    r, kmod, v, ld, kk, a = _rwkv_features(us, w0_ref[...], wup_ref[...], a0_ref[...], aup_ref[...],
                                           kkw_ref[...], kaw_ref[...])
    P = RWKV_PASSES
    NC = TT // T
    cum = _cumsum_rows(ld, T)
    cum_last = jnp.concatenate(
        [jnp.broadcast_to(cum[(c + 1) * T - 1:(c + 1) * T, :], (T, W_BR)) for c in range(NC)], axis=0)
    g_inv = jnp.exp(-cum)
    g_tail = jnp.exp(cum_last - cum)
    kka = kk * a
    at = -kk * jnp.exp(cum - ld)
    rt = r * jnp.exp(cum)
    bt = kka * g_inv
    kt = kmod * g_inv
    btg = kka * g_tail
    ktg = kmod * g_tail

    ri = _iota((2 * T, 2 * T), 0)
    ci = _iota((2 * T, 2 * T), 1)
    ti = jnp.where(ri >= T, ri - T, ri)
    si = jnp.where(ci >= T, ci - T, ci)
    keep = (ti > si) | ((ri >= T) & (ti == si))
    eye_t = _eye(T)
    eye_h = _eye(HEAD_A)

    pieces = [(c, h) for c in range(NC) for h in range(H_A)]
    ar, vh, pm, x, pw = {}, {}, {}, {}, {}
    for c, h in pieces:
        rows = slice(c * T, (c + 1) * T)
        sl = slice(h * HEAD_A, (h + 1) * HEAD_A)
        vh[c, h] = v[rows, sl]
        ar[c, h] = jnp.concatenate([at[rows, sl], rt[rows, sl]], axis=0)
        bk = jnp.concatenate([bt[rows, sl], kt[rows, sl]], axis=0)
        pm[c, h] = jnp.where(keep, _dotp(ar[c, h], bk, NT, P), 0.0)
    for c, h in pieces:
        lab = pm[c, h][0:T, 0:T]
        x[c, h] = eye_t + lab
        pw[c, h] = _dotp(lab, lab, NN, P)
    for it in range(5):
        for c, h in pieces:
            x_next = x[c, h] + _dotp(pw[c, h], x[c, h], NN, P)
            if it < 4:
                pw[c, h] = _dotp(pw[c, h], pw[c, h], NN, P)
            x[c, h] = x_next
    lv, gcol, bkg, xar = {}, {}, {}, {}
    for c, h in pieces:
        rows = slice(c * T, (c + 1) * T)
        sl = slice(h * HEAD_A, (h + 1) * HEAD_A)
        lv[c, h] = _dotp(pm[c, h][:, T:2 * T], vh[c, h], NN, P)
        g_last = jnp.exp(cum[(c + 1) * T - 1:(c + 1) * T, sl])
        gcol[c, h] = jnp.sum(eye_h * g_last, axis=1, keepdims=True)
        bkg[c, h] = jnp.concatenate([btg[rows, sl], ktg[rows, sl]], axis=0)
    for c, h in pieces:
        xa = _dotp(x[c, h], ar[c, h][0:T], NN, P)
        xl = _dotp(x[c, h], lv[c, h][0:T], NN, P)
        xar[c, h] = jnp.concatenate([xa, ar[c, h][T:2 * T]], axis=0)
        lv[c, h] = jnp.concatenate([xl, lv[c, h][T:2 * T]], axis=0)

    st = [st_scr[h] for h in range(H_A)]
    heads = range(H_A)
    for c in range(NC):
        rows = slice(c * T, (c + 1) * T)
        base = [_dotp(xar[c, h], st[h], NN, P) + lv[c, h] for h in heads]
        st = [gcol[c, h] * st[h] + _dotp(bkg[c, h], jnp.concatenate([base[h][0:T], vh[c, h]], axis=0), TN, P)
              for h in heads]
        for h in heads:
            sl = slice(h * HEAD_A, (h + 1) * HEAD_A)
            o_scr[rows, sl] = base[h][T:2 * T] + _dotp(pm[c, h][T:2 * T, 0:T], base[h][0:T], NN, P)
    for h in heads:
        st_scr[h] = st[h]

    y_ref[0] = _rwkv_finish(o_scr[...], r, kmod, v, rk_ref[...], lng_ref[...], lnb_ref[...])

    @pl.when(step == pl.num_programs(1) - 1)
    def _():
        s_ref[0] = st_scr[...]


def _rwkv_prompt(pa, prm):
    b, l, _ = pa.shape
    T = RWKV_STEP_ROWS
    row = lambda w: _const_spec((1, w))
    return pl.pallas_call(
        _rwkv_prompt_kernel,
        grid=(b, l // T),
        in_specs=[pl.BlockSpec((1, T, A_PAD), lambda i, c: (i, c, 0)), row(A_PAD), row(W_BR),
                  _const_spec((LANE, W_BR)), row(W_BR), _const_spec((LANE, W_BR)),
                  row(W_BR), row(W_BR), row(W_BR), row(W_BR), row(W_BR)],
        out_specs=[pl.BlockSpec((1, T, W_BR), lambda i, c: (i, c, 0)),
                   pl.BlockSpec((1, H_A, HEAD_A, HEAD_A), lambda i, c: (i, 0, 0, 0))],
        out_shape=[jax.ShapeDtypeStruct((b, l, W_BR), f32),
                   jax.ShapeDtypeStruct((b, H_A, HEAD_A, HEAD_A), f32)],
        scratch_shapes=[pltpu.VMEM((H_A, HEAD_A, HEAD_A), f32), pltpu.VMEM((1, A_PAD), f32),
                        pltpu.VMEM((T, W_BR), f32)],
        compiler_params=pltpu.CompilerParams(dimension_semantics=("arbitrary", "arbitrary"),
                                             vmem_limit_bytes=VMEM_LIMIT),
        name="rwkv_prompt",
    )(pa, prm["mu"], prm["w0"], prm["wup"], prm["a0"], prm["aup"], prm["kkw"], prm["kaw"], prm["rk"],
      prm["lng"], prm["lnb"])


def _gla_prompt_kernel(f_ref, up_ref, bias_ref, ng_ref, y_ref, s_ref, s_scr, o_scr):
    step = pl.program_id(1)
    T = GLA_CHUNK
    TT = GLA_STEP_ROWS
    NC = TT // T

    @pl.when(step == 0)
    def _():
        s_scr[...] = jnp.zeros_like(s_scr)

    f = f_ref[0]
    hk = H_B * DK_B
    q = f[:, 0:hk] * (DK_B ** -0.5)
    k = f[:, hk:2 * hk]
    v = f[:, 2 * hk:2 * hk + W_BR]
    gl = f[:, 2 * hk + W_BR:B_PAD]
    P = GLA_PASSES
    g = _log_sigmoid(_bdot(gl, up_ref[...]) + bias_ref[...]) * (1.0 / GLA_TAU)
    bcum = _cumsum_rows(g, T)
    b_last = jnp.concatenate(
        [jnp.broadcast_to(bcum[(c + 1) * T - 1:(c + 1) * T, :], (T, hk)) for c in range(NC)], axis=0)
    qe = q * jnp.exp(bcum)
    ke = k * jnp.exp(-bcum)
    kl = k * jnp.exp(b_last - bcum)
    causal = _iota((T, T), 0) >= _iota((T, T), 1)
    eye_k = _eye(DK_B)
    pieces = [(c, h) for c in range(NC) for h in range(H_B)]
    av, kv, ecol = {}, {}, {}
    for c, h in pieces:
        rows = slice(c * T, (c + 1) * T)
        ks = slice(h * DK_B, (h + 1) * DK_B)
        vs = slice(h * DV_B, (h + 1) * DV_B)
        att = jnp.where(causal, _dotp(qe[rows, ks], ke[rows, ks], NT, P), 0.0)
        av[c, h] = _dotp(att, v[rows, vs], NN, P)
        kv[c, h] = _dotp(kl[rows, ks], v[rows, vs], TN, P)
        e_last = jnp.exp(bcum[(c + 1) * T - 1:(c + 1) * T, ks])
        ecol[c, h] = jnp.sum(eye_k * e_last, axis=1, keepdims=True)
    s = [s_scr[h] for h in range(H_B)]
    for c in range(NC):
        rows = slice(c * T, (c + 1) * T)
        for h in range(H_B):
            ks = slice(h * DK_B, (h + 1) * DK_B)
            o_scr[rows, h * DV_B:(h + 1) * DV_B] = av[c, h] + _dotp(qe[rows, ks], s[h], NN, P)
            s[h] = ecol[c, h] * s[h] + kv[c, h]
    for h in range(H_B):
        s_scr[h] = s[h]
    o = o_scr[...]
    ms = _ones_dot(o * o, _block_ones(W_BR, DV_B).astype(bf16)) * (1.0 / DV_B)
    y_ref[0] = o * lax.rsqrt(ms + NORM_EPS) * ng_ref[...]

    @pl.when(step == pl.num_programs(1) - 1)
    def _():
        s_ref[0] = s_scr[...]


def _gla_prompt(pb, prm):
    b, l, _ = pb.shape
    T = GLA_STEP_ROWS
    return pl.pallas_call(
        _gla_prompt_kernel,
        grid=(b, l // T),
        in_specs=[pl.BlockSpec((1, T, B_PAD), lambda i, c: (i, c, 0)),
                  _const_spec((LANE, H_B * DK_B)), _const_spec((1, H_B * DK_B)), _const_spec((1, W_BR))],
        out_specs=[pl.BlockSpec((1, T, W_BR), lambda i, c: (i, c, 0)),
                   pl.BlockSpec((1, H_B, DK_B, DV_B), lambda i, c: (i, 0, 0, 0))],
        out_shape=[jax.ShapeDtypeStruct((b, l, W_BR), f32),
                   jax.ShapeDtypeStruct((b, H_B, DK_B, DV_B), f32)],
        scratch_shapes=[pltpu.VMEM((H_B, DK_B, DV_B), f32), pltpu.VMEM((T, W_BR), f32)],
        compiler_params=pltpu.CompilerParams(dimension_semantics=("arbitrary", "arbitrary"),
                                             vmem_limit_bytes=VMEM_LIMIT),
        name="gla_prompt",
    )(pb, prm["gup"], prm["gbias"], prm["gng"])


def _swa_prompt_kernel(cur_ref, prev_ref, sink_ref, y_ref):
    i = pl.program_id(1)
    W = WINDOW
    cur = cur_ref[0]
    prev = prev_ref[0]
    qo, ko, vo = 0, H_C * HD_C, H_C * HD_C + KV_C * HD_C
    t = _iota((W, W), 0)
    s = _iota((W, W), 1)
    dist_prev = (W + t - s).astype(f32)
    dist_cur = (t - s).astype(f32)
    ok_prev = (s >= t) & (i > 0)
    ok_cur = s <= t
    scale = HD_C ** -0.5
    for g in range(KV_C):
        k_cur = cur[:, ko + g * HD_C:ko + (g + 1) * HD_C]
        k_prev = prev[:, ko + g * HD_C:ko + (g + 1) * HD_C]
        v_cur = cur[:, vo + g * HD_C:vo + (g + 1) * HD_C]
        v_prev = prev[:, vo + g * HD_C:vo + (g + 1) * HD_C]
        for j in range(G_C):
            h = g * G_C + j
            qh = cur[:, qo + h * HD_C:qo + (h + 1) * HD_C]
            sp = jnp.where(ok_prev, _bdot_nt(qh, k_prev) * scale - ALIBI_SLOPES[h] * dist_prev, NEG_BIG)
            sc = jnp.where(ok_cur, _bdot_nt(qh, k_cur) * scale - ALIBI_SLOPES[h] * dist_cur, NEG_BIG)
            sink = sink_ref[:, h:h + 1]
            m = jnp.maximum(jnp.maximum(jnp.max(sp, -1, keepdims=True), jnp.max(sc, -1, keepdims=True)), sink)
            pp = jnp.exp(sp - m)
            pc = jnp.exp(sc - m)
            den = jnp.sum(pp, -1, keepdims=True) + jnp.sum(pc, -1, keepdims=True) + jnp.exp(sink - m)
            out = _bdot(pp, v_prev) + _bdot(pc, v_cur)
            y_ref[0, :, h * HD_C:(h + 1) * HD_C] = out / den


def _swa_prompt(pc, sinks):
    b, l, _ = pc.shape
    W = WINDOW
    return pl.pallas_call(
        _swa_prompt_kernel,
        grid=(b, l // W),
        in_specs=[pl.BlockSpec((1, W, C_COLS), lambda i, c: (i, c, 0)),
                  pl.BlockSpec((1, W, C_COLS), lambda i, c: (i, jnp.maximum(c - 1, 0), 0)),
                  _const_spec((1, H_C))],
        out_specs=pl.BlockSpec((1, W, W_BR), lambda i, c: (i, c, 0)),
        out_shape=jax.ShapeDtypeStruct((b, l, W_BR), f32),
        compiler_params=pltpu.CompilerParams(dimension_semantics=("arbitrary", "arbitrary"),
                                             vmem_limit_bytes=VMEM_LIMIT),
        name="swa_prompt",
    )(pc, pc, sinks)


def _lru_gates(xc, wa, ba, wx, bx, lam):
    r = _sigmoid(_bdot(xc, wa) + ba)
    i = _sigmoid(_bdot(xc, wx) + bx)
    log_a = C_RG * r * _log_sigmoid(lam)
    a = jnp.exp(log_a)
    bterm = jnp.sqrt(1.0 - jnp.exp(2.0 * log_a)) * (i * xc)
    return a, bterm


def _lru_prompt_kernel(x_ref, cw_ref, cb_ref, wa_ref, ba_ref, wx_ref, bx_ref, lam_ref, y_ref,
                       xbuf_scr, h_scr):
    c = pl.program_id(1)
    T = LRU_CHUNK
    PADR = 8

    @pl.when(c == 0)
    def _():
        xbuf_scr[0:PADR, :] = jnp.zeros((PADR, W_BR), f32)
        h_scr[...] = jnp.zeros_like(h_scr)

    x = x_ref[0]
    xbuf_scr[PADR:PADR + T, :] = x
    xc = cb_ref[...] + x * cw_ref[CONV_W - 1:CONV_W, :]
    for j in range(1, CONV_W):
        xc = xc + xbuf_scr[PADR - j:PADR - j + T, :] * cw_ref[CONV_W - 1 - j:CONV_W - j, :]
    xbuf_scr[0:PADR, :] = x[T - PADR:T, :]
    a, bv = _lru_gates(xc, wa_ref[...], ba_ref[...], wx_ref[...], bx_ref[...], lam_ref[...])
    row = _iota((T, W_BR), 0)
    d = 1
    while d < T:
        keep = row >= d
        a_sh = jnp.where(keep, pltpu.roll(a, d, 0), 1.0)
        b_sh = jnp.where(keep, pltpu.roll(bv, d, 0), 0.0)
        bv = a * b_sh + bv
        a = a * a_sh
        d *= 2
    h = a * h_scr[...] + bv
    y_ref[0] = h
    h_scr[...] = h[T - 1:T, :]


def _lru_prompt(pd, prm):
    b, l, _ = pd.shape
    T = LRU_CHUNK
    row = lambda: _const_spec((1, W_BR))
    return pl.pallas_call(
        _lru_prompt_kernel,
        grid=(b, l // T),
        in_specs=[pl.BlockSpec((1, T, W_BR), lambda i, c: (i, c, 0)), _const_spec((CONV_W, W_BR)), row(),
                  _const_spec((W_BR, W_BR)), row(), _const_spec((W_BR, W_BR)), row(), row()],
        out_specs=pl.BlockSpec((1, T, W_BR), lambda i, c: (i, c, 0)),
        out_shape=jax.ShapeDtypeStruct((b, l, W_BR), f32),
        scratch_shapes=[pltpu.VMEM((T + 8, W_BR), f32), pltpu.VMEM((1, W_BR), f32)],
        compiler_params=pltpu.CompilerParams(dimension_semantics=("arbitrary", "arbitrary"),
                                             vmem_limit_bytes=VMEM_LIMIT),
        name="lru_prompt",
    )(pd, prm["cw"], prm["cb"], prm["lwa"], prm["lba"], prm["lwx"], prm["lbx"], prm["lam"])


def _decode_kernel(pa_ref, pb_ref, pc_ref, pd_ref, shift_ref, swkv_ref, sgla_ref, kbuf_ref, vbuf_ref,
                   c0_ref, c1_ref, c2_ref, h0_ref,
                   mu_ref, w0_ref, wup_ref, a0_ref, aup_ref, kkw_ref, kaw_ref, rk_ref, lng_ref, lnb_ref,
                   gup_ref, gbias_ref, gng_ref, sink_ref,
                   cw_ref, cb_ref, wa_ref, ba_ref, wx_ref, bx_ref, lam_ref,
                   ya_ref, yb_ref, yc_ref, yd_ref, swkv_out, sgla_out,
                   oa_scr, ob_scr):
    bt = pa_ref.shape[0]
    u = pa_ref[...]
    us = u + (shift_ref[...] - u) * mu_ref[...]
    r, kmod, v, ld, kk, a = _rwkv_features(us, w0_ref[...], wup_ref[...], a0_ref[...], aup_ref[...],
                                           kkw_ref[...], kaw_ref[...])
    decay = jnp.exp(ld)
    kka = kk * a
    fb = pb_ref[...]
    hk = H_B * DK_B
    qb = fb[:, 0:hk] * (DK_B ** -0.5)
    kb = fb[:, hk:2 * hk]
    vb = fb[:, 2 * hk:2 * hk + W_BR]
    gb = _log_sigmoid(_mm(fb[:, 2 * hk + W_BR:B_PAD], gup_ref[...]) + gbias_ref[...]) * (1.0 / GLA_TAU)
    eg = jnp.exp(gb)
    qeg = qb * eg
    fc = pc_ref[...]

    eye_a = _eye(HEAD_A)
    eye_b = _eye(DK_B)
    wdist = (WINDOW - _iota((WINDOW, 1), 0)).astype(f32)
    scale = HD_C ** -0.5

    for b in range(bt):
        rw = lambda t: t[b:b + 1, :]
        r_r, km_r, v_r, w_r, kk_r, kka_r = rw(r), rw(kmod), rw(v), rw(decay), rw(kk), rw(kka)
        for h in range(H_A):
            sl = slice(h * HEAD_A, (h + 1) * HEAD_A)
            s = swkv_ref[b, h]
            v_c = jnp.sum(eye_a * v_r[:, sl], axis=1, keepdims=True)
            sa_c = -jnp.sum(s * kk_r[:, sl], axis=1, keepdims=True)
            s_new = s * w_r[:, sl] + sa_c * kka_r[:, sl] + v_c * km_r[:, sl]
            swkv_out[b, h] = s_new
            o_c = jnp.sum(s_new * r_r[:, sl], axis=1, keepdims=True)
            oa_scr[b:b + 1, sl] = jnp.sum(eye_a * o_c, axis=0, keepdims=True)
        q_all, k_all, eg_all, qe_all, vb_r = rw(qb), rw(kb), rw(eg), rw(qeg), rw(vb)
        for h in range(H_B):
            ks = slice(h * DK_B, (h + 1) * DK_B)
            vs = slice(h * DV_B, (h + 1) * DV_B)
            s = sgla_ref[b, h]
            q_r, k_r, eg_r, qe_r = q_all[:, ks], k_all[:, ks], eg_all[:, ks], qe_all[:, ks]
            k_c = jnp.sum(eye_b * k_r, axis=1, keepdims=True)
            eg_c = jnp.sum(eye_b * eg_r, axis=1, keepdims=True)
            qe_c = jnp.sum(eye_b * qe_r, axis=1, keepdims=True)
            qk = jnp.sum(q_r * k_r, axis=1, keepdims=True)
            ob_scr[b:b + 1, vs] = qk * vb_r[:, vs] + jnp.sum(qe_c * s, axis=0, keepdims=True)
            sgla_out[b, h] = eg_c * s + k_c * vb_r[:, vs]
        q_row, kv_new = rw(fc[:, 0:W_BR]), rw(fc[:, W_BR:2 * W_BR])
        kc = kbuf_ref[b]
        vc = vbuf_ref[b]
        for g in range(KV_C):
            gs = slice(g * HD_C, (g + 1) * HD_C)
            kg, vg = kc[:, gs], vc[:, gs]
            kn = kv_new[:, gs]
            vn = kv_new[:, KV_C * HD_C + g * HD_C:KV_C * HD_C + (g + 1) * HD_C]
            for j in range(G_C):
                h = g * G_C + j
                qh = q_row[:, h * HD_C:(h + 1) * HD_C]
                sc = jnp.sum(kg * qh, axis=1, keepdims=True) * scale - ALIBI_SLOPES[h] * wdist
                sn = jnp.sum(kn * qh, axis=1, keepdims=True) * scale
                sink = sink_ref[:, h:h + 1]
                m = jnp.maximum(jnp.maximum(jnp.max(sc, axis=0, keepdims=True), sn), sink)
                p = jnp.exp(sc - m)
                pn = jnp.exp(sn - m)
                den = jnp.sum(p, axis=0, keepdims=True) + pn + jnp.exp(sink - m)
                out = (jnp.sum(p * vg, axis=0, keepdims=True) + pn * vn) / den
                yc_ref[b:b + 1, h * HD_C:(h + 1) * HD_C] = out

    ya_ref[...] = _rwkv_finish(oa_scr[...], r, kmod, v, rk_ref[...], lng_ref[...], lnb_ref[...])
    ob = ob_scr[...]
    ms = _mm(ob * ob, _block_ones(W_BR, DV_B)) * (1.0 / DV_B)
    yb_ref[...] = ob * lax.rsqrt(ms + NORM_EPS) * gng_ref[...]
    xd = pd_ref[...]
    xc = (cb_ref[...] + c0_ref[...] * cw_ref[0:1, :] + c1_ref[...] * cw_ref[1:2, :]
          + c2_ref[...] * cw_ref[2:3, :] + xd * cw_ref[3:4, :])
    al, bterm = _lru_gates(xc, wa_ref[...], ba_ref[...], wx_ref[...], bx_ref[...], lam_ref[...])
    yd_ref[...] = al * h0_ref[...] + bterm


def _decode(pa, pb, pc, pd, shift0, swkv, sgla, kbuf, vbuf, c0, c1, c2, h0, prm, bt):
    n = pa.shape[0]
    rows = lambda w: pl.BlockSpec((bt, w), lambda i: (i, 0))
    crow = lambda w: _const_spec((1, w))
    in_specs = [rows(A_PAD), rows(B_PAD), rows(C_COLS), rows(D_COLS), rows(A_PAD),
                pl.BlockSpec((bt, H_A, HEAD_A, HEAD_A), lambda i: (i, 0, 0, 0)),
                pl.BlockSpec((bt, H_B, DK_B, DV_B), lambda i: (i, 0, 0, 0)),
                pl.BlockSpec((bt, WINDOW, KV_C * HD_C), lambda i: (i, 0, 0)),
                pl.BlockSpec((bt, WINDOW, KV_C * HD_C), lambda i: (i, 0, 0)),
                rows(W_BR), rows(W_BR), rows(W_BR), rows(W_BR),
                crow(A_PAD), crow(W_BR), _const_spec((LANE, W_BR)), crow(W_BR), _const_spec((LANE, W_BR)),
                crow(W_BR), crow(W_BR), crow(W_BR), crow(W_BR), crow(W_BR),
                _const_spec((LANE, H_B * DK_B)), crow(H_B * DK_B), crow(W_BR), crow(H_C),
                _const_spec((CONV_W, W_BR)), crow(W_BR), _const_spec((W_BR, W_BR)), crow(W_BR),
                _const_spec((W_BR, W_BR)), crow(W_BR), crow(W_BR)]
    out_specs = [rows(W_BR), rows(W_BR), rows(W_BR), rows(W_BR),
                 pl.BlockSpec((bt, H_A, HEAD_A, HEAD_A), lambda i: (i, 0, 0, 0)),
                 pl.BlockSpec((bt, H_B, DK_B, DV_B), lambda i: (i, 0, 0, 0))]
    out_shape = [jax.ShapeDtypeStruct((n, W_BR), f32)] * 4 + [
        jax.ShapeDtypeStruct((n, H_A, HEAD_A, HEAD_A), f32),
        jax.ShapeDtypeStruct((n, H_B, DK_B, DV_B), f32)]
    return pl.pallas_call(
        _decode_kernel,
        grid=(n // bt,),
        in_specs=in_specs,
        out_specs=out_specs,
        out_shape=out_shape,
        scratch_shapes=[pltpu.VMEM((bt, W_BR), f32), pltpu.VMEM((bt, W_BR), f32)],
        compiler_params=pltpu.CompilerParams(dimension_semantics=("arbitrary",),
                                             vmem_limit_bytes=VMEM_LIMIT),
        name="decode_mixers",
    )(pa, pb, pc, pd, shift0, swkv, sgla, kbuf, vbuf, c0, c1, c2, h0,
      prm["mu"], prm["w0"], prm["wup"], prm["a0"], prm["aup"], prm["kkw"], prm["kaw"], prm["rk"],
      prm["lng"], prm["lnb"], prm["gup"], prm["gbias"], prm["gng"], prm["sinks"],
      prm["cw"], prm["cb"], prm["lwa"], prm["lba"], prm["lwx"], prm["lbx"], prm["lam"])


def _merge_kernel(x_ref, ya_ref, yb_ref, yc_ref, yd_ref, g_ref, wz_ref, wg_ref, wbr_ref, wout_ref,
                  fg_ref, o_ref, *, final):
    x = x_ref[...]
    hn = _rms(x, g_ref[...]).astype(bf16)
    ys = (ya_ref, yb_ref, yc_ref, yd_ref)
    merged = None
    for n in range(N_BRANCH):
        z = jnp.dot(hn, wz_ref[:, n * W_BR:(n + 1) * W_BR], preferred_element_type=f32)
        yz = ys[n][...] * (z * _sigmoid(z))
        br = jnp.dot(yz.astype(bf16), wbr_ref[n], preferred_element_type=f32)
        gate = _sigmoid(jnp.dot(hn, wg_ref[:, n * D_MODEL:(n + 1) * D_MODEL], preferred_element_type=f32))
        merged = gate * br if merged is None else merged + gate * br
    out = x + jnp.dot(merged.astype(bf16), wout_ref[...], preferred_element_type=f32)
    if final:
        out = _rms(out, fg_ref[...])
    o_ref[...] = out


def _merge(x, ya, yb, yc, yd, g, wz, wg, wbr, wout, fg, tm, final):
    m = x.shape[0]
    tile = lambda w: pl.BlockSpec((tm, w), lambda i: (i, 0))
    single = lambda shape: pl.BlockSpec(shape, lambda *_: (0,) * len(shape), pipeline_mode=pl.Buffered(1))
    return pl.pallas_call(
        functools.partial(_merge_kernel, final=final),
        grid=(m // tm,),
        in_specs=[tile(D_MODEL), tile(W_BR), tile(W_BR), tile(W_BR), tile(W_BR), _const_spec((1, D_MODEL)),
                  single((D_MODEL, Z_COLS)), single((D_MODEL, G_COLS)),
                  single((N_BRANCH, W_BR, D_MODEL)), single((D_MODEL, D_MODEL)), _const_spec((1, D_MODEL))],
        out_specs=tile(D_MODEL),
        out_shape=jax.ShapeDtypeStruct((m, D_MODEL), f32),
        compiler_params=pltpu.CompilerParams(dimension_semantics=("arbitrary",),
                                             vmem_limit_bytes=VMEM_LIMIT),
        name="merge_final" if final else "merge",
    )(x, ya, yb, yc, yd, g, wz, wg, wbr, wout, fg)


def _pad_cols(w, width):
    return jnp.pad(w, ((0, 0), (0, width - w.shape[1])))


def _pad_rows_at(w, start, total):
    return jnp.pad(w, ((start, total - start - w.shape[0]), (0, 0)))


def _block_diag(w):
    nb, bs, _ = w.shape
    out = jnp.zeros((nb * bs, nb * bs), w.dtype)
    for n in range(nb):
        out = out.at[n * bs:(n + 1) * bs, n * bs:(n + 1) * bs].set(w[n])
    return out


def _layer_params(l, norm_g, w_in, mu_shift, w0, w_decay_up, a0, a_icl_up, k_k, k_a, r_k, ln_x_g, ln_x_b,
                  gla_gate_up, gla_gate_b, gla_norm_g, swa_sinks, lru_conv_w, lru_conv_b, lru_wa, lru_ba,
                  lru_wx, lru_bx, lru_lambda, w_branch, w_out):
    w = w_in[l]
    o_b = A_COLS
    o_c = o_b + B_COLS
    o_d = o_c + C_COLS
    o_z = o_d + D_COLS
    o_g = o_z + Z_COLS
    row = lambda t: t.reshape(1, -1)
    return dict(
        g=row(norm_g[l]),
        wa=_pad_cols(w[:, 0:o_b], A_PAD).astype(bf16),
        wb=_pad_cols(w[:, o_b:o_c], B_PAD).astype(bf16),
        wc=w[:, o_c:o_d].astype(bf16),
        wd=w[:, o_d:o_z].astype(bf16),
        wz=w[:, o_z:o_g].astype(bf16),
        wg=w[:, o_g:].astype(bf16),
        wbr=w_branch[l].astype(bf16),
        wout=w_out[l].astype(bf16),
        mu=_pad_cols(row(mu_shift[l]), A_PAD),
        w0=row(w0[l]),
        wup=_pad_rows_at(w_decay_up[l], 0, LANE),
        a0=row(a0[l]),
        aup=_pad_rows_at(a_icl_up[l], R_DECAY, LANE),
        kkw=row(k_k[l]), kaw=row(k_a[l]), rk=row(r_k[l]), lng=row(ln_x_g[l]), lnb=row(ln_x_b[l]),
        gup=_pad_rows_at(gla_gate_up[l], 0, LANE),
        gbias=row(gla_gate_b[l]),
        gng=row(jnp.tile(gla_norm_g[l], H_B)),
        sinks=row(swa_sinks[l]),
        cw=lru_conv_w[l], cb=row(lru_conv_b[l]),
        lwa=_block_diag(lru_wa[l]).astype(bf16), lba=row(lru_ba[l]),
        lwx=_block_diag(lru_wx[l]).astype(bf16), lbx=row(lru_bx[l]),
        lam=row(lru_lambda[l]),
    )


def kernel(x_prompt, x_sample, state_wkv, state_shift, state_gla, cache_swa_k, cache_swa_v, state_lru_conv, state_lru_h, norm_g, w_in, mu_shift, w0, w_decay_up, a0, a_icl_up, k_k, k_a, r_k, ln_x_g, ln_x_b, gla_gate_up, gla_gate_b, gla_norm_g, swa_sinks, lru_conv_w, lru_conv_b, lru_wa, lru_ba, lru_wx, lru_bx, lru_lambda, w_branch, w_out, final_norm_g):
    bp, lp, _ = x_prompt.shape
    bs = x_sample.shape[0]
    depth = w_in.shape[0]
    fg = final_norm_g.reshape(1, -1)
    xp = x_prompt.reshape(bp * lp, D_MODEL)
    xs = x_sample.reshape(bs, D_MODEL)
    outs_p = [[] for _ in range(7)]
    outs_s = [[] for _ in range(7)]
    for l in range(depth):
        prm = _layer_params(l, norm_g, w_in, mu_shift, w0, w_decay_up, a0, a_icl_up, k_k, k_a, r_k, ln_x_g,
                            ln_x_b, gla_gate_up, gla_gate_b, gla_norm_g, swa_sinks, lru_conv_w, lru_conv_b,
                            lru_wa, lru_ba, lru_wx, lru_bx, lru_lambda, w_branch, w_out)
        final = l == depth - 1
        pa, pb, pc, pd = _inproj(xp, prm["g"], prm["wa"], prm["wb"], prm["wc"], prm["wd"], tm=512)
        pa3, pb3 = pa.reshape(bp, lp, A_PAD), pb.reshape(bp, lp, B_PAD)
        pc3, pd3 = pc.reshape(bp, lp, C_COLS), pd.reshape(bp, lp, D_COLS)
        ya, st_t = _rwkv_prompt(pa3, prm)
        yb, sgla = _gla_prompt(pb3, prm)
        yc = _swa_prompt(pc3, prm["sinks"])
        yd = _lru_prompt(pd3, prm)
        flat = lambda t: t.reshape(bp * lp, W_BR)
        xp = _merge(xp, flat(ya), flat(yb), flat(yc), flat(yd), prm["g"], prm["wz"], prm["wg"], prm["wbr"],
                    prm["wout"], fg, tm=256, final=final)
        kv = pc3[:, lp - WINDOW:, H_C * HD_C:]
        outs_p[0].append(jnp.swapaxes(st_t, -1, -2))
        outs_p[1].append(pa3[:, lp - 1, :A_COLS])
        outs_p[2].append(sgla)
        outs_p[3].append(kv[:, :, :KV_C * HD_C].reshape(bp, WINDOW, KV_C, HD_C))
        outs_p[4].append(kv[:, :, KV_C * HD_C:].reshape(bp, WINDOW, KV_C, HD_C))
        outs_p[5].append(pd3[:, lp - (CONV_W - 1):, :])
        outs_p[6].append(yd[:, lp - 1, :])
        sa, sb, sc, sd = _inproj(xs, prm["g"], prm["wa"], prm["wb"], prm["wc"], prm["wd"], tm=bs)
        kbuf = cache_swa_k[l].reshape(bs, WINDOW, KV_C * HD_C)
        vbuf = cache_swa_v[l].reshape(bs, WINDOW, KV_C * HD_C)
        conv0 = state_lru_conv[l]
        ya_s, yb_s, yc_s, yd_s, wkv1, gla1 = _decode(
            sa, sb, sc, sd, _pad_cols(state_shift[l], A_PAD), state_wkv[l], state_gla[l], kbuf, vbuf,
            conv0[:, 0], conv0[:, 1], conv0[:, 2], state_lru_h[l], prm, bt=8)
        xs = _merge(xs, ya_s, yb_s, yc_s, yd_s, prm["g"], prm["wz"], prm["wg"], prm["wbr"], prm["wout"], fg,
                    tm=bs, final=final)
        k_new = sc[:, W_BR:W_BR + KV_C * HD_C]
        v_new = sc[:, W_BR + KV_C * HD_C:]
        outs_s[0].append(wkv1)
        outs_s[1].append(sa[:, :A_COLS])
        outs_s[2].append(gla1)
        outs_s[3].append(jnp.concatenate([kbuf[:, 1:], k_new[:, None]], axis=1).reshape(bs, WINDOW, KV_C, HD_C))
        outs_s[4].append(jnp.concatenate([vbuf[:, 1:], v_new[:, None]], axis=1).reshape(bs, WINDOW, KV_C, HD_C))
        outs_s[5].append(jnp.stack([conv0[:, 1], conv0[:, 2], sd], axis=1))
        outs_s[6].append(yd_s)
    y_prompt = xp.reshape(bp, lp, D_MODEL)
    y_sample = xs.reshape(bs, 1, D_MODEL)
    sp = [jnp.stack(t) for t in outs_p]
    ss = [jnp.stack(t) for t in outs_s]
    return (y_prompt, y_sample, sp[0], ss[0], sp[1], ss[1], sp[2], ss[2], sp[3], ss[3], sp[4], ss[4],
            sp[5], ss[5], sp[6], ss[6])
```

```python
import functools

import jax
import jax.numpy as jnp
from jax import lax
from jax.experimental import pallas as pl
from jax.experimental.pallas import tpu as pltpu

f32 = jnp.float32
bf16 = jnp.bfloat16
HI = lax.Precision.HIGHEST

D_MODEL = 1024
N_BRANCH = 4
W_BR = 256
HEAD_A = 64
H_A = 4
R_DECAY = 32
R_ICL = 32
GN_EPS_A = 64e-5
H_B = 4
DK_B = 32
DV_B = 64
R_GATE_B = 16
GLA_TAU = 16.0
GLA_CHUNK = 64
GLA_STEP_ROWS = 256
H_C = 4
KV_C = 2
HD_C = 64
G_C = 2
WINDOW = 128
SWA_STEP_BLOCKS = 4
CONV_W = 4
C_RG = 8.0
NORM_EPS = 1e-6

A_COLS = 3 * W_BR + R_DECAY + R_ICL
B_COLS = 2 * H_B * DK_B + W_BR + R_GATE_B
C_COLS = H_C * HD_C + 2 * KV_C * HD_C
D_COLS = W_BR
Z_COLS = N_BRANCH * W_BR
G_COLS = N_BRANCH * D_MODEL

LANE = 128
A_PAD = 7 * LANE
B_PAD = 5 * LANE
RWKV_CHUNK = 64
RWKV_STEP_ROWS = 256
RWKV_PASSES = 1
GLA_PASSES = 1
DECODE_PASSES = 1
LRU_CHUNK = 256
VMEM_LIMIT = 56 * 1024 * 1024
NEG_BIG = -1e30

ALIBI_SLOPES = tuple(2.0 ** (-8.0 * (h + 1) / H_C) for h in range(H_C))


def _mm(a, b):
    return jnp.dot(a, b, precision=HI, preferred_element_type=f32)


def _mm_nt(a, b):
    return lax.dot_general(a, b, (((1,), (1,)), ((), ())), precision=HI, preferred_element_type=f32)


def _mm_tn(a, b):
    return lax.dot_general(a, b, (((0,), (0,)), ((), ())), precision=HI, preferred_element_type=f32)


def _bdot(a, b):
    return jnp.dot(a.astype(bf16), b.astype(bf16), preferred_element_type=f32)


def _bdot_nt(a, b):
    return lax.dot_general(a.astype(bf16), b.astype(bf16), (((1,), (1,)), ((), ())),
                           preferred_element_type=f32)


NN = ((1,), (0,))
NT = ((1,), (1,))
TN = ((0,), (0,))


def _dg(a, b, dims):
    return lax.dot_general(a, b, (dims, ((), ())), preferred_element_type=f32)


def _split_bf16(a):
    hi = a.astype(bf16)
    return hi, (a - hi.astype(f32)).astype(bf16)


def _dotp(a, b, dims, passes):
    if passes == 1:
        return _dg(a.astype(bf16), b.astype(bf16), dims)
    ah, al = _split_bf16(a)
    bh, bl = _split_bf16(b)
    return _dg(ah, bh, dims) + (_dg(ah, bl, dims) + _dg(al, bh, dims))


def _ones_dot(a, ones_bf16):
    ah, al = _split_bf16(a)
    return _dg(ah, ones_bf16, NN) + _dg(al, ones_bf16, NN)


def _cumsum_rows(x, seg):
    pos = _iota(x.shape, 0) % seg
    d = 1
    while d < seg:
        x = x + jnp.where(pos >= d, pltpu.roll(x, d, 0), 0.0)
        d *= 2
    return x


def _iota(shape, dim):
    return lax.broadcasted_iota(jnp.int32, shape, dim)


def _eye(n):
    return (_iota((n, n), 0) == _iota((n, n), 1)).astype(f32)


def _block_ones(n, blk):
    return ((_iota((n, n), 0) // blk) == (_iota((n, n), 1) // blk)).astype(f32)


def _tril_ones(n):
    return (_iota((n, n), 0) >= _iota((n, n), 1)).astype(f32)


def _softplus(x):
    return jnp.maximum(x, 0.0) + jnp.log(1.0 + jnp.exp(-jnp.abs(x)))


def _log_sigmoid(x):
    return -_softplus(-x)


def _sigmoid(x):
    return 1.0 / (1.0 + jnp.exp(-x))


def _rms(x, g):
    return x * lax.rsqrt(jnp.mean(x * x, -1, keepdims=True) + NORM_EPS) * g


def _inproj_kernel(x_ref, g_ref, wa_ref, wb_ref, wc_ref, wd_ref, oa_ref, ob_ref, oc_ref, od_ref):
    hn = _rms(x_ref[...], g_ref[...]).astype(bf16)
    oa_ref[...] = jnp.dot(hn, wa_ref[...], preferred_element_type=f32)
    ob_ref[...] = jnp.dot(hn, wb_ref[...], preferred_element_type=f32)
    oc_ref[...] = jnp.dot(hn, wc_ref[...], preferred_element_type=f32)
    od_ref[...] = jnp.dot(hn, wd_ref[...], preferred_element_type=f32)


def _const_spec(shape):
    return pl.BlockSpec(shape, lambda *_: (0,) * len(shape))


def _inproj(x, g, wa, wb, wc, wd, tm):
    m = x.shape[0]
    widths = (A_PAD, B_PAD, C_COLS, D_COLS)
    return pl.pallas_call(
        _inproj_kernel,
        grid=(m // tm,),
        in_specs=[pl.BlockSpec((tm, D_MODEL), lambda i: (i, 0)), _const_spec((1, D_MODEL))]
        + [_const_spec((D_MODEL, w)) for w in widths],
        out_specs=[pl.BlockSpec((tm, w), lambda i: (i, 0)) for w in widths],
        out_shape=[jax.ShapeDtypeStruct((m, w), f32) for w in widths],
        compiler_params=pltpu.CompilerParams(dimension_semantics=("arbitrary",),
                                             vmem_limit_bytes=VMEM_LIMIT),
        name="inproj",
    )(x, g, wa, wb, wc, wd)


def _rwkv_features(us, w0, wup, a0, aup, kk_w, ka_w):
    r = us[:, 0:W_BR]
    k = us[:, W_BR:2 * W_BR]
    v = us[:, 2 * W_BR:3 * W_BR]
    lora = us[:, 3 * W_BR:A_PAD]
    w = -_softplus(-(w0 + _bdot(jnp.tanh(lora), wup))) - 0.5
    logdecay = -jnp.exp(w)
    a = _sigmoid(a0 + _bdot(lora, aup))
    kk = k * kk_w
    ss = _ones_dot(kk * kk, _block_ones(W_BR, HEAD_A).astype(bf16))
    kk = kk / jnp.maximum(jnp.sqrt(ss), 1e-12)
    kmod = k * (1.0 + (a - 1.0) * ka_w)
    return r, kmod, v, logdecay, kk, a


def _rwkv_finish(o, r, kmod, v, rk, lng, lnb):
    ones = _block_ones(W_BR, HEAD_A).astype(bf16)
    mean = _ones_dot(o, ones) * (1.0 / HEAD_A)
    cen = o - mean
    var = _ones_dot(cen * cen, ones) * (1.0 / HEAD_A)
    o = cen * lax.rsqrt(var + GN_EPS_A) * lng + lnb
    bonus = _ones_dot(r * kmod * rk, ones) * v
    return o + bonus


def _rwkv_prompt_kernel(u_ref, mu_ref, w0_ref, wup_ref, a0_ref, aup_ref, kkw_ref, kaw_ref, rk_ref,
                        lng_ref, lnb_ref, y_ref, s_ref, st_scr, prev_scr, o_scr):
    step = pl.program_id(1)
    T = RWKV_CHUNK
    TT = RWKV_STEP_ROWS

    @pl.when(step == 0)
    def _():
        st_scr[...] = jnp.zeros_like(st_scr)
        prev_scr[...] = jnp.zeros_like(prev_scr)

    u = u_ref[0]
    row = _iota(u.shape, 0)
    u_prev = jnp.where(row == 0, prev_scr[...], pltpu.roll(u, 1, 0))
    prev_scr[...] = u[TT - 1:TT, :]
    us = u + (u_prev - u) * mu_ref[...]
    r, kmod, v, ld, kk, a = _rwkv_features(us, w0_ref[...], wup_ref[...], a0_ref[...], aup_ref[...],
                                           kkw_ref[...], kaw_ref[...])
    P = RWKV_PASSES
    NC = TT // T
    cum = _cumsum_rows(ld, T)
    cum_last = jnp.concatenate(
        [jnp.broadcast_to(cum[(c + 1) * T - 1:(c + 1) * T, :], (T, W_BR)) for c in range(NC)], axis=0)
    g_inv = jnp.exp(-cum)
    g_tail = jnp.exp(cum_last - cum)
    kka = kk * a
    at = -kk * jnp.exp(cum - ld)
    rt = r * jnp.exp(cum)
    bt = kka * g_inv
    kt = kmod * g_inv
    btg = kka * g_tail
    ktg = kmod * g_tail

    ri = _iota((2 * T, 2 * T), 0)
    ci = _iota((2 * T, 2 * T), 1)
    ti = jnp.where(ri >= T, ri - T, ri)
    si = jnp.where(ci >= T, ci - T, ci)
    keep = (ti > si) | ((ri >= T) & (ti == si))
    eye_t = _eye(T)
    eye_h = _eye(HEAD_A)

    pieces = [(c, h) for c in range(NC) for h in range(H_A)]
    ar, vh, pm, x, pw = {}, {}, {}, {}, {}
    for c, h in pieces:
        rows = slice(c * T, (c + 1) * T)
        sl = slice(h * HEAD_A, (h + 1) * HEAD_A)
        vh[c, h] = v[rows, sl]
        ar[c, h] = jnp.concatenate([at[rows, sl], rt[rows, sl]], axis=0)
        bk = jnp.concatenate([bt[rows, sl], kt[rows, sl]], axis=0)
        pm[c, h] = jnp.where(keep, _dotp(ar[c, h], bk, NT, P), 0.0)
    for c, h in pieces:
        lab = pm[c, h][0:T, 0:T]
        x[c, h] = eye_t + lab
        pw[c, h] = _dotp(lab, lab, NN, P)
    for it in range(5):
        for c, h in pieces:
            x_next = x[c, h] + _dotp(pw[c, h], x[c, h], NN, P)
            if it < 4:
                pw[c, h] = _dotp(pw[c, h], pw[c, h], NN, P)
            x[c, h] = x_next
    lv, gcol, bkg, xar = {}, {}, {}, {}
    for c, h in pieces:
        rows = slice(c * T, (c + 1) * T)
        sl = slice(h * HEAD_A, (h + 1) * HEAD_A)
        lv[c, h] = _dotp(pm[c, h][:, T:2 * T], vh[c, h], NN, P)
        g_last = jnp.exp(cum[(c + 1) * T - 1:(c + 1) * T, sl])
        gcol[c, h] = jnp.sum(eye_h * g_last, axis=1, keepdims=True)
        bkg[c, h] = jnp.concatenate([btg[rows, sl], ktg[rows, sl]], axis=0)
    for c, h in pieces:
        xa = _dotp(x[c, h], ar[c, h][0:T], NN, P)
        xl = _dotp(x[c, h], lv[c, h][0:T], NN, P)
        xar[c, h] = jnp.concatenate([xa, ar[c, h][T:2 * T]], axis=0)
        lv[c, h] = jnp.concatenate([xl, lv[c, h][T:2 * T]], axis=0)

    st = [st_scr[h] for h in range(H_A)]
    heads = range(H_A)
    for c in range(NC):
        rows = slice(c * T, (c + 1) * T)
        base = [_dotp(xar[c, h], st[h], NN, P) + lv[c, h] for h in heads]
        st = [gcol[c, h] * st[h] + _dotp(bkg[c, h], jnp.concatenate([base[h][0:T], vh[c, h]], axis=0), TN, P)
              for h in heads]
        for h in heads:
            sl = slice(h * HEAD_A, (h + 1) * HEAD_A)
            o_scr[rows, sl] = base[h][T:2 * T] + _dotp(pm[c, h][T:2 * T, 0:T], base[h][0:T], NN, P)
    for h in heads:
        st_scr[h] = st[h]

    y_ref[0] = _rwkv_finish(o_scr[...], r, kmod, v, rk_ref[...], lng_ref[...], lnb_ref[...])

    @pl.when(step == pl.num_programs(1) - 1)
    def _():
        s_ref[0] = st_scr[...]


def _rwkv_prompt(pa, prm):
    b, l, _ = pa.shape
    T = RWKV_STEP_ROWS
    row = lambda w: _const_spec((1, w))
    return pl.pallas_call(
        _rwkv_prompt_kernel,
        grid=(b, l // T),
        in_specs=[pl.BlockSpec((1, T, A_PAD), lambda i, c: (i, c, 0)), row(A_PAD), row(W_BR),
                  _const_spec((LANE, W_BR)), row(W_BR), _const_spec((LANE, W_BR)),
                  row(W_BR), row(W_BR), row(W_BR), row(W_BR), row(W_BR)],
        out_specs=[pl.BlockSpec((1, T, W_BR), lambda i, c: (i, c, 0)),
                   pl.BlockSpec((1, H_A, HEAD_A, HEAD_A), lambda i, c: (i, 0, 0, 0))],
        out_shape=[jax.ShapeDtypeStruct((b, l, W_BR), f32),
                   jax.ShapeDtypeStruct((b, H_A, HEAD_A, HEAD_A), f32)],
        scratch_shapes=[pltpu.VMEM((H_A, HEAD_A, HEAD_A), f32), pltpu.VMEM((1, A_PAD), f32),
                        pltpu.VMEM((T, W_BR), f32)],
        compiler_params=pltpu.CompilerParams(dimension_semantics=("arbitrary", "arbitrary"),
                                             vmem_limit_bytes=VMEM_LIMIT),
        name="rwkv_prompt",
    )(pa, prm["mu"], prm["w0"], prm["wup"], prm["a0"], prm["aup"], prm["kkw"], prm["kaw"], prm["rk"],
      prm["lng"], prm["lnb"])


def _gla_prompt_kernel(f_ref, up_ref, bias_ref, ng_ref, y_ref, s_ref, s_scr, o_scr):
    step = pl.program_id(1)
    T = GLA_CHUNK
    TT = GLA_STEP_ROWS
    NC = TT // T

    @pl.when(step == 0)
    def _():
        s_scr[...] = jnp.zeros_like(s_scr)

    f = f_ref[0]
    hk = H_B * DK_B
    q = f[:, 0:hk] * (DK_B ** -0.5)
    k = f[:, hk:2 * hk]
    v = f[:, 2 * hk:2 * hk + W_BR]
    gl = f[:, 2 * hk + W_BR:B_PAD]
    P = GLA_PASSES
    g = _log_sigmoid(_bdot(gl, up_ref[...]) + bias_ref[...]) * (1.0 / GLA_TAU)
    bcum = _cumsum_rows(g, T)
    b_last = jnp.concatenate(
        [jnp.broadcast_to(bcum[(c + 1) * T - 1:(c + 1) * T, :], (T, hk)) for c in range(NC)], axis=0)
    qe = q * jnp.exp(bcum)
    ke = k * jnp.exp(-bcum)
    kl = k * jnp.exp(b_last - bcum)
    causal = _iota((T, T), 0) >= _iota((T, T), 1)
    eye_k = _eye(DK_B)
    pieces = [(c, h) for c in range(NC) for h in range(H_B)]
    av, kv, ecol = {}, {}, {}
    for c, h in pieces:
        rows = slice(c * T, (c + 1) * T)
        ks = slice(h * DK_B, (h + 1) * DK_B)
        vs = slice(h * DV_B, (h + 1) * DV_B)
        att = jnp.where(causal, _dotp(qe[rows, ks], ke[rows, ks], NT, P), 0.0)
        av[c, h] = _dotp(att, v[rows, vs], NN, P)
        kv[c, h] = _dotp(kl[rows, ks], v[rows, vs], TN, P)
        e_last = jnp.exp(bcum[(c + 1) * T - 1:(c + 1) * T, ks])
        ecol[c, h] = jnp.sum(eye_k * e_last, axis=1, keepdims=True)
    s = [s_scr[h] for h in range(H_B)]
    for c in range(NC):
        rows = slice(c * T, (c + 1) * T)
        for h in range(H_B):
            ks = slice(h * DK_B, (h + 1) * DK_B)
            o_scr[rows, h * DV_B:(h + 1) * DV_B] = av[c, h] + _dotp(qe[rows, ks], s[h], NN, P)
            s[h] = ecol[c, h] * s[h] + kv[c, h]
    for h in range(H_B):
        s_scr[h] = s[h]
    o = o_scr[...]
    ms = _ones_dot(o * o, _block_ones(W_BR, DV_B).astype(bf16)) * (1.0 / DV_B)
    y_ref[0] = o * lax.rsqrt(ms + NORM_EPS) * ng_ref[...]

    @pl.when(step == pl.num_programs(1) - 1)
    def _():
        s_ref[0] = s_scr[...]


def _gla_prompt(pb, prm):
    b, l, _ = pb.shape
    T = GLA_STEP_ROWS
    return pl.pallas_call(
        _gla_prompt_kernel,
        grid=(b, l // T),
        in_specs=[pl.BlockSpec((1, T, B_PAD), lambda i, c: (i, c, 0)),
                  _const_spec((LANE, H_B * DK_B)), _const_spec((1, H_B * DK_B)), _const_spec((1, W_BR))],
        out_specs=[pl.BlockSpec((1, T, W_BR), lambda i, c: (i, c, 0)),
                   pl.BlockSpec((1, H_B, DK_B, DV_B), lambda i, c: (i, 0, 0, 0))],
        out_shape=[jax.ShapeDtypeStruct((b, l, W_BR), f32),
                   jax.ShapeDtypeStruct((b, H_B, DK_B, DV_B), f32)],
        scratch_shapes=[pltpu.VMEM((H_B, DK_B, DV_B), f32), pltpu.VMEM((T, W_BR), f32)],
        compiler_params=pltpu.CompilerParams(dimension_semantics=("arbitrary", "arbitrary"),
                                             vmem_limit_bytes=VMEM_LIMIT),
        name="gla_prompt",
    )(pb, prm["gup"], prm["gbias"], prm["gng"])


def _swa_prompt_kernel(cur_ref, prev_ref, sink_ref, y_ref):
    step = pl.program_id(1)
    W = WINDOW
    NB = SWA_STEP_BLOCKS
    qo, ko, vo = 0, H_C * HD_C, H_C * HD_C + KV_C * HD_C
    assert G_C == 2
    row = _iota((G_C * W, 2 * W), 0)
    s = _iota((G_C * W, 2 * W), 1)
    t = jnp.where(row >= W, row - W, row)
    dist = W + t - s
    ok = (dist >= 0) & (dist <= W)
    ok_first = ok & ((s >= W) | (step > 0))
    distf = dist.astype(f32)
    second = _iota((G_C * W, 1), 0) >= W
    scale = HD_C ** -0.5

    def band(col, j):
        if j == 0:
            return jnp.concatenate([prev_ref[0, :, col:col + HD_C], cur_ref[0, 0:W, col:col + HD_C]], axis=0)
        return cur_ref[0, (j - 1) * W:(j + 1) * W, col:col + HD_C]

    pieces = [(j, g) for j in range(NB) for g in range(KV_C)]
    scores, sinks = {}, {}
    for g in range(KV_C):
        h0, h1 = g * G_C, g * G_C + 1
        sinks[g] = jnp.where(second, sink_ref[:, h1:h1 + 1], sink_ref[:, h0:h0 + 1])
    for j, g in pieces:
        h0, h1 = g * G_C, g * G_C + 1
        q2 = jnp.concatenate([cur_ref[0, j * W:(j + 1) * W, qo + h0 * HD_C:qo + (h0 + 1) * HD_C],
                              cur_ref[0, j * W:(j + 1) * W, qo + h1 * HD_C:qo + (h1 + 1) * HD_C]], axis=0)
        slope = jnp.where(second, ALIBI_SLOPES[h1], ALIBI_SLOPES[h0])
        raw = _bdot_nt(q2, band(ko + g * HD_C, j)) * scale - slope * distf
        scores[j, g] = jnp.where(ok_first if j == 0 else ok, raw, NEG_BIG)
    probs, dens = {}, {}
    for j, g in pieces:
        m = jnp.maximum(jnp.max(scores[j, g], -1, keepdims=True), sinks[g])
        p = jnp.exp(scores[j, g] - m)
        probs[j, g] = p
        dens[j, g] = jnp.sum(p, -1, keepdims=True) + jnp.exp(sinks[g] - m)
    for j, g in pieces:
        out = _bdot(probs[j, g], band(vo + g * HD_C, j)) / dens[j, g]
        for jj in range(G_C):
            h = g * G_C + jj
            y_ref[0, j * W:(j + 1) * W, h * HD_C:(h + 1) * HD_C] = out[jj * W:(jj + 1) * W]


def _swa_prompt(pc, sinks):
    b, l, _ = pc.shape
    W = WINDOW
    NB = SWA_STEP_BLOCKS
    return pl.pallas_call(
        _swa_prompt_kernel,
        grid=(b, l // (NB * W)),
        in_specs=[pl.BlockSpec((1, NB * W, C_COLS), lambda i, c: (i, c, 0)),
                  pl.BlockSpec((1, W, C_COLS), lambda i, c: (i, jnp.maximum(NB * c - 1, 0), 0)),
                  _const_spec((1, H_C))],
        out_specs=pl.BlockSpec((1, NB * W, W_BR), lambda i, c: (i, c, 0)),
        out_shape=jax.ShapeDtypeStruct((b, l, W_BR), f32),
        compiler_params=pltpu.CompilerParams(dimension_semantics=("arbitrary", "arbitrary"),
                                             vmem_limit_bytes=VMEM_LIMIT),
        name="swa_prompt",
    )(pc, pc, sinks)


def _lru_gates(xc, wa, ba, wx, bx, lam):
    r = _sigmoid(_bdot(xc, wa) + ba)
    i = _sigmoid(_bdot(xc, wx) + bx)
    log_a = C_RG * r * _log_sigmoid(lam)
    a = jnp.exp(log_a)
    bterm = jnp.sqrt(1.0 - jnp.exp(2.0 * log_a)) * (i * xc)
    return a, bterm


def _lru_prompt_kernel(x_ref, cw_ref, cb_ref, wa_ref, ba_ref, wx_ref, bx_ref, lam_ref, y_ref,
                       xbuf_scr, h_scr):
    c = pl.program_id(1)
    T = LRU_CHUNK
    PADR = 8

    @pl.when(c == 0)
    def _():
        xbuf_scr[0:PADR, :] = jnp.zeros((PADR, W_BR), f32)
        h_scr[...] = jnp.zeros_like(h_scr)

    x = x_ref[0]
    xbuf_scr[PADR:PADR + T, :] = x
    xc = cb_ref[...] + x * cw_ref[CONV_W - 1:CONV_W, :]
    for j in range(1, CONV_W):
        xc = xc + xbuf_scr[PADR - j:PADR - j + T, :] * cw_ref[CONV_W - 1 - j:CONV_W - j, :]
    xbuf_scr[0:PADR, :] = x[T - PADR:T, :]
    a, bv = _lru_gates(xc, wa_ref[...], ba_ref[...], wx_ref[...], bx_ref[...], lam_ref[...])
    row = _iota((T, W_BR), 0)
    d = 1
    while d < T:
        keep = row >= d
        a_sh = jnp.where(keep, pltpu.roll(a, d, 0), 1.0)
        b_sh = jnp.where(keep, pltpu.roll(bv, d, 0), 0.0)
        bv = a * b_sh + bv
        a = a * a_sh
        d *= 2
    h = a * h_scr[...] + bv
    y_ref[0] = h
    h_scr[...] = h[T - 1:T, :]


def _lru_prompt(pd, prm):
    b, l, _ = pd.shape
    T = LRU_CHUNK
    row = lambda: _const_spec((1, W_BR))
    return pl.pallas_call(
        _lru_prompt_kernel,
        grid=(b, l // T),
        in_specs=[pl.BlockSpec((1, T, W_BR), lambda i, c: (i, c, 0)), _const_spec((CONV_W, W_BR)), row(),
                  _const_spec((W_BR, W_BR)), row(), _const_spec((W_BR, W_BR)), row(), row()],
        out_specs=pl.BlockSpec((1, T, W_BR), lambda i, c: (i, c, 0)),
        out_shape=jax.ShapeDtypeStruct((b, l, W_BR), f32),
        scratch_shapes=[pltpu.VMEM((T + 8, W_BR), f32), pltpu.VMEM((1, W_BR), f32)],
        compiler_params=pltpu.CompilerParams(dimension_semantics=("arbitrary", "arbitrary"),
                                             vmem_limit_bytes=VMEM_LIMIT),
        name="lru_prompt",
    )(pd, prm["cw"], prm["cb"], prm["lwa"], prm["lba"], prm["lwx"], prm["lbx"], prm["lam"])


def _colbcast(row, n_out, eye_bf16):
    c = row.shape[1]
    hi, lo = _split_bf16(row)
    return (_dg(eye_bf16, jnp.broadcast_to(hi, (n_out, c)), NT)
            + _dg(eye_bf16, jnp.broadcast_to(lo, (n_out, c)), NT))


def _decode_kernel(pa_ref, pb_ref, q8_ref, kvn_ref, pd_ref, shift_ref, swkv_ref, sgla_ref, kbuf_ref, vbuf_ref,
                   c0_ref, c1_ref, c2_ref, h0_ref,
                   mu_ref, w0_ref, wup_ref, a0_ref, aup_ref, kkw_ref, kaw_ref, rk_ref, lng_ref, lnb_ref,
                   gup_ref, gbias_ref, gng_ref, sink8_ref, slope8_ref,
                   cw_ref, cb_ref, wa_ref, ba_ref, wx_ref, bx_ref, lam_ref,
                   ya_ref, yb_ref, yc8_ref, yd_ref, swkv_out, sgla_out, kout_ref, vout_ref,
                   oa_scr, ob_scr):
    bt = pa_ref.shape[0]
    samples = range(bt)
    u = pa_ref[...]
    us = u + (shift_ref[...] - u) * mu_ref[...]
    r, kmod, v, ld, kk, a = _rwkv_features(us, w0_ref[...], wup_ref[...], a0_ref[...], aup_ref[...],
                                           kkw_ref[...], kaw_ref[...])
    decay = jnp.exp(ld)
    kka = kk * a
    fb = pb_ref[...]
    hk = H_B * DK_B
    qb = fb[:, 0:hk] * (DK_B ** -0.5)
    kb = fb[:, hk:2 * hk]
    vb = fb[:, 2 * hk:2 * hk + W_BR]
    gb = _log_sigmoid(_bdot(fb[:, 2 * hk + W_BR:B_PAD], gup_ref[...]) + gbias_ref[...]) * (1.0 / GLA_TAU)
    eg = jnp.exp(gb)
    eye_a = _eye(HEAD_A).astype(bf16)
    eye_b = _eye(DK_B).astype(bf16)
    row_of = lambda t, b: t[b:b + 1, :]

    v_col, eg_col, k_col, sa = {}, {}, {}, {}
    for b in samples:
        for h in range(H_A):
            sl = slice(h * HEAD_A, (h + 1) * HEAD_A)
            v_col[b, h] = _colbcast(row_of(v, b)[:, sl], HEAD_A, eye_a)
            kk_rows = jnp.broadcast_to(row_of(kk, b)[:, sl], (HEAD_A, HEAD_A))
            sa[b, h] = -_dotp(swkv_ref[b, h], kk_rows, NT, DECODE_PASSES)
        for h in range(H_B):
            ks = slice(h * DK_B, (h + 1) * DK_B)
            eg_col[b, h] = _colbcast(row_of(eg, b)[:, ks], DV_B, eye_b)
            k_col[b, h] = _colbcast(row_of(kb, b)[:, ks], DV_B, eye_b)
    for b in samples:
        for h in range(H_A):
            sl = slice(h * HEAD_A, (h + 1) * HEAD_A)
            s = swkv_ref[b, h]
            swkv_out[b, h] = (s * row_of(decay, b)[:, sl] + sa[b, h] * row_of(kka, b)[:, sl]
                              + v_col[b, h] * row_of(kmod, b)[:, sl])
        for h in range(H_B):
            vs = slice(h * DV_B, (h + 1) * DV_B)
            sgla_out[b, h] = eg_col[b, h] * sgla_ref[b, h] + k_col[b, h] * row_of(vb, b)[:, vs]
    for b in samples:
        for h in range(H_A):
            sl = slice(h * HEAD_A, (h + 1) * HEAD_A)
            r_rows = jnp.broadcast_to(row_of(r, b)[:, sl], (8, HEAD_A))
            oa_scr[b:b + 1, sl] = _dotp(r_rows, swkv_out[b, h], NT, DECODE_PASSES)[0:1, :]
        for h in range(H_B):
            ks = slice(h * DK_B, (h + 1) * DK_B)
            q_rows = jnp.broadcast_to(row_of(qb, b)[:, ks], (8, DK_B))
            ob_scr[b:b + 1, h * DV_B:(h + 1) * DV_B] = _dotp(q_rows, sgla_out[b, h], NN, DECODE_PASSES)[0:1, :]
    wdist = (WINDOW - _iota((1, WINDOW), 1)).astype(f32)
    scale = HD_C ** -0.5
    sink8 = sink8_ref[...]
    slope8 = slope8_ref[...]
    half = KV_C * HD_C
    kn = lambda b: kvn_ref[b:b + 1, 0:half]
    vn = lambda b: kvn_ref[b:b + 1, half:2 * half]
    scores = [_bdot_nt(q8_ref[b], kbuf_ref[b]) * scale - slope8 * wdist for b in samples]
    probs, tails = [], []
    for b in samples:
        sn = jnp.sum(q8_ref[b] * kn(b), axis=1, keepdims=True) * scale
        m = jnp.maximum(jnp.maximum(jnp.max(scores[b], axis=1, keepdims=True), sn), sink8)
        p = jnp.exp(scores[b] - m)
        pn = jnp.exp(sn - m)
        probs.append(p)
        tails.append((pn, jnp.sum(p, axis=1, keepdims=True) + pn + jnp.exp(sink8 - m)))
    for b in samples:
        pn, den = tails[b]
        yc8_ref[b] = (_bdot(probs[b], vbuf_ref[b]) + pn * vn(b)) / den
    for b in samples:
        kout_ref[b, 0:WINDOW - 1, :] = kbuf_ref[b, 1:WINDOW, :]
        kout_ref[b, WINDOW - 1:WINDOW, :] = kn(b)
        vout_ref[b, 0:WINDOW - 1, :] = vbuf_ref[b, 1:WINDOW, :]
        vout_ref[b, WINDOW - 1:WINDOW, :] = vn(b)

    ya_ref[...] = _rwkv_finish(oa_scr[...], r, kmod, v, rk_ref[...], lng_ref[...], lnb_ref[...])
    ob = ob_scr[...]
    ms = _ones_dot(ob * ob, _block_ones(W_BR, DV_B).astype(bf16)) * (1.0 / DV_B)
    yb_ref[...] = ob * lax.rsqrt(ms + NORM_EPS) * gng_ref[...]
    xd = pd_ref[...]
    xc = (cb_ref[...] + c0_ref[...] * cw_ref[0:1, :] + c1_ref[...] * cw_ref[1:2, :]
          + c2_ref[...] * cw_ref[2:3, :] + xd * cw_ref[3:4, :])
    al, bterm = _lru_gates(xc, wa_ref[...], ba_ref[...], wx_ref[...], bx_ref[...], lam_ref[...])
    yd_ref[...] = al * h0_ref[...] + bterm


def _decode(pa, pb, pc, pd, shift0, swkv, sgla, kbuf, vbuf, c0, c1, c2, h0, prm, bt):
    n = pa.shape[0]
    half = KV_C * HD_C
    q4 = pc[:, 0:W_BR].reshape(n, H_C, HD_C)
    q8 = jnp.concatenate(
        [jnp.pad(q4[:, h:h + 1], ((0, 0), (0, 0), ((h // G_C) * HD_C, half - HD_C - (h // G_C) * HD_C)))
         for h in range(H_C)] + [jnp.zeros((n, 8 - H_C, half), f32)], axis=1)
    kvn = pc[:, W_BR:]
    pad8 = lambda t: jnp.pad(t.reshape(-1, 1), ((0, 8 - H_C), (0, 0)))
    sink8 = pad8(prm["sinks"])
    slope8 = pad8(jnp.asarray(ALIBI_SLOPES, f32))
    rows = lambda w: pl.BlockSpec((bt, w), lambda i: (i, 0))
    crow = lambda w: _const_spec((1, w))
    cube = lambda d1, d2: pl.BlockSpec((bt, d1, d2), lambda i: (i, 0, 0))
    wkv_spec = pl.BlockSpec((bt, H_A, HEAD_A, HEAD_A), lambda i: (i, 0, 0, 0))
    gla_spec = pl.BlockSpec((bt, H_B, DK_B, DV_B), lambda i: (i, 0, 0, 0))
    in_specs = [rows(A_PAD), rows(B_PAD), cube(8, half), rows(2 * half), rows(D_COLS), rows(A_PAD),
                wkv_spec, gla_spec, cube(WINDOW, half), cube(WINDOW, half),
                rows(W_BR), rows(W_BR), rows(W_BR), rows(W_BR),
                crow(A_PAD), crow(W_BR), _const_spec((LANE, W_BR)), crow(W_BR), _const_spec((LANE, W_BR)),
                crow(W_BR), crow(W_BR), crow(W_BR), crow(W_BR), crow(W_BR),
                _const_spec((LANE, H_B * DK_B)), crow(H_B * DK_B), crow(W_BR),
                _const_spec((8, 1)), _const_spec((8, 1)),
                _const_spec((CONV_W, W_BR)), crow(W_BR), _const_spec((W_BR, W_BR)), crow(W_BR),
                _const_spec((W_BR, W_BR)), crow(W_BR), crow(W_BR)]
    out_specs = [rows(W_BR), rows(W_BR), cube(8, half), rows(W_BR), wkv_spec, gla_spec,
                 cube(WINDOW, half), cube(WINDOW, half)]
    out_shape = [jax.ShapeDtypeStruct((n, W_BR), f32), jax.ShapeDtypeStruct((n, W_BR), f32),
                 jax.ShapeDtypeStruct((n, 8, half), f32), jax.ShapeDtypeStruct((n, W_BR), f32),
                 jax.ShapeDtypeStruct((n, H_A, HEAD_A, HEAD_A), f32),
                 jax.ShapeDtypeStruct((n, H_B, DK_B, DV_B), f32),
                 jax.ShapeDtypeStruct((n, WINDOW, half), f32), jax.ShapeDtypeStruct((n, WINDOW, half), f32)]
    ya, yb, yc8, yd, wkv1, gla1, k1, v1 = pl.pallas_call(
        _decode_kernel,
        grid=(n // bt,),
        in_specs=in_specs,
        out_specs=out_specs,
        out_shape=out_shape,
        scratch_shapes=[pltpu.VMEM((bt, W_BR), f32), pltpu.VMEM((bt, W_BR), f32)],
        compiler_params=pltpu.CompilerParams(dimension_semantics=("arbitrary",),
                                             vmem_limit_bytes=VMEM_LIMIT),
        name="decode_mixers",
    )(pa, pb, q8, kvn, pd, shift0, swkv, sgla, kbuf, vbuf, c0, c1, c2, h0,
      prm["mu"], prm["w0"], prm["wup"], prm["a0"], prm["aup"], prm["kkw"], prm["kaw"], prm["rk"],
      prm["lng"], prm["lnb"], prm["gup"], prm["gbias"], prm["gng"], sink8, slope8,
      prm["cw"], prm["cb"], prm["lwa"], prm["lba"], prm["lwx"], prm["lbx"], prm["lam"])
    yc = jnp.concatenate([yc8[:, h, (h // G_C) * HD_C:(h // G_C + 1) * HD_C] for h in range(H_C)], axis=1)
    return ya, yb, yc, yd, wkv1, gla1, k1, v1


def _merge_kernel(x_ref, ya_ref, yb_ref, yc_ref, yd_ref, g_ref, wz_ref, wg_ref, wbr_ref, wout_ref,
                  fg_ref, o_ref, *, final):
    x = x_ref[...]
    hn = _rms(x, g_ref[...]).astype(bf16)
    ys = (ya_ref, yb_ref, yc_ref, yd_ref)
    merged = None
    for n in range(N_BRANCH):
        z = jnp.dot(hn, wz_ref[:, n * W_BR:(n + 1) * W_BR], preferred_element_type=f32)
        yz = ys[n][...] * (z * _sigmoid(z))
        br = jnp.dot(yz.astype(bf16), wbr_ref[n], preferred_element_type=f32)
        gate = _sigmoid(jnp.dot(hn, wg_ref[:, n * D_MODEL:(n + 1) * D_MODEL], preferred_element_type=f32))
        merged = gate * br if merged is None else merged + gate * br
    out = x + jnp.dot(merged.astype(bf16), wout_ref[...], preferred_element_type=f32)
    if final:
        out = _rms(out, fg_ref[...])
    o_ref[...] = out


def _merge(x, ya, yb, yc, yd, g, wz, wg, wbr, wout, fg, tm, final):
    m = x.shape[0]
    tile = lambda w: pl.BlockSpec((tm, w), lambda i: (i, 0))
    single = lambda shape: pl.BlockSpec(shape, lambda *_: (0,) * len(shape), pipeline_mode=pl.Buffered(1))
    return pl.pallas_call(
        functools.partial(_merge_kernel, final=final),
        grid=(m // tm,),
        in_specs=[tile(D_MODEL), tile(W_BR), tile(W_BR), tile(W_BR), tile(W_BR), _const_spec((1, D_MODEL)),
                  single((D_MODEL, Z_COLS)), single((D_MODEL, G_COLS)),
                  single((N_BRANCH, W_BR, D_MODEL)), single((D_MODEL, D_MODEL)), _const_spec((1, D_MODEL))],
        out_specs=tile(D_MODEL),
        out_shape=jax.ShapeDtypeStruct((m, D_MODEL), f32),
        compiler_params=pltpu.CompilerParams(dimension_semantics=("arbitrary",),
                                             vmem_limit_bytes=VMEM_LIMIT),
        name="merge_final" if final else "merge",
    )(x, ya, yb, yc, yd, g, wz, wg, wbr, wout, fg)


def _pad_cols(w, width):
    return jnp.pad(w, ((0, 0), (0, width - w.shape[1])))


def _pad_rows_at(w, start, total):
    return jnp.pad(w, ((start, total - start - w.shape[0]), (0, 0)))


def _block_diag(w):
    nb, bs, _ = w.shape
    out = jnp.zeros((nb * bs, nb * bs), w.dtype)
    for n in range(nb):
        out = out.at[n * bs:(n + 1) * bs, n * bs:(n + 1) * bs].set(w[n])
    return out


def _layer_params(l, norm_g, w_in, mu_shift, w0, w_decay_up, a0, a_icl_up, k_k, k_a, r_k, ln_x_g, ln_x_b,
                  gla_gate_up, gla_gate_b, gla_norm_g, swa_sinks, lru_conv_w, lru_conv_b, lru_wa, lru_ba,
                  lru_wx, lru_bx, lru_lambda, w_branch, w_out):
    w = w_in[l]
    o_b = A_COLS
    o_c = o_b + B_COLS
    o_d = o_c + C_COLS
    o_z = o_d + D_COLS
    o_g = o_z + Z_COLS
    row = lambda t: t.reshape(1, -1)
    return dict(
        g=row(norm_g[l]),
        wa=_pad_cols(w[:, 0:o_b], A_PAD).astype(bf16),
        wb=_pad_cols(w[:, o_b:o_c], B_PAD).astype(bf16),
        wc=w[:, o_c:o_d].astype(bf16),
        wd=w[:, o_d:o_z].astype(bf16),
        wz=w[:, o_z:o_g].astype(bf16),
        wg=w[:, o_g:].astype(bf16),
        wbr=w_branch[l].astype(bf16),
        wout=w_out[l].astype(bf16),
        mu=_pad_cols(row(mu_shift[l]), A_PAD),
        w0=row(w0[l]),
        wup=_pad_rows_at(w_decay_up[l], 0, LANE),
        a0=row(a0[l]),
        aup=_pad_rows_at(a_icl_up[l], R_DECAY, LANE),
        kkw=row(k_k[l]), kaw=row(k_a[l]), rk=row(r_k[l]), lng=row(ln_x_g[l]), lnb=row(ln_x_b[l]),
        gup=_pad_rows_at(gla_gate_up[l], 0, LANE),
        gbias=row(gla_gate_b[l]),
        gng=row(jnp.tile(gla_norm_g[l], H_B)),
        sinks=row(swa_sinks[l]),
        cw=lru_conv_w[l], cb=row(lru_conv_b[l]),
        lwa=_block_diag(lru_wa[l]).astype(bf16), lba=row(lru_ba[l]),
        lwx=_block_diag(lru_wx[l]).astype(bf16), lbx=row(lru_bx[l]),
        lam=row(lru_lambda[l]),
    )


def kernel(x_prompt, x_sample, state_wkv, state_shift, state_gla, cache_swa_k, cache_swa_v, state_lru_conv, state_lru_h, norm_g, w_in, mu_shift, w0, w_decay_up, a0, a_icl_up, k_k, k_a, r_k, ln_x_g, ln_x_b, gla_gate_up, gla_gate_b, gla_norm_g, swa_sinks, lru_conv_w, lru_conv_b, lru_wa, lru_ba, lru_wx, lru_bx, lru_lambda, w_branch, w_out, final_norm_g):
    bp, lp, _ = x_prompt.shape
    bs = x_sample.shape[0]
    depth = w_in.shape[0]
    fg = final_norm_g.reshape(1, -1)
    xp = x_prompt.reshape(bp * lp, D_MODEL)
    xs = x_sample.reshape(bs, D_MODEL)
    outs_p = [[] for _ in range(7)]
    outs_s = [[] for _ in range(7)]
    for l in range(depth):
        prm = _layer_params(l, norm_g, w_in, mu_shift, w0, w_decay_up, a0, a_icl_up, k_k, k_a, r_k, ln_x_g,
                            ln_x_b, gla_gate_up, gla_gate_b, gla_norm_g, swa_sinks, lru_conv_w, lru_conv_b,
                            lru_wa, lru_ba, lru_wx, lru_bx, lru_lambda, w_branch, w_out)
        final = l == depth - 1
        pa, pb, pc, pd = _inproj(xp, prm["g"], prm["wa"], prm["wb"], prm["wc"], prm["wd"], tm=512)
        pa3, pb3 = pa.reshape(bp, lp, A_PAD), pb.reshape(bp, lp, B_PAD)
        pc3, pd3 = pc.reshape(bp, lp, C_COLS), pd.reshape(bp, lp, D_COLS)
        ya, st_t = _rwkv_prompt(pa3, prm)
        yb, sgla = _gla_prompt(pb3, prm)
        yc = _swa_prompt(pc3, prm["sinks"])
        yd = _lru_prompt(pd3, prm)
        flat = lambda t: t.reshape(bp * lp, W_BR)
        xp = _merge(xp, flat(ya), flat(yb), flat(yc), flat(yd), prm["g"], prm["wz"], prm["wg"], prm["wbr"],
                    prm["wout"], fg, tm=256, final=final)
        kv = pc3[:, lp - WINDOW:, H_C * HD_C:]
        outs_p[0].append(jnp.swapaxes(st_t, -1, -2))
        outs_p[1].append(pa3[:, lp - 1, :A_COLS])
        outs_p[2].append(sgla)
        outs_p[3].append(kv[:, :, :KV_C * HD_C].reshape(bp, WINDOW, KV_C, HD_C))
        outs_p[4].append(kv[:, :, KV_C * HD_C:].reshape(bp, WINDOW, KV_C, HD_C))
        outs_p[5].append(pd3[:, lp - (CONV_W - 1):, :])
        outs_p[6].append(yd[:, lp - 1, :])
        sa, sb, sc, sd = _inproj(xs, prm["g"], prm["wa"], prm["wb"], prm["wc"], prm["wd"], tm=bs)
        kbuf = cache_swa_k[l].reshape(bs, WINDOW, KV_C * HD_C)
        vbuf = cache_swa_v[l].reshape(bs, WINDOW, KV_C * HD_C)
        conv0 = state_lru_conv[l]
        ya_s, yb_s, yc_s, yd_s, wkv1, gla1, k1_s, v1_s = _decode(
            sa, sb, sc, sd, _pad_cols(state_shift[l], A_PAD), state_wkv[l], state_gla[l], kbuf, vbuf,
            conv0[:, 0], conv0[:, 1], conv0[:, 2], state_lru_h[l], prm, bt=8)
        xs = _merge(xs, ya_s, yb_s, yc_s, yd_s, prm["g"], prm["wz"], prm["wg"], prm["wbr"], prm["wout"], fg,
                    tm=bs, final=final)
        outs_s[0].append(wkv1)
        outs_s[1].append(sa[:, :A_COLS])
        outs_s[2].append(gla1)
        outs_s[3].append(k1_s.reshape(bs, WINDOW, KV_C, HD_C))
        outs_s[4].append(v1_s.reshape(bs, WINDOW, KV_C, HD_C))
        outs_s[5].append(jnp.stack([conv0[:, 1], conv0[:, 2], sd], axis=1))
        outs_s[6].append(yd_s)
    y_prompt = xp.reshape(bp, lp, D_MODEL)
    y_sample = xs.reshape(bs, 1, D_MODEL)
    sp = [jnp.stack(t) for t in outs_p]
    ss = [jnp.stack(t) for t in outs_s]
    return (y_prompt, y_sample, sp[0], ss[0], sp[1], ss[1], sp[2], ss[2], sp[3], ss[3], sp[4], ss[4],
            sp[5], ss[5], sp[6], ss[6])
```

```python
import functools

import jax
import jax.numpy as jnp
from jax import lax
from jax.experimental import pallas as pl
from jax.experimental.pallas import tpu as pltpu

f32 = jnp.float32
bf16 = jnp.bfloat16
HI = lax.Precision.HIGHEST

D_MODEL = 1024
N_BRANCH = 4
W_BR = 256
HEAD_A = 64
H_A = 4
R_DECAY = 32
R_ICL = 32
GN_EPS_A = 64e-5
H_B = 4
DK_B = 32
DV_B = 64
R_GATE_B = 16
GLA_TAU = 16.0
GLA_CHUNK = 64
GLA_STEP_ROWS = 256
H_C = 4
KV_C = 2
HD_C = 64
G_C = 2
WINDOW = 128
SWA_STEP_BLOCKS = 4
CONV_W = 4
C_RG = 8.0
NORM_EPS = 1e-6

A_COLS = 3 * W_BR + R_DECAY + R_ICL
B_COLS = 2 * H_B * DK_B + W_BR + R_GATE_B
C_COLS = H_C * HD_C + 2 * KV_C * HD_C
D_COLS = W_BR
Z_COLS = N_BRANCH * W_BR
G_COLS = N_BRANCH * D_MODEL

LANE = 128
A_PAD = 7 * LANE
B_PAD = 5 * LANE
RWKV_CHUNK = 64
RWKV_STEP_ROWS = 256
RWKV_PASSES = 1
GLA_PASSES = 1
DECODE_PASSES = 1
LRU_CHUNK = 256
VMEM_LIMIT = 56 * 1024 * 1024
NEG_BIG = -1e30

ALIBI_SLOPES = tuple(2.0 ** (-8.0 * (h + 1) / H_C) for h in range(H_C))


def _mm(a, b):
    return jnp.dot(a, b, precision=HI, preferred_element_type=f32)


def _mm_nt(a, b):
    return lax.dot_general(a, b, (((1,), (1,)), ((), ())), precision=HI, preferred_element_type=f32)


def _mm_tn(a, b):
    return lax.dot_general(a, b, (((0,), (0,)), ((), ())), precision=HI, preferred_element_type=f32)


def _bdot(a, b):
    return jnp.dot(a.astype(bf16), b.astype(bf16), preferred_element_type=f32)


def _bdot_nt(a, b):
    return lax.dot_general(a.astype(bf16), b.astype(bf16), (((1,), (1,)), ((), ())),
                           preferred_element_type=f32)


NN = ((1,), (0,))
NT = ((1,), (1,))
TN = ((0,), (0,))


def _dg(a, b, dims):
    return lax.dot_general(a, b, (dims, ((), ())), preferred_element_type=f32)


def _split_bf16(a):
    hi = a.astype(bf16)
    return hi, (a - hi.astype(f32)).astype(bf16)


def _dotp(a, b, dims, passes):
    if passes == 1:
        return _dg(a.astype(bf16), b.astype(bf16), dims)
    ah, al = _split_bf16(a)
    bh, bl = _split_bf16(b)
    return _dg(ah, bh, dims) + (_dg(ah, bl, dims) + _dg(al, bh, dims))


def _ones_dot(a, ones_bf16):
    ah, al = _split_bf16(a)
    return _dg(ah, ones_bf16, NN) + _dg(al, ones_bf16, NN)


def _cumsum_rows(x, seg):
    pos = _iota(x.shape, 0) % seg
    d = 1
    while d < seg:
        x = x + jnp.where(pos >= d, pltpu.roll(x, d, 0), 0.0)
        d *= 2
    return x


def _iota(shape, dim):
    return lax.broadcasted_iota(jnp.int32, shape, dim)


def _eye(n):
    return (_iota((n, n), 0) == _iota((n, n), 1)).astype(f32)


def _block_ones(n, blk):
    return ((_iota((n, n), 0) // blk) == (_iota((n, n), 1) // blk)).astype(f32)


def _tril_ones(n):
    return (_iota((n, n), 0) >= _iota((n, n), 1)).astype(f32)


def _softplus(x):
    return jnp.maximum(x, 0.0) + jnp.log(1.0 + jnp.exp(-jnp.abs(x)))


def _log_sigmoid(x):
    return -_softplus(-x)


def _sigmoid(x):
    return 1.0 / (1.0 + jnp.exp(-x))


def _rms(x, g):
    return x * lax.rsqrt(jnp.mean(x * x, -1, keepdims=True) + NORM_EPS) * g


MIX_WIDTHS = (A_PAD, B_PAD, C_COLS, D_COLS)
MIX_BLOCK = 2560
GATE_BLOCK = Z_COLS + G_COLS
PACKED_COLS = GATE_BLOCK + MIX_BLOCK


def _inproj_kernel(x_ref, g_ref, w_ref, oa_ref, ob_ref, oc_ref, od_ref):
    hn = _rms(x_ref[...], g_ref[...]).astype(bf16)
    start = 0
    for o_ref, width in zip((oa_ref, ob_ref, oc_ref, od_ref), MIX_WIDTHS):
        o_ref[...] = jnp.dot(hn, w_ref[:, start:start + width], preferred_element_type=f32)
        start += width


def _const_spec(shape):
    return pl.BlockSpec(shape, lambda *_: (0,) * len(shape))


def _inproj(x, g, w_packed, layer, tm):
    m = x.shape[0]
    return pl.pallas_call(
        _inproj_kernel,
        grid=(m // tm,),
        in_specs=[pl.BlockSpec((tm, D_MODEL), lambda i: (i, 0)), _const_spec((1, D_MODEL)),
                  pl.BlockSpec((None, D_MODEL, MIX_BLOCK), lambda i: (layer, 0, GATE_BLOCK // MIX_BLOCK))],
        out_specs=[pl.BlockSpec((tm, w), lambda i: (i, 0)) for w in MIX_WIDTHS],
        out_shape=[jax.ShapeDtypeStruct((m, w), f32) for w in MIX_WIDTHS],
        compiler_params=pltpu.CompilerParams(dimension_semantics=("arbitrary",),
                                             vmem_limit_bytes=VMEM_LIMIT),
        name="inproj",
    )(x, g, w_packed)


def _rwkv_features(us, w0, wup, a0, aup, kk_w, ka_w):
    r = us[:, 0:W_BR]
    k = us[:, W_BR:2 * W_BR]
    v = us[:, 2 * W_BR:3 * W_BR]
    lora = us[:, 3 * W_BR:A_PAD]
    w = -_softplus(-(w0 + _bdot(jnp.tanh(lora), wup))) - 0.5
    logdecay = -jnp.exp(w)
    a = _sigmoid(a0 + _bdot(lora, aup))
    kk = k * kk_w
    ss = _ones_dot(kk * kk, _block_ones(W_BR, HEAD_A).astype(bf16))
    kk = kk / jnp.maximum(jnp.sqrt(ss), 1e-12)
    kmod = k * (1.0 + (a - 1.0) * ka_w)
    return r, kmod, v, logdecay, kk, a


def _rwkv_finish(o, r, kmod, v, rk, lng, lnb):
    ones = _block_ones(W_BR, HEAD_A).astype(bf16)
    mean = _ones_dot(o, ones) * (1.0 / HEAD_A)
    cen = o - mean
    var = _ones_dot(cen * cen, ones) * (1.0 / HEAD_A)
    o = cen * lax.rsqrt(var + GN_EPS_A) * lng + lnb
    bonus = _ones_dot(r * kmod * rk, ones) * v
    return o + bonus


def _rwkv_prompt_kernel(u_ref, mu_ref, w0_ref, wup_ref, a0_ref, aup_ref, kkw_ref, kaw_ref, rk_ref,
                        lng_ref, lnb_ref, y_ref, s_ref, st_scr, prev_scr, o_scr):
    step = pl.program_id(1)
    T = RWKV_CHUNK
    TT = RWKV_STEP_ROWS

    @pl.when(step == 0)
    def _():
        st_scr[...] = jnp.zeros_like(st_scr)
        prev_scr[...] = jnp.zeros_like(prev_scr)

    u = u_ref[0]
    row = _iota(u.shape, 0)
    u_prev = jnp.where(row == 0, prev_scr[...], pltpu.roll(u, 1, 0))
    prev_scr[...] = u[TT - 1:TT, :]
    us = u + (u_prev - u) * mu_ref[...]
    r, kmod, v, ld, kk, a = _rwkv_features(us, w0_ref[...], wup_ref[...], a0_ref[...], aup_ref[...],
                                           kkw_ref[...], kaw_ref[...])
    P = RWKV_PASSES
    NC = TT // T
    cum = _cumsum_rows(ld, T)
    cum_last = jnp.concatenate(
        [jnp.broadcast_to(cum[(c + 1) * T - 1:(c + 1) * T, :], (T, W_BR)) for c in range(NC)], axis=0)
    g_inv = jnp.exp(-cum)
    g_tail = jnp.exp(cum_last - cum)
    kka = kk * a
    at = -kk * jnp.exp(cum - ld)
    rt = r * jnp.exp(cum)
    bt = kka * g_inv
    kt = kmod * g_inv
    btg = kka * g_tail
    ktg = kmod * g_tail

    ri = _iota((2 * T, 2 * T), 0)
    ci = _iota((2 * T, 2 * T), 1)
    ti = jnp.where(ri >= T, ri - T, ri)
    si = jnp.where(ci >= T, ci - T, ci)
    keep = (ti > si) | ((ri >= T) & (ti == si))
    eye_t = _eye(T)
    eye_h = _eye(HEAD_A)

    pieces = [(c, h) for c in range(NC) for h in range(H_A)]
    ar, vh, pm, x, pw = {}, {}, {}, {}, {}
    for c, h in pieces:
        rows = slice(c * T, (c + 1) * T)
        sl = slice(h * HEAD_A, (h + 1) * HEAD_A)
        vh[c, h] = v[rows, sl]
        ar[c, h] = jnp.concatenate([at[rows, sl], rt[rows, sl]], axis=0)
        bk = jnp.concatenate([bt[rows, sl], kt[rows, sl]], axis=0)
        pm[c, h] = jnp.where(keep, _dotp(ar[c, h], bk, NT, P), 0.0)
    for c, h in pieces:
        lab = pm[c, h][0:T, 0:T]
        x[c, h] = eye_t + lab
        pw[c, h] = _dotp(lab, lab, NN, P)
    for it in range(5):
        for c, h in pieces:
            x_next = x[c, h] + _dotp(pw[c, h], x[c, h], NN, P)
            if it < 4:
                pw[c, h] = _dotp(pw[c, h], pw[c, h], NN, P)
            x[c, h] = x_next
    lv, gcol, bkg, xar = {}, {}, {}, {}
    for c, h in pieces:
        rows = slice(c * T, (c + 1) * T)
        sl = slice(h * HEAD_A, (h + 1) * HEAD_A)
        lv[c, h] = _dotp(pm[c, h][:, T:2 * T], vh[c, h], NN, P)
        g_last = jnp.exp(cum[(c + 1) * T - 1:(c + 1) * T, sl])
        gcol[c, h] = jnp.sum(eye_h * g_last, axis=1, keepdims=True)
        bkg[c, h] = jnp.concatenate([btg[rows, sl], ktg[rows, sl]], axis=0)
    for c, h in pieces:
        xa = _dotp(x[c, h], ar[c, h][0:T], NN, P)
        xl = _dotp(x[c, h], lv[c, h][0:T], NN, P)
        xar[c, h] = jnp.concatenate([xa, ar[c, h][T:2 * T]], axis=0)
        lv[c, h] = jnp.concatenate([xl, lv[c, h][T:2 * T]], axis=0)

    st = [st_scr[h] for h in range(H_A)]
    heads = range(H_A)
    for c in range(NC):
        rows = slice(c * T, (c + 1) * T)
        base = [_dotp(xar[c, h], st[h], NN, P) + lv[c, h] for h in heads]
        st = [gcol[c, h] * st[h] + _dotp(bkg[c, h], jnp.concatenate([base[h][0:T], vh[c, h]], axis=0), TN, P)
              for h in heads]
        for h in heads:
            sl = slice(h * HEAD_A, (h + 1) * HEAD_A)
            o_scr[rows, sl] = base[h][T:2 * T] + _dotp(pm[c, h][T:2 * T, 0:T], base[h][0:T], NN, P)
    for h in heads:
        st_scr[h] = st[h]

    y_ref[0] = _rwkv_finish(o_scr[...], r, kmod, v, rk_ref[...], lng_ref[...], lnb_ref[...])

    @pl.when(step == pl.num_programs(1) - 1)
    def _():
        s_ref[0] = st_scr[...]


def _rwkv_prompt(pa, prm):
    b, l, _ = pa.shape
    T = RWKV_STEP_ROWS
    row = lambda w: _const_spec((1, w))
    return pl.pallas_call(
        _rwkv_prompt_kernel,
        grid=(b, l // T),
        in_specs=[pl.BlockSpec((1, T, A_PAD), lambda i, c: (i, c, 0)), row(A_PAD), row(W_BR),
                  _const_spec((LANE, W_BR)), row(W_BR), _const_spec((LANE, W_BR)),
                  row(W_BR), row(W_BR), row(W_BR), row(W_BR), row(W_BR)],
        out_specs=[pl.BlockSpec((1, T, W_BR), lambda i, c: (i, c, 0)),
                   pl.BlockSpec((1, H_A, HEAD_A, HEAD_A), lambda i, c: (i, 0, 0, 0))],
        out_shape=[jax.ShapeDtypeStruct((b, l, W_BR), f32),
                   jax.ShapeDtypeStruct((b, H_A, HEAD_A, HEAD_A), f32)],
        scratch_shapes=[pltpu.VMEM((H_A, HEAD_A, HEAD_A), f32), pltpu.VMEM((1, A_PAD), f32),
                        pltpu.VMEM((T, W_BR), f32)],
        compiler_params=pltpu.CompilerParams(dimension_semantics=("arbitrary", "arbitrary"),
                                             vmem_limit_bytes=VMEM_LIMIT),
        name="rwkv_prompt",
    )(pa, prm["mu"], prm["w0"], prm["wup"], prm["a0"], prm["aup"], prm["kkw"], prm["kaw"], prm["rk"],
      prm["lng"], prm["lnb"])


def _gla_prompt_kernel(f_ref, up_ref, bias_ref, ng_ref, y_ref, s_ref, s_scr, o_scr):
    step = pl.program_id(1)
    T = GLA_CHUNK
    TT = GLA_STEP_ROWS
    NC = TT // T

    @pl.when(step == 0)
    def _():
        s_scr[...] = jnp.zeros_like(s_scr)

    f = f_ref[0]
    hk = H_B * DK_B
    q = f[:, 0:hk] * (DK_B ** -0.5)
    k = f[:, hk:2 * hk]
    v = f[:, 2 * hk:2 * hk + W_BR]
    gl = f[:, 2 * hk + W_BR:B_PAD]
    P = GLA_PASSES
    g = _log_sigmoid(_bdot(gl, up_ref[...]) + bias_ref[...]) * (1.0 / GLA_TAU)
    bcum = _cumsum_rows(g, T)
    b_last = jnp.concatenate(
        [jnp.broadcast_to(bcum[(c + 1) * T - 1:(c + 1) * T, :], (T, hk)) for c in range(NC)], axis=0)
    qe = q * jnp.exp(bcum)
    ke = k * jnp.exp(-bcum)
    kl = k * jnp.exp(b_last - bcum)
    causal = _iota((T, T), 0) >= _iota((T, T), 1)
    eye_k = _eye(DK_B)
    pieces = [(c, h) for c in range(NC) for h in range(H_B)]
    av, kv, ecol = {}, {}, {}
    for c, h in pieces:
        rows = slice(c * T, (c + 1) * T)
        ks = slice(h * DK_B, (h + 1) * DK_B)
        vs = slice(h * DV_B, (h + 1) * DV_B)
        att = jnp.where(causal, _dotp(qe[rows, ks], ke[rows, ks], NT, P), 0.0)
        av[c, h] = _dotp(att, v[rows, vs], NN, P)
        kv[c, h] = _dotp(kl[rows, ks], v[rows, vs], TN, P)
        e_last = jnp.exp(bcum[(c + 1) * T - 1:(c + 1) * T, ks])
        ecol[c, h] = jnp.sum(eye_k * e_last, axis=1, keepdims=True)
    s = [s_scr[h] for h in range(H_B)]
    for c in range(NC):
        rows = slice(c * T, (c + 1) * T)
        for h in range(H_B):
            ks = slice(h * DK_B, (h + 1) * DK_B)
            o_scr[rows, h * DV_B:(h + 1) * DV_B] = av[c, h] + _dotp(qe[rows, ks], s[h], NN, P)
            s[h] = ecol[c, h] * s[h] + kv[c, h]
    for h in range(H_B):
        s_scr[h] = s[h]
    o = o_scr[...]
    ms = _ones_dot(o * o, _block_ones(W_BR, DV_B).astype(bf16)) * (1.0 / DV_B)
    y_ref[0] = o * lax.rsqrt(ms + NORM_EPS) * ng_ref[...]

    @pl.when(step == pl.num_programs(1) - 1)
    def _():
        s_ref[0] = s_scr[...]


def _gla_prompt(pb, prm):
    b, l, _ = pb.shape
    T = GLA_STEP_ROWS
    return pl.pallas_call(
        _gla_prompt_kernel,
        grid=(b, l // T),
        in_specs=[pl.BlockSpec((1, T, B_PAD), lambda i, c: (i, c, 0)),
                  _const_spec((LANE, H_B * DK_B)), _const_spec((1, H_B * DK_B)), _const_spec((1, W_BR))],
        out_specs=[pl.BlockSpec((1, T, W_BR), lambda i, c: (i, c, 0)),
                   pl.BlockSpec((1, H_B, DK_B, DV_B), lambda i, c: (i, 0, 0, 0))],
        out_shape=[jax.ShapeDtypeStruct((b, l, W_BR), f32),
                   jax.ShapeDtypeStruct((b, H_B, DK_B, DV_B), f32)],
        scratch_shapes=[pltpu.VMEM((H_B, DK_B, DV_B), f32), pltpu.VMEM((T, W_BR), f32)],
        compiler_params=pltpu.CompilerParams(dimension_semantics=("arbitrary", "arbitrary"),
                                             vmem_limit_bytes=VMEM_LIMIT),
        name="gla_prompt",
    )(pb, prm["gup"], prm["gbias"], prm["gng"])


def _swa_prompt_kernel(cur_ref, prev_ref, sink_ref, y_ref):
    step = pl.program_id(1)
    W = WINDOW
    NB = SWA_STEP_BLOCKS
    qo, ko, vo = 0, H_C * HD_C, H_C * HD_C + KV_C * HD_C
    assert G_C == 2
    row = _iota((G_C * W, 2 * W), 0)
    s = _iota((G_C * W, 2 * W), 1)
    t = jnp.where(row >= W, row - W, row)
    dist = W + t - s
    ok = (dist >= 0) & (dist <= W)
    ok_first = ok & ((s >= W) | (step > 0))
    distf = dist.astype(f32)
    second = _iota((G_C * W, 1), 0) >= W
    scale = HD_C ** -0.5

    def band(col, j):
        if j == 0:
            return jnp.concatenate([prev_ref[0, :, col:col + HD_C], cur_ref[0, 0:W, col:col + HD_C]], axis=0)
        return cur_ref[0, (j - 1) * W:(j + 1) * W, col:col + HD_C]

    pieces = [(j, g) for j in range(NB) for g in range(KV_C)]
    scores, sinks = {}, {}
    for g in range(KV_C):
        h0, h1 = g * G_C, g * G_C + 1
        sinks[g] = jnp.where(second, sink_ref[:, h1:h1 + 1], sink_ref[:, h0:h0 + 1])
    for j, g in pieces:
        h0, h1 = g * G_C, g * G_C + 1
        q2 = jnp.concatenate([cur_ref[0, j * W:(j + 1) * W, qo + h0 * HD_C:qo + (h0 + 1) * HD_C],
                              cur_ref[0, j * W:(j + 1) * W, qo + h1 * HD_C:qo + (h1 + 1) * HD_C]], axis=0)
        slope = jnp.where(second, ALIBI_SLOPES[h1], ALIBI_SLOPES[h0])
        raw = _bdot_nt(q2, band(ko + g * HD_C, j)) * scale - slope * distf
        scores[j, g] = jnp.where(ok_first if j == 0 else ok, raw, NEG_BIG)
    probs, dens = {}, {}
    for j, g in pieces:
        m = jnp.maximum(jnp.max(scores[j, g], -1, keepdims=True), sinks[g])
        p = jnp.exp(scores[j, g] - m)
        probs[j, g] = p
        dens[j, g] = jnp.sum(p, -1, keepdims=True) + jnp.exp(sinks[g] - m)
    for j, g in pieces:
        out = _bdot(probs[j, g], band(vo + g * HD_C, j)) / dens[j, g]
        for jj in range(G_C):
            h = g * G_C + jj
            y_ref[0, j * W:(j + 1) * W, h * HD_C:(h + 1) * HD_C] = out[jj * W:(jj + 1) * W]


def _swa_prompt(pc, sinks):
    b, l, _ = pc.shape
    W = WINDOW
    NB = SWA_STEP_BLOCKS
    return pl.pallas_call(
        _swa_prompt_kernel,
        grid=(b, l // (NB * W)),
        in_specs=[pl.BlockSpec((1, NB * W, C_COLS), lambda i, c: (i, c, 0)),
                  pl.BlockSpec((1, W, C_COLS), lambda i, c: (i, jnp.maximum(NB * c - 1, 0), 0)),
                  _const_spec((1, H_C))],
        out_specs=pl.BlockSpec((1, NB * W, W_BR), lambda i, c: (i, c, 0)),
        out_shape=jax.ShapeDtypeStruct((b, l, W_BR), f32),
        compiler_params=pltpu.CompilerParams(dimension_semantics=("arbitrary", "arbitrary"),
                                             vmem_limit_bytes=VMEM_LIMIT),
        name="swa_prompt",
    )(pc, pc, sinks)


def _lru_gates(xc, wa, ba, wx, bx, lam):
    r = _sigmoid(_bdot(xc, wa) + ba)
    i = _sigmoid(_bdot(xc, wx) + bx)
    log_a = C_RG * r * _log_sigmoid(lam)
    a = jnp.exp(log_a)
    bterm = jnp.sqrt(1.0 - jnp.exp(2.0 * log_a)) * (i * xc)
    return a, bterm


def _lru_prompt_kernel(x_ref, cw_ref, cb_ref, wa_ref, ba_ref, wx_ref, bx_ref, lam_ref, y_ref,
                       xbuf_scr, h_scr):
    c = pl.program_id(1)
    T = LRU_CHUNK
    PADR = 8

    @pl.when(c == 0)
    def _():
        xbuf_scr[0:PADR, :] = jnp.zeros((PADR, W_BR), f32)
        h_scr[...] = jnp.zeros_like(h_scr)

    x = x_ref[0]
    xbuf_scr[PADR:PADR + T, :] = x
    xc = cb_ref[...] + x * cw_ref[CONV_W - 1:CONV_W, :]
    for j in range(1, CONV_W):
        xc = xc + xbuf_scr[PADR - j:PADR - j + T, :] * cw_ref[CONV_W - 1 - j:CONV_W - j, :]
    xbuf_scr[0:PADR, :] = x[T - PADR:T, :]
    a, bv = _lru_gates(xc, wa_ref[...], ba_ref[...], wx_ref[...], bx_ref[...], lam_ref[...])
    row = _iota((T, W_BR), 0)
    d = 1
    while d < T:
        keep = row >= d
        a_sh = jnp.where(keep, pltpu.roll(a, d, 0), 1.0)
        b_sh = jnp.where(keep, pltpu.roll(bv, d, 0), 0.0)
        bv = a * b_sh + bv
        a = a * a_sh
        d *= 2
    h = a * h_scr[...] + bv
    y_ref[0] = h
    h_scr[...] = h[T - 1:T, :]


def _lru_prompt(pd, prm):
    b, l, _ = pd.shape
    T = LRU_CHUNK
    row = lambda: _const_spec((1, W_BR))
    return pl.pallas_call(
        _lru_prompt_kernel,
        grid=(b, l // T),
        in_specs=[pl.BlockSpec((1, T, W_BR), lambda i, c: (i, c, 0)), _const_spec((CONV_W, W_BR)), row(),
                  _const_spec((W_BR, W_BR)), row(), _const_spec((W_BR, W_BR)), row(), row()],
        out_specs=pl.BlockSpec((1, T, W_BR), lambda i, c: (i, c, 0)),
        out_shape=jax.ShapeDtypeStruct((b, l, W_BR), f32),
        scratch_shapes=[pltpu.VMEM((T + 8, W_BR), f32), pltpu.VMEM((1, W_BR), f32)],
        compiler_params=pltpu.CompilerParams(dimension_semantics=("arbitrary", "arbitrary"),
                                             vmem_limit_bytes=VMEM_LIMIT),
        name="lru_prompt",
    )(pd, prm["cw"], prm["cb"], prm["lwa"], prm["lba"], prm["lwx"], prm["lbx"], prm["lam"])


def _colbcast(row, n_out, eye_bf16):
    c = row.shape[1]
    hi, lo = _split_bf16(row)
    return (_dg(eye_bf16, jnp.broadcast_to(hi, (n_out, c)), NT)
            + _dg(eye_bf16, jnp.broadcast_to(lo, (n_out, c)), NT))


def _decode_kernel(pa_ref, pb_ref, q8_ref, kvn_ref, pd_ref, shift_ref, swkv_ref, sgla_ref, kbuf_ref, vbuf_ref,
                   c0_ref, c1_ref, c2_ref, h0_ref,
                   mu_ref, w0_ref, wup_ref, a0_ref, aup_ref, kkw_ref, kaw_ref, rk_ref, lng_ref, lnb_ref,
                   gup_ref, gbias_ref, gng_ref, sink8_ref, slope8_ref,
                   cw_ref, cb_ref, wa_ref, ba_ref, wx_ref, bx_ref, lam_ref, *rest, n_prev):
    earlier, rest = rest[:4 if n_prev else 0], rest[4 if n_prev else 0:]
    ya_ref, yb_ref, yc8_ref, yd_ref, swkv_all, sgla_all, kout_all, vout_all, oa_scr, ob_scr = rest
    for src, dst in zip(earlier, (swkv_all, sgla_all, kout_all, vout_all)):
        dst[0:n_prev] = src[...]
    swkv_out, sgla_out = swkv_all.at[n_prev], sgla_all.at[n_prev]
    kout_ref, vout_ref = kout_all.at[n_prev], vout_all.at[n_prev]
    bt = pa_ref.shape[0]
    samples = range(bt)
    u = pa_ref[...]
    us = u + (shift_ref[...] - u) * mu_ref[...]
    r, kmod, v, ld, kk, a = _rwkv_features(us, w0_ref[...], wup_ref[...], a0_ref[...], aup_ref[...],
                                           kkw_ref[...], kaw_ref[...])
    decay = jnp.exp(ld)
    kka = kk * a
    fb = pb_ref[...]
    hk = H_B * DK_B
    qb = fb[:, 0:hk] * (DK_B ** -0.5)
    kb = fb[:, hk:2 * hk]
    vb = fb[:, 2 * hk:2 * hk + W_BR]
    gb = _log_sigmoid(_bdot(fb[:, 2 * hk + W_BR:B_PAD], gup_ref[...]) + gbias_ref[...]) * (1.0 / GLA_TAU)
    eg = jnp.exp(gb)
    eye_a = _eye(HEAD_A).astype(bf16)
    eye_b = _eye(DK_B).astype(bf16)
    row_of = lambda t, b: t[b:b + 1, :]

    v_col, eg_col, k_col, sa = {}, {}, {}, {}
    for b in samples:
        for h in range(H_A):
            sl = slice(h * HEAD_A, (h + 1) * HEAD_A)
            v_col[b, h] = _colbcast(row_of(v, b)[:, sl], HEAD_A, eye_a)
            kk_rows = jnp.broadcast_to(row_of(kk, b)[:, sl], (HEAD_A, HEAD_A))
            sa[b, h] = -_dotp(swkv_ref[b, h], kk_rows, NT, DECODE_PASSES)
        for h in range(H_B):
            ks = slice(h * DK_B, (h + 1) * DK_B)
            eg_col[b, h] = _colbcast(row_of(eg, b)[:, ks], DV_B, eye_b)
            k_col[b, h] = _colbcast(row_of(kb, b)[:, ks], DV_B, eye_b)
    for b in samples:
        for h in range(H_A):
            sl = slice(h * HEAD_A, (h + 1) * HEAD_A)
            s = swkv_ref[b, h]
            swkv_out[b, h] = (s * row_of(decay, b)[:, sl] + sa[b, h] * row_of(kka, b)[:, sl]
                              + v_col[b, h] * row_of(kmod, b)[:, sl])
        for h in range(H_B):
            vs = slice(h * DV_B, (h + 1) * DV_B)
            sgla_out[b, h] = eg_col[b, h] * sgla_ref[b, h] + k_col[b, h] * row_of(vb, b)[:, vs]
    for b in samples:
        for h in range(H_A):
            sl = slice(h * HEAD_A, (h + 1) * HEAD_A)
            r_rows = jnp.broadcast_to(row_of(r, b)[:, sl], (8, HEAD_A))
            oa_scr[b:b + 1, sl] = _dotp(r_rows, swkv_out[b, h], NT, DECODE_PASSES)[0:1, :]
        for h in range(H_B):
            ks = slice(h * DK_B, (h + 1) * DK_B)
            q_rows = jnp.broadcast_to(row_of(qb, b)[:, ks], (8, DK_B))
            ob_scr[b:b + 1, h * DV_B:(h + 1) * DV_B] = _dotp(q_rows, sgla_out[b, h], NN, DECODE_PASSES)[0:1, :]
    wdist = (WINDOW - _iota((1, WINDOW), 1)).astype(f32)
    scale = HD_C ** -0.5
    sink8 = sink8_ref[...]
    slope8 = slope8_ref[...]
    half = KV_C * HD_C
    kn = lambda b: kvn_ref[b:b + 1, 0:half]
    vn = lambda b: kvn_ref[b:b + 1, half:2 * half]
    scores = [_bdot_nt(q8_ref[b], kbuf_ref[b]) * scale - slope8 * wdist for b in samples]
    probs, tails = [], []
    for b in samples:
        sn = jnp.sum(q8_ref[b] * kn(b), axis=1, keepdims=True) * scale
        m = jnp.maximum(jnp.maximum(jnp.max(scores[b], axis=1, keepdims=True), sn), sink8)
        p = jnp.exp(scores[b] - m)
        pn = jnp.exp(sn - m)
        probs.append(p)
        tails.append((pn, jnp.sum(p, axis=1, keepdims=True) + pn + jnp.exp(sink8 - m)))
    for b in samples:
        pn, den = tails[b]
        yc8_ref[b] = (_bdot(probs[b], vbuf_ref[b]) + pn * vn(b)) / den
    for b in samples:
        kout_ref[b, 0:WINDOW - 1, :] = kbuf_ref[b, 1:WINDOW, :]
        kout_ref[b, WINDOW - 1:WINDOW, :] = kn(b)
        vout_ref[b, 0:WINDOW - 1, :] = vbuf_ref[b, 1:WINDOW, :]
        vout_ref[b, WINDOW - 1:WINDOW, :] = vn(b)

    ya_ref[...] = _rwkv_finish(oa_scr[...], r, kmod, v, rk_ref[...], lng_ref[...], lnb_ref[...])
    ob = ob_scr[...]
    ms = _ones_dot(ob * ob, _block_ones(W_BR, DV_B).astype(bf16)) * (1.0 / DV_B)
    yb_ref[...] = ob * lax.rsqrt(ms + NORM_EPS) * gng_ref[...]
    xd = pd_ref[...]
    xc = (cb_ref[...] + c0_ref[...] * cw_ref[0:1, :] + c1_ref[...] * cw_ref[1:2, :]
          + c2_ref[...] * cw_ref[2:3, :] + xd * cw_ref[3:4, :])
    al, bterm = _lru_gates(xc, wa_ref[...], ba_ref[...], wx_ref[...], bx_ref[...], lam_ref[...])
    yd_ref[...] = al * h0_ref[...] + bterm


def _decode(pa, pb, pc, pd, shift0, swkv_all, sgla_all, kbuf, vbuf, c0, c1, c2, h0, prm, layer, earlier, bt):
    n = pa.shape[0]
    n_prev = layer
    half = KV_C * HD_C
    q4 = pc[:, 0:W_BR].reshape(n, H_C, HD_C)
    q8 = jnp.concatenate(
        [jnp.pad(q4[:, h:h + 1], ((0, 0), (0, 0), ((h // G_C) * HD_C, half - HD_C - (h // G_C) * HD_C)))
         for h in range(H_C)] + [jnp.zeros((n, 8 - H_C, half), f32)], axis=1)
    kvn = pc[:, W_BR:]
    pad8 = lambda t: jnp.pad(t.reshape(-1, 1), ((0, 8 - H_C), (0, 0)))
    sink8 = pad8(prm["sinks"])
    slope8 = pad8(jnp.asarray(ALIBI_SLOPES, f32))
    rows = lambda w: pl.BlockSpec((bt, w), lambda i: (i, 0))
    crow = lambda w: _const_spec((1, w))
    cube = lambda d1, d2: pl.BlockSpec((bt, d1, d2), lambda i: (i, 0, 0))
    wkv_dims, gla_dims, kv_dims = (H_A, HEAD_A, HEAD_A), (H_B, DK_B, DV_B), (WINDOW, half)
    of_layer = lambda dims: pl.BlockSpec((None, bt) + dims, lambda i: (layer, i) + (0,) * len(dims))
    stacked = lambda nl, dims: pl.BlockSpec((nl, bt) + dims, lambda i: (0, i) + (0,) * len(dims))
    state_dims = (wkv_dims, gla_dims, kv_dims, kv_dims)
    in_specs = [rows(A_PAD), rows(B_PAD), cube(8, half), rows(2 * half), rows(D_COLS), rows(A_PAD),
                of_layer(wkv_dims), of_layer(gla_dims), cube(WINDOW, half), cube(WINDOW, half),
                rows(W_BR), rows(W_BR), rows(W_BR), rows(W_BR),
                crow(A_PAD), crow(W_BR), _const_spec((LANE, W_BR)), crow(W_BR), _const_spec((LANE, W_BR)),
                crow(W_BR), crow(W_BR), crow(W_BR), crow(W_BR), crow(W_BR),
                _const_spec((LANE, H_B * DK_B)), crow(H_B * DK_B), crow(W_BR),
                _const_spec((8, 1)), _const_spec((8, 1)),
                _const_spec((CONV_W, W_BR)), crow(W_BR), _const_spec((W_BR, W_BR)), crow(W_BR),
                _const_spec((W_BR, W_BR)), crow(W_BR), crow(W_BR)]
    if n_prev:
        in_specs += [stacked(n_prev, dims) for dims in state_dims]
    out_specs = [rows(W_BR), rows(W_BR), cube(8, half), rows(W_BR)] + [stacked(n_prev + 1, d) for d in state_dims]
    out_shape = [jax.ShapeDtypeStruct((n, W_BR), f32), jax.ShapeDtypeStruct((n, W_BR), f32),
                 jax.ShapeDtypeStruct((n, 8, half), f32), jax.ShapeDtypeStruct((n, W_BR), f32)]
    out_shape += [jax.ShapeDtypeStruct((n_prev + 1, n) + dims, f32) for dims in state_dims]
    ya, yb, yc8, yd, wkv1, gla1, k1, v1 = pl.pallas_call(
        functools.partial(_decode_kernel, n_prev=n_prev),
        grid=(n // bt,),
        in_specs=in_specs,
        out_specs=out_specs,
        out_shape=out_shape,
        scratch_shapes=[pltpu.VMEM((bt, W_BR), f32), pltpu.VMEM((bt, W_BR), f32)],
        compiler_params=pltpu.CompilerParams(dimension_semantics=("arbitrary",),
                                             vmem_limit_bytes=VMEM_LIMIT),
        name="decode_mixers",
    )(pa, pb, q8, kvn, pd, shift0, swkv_all, sgla_all, kbuf, vbuf, c0, c1, c2, h0,
      prm["mu"], prm["w0"], prm["wup"], prm["a0"], prm["aup"], prm["kkw"], prm["kaw"], prm["rk"],
      prm["lng"], prm["lnb"], prm["gup"], prm["gbias"], prm["gng"], sink8, slope8,
      prm["cw"], prm["cb"], prm["lwa"], prm["lba"], prm["lwx"], prm["lbx"], prm["lam"],
      *(earlier if n_prev else ()))
    yc = jnp.concatenate([yc8[:, h, (h // G_C) * HD_C:(h // G_C + 1) * HD_C] for h in range(H_C)], axis=1)
    return ya, yb, yc, yd, wkv1, gla1, k1, v1


def _merge_kernel(x_ref, ya_ref, yb_ref, yc_ref, yd_ref, g_ref, w_ref, wbr_ref, wout_ref,
                  fg_ref, o_ref, *, final):
    x = x_ref[...]
    hn = _rms(x, g_ref[...]).astype(bf16)
    ys = (ya_ref, yb_ref, yc_ref, yd_ref)
    merged = None
    for n in range(N_BRANCH):
        z = jnp.dot(hn, w_ref[:, n * W_BR:(n + 1) * W_BR], preferred_element_type=f32)
        yz = ys[n][...] * (z * _sigmoid(z))
        br = jnp.dot(yz.astype(bf16), wbr_ref[n], preferred_element_type=f32)
        gate = _sigmoid(jnp.dot(hn, w_ref[:, Z_COLS + n * D_MODEL:Z_COLS + (n + 1) * D_MODEL],
                                preferred_element_type=f32))
        merged = gate * br if merged is None else merged + gate * br
    out = x + jnp.dot(merged.astype(bf16), wout_ref[...], preferred_element_type=f32)
    if final:
        out = _rms(out, fg_ref[...])
    o_ref[...] = out


def _merge(x, ya, yb, yc, yd, g, w_packed, wbr, wout, fg, layer, tm, final):
    m = x.shape[0]
    tile = lambda w: pl.BlockSpec((tm, w), lambda i: (i, 0))
    per_layer = lambda shape: pl.BlockSpec((None,) + shape, lambda *_: (layer,) + (0,) * len(shape),
                                           pipeline_mode=pl.Buffered(1))
    return pl.pallas_call(
        functools.partial(_merge_kernel, final=final),
        grid=(m // tm,),
        in_specs=[tile(D_MODEL), tile(W_BR), tile(W_BR), tile(W_BR), tile(W_BR), _const_spec((1, D_MODEL)),
                  per_layer((D_MODEL, GATE_BLOCK)), per_layer((N_BRANCH, W_BR, D_MODEL)),
                  per_layer((D_MODEL, D_MODEL)), _const_spec((1, D_MODEL))],
        out_specs=tile(D_MODEL),
        out_shape=jax.ShapeDtypeStruct((m, D_MODEL), f32),
        compiler_params=pltpu.CompilerParams(dimension_semantics=("arbitrary",),
                                             vmem_limit_bytes=VMEM_LIMIT),
        name="merge_final" if final else "merge",
    )(x, ya, yb, yc, yd, g, w_packed, wbr, wout, fg)


def _pad_cols(w, width):
    return jnp.pad(w, ((0, 0), (0, width - w.shape[1])))


def _pad_rows_at(w, start, total):
    return jnp.pad(w, ((start, total - start - w.shape[0]), (0, 0)))


def _block_diag(w):
    nb, bs, _ = w.shape
    out = jnp.zeros((nb * bs, nb * bs), w.dtype)
    for n in range(nb):
        out = out.at[n * bs:(n + 1) * bs, n * bs:(n + 1) * bs].set(w[n])
    return out


def _pack_w_in(w_in):
    o_b = A_COLS
    o_c = o_b + B_COLS
    o_d = o_c + C_COLS
    o_z = o_d + D_COLS
    o_g = o_z + Z_COLS
    zeros = lambda n: jnp.zeros(w_in.shape[:2] + (n,), w_in.dtype)
    parts = [w_in[..., o_z:o_g], w_in[..., o_g:],
             w_in[..., 0:o_b], zeros(A_PAD - A_COLS), w_in[..., o_b:o_c], zeros(B_PAD - B_COLS),
             w_in[..., o_c:o_d], w_in[..., o_d:o_z], zeros(MIX_BLOCK - sum(MIX_WIDTHS))]
    return jnp.concatenate(parts, axis=-1).astype(bf16)


def _layer_params(l, norm_g, w_in, mu_shift, w0, w_decay_up, a0, a_icl_up, k_k, k_a, r_k, ln_x_g, ln_x_b,
                  gla_gate_up, gla_gate_b, gla_norm_g, swa_sinks, lru_conv_w, lru_conv_b, lru_wa, lru_ba,
                  lru_wx, lru_bx, lru_lambda, w_branch, w_out):
    row = lambda t: t.reshape(1, -1)
    return dict(
        g=row(norm_g[l]),
        mu=_pad_cols(row(mu_shift[l]), A_PAD),
        w0=row(w0[l]),
        wup=_pad_rows_at(w_decay_up[l], 0, LANE),
        a0=row(a0[l]),
        aup=_pad_rows_at(a_icl_up[l], R_DECAY, LANE),
        kkw=row(k_k[l]), kaw=row(k_a[l]), rk=row(r_k[l]), lng=row(ln_x_g[l]), lnb=row(ln_x_b[l]),
        gup=_pad_rows_at(gla_gate_up[l], 0, LANE),
        gbias=row(gla_gate_b[l]),
        gng=row(jnp.tile(gla_norm_g[l], H_B)),
        sinks=row(swa_sinks[l]),
        cw=lru_conv_w[l], cb=row(lru_conv_b[l]),
        lwa=_block_diag(lru_wa[l]).astype(bf16), lba=row(lru_ba[l]),
        lwx=_block_diag(lru_wx[l]).astype(bf16), lbx=row(lru_bx[l]),
        lam=row(lru_lambda[l]),
    )


def kernel(x_prompt, x_sample, state_wkv, state_shift, state_gla, cache_swa_k, cache_swa_v, state_lru_conv, state_lru_h, norm_g, w_in, mu_shift, w0, w_decay_up, a0, a_icl_up, k_k, k_a, r_k, ln_x_g, ln_x_b, gla_gate_up, gla_gate_b, gla_norm_g, swa_sinks, lru_conv_w, lru_conv_b, lru_wa, lru_ba, lru_wx, lru_bx, lru_lambda, w_branch, w_out, final_norm_g):
    bp, lp, _ = x_prompt.shape
    bs = x_sample.shape[0]
    depth = w_in.shape[0]
    fg = final_norm_g.reshape(1, -1)
    xp = x_prompt.reshape(bp * lp, D_MODEL)
    xs = x_sample.reshape(bs, D_MODEL)
    outs_p = [[] for _ in range(7)]
    outs_s = [[] for _ in range(7)]
    w_packed = _pack_w_in(w_in)
    wbr_all = w_branch.astype(bf16)
    wout_all = w_out.astype(bf16)
    stacked_s = None
    for l in range(depth):
        prm = _layer_params(l, norm_g, w_in, mu_shift, w0, w_decay_up, a0, a_icl_up, k_k, k_a, r_k, ln_x_g,
                            ln_x_b, gla_gate_up, gla_gate_b, gla_norm_g, swa_sinks, lru_conv_w, lru_conv_b,
                            lru_wa, lru_ba, lru_wx, lru_bx, lru_lambda, w_branch, w_out)
        final = l == depth - 1
        pa, pb, pc, pd = _inproj(xp, prm["g"], w_packed, l, tm=512)
        pa3, pb3 = pa.reshape(bp, lp, A_PAD), pb.reshape(bp, lp, B_PAD)
        pc3, pd3 = pc.reshape(bp, lp, C_COLS), pd.reshape(bp, lp, D_COLS)
        ya, st_t = _rwkv_prompt(pa3, prm)
        yb, sgla = _gla_prompt(pb3, prm)
        yc = _swa_prompt(pc3, prm["sinks"])
        yd = _lru_prompt(pd3, prm)
        flat = lambda t: t.reshape(bp * lp, W_BR)
        xp = _merge(xp, flat(ya), flat(yb), flat(yc), flat(yd), prm["g"], w_packed, wbr_all, wout_all, fg, l,
                    tm=512, final=final)
        kv = pc3[:, lp - WINDOW:, H_C * HD_C:]
        outs_p[0].append(jnp.swapaxes(st_t, -1, -2))
        outs_p[1].append(pa3[:, lp - 1, :A_COLS])
        outs_p[2].append(sgla)
        outs_p[3].append(kv[:, :, :KV_C * HD_C].reshape(bp, WINDOW, KV_C, HD_C))
        outs_p[4].append(kv[:, :, KV_C * HD_C:].reshape(bp, WINDOW, KV_C, HD_C))
        outs_p[5].append(pd3[:, lp - (CONV_W - 1):, :])
        outs_p[6].append(yd[:, lp - 1, :])
        sa, sb, sc, sd = _inproj(xs, prm["g"], w_packed, l, tm=bs)
        kbuf = cache_swa_k[l].reshape(bs, WINDOW, KV_C * HD_C)
        vbuf = cache_swa_v[l].reshape(bs, WINDOW, KV_C * HD_C)
        conv0 = state_lru_conv[l]
        ya_s, yb_s, yc_s, yd_s, *stacked_s = _decode(
            sa, sb, sc, sd, _pad_cols(state_shift[l], A_PAD), state_wkv, state_gla, kbuf, vbuf,
            conv0[:, 0], conv0[:, 1], conv0[:, 2], state_lru_h[l], prm, l, stacked_s, bt=8)
        xs = _merge(xs, ya_s, yb_s, yc_s, yd_s, prm["g"], w_packed, wbr_all, wout_all, fg, l, tm=bs, final=final)
        outs_s[1].append(sa[:, :A_COLS])
        outs_s[5].append(jnp.stack([conv0[:, 1], conv0[:, 2], sd], axis=1))
        outs_s[6].append(yd_s)
    y_prompt = xp.reshape(bp, lp, D_MODEL)
    y_sample = xs.reshape(bs, 1, D_MODEL)
    sp = [jnp.stack(t) for t in outs_p]
    wkv_s, gla_s, k_s, v_s = stacked_s
    k_s = k_s.reshape(depth, bs, WINDOW, KV_C, HD_C)
    v_s = v_s.reshape(depth, bs, WINDOW, KV_C, HD_C)
    shift_s, conv_s, h_s = (jnp.stack(outs_s[i]) for i in (1, 5, 6))
    return (y_prompt, y_sample, sp[0], wkv_s, sp[1], shift_s, sp[2], gla_s, sp[3], k_s, sp[4], v_s,
            sp[5], conv_s, sp[6], h_s)
```

```python
import functools

import jax
import jax.numpy as jnp
from jax import lax
from jax.experimental import pallas as pl
from jax.experimental.pallas import tpu as pltpu

f32 = jnp.float32
bf16 = jnp.bfloat16
HI = lax.Precision.HIGHEST

D_MODEL = 1024
N_BRANCH = 4
W_BR = 256
HEAD_A = 64
H_A = 4
R_DECAY = 32
R_ICL = 32
GN_EPS_A = 64e-5
H_B = 4
DK_B = 32
DV_B = 64
R_GATE_B = 16
GLA_TAU = 16.0
GLA_CHUNK = 64
GLA_STEP_ROWS = 256
H_C = 4
KV_C = 2
HD_C = 64
G_C = 2
WINDOW = 128
SWA_STEP_BLOCKS = 4
CONV_W = 4
C_RG = 8.0
NORM_EPS = 1e-6

A_COLS = 3 * W_BR + R_DECAY + R_ICL
B_COLS = 2 * H_B * DK_B + W_BR + R_GATE_B
C_COLS = H_C * HD_C + 2 * KV_C * HD_C
D_COLS = W_BR
Z_COLS = N_BRANCH * W_BR
G_COLS = N_BRANCH * D_MODEL

LANE = 128
A_PAD = 7 * LANE
B_PAD = 5 * LANE
RWKV_CHUNK = 64
RWKV_STEP_ROWS = 256
RWKV_STEP_BATCH = 4
RWKV_PASSES = 1
GLA_PASSES = 1
DECODE_PASSES = 1
LRU_CHUNK = 256
VMEM_LIMIT = 56 * 1024 * 1024
NEG_BIG = -1e30

ALIBI_SLOPES = tuple(2.0 ** (-8.0 * (h + 1) / H_C) for h in range(H_C))


def _mm(a, b):
    return jnp.dot(a, b, precision=HI, preferred_element_type=f32)


def _mm_nt(a, b):
    return lax.dot_general(a, b, (((1,), (1,)), ((), ())), precision=HI, preferred_element_type=f32)


def _mm_tn(a, b):
    return lax.dot_general(a, b, (((0,), (0,)), ((), ())), precision=HI, preferred_element_type=f32)


def _bdot(a, b):
    return jnp.dot(a.astype(bf16), b.astype(bf16), preferred_element_type=f32)


def _bdot_nt(a, b):
    return lax.dot_general(a.astype(bf16), b.astype(bf16), (((1,), (1,)), ((), ())),
                           preferred_element_type=f32)


NN = ((1,), (0,))
NT = ((1,), (1,))
TN = ((0,), (0,))


def _dg(a, b, dims):
    return lax.dot_general(a, b, (dims, ((), ())), preferred_element_type=f32)


def _split_bf16(a):
    hi = a.astype(bf16)
    return hi, (a - hi.astype(f32)).astype(bf16)


def _dotp(a, b, dims, passes):
    if passes == 1:
        return _dg(a.astype(bf16), b.astype(bf16), dims)
    ah, al = _split_bf16(a)
    bh, bl = _split_bf16(b)
    return _dg(ah, bh, dims) + (_dg(ah, bl, dims) + _dg(al, bh, dims))


def _ones_dot(a, ones_bf16):
    ah, al = _split_bf16(a)
    return _dg(ah, ones_bf16, NN) + _dg(al, ones_bf16, NN)


def _cumsum_rows(x, seg, pos=None):
    if pos is None:
        pos = _iota(x.shape, 0) % seg
    d = 1
    while d < seg:
        x = x + jnp.where(pos >= d, pltpu.roll(x, d, 0), 0.0)
        d *= 2
    return x


def _iota(shape, dim):
    return lax.broadcasted_iota(jnp.int32, shape, dim)


def _eye(n):
    return (_iota((n, n), 0) == _iota((n, n), 1)).astype(f32)


def _block_ones(n, blk):
    return ((_iota((n, n), 0) // blk) == (_iota((n, n), 1) // blk)).astype(f32)


def _tril_ones(n):
    return (_iota((n, n), 0) >= _iota((n, n), 1)).astype(f32)


def _softplus(x):
    return jnp.maximum(x, 0.0) + jnp.log(1.0 + jnp.exp(-jnp.abs(x)))


def _log_sigmoid(x):
    return -_softplus(-x)


def _sigmoid(x):
    return 1.0 / (1.0 + jnp.exp(-x))


def _rms(x, g):
    return x * lax.rsqrt(jnp.mean(x * x, -1, keepdims=True) + NORM_EPS) * g


MIX_WIDTHS = (A_PAD, B_PAD, C_COLS, D_COLS)
MIX_BLOCK = 2560
GATE_BLOCK = Z_COLS + G_COLS
PACKED_COLS = GATE_BLOCK + MIX_BLOCK


def _inproj_kernel(x_ref, g_ref, w_ref, oa_ref, ob_ref, oc_ref, od_ref):
    hn = _rms(x_ref[...], g_ref[...]).astype(bf16)
    start = 0
    for o_ref, width in zip((oa_ref, ob_ref, oc_ref, od_ref), MIX_WIDTHS):
        o_ref[...] = jnp.dot(hn, w_ref[:, start:start + width], preferred_element_type=f32)
        start += width


def _const_spec(shape):
    return pl.BlockSpec(shape, lambda *_: (0,) * len(shape))


def _inproj(x, g, w_packed, layer, tm):
    m = x.shape[0]
    return pl.pallas_call(
        _inproj_kernel,
        grid=(m // tm,),
        in_specs=[pl.BlockSpec((tm, D_MODEL), lambda i: (i, 0)), _const_spec((1, D_MODEL)),
                  pl.BlockSpec((None, D_MODEL, MIX_BLOCK), lambda i: (layer, 0, GATE_BLOCK // MIX_BLOCK))],
        out_specs=[pl.BlockSpec((tm, w), lambda i: (i, 0)) for w in MIX_WIDTHS],
        out_shape=[jax.ShapeDtypeStruct((m, w), f32) for w in MIX_WIDTHS],
        compiler_params=pltpu.CompilerParams(dimension_semantics=("arbitrary",),
                                             vmem_limit_bytes=VMEM_LIMIT),
        name="inproj",
    )(x, g, w_packed)


def _head_ones():
    return _block_ones(W_BR, HEAD_A).astype(bf16)


def _rwkv_features(us, w0, wup, a0, aup, kk_w, ka_w, ones):
    r = us[:, 0:W_BR]
    k = us[:, W_BR:2 * W_BR]
    v = us[:, 2 * W_BR:3 * W_BR]
    lora = us[:, 3 * W_BR:A_PAD]
    w = -_softplus(-(w0 + _bdot(jnp.tanh(lora), wup))) - 0.5
    logdecay = -jnp.exp(w)
    a = _sigmoid(a0 + _bdot(lora, aup))
    kk = k * kk_w
    ss = _ones_dot(kk * kk, ones)
    kk = kk / jnp.maximum(jnp.sqrt(ss), 1e-12)
    kmod = k * (1.0 + (a - 1.0) * ka_w)
    return r, kmod, v, logdecay, kk, a


def _rwkv_finish(o, r, kmod, v, rk, lng, lnb, ones):
    mean = _ones_dot(o, ones) * (1.0 / HEAD_A)
    cen = o - mean
    var = _ones_dot(cen * cen, ones) * (1.0 / HEAD_A)
    o = cen * lax.rsqrt(var + GN_EPS_A) * lng + lnb
    bonus = _ones_dot(r * kmod * rk, ones) * v
    return o + bonus


def _rwkv_prompt_kernel(u_ref, mu_ref, w0_ref, wup_ref, a0_ref, aup_ref, kkw_ref, kaw_ref, rk_ref,
                        lng_ref, lnb_ref, y_ref, s_ref, st_scr, prev_scr, o_scr):
    step = pl.program_id(1)
    T = RWKV_CHUNK
    TT = RWKV_STEP_ROWS

    @pl.when(step == 0)
    def _():
        st_scr[...] = jnp.zeros_like(st_scr)
        prev_scr[...] = jnp.zeros_like(prev_scr)

    R = RWKV_STEP_BATCH
    P = RWKV_PASSES
    NC = TT // T
    pieces = [(c, h) for c in range(NC) for h in range(H_A)]
    row0 = _iota((TT, A_PAD), 0) == 0
    pos = _iota((TT, W_BR), 0) % T
    ones = _head_ones()

    def features(q):
        u = u_ref[q]
        u_prev = jnp.where(row0, prev_scr[q], pltpu.roll(u, 1, 0))
        prev_scr[q] = u[TT - 1:TT, :]
        us = u + (u_prev - u) * mu_ref[...]
        r, kmod, v, ld, kk, a = _rwkv_features(us, w0_ref[...], wup_ref[...], a0_ref[...], aup_ref[...],
                                               kkw_ref[...], kaw_ref[...], ones)
        cum = _cumsum_rows(ld, T, pos)
        cum_last = jnp.concatenate(
            [jnp.broadcast_to(cum[(c + 1) * T - 1:(c + 1) * T, :], (T, W_BR)) for c in range(NC)], axis=0)
        g_inv = jnp.exp(-cum)
        g_tail = jnp.exp(cum_last - cum)
        kka = kk * a
        return dict(r=r, kmod=kmod, v=v, cum=cum, at=-kk * jnp.exp(cum - ld), rt=r * jnp.exp(cum),
                    bt=kka * g_inv, kt=kmod * g_inv, btg=kka * g_tail, ktg=kmod * g_tail)

    ri = _iota((2 * T, 2 * T), 0)
    ci = _iota((2 * T, 2 * T), 1)
    ti = jnp.where(ri >= T, ri - T, ri)
    si = jnp.where(ci >= T, ci - T, ci)
    keep = (ti > si) | ((ri >= T) & (ti == si))
    eye_t = _eye(T)
    eye_h = _eye(HEAD_A)

    def independent_stages(f):
        d = dict(ar={}, vh={}, pm={}, x={}, pw={}, lv={}, gcol={}, bkg={}, xar={})

        def products():
            for c, h in pieces:
                rows = slice(c * T, (c + 1) * T)
                sl = slice(h * HEAD_A, (h + 1) * HEAD_A)
                d["vh"][c, h] = f["v"][rows, sl]
                d["ar"][c, h] = jnp.concatenate([f["at"][rows, sl], f["rt"][rows, sl]], axis=0)
                bk = jnp.concatenate([f["bt"][rows, sl], f["kt"][rows, sl]], axis=0)
                d["pm"][c, h] = jnp.where(keep, _dotp(d["ar"][c, h], bk, NT, P), 0.0)

        def squares_and_values():
            for c, h in pieces:
                rows = slice(c * T, (c + 1) * T)
                sl = slice(h * HEAD_A, (h + 1) * HEAD_A)
                lab = d["pm"][c, h][0:T, 0:T]
                d["x"][c, h] = eye_t + lab
                d["pw"][c, h] = _dotp(lab, lab, NN, P)
                d["lv"][c, h] = _dotp(d["pm"][c, h][:, T:2 * T], d["vh"][c, h], NN, P)
                g_last = jnp.exp(f["cum"][(c + 1) * T - 1:(c + 1) * T, sl])
                d["gcol"][c, h] = jnp.sum(eye_h * g_last, axis=1, keepdims=True)
                d["bkg"][c, h] = jnp.concatenate([f["btg"][rows, sl], f["ktg"][rows, sl]], axis=0)

        def inverse_round(last):
            def run():
                for c, h in pieces:
                    x_next = d["x"][c, h] + _dotp(d["pw"][c, h], d["x"][c, h], NN, P)
                    if not last:
                        d["pw"][c, h] = _dotp(d["pw"][c, h], d["pw"][c, h], NN, P)
                    d["x"][c, h] = x_next
            return run

        def fold_inverse():
            for c, h in pieces:
                xa = _dotp(d["x"][c, h], d["ar"][c, h][0:T], NN, P)
                xl = _dotp(d["x"][c, h], d["lv"][c, h][0:T], NN, P)
                d["xar"][c, h] = jnp.concatenate([xa, d["ar"][c, h][T:2 * T]], axis=0)
                d["lv"][c, h] = jnp.concatenate([xl, d["lv"][c, h][T:2 * T]], axis=0)

        stages = [products, squares_and_values] + [inverse_round(it == 4) for it in range(5)] + [fold_inverse]
        return stages, d

    def dependent_stages(q, f, d):
        st = {}
        base = {}

        def load():
            for h in range(H_A):
                st[h] = st_scr[q, h]

        def read(c):
            def run():
                if c == 0:
                    load()
                for h in range(H_A):
                    base[h] = _dotp(d["xar"][c, h], st[h], NN, P) + d["lv"][c, h]
            return run

        def update(c):
            def run():
                for h in range(H_A):
                    st[h] = d["gcol"][c, h] * st[h] + _dotp(
                        d["bkg"][c, h], jnp.concatenate([base[h][0:T], d["vh"][c, h]], axis=0), TN, P)
                for h in range(H_A):
                    o_scr[q, c * T:(c + 1) * T, h * HEAD_A:(h + 1) * HEAD_A] = (
                        base[h][T:2 * T] + _dotp(d["pm"][c, h][T:2 * T, 0:T], base[h][0:T], NN, P))
                if c == NC - 1:
                    for h in range(H_A):
                        st_scr[q, h] = st[h]
                    y_ref[q] = _rwkv_finish(o_scr[q], f["r"], f["kmod"], f["v"], rk_ref[...], lng_ref[...],
                                            lnb_ref[...], ones)
            return run

        return [stage for c in range(NC) for stage in (read(c), update(c))]

    feats = {q: features(q) for q in range(min(2, R))}
    pending = []
    for q in range(R):
        stages, d = independent_stages(feats[q])
        if q + 2 < R:
            feats[q + 2] = features(q + 2)
        for k in range(max(len(stages), len(pending))):
            if k < len(stages):
                stages[k]()
            if k < len(pending):
                pending[k]()
        pending = dependent_stages(q, feats[q], d)
    for stage in pending:
        stage()

    @pl.when(step == pl.num_programs(1) - 1)
    def _():
        s_ref[...] = st_scr[...]


def _rwkv_prompt(pa, prm):
    b, l, _ = pa.shape
    T = RWKV_STEP_ROWS
    R = RWKV_STEP_BATCH
    row = lambda w: _const_spec((1, w))
    return pl.pallas_call(
        _rwkv_prompt_kernel,
        grid=(b // R, l // T),
        in_specs=[pl.BlockSpec((R, T, A_PAD), lambda i, c: (i, c, 0)), row(A_PAD), row(W_BR),
                  _const_spec((LANE, W_BR)), row(W_BR), _const_spec((LANE, W_BR)),
                  row(W_BR), row(W_BR), row(W_BR), row(W_BR), row(W_BR)],
        out_specs=[pl.BlockSpec((R, T, W_BR), lambda i, c: (i, c, 0)),
                   pl.BlockSpec((R, H_A, HEAD_A, HEAD_A), lambda i, c: (i, 0, 0, 0))],
        out_shape=[jax.ShapeDtypeStruct((b, l, W_BR), f32),
                   jax.ShapeDtypeStruct((b, H_A, HEAD_A, HEAD_A), f32)],
        scratch_shapes=[pltpu.VMEM((R, H_A, HEAD_A, HEAD_A), f32), pltpu.VMEM((R, 1, A_PAD), f32),
                        pltpu.VMEM((R, T, W_BR), f32)],
        compiler_params=pltpu.CompilerParams(dimension_semantics=("arbitrary", "arbitrary"),
                                             vmem_limit_bytes=VMEM_LIMIT),
        name="rwkv_prompt",
    )(pa, prm["mu"], prm["w0"], prm["wup"], prm["a0"], prm["aup"], prm["kkw"], prm["kaw"], prm["rk"],
      prm["lng"], prm["lnb"])


def _gla_prompt_kernel(f_ref, up_ref, bias_ref, ng_ref, y_ref, s_ref, s_scr, o_scr):
    step = pl.program_id(1)
    T = GLA_CHUNK
    TT = GLA_STEP_ROWS
    NC = TT // T

    @pl.when(step == 0)
    def _():
        s_scr[...] = jnp.zeros_like(s_scr)

    f = f_ref[0]
    hk = H_B * DK_B
    q = f[:, 0:hk] * (DK_B ** -0.5)
    k = f[:, hk:2 * hk]
    v = f[:, 2 * hk:2 * hk + W_BR]
    gl = f[:, 2 * hk + W_BR:B_PAD]
    P = GLA_PASSES
    g = _log_sigmoid(_bdot(gl, up_ref[...]) + bias_ref[...]) * (1.0 / GLA_TAU)
    bcum = _cumsum_rows(g, T)
    b_last = jnp.concatenate(
        [jnp.broadcast_to(bcum[(c + 1) * T - 1:(c + 1) * T, :], (T, hk)) for c in range(NC)], axis=0)
    qe = q * jnp.exp(bcum)
    ke = k * jnp.exp(-bcum)
    kl = k * jnp.exp(b_last - bcum)
    causal = _iota((T, T), 0) >= _iota((T, T), 1)
    eye_k = _eye(DK_B)
    pieces = [(c, h) for c in range(NC) for h in range(H_B)]
    av, kv, ecol = {}, {}, {}
    for c, h in pieces:
        rows = slice(c * T, (c + 1) * T)
        ks = slice(h * DK_B, (h + 1) * DK_B)
        vs = slice(h * DV_B, (h + 1) * DV_B)
        att = jnp.where(causal, _dotp(qe[rows, ks], ke[rows, ks], NT, P), 0.0)
        av[c, h] = _dotp(att, v[rows, vs], NN, P)
        kv[c, h] = _dotp(kl[rows, ks], v[rows, vs], TN, P)
        e_last = jnp.exp(bcum[(c + 1) * T - 1:(c + 1) * T, ks])
        ecol[c, h] = jnp.sum(eye_k * e_last, axis=1, keepdims=True)
    s = [s_scr[h] for h in range(H_B)]
    for c in range(NC):
        rows = slice(c * T, (c + 1) * T)
        for h in range(H_B):
            ks = slice(h * DK_B, (h + 1) * DK_B)
            o_scr[rows, h * DV_B:(h + 1) * DV_B] = av[c, h] + _dotp(qe[rows, ks], s[h], NN, P)
            s[h] = ecol[c, h] * s[h] + kv[c, h]
    for h in range(H_B):
        s_scr[h] = s[h]
    o = o_scr[...]
    ms = _ones_dot(o * o, _block_ones(W_BR, DV_B).astype(bf16)) * (1.0 / DV_B)
    y_ref[0] = o * lax.rsqrt(ms + NORM_EPS) * ng_ref[...]

    @pl.when(step == pl.num_programs(1) - 1)
    def _():
        s_ref[0] = s_scr[...]


def _gla_prompt(pb, prm):
    b, l, _ = pb.shape
    T = GLA_STEP_ROWS
    return pl.pallas_call(
        _gla_prompt_kernel,
        grid=(b, l // T),
        in_specs=[pl.BlockSpec((1, T, B_PAD), lambda i, c: (i, c, 0)),
                  _const_spec((LANE, H_B * DK_B)), _const_spec((1, H_B * DK_B)), _const_spec((1, W_BR))],
        out_specs=[pl.BlockSpec((1, T, W_BR), lambda i, c: (i, c, 0)),
                   pl.BlockSpec((1, H_B, DK_B, DV_B), lambda i, c: (i, 0, 0, 0))],
        out_shape=[jax.ShapeDtypeStruct((b, l, W_BR), f32),
                   jax.ShapeDtypeStruct((b, H_B, DK_B, DV_B), f32)],
        scratch_shapes=[pltpu.VMEM((H_B, DK_B, DV_B), f32), pltpu.VMEM((T, W_BR), f32)],
        compiler_params=pltpu.CompilerParams(dimension_semantics=("arbitrary", "arbitrary"),
                                             vmem_limit_bytes=VMEM_LIMIT),
        name="gla_prompt",
    )(pb, prm["gup"], prm["gbias"], prm["gng"])


def _swa_prompt_kernel(cur_ref, prev_ref, sink_ref, y_ref):
    step = pl.program_id(1)
    W = WINDOW
    NB = SWA_STEP_BLOCKS
    qo, ko, vo = 0, H_C * HD_C, H_C * HD_C + KV_C * HD_C
    assert G_C == 2
    row = _iota((G_C * W, 2 * W), 0)
    s = _iota((G_C * W, 2 * W), 1)
    t = jnp.where(row >= W, row - W, row)
    dist = W + t - s
    ok = (dist >= 0) & (dist <= W)
    ok_first = ok & ((s >= W) | (step > 0))
    distf = dist.astype(f32)
    second = _iota((G_C * W, 1), 0) >= W
    scale = HD_C ** -0.5

    def band(col, j):
        if j == 0:
            return jnp.concatenate([prev_ref[0, :, col:col + HD_C], cur_ref[0, 0:W, col:col + HD_C]], axis=0)
        return cur_ref[0, (j - 1) * W:(j + 1) * W, col:col + HD_C]

    pieces = [(j, g) for j in range(NB) for g in range(KV_C)]
    scores, sinks = {}, {}
    for g in range(KV_C):
        h0, h1 = g * G_C, g * G_C + 1
        sinks[g] = jnp.where(second, sink_ref[:, h1:h1 + 1], sink_ref[:, h0:h0 + 1])
    for j, g in pieces:
        h0, h1 = g * G_C, g * G_C + 1
        q2 = jnp.concatenate([cur_ref[0, j * W:(j + 1) * W, qo + h0 * HD_C:qo + (h0 + 1) * HD_C],
                              cur_ref[0, j * W:(j + 1) * W, qo + h1 * HD_C:qo + (h1 + 1) * HD_C]], axis=0)
        slope = jnp.where(second, ALIBI_SLOPES[h1], ALIBI_SLOPES[h0])
        raw = _bdot_nt(q2, band(ko + g * HD_C, j)) * scale - slope * distf
        scores[j, g] = jnp.where(ok_first if j == 0 else ok, raw, NEG_BIG)
    probs, dens = {}, {}
    for j, g in pieces:
        m = jnp.maximum(jnp.max(scores[j, g], -1, keepdims=True), sinks[g])
        p = jnp.exp(scores[j, g] - m)
        probs[j, g] = p
        dens[j, g] = jnp.sum(p, -1, keepdims=True) + jnp.exp(sinks[g] - m)
    for j, g in pieces:
        out = _bdot(probs[j, g], band(vo + g * HD_C, j)) / dens[j, g]
        for jj in range(G_C):
            h = g * G_C + jj
            y_ref[0, j * W:(j + 1) * W, h * HD_C:(h + 1) * HD_C] = out[jj * W:(jj + 1) * W]


def _swa_prompt(pc, sinks):
    b, l, _ = pc.shape
    W = WINDOW
    NB = SWA_STEP_BLOCKS
    return pl.pallas_call(
        _swa_prompt_kernel,
        grid=(b, l // (NB * W)),
        in_specs=[pl.BlockSpec((1, NB * W, C_COLS), lambda i, c: (i, c, 0)),
                  pl.BlockSpec((1, W, C_COLS), lambda i, c: (i, jnp.maximum(NB * c - 1, 0), 0)),
                  _const_spec((1, H_C))],
        out_specs=pl.BlockSpec((1, NB * W, W_BR), lambda i, c: (i, c, 0)),
        out_shape=jax.ShapeDtypeStruct((b, l, W_BR), f32),
        compiler_params=pltpu.CompilerParams(dimension_semantics=("arbitrary", "arbitrary"),
                                             vmem_limit_bytes=VMEM_LIMIT),
        name="swa_prompt",
    )(pc, pc, sinks)


def _lru_gates(xc, wa, ba, wx, bx, lam):
    r = _sigmoid(_bdot(xc, wa) + ba)
    i = _sigmoid(_bdot(xc, wx) + bx)
    log_a = C_RG * r * _log_sigmoid(lam)
    a = jnp.exp(log_a)
    bterm = jnp.sqrt(1.0 - jnp.exp(2.0 * log_a)) * (i * xc)
    return a, bterm


def _lru_prompt_kernel(x_ref, cw_ref, cb_ref, wa_ref, ba_ref, wx_ref, bx_ref, lam_ref, y_ref,
                       xbuf_scr, h_scr):
    c = pl.program_id(1)
    T = LRU_CHUNK
    PADR = 8

    @pl.when(c == 0)
    def _():
        xbuf_scr[0:PADR, :] = jnp.zeros((PADR, W_BR), f32)
        h_scr[...] = jnp.zeros_like(h_scr)

    x = x_ref[0]
    xbuf_scr[PADR:PADR + T, :] = x
    xc = cb_ref[...] + x * cw_ref[CONV_W - 1:CONV_W, :]
    for j in range(1, CONV_W):
        xc = xc + xbuf_scr[PADR - j:PADR - j + T, :] * cw_ref[CONV_W - 1 - j:CONV_W - j, :]
    xbuf_scr[0:PADR, :] = x[T - PADR:T, :]
    a, bv = _lru_gates(xc, wa_ref[...], ba_ref[...], wx_ref[...], bx_ref[...], lam_ref[...])
    row = _iota((T, W_BR), 0)
    d = 1
    while d < T:
        keep = row >= d
        a_sh = jnp.where(keep, pltpu.roll(a, d, 0), 1.0)
        b_sh = jnp.where(keep, pltpu.roll(bv, d, 0), 0.0)
        bv = a * b_sh + bv
        a = a * a_sh
        d *= 2
    h = a * h_scr[...] + bv
    y_ref[0] = h
    h_scr[...] = h[T - 1:T, :]


def _lru_prompt(pd, prm):
    b, l, _ = pd.shape
    T = LRU_CHUNK
    row = lambda: _const_spec((1, W_BR))
    return pl.pallas_call(
        _lru_prompt_kernel,
        grid=(b, l // T),
        in_specs=[pl.BlockSpec((1, T, W_BR), lambda i, c: (i, c, 0)), _const_spec((CONV_W, W_BR)), row(),
                  _const_spec((W_BR, W_BR)), row(), _const_spec((W_BR, W_BR)), row(), row()],
        out_specs=pl.BlockSpec((1, T, W_BR), lambda i, c: (i, c, 0)),
        out_shape=jax.ShapeDtypeStruct((b, l, W_BR), f32),
        scratch_shapes=[pltpu.VMEM((T + 8, W_BR), f32), pltpu.VMEM((1, W_BR), f32)],
        compiler_params=pltpu.CompilerParams(dimension_semantics=("arbitrary", "arbitrary"),
                                             vmem_limit_bytes=VMEM_LIMIT),
        name="lru_prompt",
    )(pd, prm["cw"], prm["cb"], prm["lwa"], prm["lba"], prm["lwx"], prm["lbx"], prm["lam"])


def _colbcast(row, n_out, eye_bf16):
    c = row.shape[1]
    hi, lo = _split_bf16(row)
    return (_dg(eye_bf16, jnp.broadcast_to(hi, (n_out, c)), NT)
            + _dg(eye_bf16, jnp.broadcast_to(lo, (n_out, c)), NT))


def _decode_kernel(pa_ref, pb_ref, q8_ref, kvn_ref, pd_ref, shift_ref, swkv_ref, sgla_ref, kbuf_ref, vbuf_ref,
                   c0_ref, c1_ref, c2_ref, h0_ref,
                   mu_ref, w0_ref, wup_ref, a0_ref, aup_ref, kkw_ref, kaw_ref, rk_ref, lng_ref, lnb_ref,
                   gup_ref, gbias_ref, gng_ref, sink8_ref, slope8_ref,
                   cw_ref, cb_ref, wa_ref, ba_ref, wx_ref, bx_ref, lam_ref, *rest, n_prev):
    earlier, rest = rest[:4 if n_prev else 0], rest[4 if n_prev else 0:]
    ya_ref, yb_ref, yc8_ref, yd_ref, swkv_all, sgla_all, kout_all, vout_all, oa_scr, ob_scr = rest
    for src, dst in zip(earlier, (swkv_all, sgla_all, kout_all, vout_all)):
        dst[0:n_prev] = src[...]
    swkv_out, sgla_out = swkv_all.at[n_prev], sgla_all.at[n_prev]
    kout_ref, vout_ref = kout_all.at[n_prev], vout_all.at[n_prev]
    bt = pa_ref.shape[0]
    samples = range(bt)
    u = pa_ref[...]
    us = u + (shift_ref[...] - u) * mu_ref[...]
    ones = _head_ones()
    r, kmod, v, ld, kk, a = _rwkv_features(us, w0_ref[...], wup_ref[...], a0_ref[...], aup_ref[...],
                                           kkw_ref[...], kaw_ref[...], ones)
    decay = jnp.exp(ld)
    kka = kk * a
    fb = pb_ref[...]
    hk = H_B * DK_B
    qb = fb[:, 0:hk] * (DK_B ** -0.5)
    kb = fb[:, hk:2 * hk]
    vb = fb[:, 2 * hk:2 * hk + W_BR]
    gb = _log_sigmoid(_bdot(fb[:, 2 * hk + W_BR:B_PAD], gup_ref[...]) + gbias_ref[...]) * (1.0 / GLA_TAU)
    eg = jnp.exp(gb)
    eye_a = _eye(HEAD_A).astype(bf16)
    eye_b = _eye(DK_B).astype(bf16)
    row_of = lambda t, b: t[b:b + 1, :]

    v_col, eg_col, k_col, sa = {}, {}, {}, {}
    for b in samples:
        for h in range(H_A):
            sl = slice(h * HEAD_A, (h + 1) * HEAD_A)
            v_col[b, h] = _colbcast(row_of(v, b)[:, sl], HEAD_A, eye_a)
            kk_rows = jnp.broadcast_to(row_of(kk, b)[:, sl], (HEAD_A, HEAD_A))
            sa[b, h] = -_dotp(swkv_ref[b, h], kk_rows, NT, DECODE_PASSES)
        for h in range(H_B):
            ks = slice(h * DK_B, (h + 1) * DK_B)
            eg_col[b, h] = _colbcast(row_of(eg, b)[:, ks], DV_B, eye_b)
            k_col[b, h] = _colbcast(row_of(kb, b)[:, ks], DV_B, eye_b)
    for b in samples:
        for h in range(H_A):
            sl = slice(h * HEAD_A, (h + 1) * HEAD_A)
            s = swkv_ref[b, h]
            swkv_out[b, h] = (s * row_of(decay, b)[:, sl] + sa[b, h] * row_of(kka, b)[:, sl]
                              + v_col[b, h] * row_of(kmod, b)[:, sl])
        for h in range(H_B):
            vs = slice(h * DV_B, (h + 1) * DV_B)
            sgla_out[b, h] = eg_col[b, h] * sgla_ref[b, h] + k_col[b, h] * row_of(vb, b)[:, vs]
    for b in samples:
        for h in range(H_A):
            sl = slice(h * HEAD_A, (h + 1) * HEAD_A)
            r_rows = jnp.broadcast_to(row_of(r, b)[:, sl], (8, HEAD_A))
            oa_scr[b:b + 1, sl] = _dotp(r_rows, swkv_out[b, h], NT, DECODE_PASSES)[0:1, :]
        for h in range(H_B):
            ks = slice(h * DK_B, (h + 1) * DK_B)
            q_rows = jnp.broadcast_to(row_of(qb, b)[:, ks], (8, DK_B))
            ob_scr[b:b + 1, h * DV_B:(h + 1) * DV_B] = _dotp(q_rows, sgla_out[b, h], NN, DECODE_PASSES)[0:1, :]
    wdist = (WINDOW - _iota((1, WINDOW), 1)).astype(f32)
    scale = HD_C ** -0.5
    sink8 = sink8_ref[...]
    slope8 = slope8_ref[...]
    half = KV_C * HD_C
    kn = lambda b: kvn_ref[b:b + 1, 0:half]
    vn = lambda b: kvn_ref[b:b + 1, half:2 * half]
    scores = [_bdot_nt(q8_ref[b], kbuf_ref[b]) * scale - slope8 * wdist for b in samples]
    probs, tails = [], []
    for b in samples:
        sn = jnp.sum(q8_ref[b] * kn(b), axis=1, keepdims=True) * scale
        m = jnp.maximum(jnp.maximum(jnp.max(scores[b], axis=1, keepdims=True), sn), sink8)
        p = jnp.exp(scores[b] - m)
        pn = jnp.exp(sn - m)
        probs.append(p)
        tails.append((pn, jnp.sum(p, axis=1, keepdims=True) + pn + jnp.exp(sink8 - m)))
    for b in samples:
        pn, den = tails[b]
        yc8_ref[b] = (_bdot(probs[b], vbuf_ref[b]) + pn * vn(b)) / den
    for b in samples:
        kout_ref[b, 0:WINDOW - 1, :] = kbuf_ref[b, 1:WINDOW, :]
        kout_ref[b, WINDOW - 1:WINDOW, :] = kn(b)
        vout_ref[b, 0:WINDOW - 1, :] = vbuf_ref[b, 1:WINDOW, :]
        vout_ref[b, WINDOW - 1:WINDOW, :] = vn(b)

    ya_ref[...] = _rwkv_finish(oa_scr[...], r, kmod, v, rk_ref[...], lng_ref[...], lnb_ref[...], ones)
    ob = ob_scr[...]
    assert DV_B == HEAD_A
    ms = _ones_dot(ob * ob, ones) * (1.0 / DV_B)
    yb_ref[...] = ob * lax.rsqrt(ms + NORM_EPS) * gng_ref[...]
    xd = pd_ref[...]
    xc = (cb_ref[...] + c0_ref[...] * cw_ref[0:1, :] + c1_ref[...] * cw_ref[1:2, :]
          + c2_ref[...] * cw_ref[2:3, :] + xd * cw_ref[3:4, :])
    al, bterm = _lru_gates(xc, wa_ref[...], ba_ref[...], wx_ref[...], bx_ref[...], lam_ref[...])
    yd_ref[...] = al * h0_ref[...] + bterm


def _decode(pa, pb, pc, pd, shift0, swkv_all, sgla_all, kbuf, vbuf, c0, c1, c2, h0, prm, layer, earlier, bt):
    n = pa.shape[0]
    n_prev = layer
    half = KV_C * HD_C
    q4 = pc[:, 0:W_BR].reshape(n, H_C, HD_C)
    q8 = jnp.concatenate(
        [jnp.pad(q4[:, h:h + 1], ((0, 0), (0, 0), ((h // G_C) * HD_C, half - HD_C - (h // G_C) * HD_C)))
         for h in range(H_C)] + [jnp.zeros((n, 8 - H_C, half), f32)], axis=1)
    kvn = pc[:, W_BR:]
    pad8 = lambda t: jnp.pad(t.reshape(-1, 1), ((0, 8 - H_C), (0, 0)))
    sink8 = pad8(prm["sinks"])
    slope8 = pad8(jnp.asarray(ALIBI_SLOPES, f32))
    rows = lambda w: pl.BlockSpec((bt, w), lambda i: (i, 0))
    crow = lambda w: _const_spec((1, w))
    cube = lambda d1, d2: pl.BlockSpec((bt, d1, d2), lambda i: (i, 0, 0))
    wkv_dims, gla_dims, kv_dims = (H_A, HEAD_A, HEAD_A), (H_B, DK_B, DV_B), (WINDOW, half)
    of_layer = lambda dims: pl.BlockSpec((None, bt) + dims, lambda i: (layer, i) + (0,) * len(dims))
    stacked = lambda nl, dims: pl.BlockSpec((nl, bt) + dims, lambda i: (0, i) + (0,) * len(dims))
    state_dims = (wkv_dims, gla_dims, kv_dims, kv_dims)
    in_specs = [rows(A_PAD), rows(B_PAD), cube(8, half), rows(2 * half), rows(D_COLS), rows(A_PAD),
                of_layer(wkv_dims), of_layer(gla_dims), cube(WINDOW, half), cube(WINDOW, half),
                rows(W_BR), rows(W_BR), rows(W_BR), rows(W_BR),
                crow(A_PAD), crow(W_BR), _const_spec((LANE, W_BR)), crow(W_BR), _const_spec((LANE, W_BR)),
                crow(W_BR), crow(W_BR), crow(W_BR), crow(W_BR), crow(W_BR),
                _const_spec((LANE, H_B * DK_B)), crow(H_B * DK_B), crow(W_BR),
                _const_spec((8, 1)), _const_spec((8, 1)),
                _const_spec((CONV_W, W_BR)), crow(W_BR), _const_spec((W_BR, W_BR)), crow(W_BR),
                _const_spec((W_BR, W_BR)), crow(W_BR), crow(W_BR)]
    if n_prev:
        in_specs += [stacked(n_prev, dims) for dims in state_dims]
    out_specs = [rows(W_BR), rows(W_BR), cube(8, half), rows(W_BR)] + [stacked(n_prev + 1, d) for d in state_dims]
    out_shape = [jax.ShapeDtypeStruct((n, W_BR), f32), jax.ShapeDtypeStruct((n, W_BR), f32),
                 jax.ShapeDtypeStruct((n, 8, half), f32), jax.ShapeDtypeStruct((n, W_BR), f32)]
    out_shape += [jax.ShapeDtypeStruct((n_prev + 1, n) + dims, f32) for dims in state_dims]
    ya, yb, yc8, yd, wkv1, gla1, k1, v1 = pl.pallas_call(
        functools.partial(_decode_kernel, n_prev=n_prev),
        grid=(n // bt,),
        in_specs=in_specs,
        out_specs=out_specs,
        out_shape=out_shape,
        scratch_shapes=[pltpu.VMEM((bt, W_BR), f32), pltpu.VMEM((bt, W_BR), f32)],
        compiler_params=pltpu.CompilerParams(dimension_semantics=("arbitrary",),
                                             vmem_limit_bytes=VMEM_LIMIT),
        name="decode_mixers",
    )(pa, pb, q8, kvn, pd, shift0, swkv_all, sgla_all, kbuf, vbuf, c0, c1, c2, h0,
      prm["mu"], prm["w0"], prm["wup"], prm["a0"], prm["aup"], prm["kkw"], prm["kaw"], prm["rk"],
      prm["lng"], prm["lnb"], prm["gup"], prm["gbias"], prm["gng"], sink8, slope8,
      prm["cw"], prm["cb"], prm["lwa"], prm["lba"], prm["lwx"], prm["lbx"], prm["lam"],
      *(earlier if n_prev else ()))
    yc = jnp.concatenate([yc8[:, h, (h // G_C) * HD_C:(h // G_C + 1) * HD_C] for h in range(H_C)], axis=1)
    return ya, yb, yc, yd, wkv1, gla1, k1, v1


def _merge_kernel(x_ref, ya_ref, yb_ref, yc_ref, yd_ref, g_ref, w_ref, wbr_ref, wout_ref,
                  fg_ref, o_ref, *, final):
    x = x_ref[...]
    hn = _rms(x, g_ref[...]).astype(bf16)
    ys = (ya_ref, yb_ref, yc_ref, yd_ref)
    merged = None
    for n in range(N_BRANCH):
        z = jnp.dot(hn, w_ref[:, n * W_BR:(n + 1) * W_BR], preferred_element_type=f32)
        yz = ys[n][...] * (z * _sigmoid(z))
        br = jnp.dot(yz.astype(bf16), wbr_ref[n], preferred_element_type=f32)
        gate = _sigmoid(jnp.dot(hn, w_ref[:, Z_COLS + n * D_MODEL:Z_COLS + (n + 1) * D_MODEL],
                                preferred_element_type=f32))
        merged = gate * br if merged is None else merged + gate * br
    out = x + jnp.dot(merged.astype(bf16), wout_ref[...], preferred_element_type=f32)
    if final:
        out = _rms(out, fg_ref[...])
    o_ref[...] = out


def _merge(x, ya, yb, yc, yd, g, w_packed, wbr, wout, fg, layer, tm, final):
    m = x.shape[0]
    tile = lambda w: pl.BlockSpec((tm, w), lambda i: (i, 0))
    per_layer = lambda shape: pl.BlockSpec((None,) + shape, lambda *_: (layer,) + (0,) * len(shape),
                                           pipeline_mode=pl.Buffered(1))
    return pl.pallas_call(
        functools.partial(_merge_kernel, final=final),
        grid=(m // tm,),
        in_specs=[tile(D_MODEL), tile(W_BR), tile(W_BR), tile(W_BR), tile(W_BR), _const_spec((1, D_MODEL)),
                  per_layer((D_MODEL, GATE_BLOCK)), per_layer((N_BRANCH, W_BR, D_MODEL)),
                  per_layer((D_MODEL, D_MODEL)), _const_spec((1, D_MODEL))],
        out_specs=tile(D_MODEL),
        out_shape=jax.ShapeDtypeStruct((m, D_MODEL), f32),
        compiler_params=pltpu.CompilerParams(dimension_semantics=("arbitrary",),
                                             vmem_limit_bytes=VMEM_LIMIT),
        name="merge_final" if final else "merge",
    )(x, ya, yb, yc, yd, g, w_packed, wbr, wout, fg)


def _pad_cols(w, width):
    return jnp.pad(w, ((0, 0), (0, width - w.shape[1])))


def _pad_rows_at(w, start, total):
    return jnp.pad(w, ((start, total - start - w.shape[0]), (0, 0)))


def _block_diag(w):
    nb, bs, _ = w.shape
    out = jnp.zeros((nb * bs, nb * bs), w.dtype)
    for n in range(nb):
        out = out.at[n * bs:(n + 1) * bs, n * bs:(n + 1) * bs].set(w[n])
    return out


def _pack_layout():
    src = {"a": 0, "b": A_COLS, "c": A_COLS + B_COLS, "d": A_COLS + B_COLS + C_COLS}
    src["z"] = src["d"] + D_COLS
    src["g"] = src["z"] + Z_COLS
    moves, zeros, dst = [], [], 0
    for name, width, padded in (("z", Z_COLS, Z_COLS), ("g", G_COLS, G_COLS), ("a", A_COLS, A_PAD),
                                ("b", B_COLS, B_PAD), ("c", C_COLS, C_COLS), ("d", D_COLS, D_COLS)):
        moves.append((src[name], width, dst))
        if padded > width:
            zeros.append((dst + width, padded - width))
        dst += padded
    zeros.append((dst, PACKED_COLS - dst))
    return moves, zeros


def _pack_kernel(w_ref, o_ref):
    moves, zeros = _pack_layout()
    for s0, width, d0 in moves:
        o_ref[:, d0:d0 + width] = w_ref[:, s0:s0 + width].astype(bf16)
    for z0, width in zeros:
        o_ref[:, z0:z0 + width] = jnp.zeros((o_ref.shape[0], width), bf16)


def _pack_w_in(w_in):
    depth, d, cols = w_in.shape
    tr = 128
    return pl.pallas_call(
        _pack_kernel,
        grid=(depth, d // tr),
        in_specs=[pl.BlockSpec((None, tr, cols), lambda l, i: (l, i, 0))],
        out_specs=pl.BlockSpec((None, tr, PACKED_COLS), lambda l, i: (l, i, 0)),
        out_shape=jax.ShapeDtypeStruct((depth, d, PACKED_COLS), bf16),
        compiler_params=pltpu.CompilerParams(dimension_semantics=("arbitrary", "arbitrary"),
                                             vmem_limit_bytes=VMEM_LIMIT),
        name="pack_w_in",
    )(w_in)


def _layer_params(l, norm_g, w_in, mu_shift, w0, w_decay_up, a0, a_icl_up, k_k, k_a, r_k, ln_x_g, ln_x_b,
                  gla_gate_up, gla_gate_b, gla_norm_g, swa_sinks, lru_conv_w, lru_conv_b, lru_wa, lru_ba,
                  lru_wx, lru_bx, lru_lambda, w_branch, w_out):
    row = lambda t: t.reshape(1, -1)
    return dict(
        g=row(norm_g[l]),
        mu=_pad_cols(row(mu_shift[l]), A_PAD),
        w0=row(w0[l]),
        wup=_pad_rows_at(w_decay_up[l], 0, LANE),
        a0=row(a0[l]),
        aup=_pad_rows_at(a_icl_up[l], R_DECAY, LANE),
        kkw=row(k_k[l]), kaw=row(k_a[l]), rk=row(r_k[l]), lng=row(ln_x_g[l]), lnb=row(ln_x_b[l]),
        gup=_pad_rows_at(gla_gate_up[l], 0, LANE),
        gbias=row(gla_gate_b[l]),
        gng=row(jnp.tile(gla_norm_g[l], H_B)),
        sinks=row(swa_sinks[l]),
        cw=lru_conv_w[l], cb=row(lru_conv_b[l]),
        lwa=_block_diag(lru_wa[l]).astype(bf16), lba=row(lru_ba[l]),
        lwx=_block_diag(lru_wx[l]).astype(bf16), lbx=row(lru_bx[l]),
        lam=row(lru_lambda[l]),
    )


def kernel(x_prompt, x_sample, state_wkv, state_shift, state_gla, cache_swa_k, cache_swa_v, state_lru_conv, state_lru_h, norm_g, w_in, mu_shift, w0, w_decay_up, a0, a_icl_up, k_k, k_a, r_k, ln_x_g, ln_x_b, gla_gate_up, gla_gate_b, gla_norm_g, swa_sinks, lru_conv_w, lru_conv_b, lru_wa, lru_ba, lru_wx, lru_bx, lru_lambda, w_branch, w_out, final_norm_g):
    bp, lp, _ = x_prompt.shape
    bs = x_sample.shape[0]
    depth = w_in.shape[0]
    fg = final_norm_g.reshape(1, -1)
    xp = x_prompt.reshape(bp * lp, D_MODEL)
    xs = x_sample.reshape(bs, D_MODEL)
    outs_p = [[] for _ in range(7)]
    outs_s = [[] for _ in range(7)]
    w_packed = _pack_w_in(w_in)
    wbr_all = w_branch.astype(bf16)
    wout_all = w_out.astype(bf16)
    stacked_s = None
    for l in range(depth):
        prm = _layer_params(l, norm_g, w_in, mu_shift, w0, w_decay_up, a0, a_icl_up, k_k, k_a, r_k, ln_x_g,
                            ln_x_b, gla_gate_up, gla_gate_b, gla_norm_g, swa_sinks, lru_conv_w, lru_conv_b,
                            lru_wa, lru_ba, lru_wx, lru_bx, lru_lambda, w_branch, w_out)
        final = l == depth - 1
        pa, pb, pc, pd = _inproj(xp, prm["g"], w_packed, l, tm=512)
        pa3, pb3 = pa.reshape(bp, lp, A_PAD), pb.reshape(bp, lp, B_PAD)
        pc3, pd3 = pc.reshape(bp, lp, C_COLS), pd.reshape(bp, lp, D_COLS)
        ya, st_t = _rwkv_prompt(pa3, prm)
        yb, sgla = _gla_prompt(pb3, prm)
        yc = _swa_prompt(pc3, prm["sinks"])
        yd = _lru_prompt(pd3, prm)
        flat = lambda t: t.reshape(bp * lp, W_BR)
        xp = _merge(xp, flat(ya), flat(yb), flat(yc), flat(yd), prm["g"], w_packed, wbr_all, wout_all, fg, l,
                    tm=512, final=final)
        kv = pc3[:, lp - WINDOW:, H_C * HD_C:]
        outs_p[0].append(jnp.swapaxes(st_t, -1, -2))
        outs_p[1].append(pa3[:, lp - 1, :A_COLS])
        outs_p[2].append(sgla)
        outs_p[3].append(kv[:, :, :KV_C * HD_C].reshape(bp, WINDOW, KV_C, HD_C))
        outs_p[4].append(kv[:, :, KV_C * HD_C:].reshape(bp, WINDOW, KV_C, HD_C))
        outs_p[5].append(pd3[:, lp - (CONV_W - 1):, :])
        outs_p[6].append(yd[:, lp - 1, :])
        sa, sb, sc, sd = _inproj(xs, prm["g"], w_packed, l, tm=bs)
        kbuf = cache_swa_k[l].reshape(bs, WINDOW, KV_C * HD_C)
        vbuf = cache_swa_v[l].reshape(bs, WINDOW, KV_C * HD_C)
        conv0 = state_lru_conv[l]
        ya_s, yb_s, yc_s, yd_s, *stacked_s = _decode(
            sa, sb, sc, sd, _pad_cols(state_shift[l], A_PAD), state_wkv, state_gla, kbuf, vbuf,
            conv0[:, 0], conv0[:, 1], conv0[:, 2], state_lru_h[l], prm, l, stacked_s, bt=8)
        xs = _merge(xs, ya_s, yb_s, yc_s, yd_s, prm["g"], w_packed, wbr_all, wout_all, fg, l, tm=bs, final=final)
        outs_s[1].append(sa[:, :A_COLS])
        outs_s[5].append(jnp.stack([conv0[:, 1], conv0[:, 2], sd], axis=1))
        outs_s[6].append(yd_s)
    y_prompt = xp.reshape(bp, lp, D_MODEL)
    y_sample = xs.reshape(bs, 1, D_MODEL)
    sp = [jnp.stack(t) for t in outs_p]
    wkv_s, gla_s, k_s, v_s = stacked_s
    k_s = k_s.reshape(depth, bs, WINDOW, KV_C, HD_C)
    v_s = v_s.reshape(depth, bs, WINDOW, KV_C, HD_C)
    shift_s, conv_s, h_s = (jnp.stack(outs_s[i]) for i in (1, 5, 6))
    return (y_prompt, y_sample, sp[0], wkv_s, sp[1], shift_s, sp[2], gla_s, sp[3], k_s, sp[4], v_s,
            sp[5], conv_s, sp[6], h_s)
```

```python
import functools

import jax
import jax.numpy as jnp
from jax import lax
from jax.experimental import pallas as pl
from jax.experimental.pallas import tpu as pltpu

f32 = jnp.float32
bf16 = jnp.bfloat16
HI = lax.Precision.HIGHEST

D_MODEL = 1024
N_BRANCH = 4
W_BR = 256
HEAD_A = 64
H_A = 4
R_DECAY = 32
R_ICL = 32
GN_EPS_A = 64e-5
H_B = 4
DK_B = 32
DV_B = 64
R_GATE_B = 16
GLA_TAU = 16.0
GLA_CHUNK = 64
GLA_STEP_ROWS = 256
H_C = 4
KV_C = 2
HD_C = 64
G_C = 2
WINDOW = 128
SWA_STEP_BLOCKS = 4
CONV_W = 4
C_RG = 8.0
NORM_EPS = 1e-6

A_COLS = 3 * W_BR + R_DECAY + R_ICL
B_COLS = 2 * H_B * DK_B + W_BR + R_GATE_B
C_COLS = H_C * HD_C + 2 * KV_C * HD_C
D_COLS = W_BR
Z_COLS = N_BRANCH * W_BR
G_COLS = N_BRANCH * D_MODEL

LANE = 128
A_PAD = 7 * LANE
B_PAD = 5 * LANE
RWKV_CHUNK = 64
RWKV_STEP_ROWS = 256
RWKV_STEP_BATCH = 4
RWKV_PASSES = 1
GLA_PASSES = 1
DECODE_PASSES = 1
LRU_CHUNK = 256
VMEM_LIMIT = 56 * 1024 * 1024
NEG_BIG = -1e30

ALIBI_SLOPES = tuple(2.0 ** (-8.0 * (h + 1) / H_C) for h in range(H_C))


def _mm(a, b):
    return jnp.dot(a, b, precision=HI, preferred_element_type=f32)


def _mm_nt(a, b):
    return lax.dot_general(a, b, (((1,), (1,)), ((), ())), precision=HI, preferred_element_type=f32)


def _mm_tn(a, b):
    return lax.dot_general(a, b, (((0,), (0,)), ((), ())), precision=HI, preferred_element_type=f32)


def _bdot(a, b):
    return jnp.dot(a.astype(bf16), b.astype(bf16), preferred_element_type=f32)


def _bdot_nt(a, b):
    return lax.dot_general(a.astype(bf16), b.astype(bf16), (((1,), (1,)), ((), ())),
                           preferred_element_type=f32)


NN = ((1,), (0,))
NT = ((1,), (1,))
TN = ((0,), (0,))


def _dg(a, b, dims):
    return lax.dot_general(a, b, (dims, ((), ())), preferred_element_type=f32)


def _split_bf16(a):
    hi = a.astype(bf16)
    return hi, (a - hi.astype(f32)).astype(bf16)


def _dotp(a, b, dims, passes):
    if passes == 1:
        return _dg(a.astype(bf16), b.astype(bf16), dims)
    ah, al = _split_bf16(a)
    bh, bl = _split_bf16(b)
    return _dg(ah, bh, dims) + (_dg(ah, bl, dims) + _dg(al, bh, dims))


def _ones_dot(a, ones_bf16):
    ah, al = _split_bf16(a)
    return _dg(ah, ones_bf16, NN) + _dg(al, ones_bf16, NN)


def _cumsum_rows(x, seg, pos=None):
    if pos is None:
        pos = _iota(x.shape, 0) % seg
    d = 1
    while d < seg:
        x = x + jnp.where(pos >= d, pltpu.roll(x, d, 0), 0.0)
        d *= 2
    return x


def _iota(shape, dim):
    return lax.broadcasted_iota(jnp.int32, shape, dim)


def _eye(n):
    return (_iota((n, n), 0) == _iota((n, n), 1)).astype(f32)


def _block_ones(n, blk):
    return ((_iota((n, n), 0) // blk) == (_iota((n, n), 1) // blk)).astype(f32)


def _tril_ones(n):
    return (_iota((n, n), 0) >= _iota((n, n), 1)).astype(f32)


def _softplus(x):
    return jnp.maximum(x, 0.0) + jnp.log(1.0 + jnp.exp(-jnp.abs(x)))


def _log_sigmoid(x):
    return -_softplus(-x)


def _sigmoid(x):
    return 1.0 / (1.0 + jnp.exp(-x))


def _rms(x, g):
    return x * lax.rsqrt(jnp.mean(x * x, -1, keepdims=True) + NORM_EPS) * g


MIX_COLS = (A_COLS, B_COLS, C_COLS, D_COLS)
MIX_WIDTHS = (A_PAD, B_PAD, C_COLS, D_COLS)
MIX_ROWS = sum(MIX_COLS)
GATE_ROWS = Z_COLS + G_COLS


def _wt_spec(layer, start, rows, **kw):
    return pl.BlockSpec((pl.Element(1), pl.Element(rows), pl.Element(D_MODEL)), lambda *_: (layer, start, 0), **kw)


def _inproj_kernel(x_ref, g_ref, wt_ref, oa_ref, ob_ref, oc_ref, od_ref):
    hn = _rms(x_ref[...], g_ref[...]).astype(bf16)
    start = 0
    for o_ref, cols, width in zip((oa_ref, ob_ref, oc_ref, od_ref), MIX_COLS, MIX_WIDTHS):
        o_ref[:, 0:cols] = _dg(hn, wt_ref[0, start:start + cols, :], NT)
        if width > cols:
            o_ref[:, cols:width] = jnp.zeros((o_ref.shape[0], width - cols), f32)
        start += cols


def _const_spec(shape):
    return pl.BlockSpec(shape, lambda *_: (0,) * len(shape))


def _inproj(x, g, wt, layer, tm):
    m = x.shape[0]
    return pl.pallas_call(
        _inproj_kernel,
        grid=(m // tm,),
        in_specs=[pl.BlockSpec((tm, D_MODEL), lambda i: (i, 0)), _lspec(g, layer), _wt_spec(layer, 0, MIX_ROWS)],
        out_specs=[pl.BlockSpec((tm, w), lambda i: (i, 0)) for w in MIX_WIDTHS],
        out_shape=[jax.ShapeDtypeStruct((m, w), f32) for w in MIX_WIDTHS],
        compiler_params=pltpu.CompilerParams(dimension_semantics=("arbitrary",),
                                             vmem_limit_bytes=VMEM_LIMIT),
        name="inproj",
    )(x, g, wt)


def _head_ones():
    return _block_ones(W_BR, HEAD_A).astype(bf16)


def _rwkv_features(us, w0, wup, a0, aup, kk_w, ka_w, ones):
    r = us[:, 0:W_BR]
    k = us[:, W_BR:2 * W_BR]
    v = us[:, 2 * W_BR:3 * W_BR]
    lora = us[:, 3 * W_BR:A_PAD]
    w = -_softplus(-(w0 + _bdot(jnp.tanh(lora), wup))) - 0.5
    logdecay = -jnp.exp(w)
    a = _sigmoid(a0 + _bdot(lora, aup))
    kk = k * kk_w
    ss = _ones_dot(kk * kk, ones)
    kk = kk / jnp.maximum(jnp.sqrt(ss), 1e-12)
    kmod = k * (1.0 + (a - 1.0) * ka_w)
    return r, kmod, v, logdecay, kk, a


def _rwkv_finish(o, r, kmod, v, rk, lng, lnb, ones):
    mean = _ones_dot(o, ones) * (1.0 / HEAD_A)
    cen = o - mean
    var = _ones_dot(cen * cen, ones) * (1.0 / HEAD_A)
    o = cen * lax.rsqrt(var + GN_EPS_A) * lng + lnb
    bonus = _ones_dot(r * kmod * rk, ones) * v
    return o + bonus


def _rwkv_prompt_kernel(u_ref, mu_ref, w0_ref, wup_ref, a0_ref, aup_ref, kkw_ref, kaw_ref, rk_ref,
                        lng_ref, lnb_ref, y_ref, s_ref, st_scr, prev_scr, o_scr):
    step = pl.program_id(1)
    T = RWKV_CHUNK
    TT = RWKV_STEP_ROWS

    @pl.when(step == 0)
    def _():
        st_scr[...] = jnp.zeros_like(st_scr)
        prev_scr[...] = jnp.zeros_like(prev_scr)

    R = RWKV_STEP_BATCH
    P = RWKV_PASSES
    NC = TT // T
    pieces = [(c, h) for c in range(NC) for h in range(H_A)]
    row0 = _iota((TT, A_PAD), 0) == 0
    pos = _iota((TT, W_BR), 0) % T
    ones = _head_ones()

    def features(q):
        u = u_ref[q]
        u_prev = jnp.where(row0, prev_scr[q], pltpu.roll(u, 1, 0))
        prev_scr[q] = u[TT - 1:TT, :]
        us = u + (u_prev - u) * mu_ref[...]
        r, kmod, v, ld, kk, a = _rwkv_features(us, w0_ref[...], wup_ref[...], a0_ref[...], aup_ref[...],
                                               kkw_ref[...], kaw_ref[...], ones)
        cum = _cumsum_rows(ld, T, pos)
        cum_last = jnp.concatenate(
            [jnp.broadcast_to(cum[(c + 1) * T - 1:(c + 1) * T, :], (T, W_BR)) for c in range(NC)], axis=0)
        g_inv = jnp.exp(-cum)
        g_tail = jnp.exp(cum_last - cum)
        kka = kk * a
        return dict(r=r, kmod=kmod, v=v, cum=cum, at=-kk * jnp.exp(cum - ld), rt=r * jnp.exp(cum),
                    bt=kka * g_inv, kt=kmod * g_inv, btg=kka * g_tail, ktg=kmod * g_tail)

    ri = _iota((2 * T, 2 * T), 0)
    ci = _iota((2 * T, 2 * T), 1)
    ti = jnp.where(ri >= T, ri - T, ri)
    si = jnp.where(ci >= T, ci - T, ci)
    keep = (ti > si) | ((ri >= T) & (ti == si))
    eye_t = _eye(T)
    eye_h = _eye(HEAD_A)

    def independent_stages(f):
        d = dict(ar={}, vh={}, pm={}, x={}, pw={}, lv={}, gcol={}, bkg={}, xar={})

        def products():
            for c, h in pieces:
                rows = slice(c * T, (c + 1) * T)
                sl = slice(h * HEAD_A, (h + 1) * HEAD_A)
                d["vh"][c, h] = f["v"][rows, sl]
                d["ar"][c, h] = jnp.concatenate([f["at"][rows, sl], f["rt"][rows, sl]], axis=0)
                bk = jnp.concatenate([f["bt"][rows, sl], f["kt"][rows, sl]], axis=0)
                d["pm"][c, h] = jnp.where(keep, _dotp(d["ar"][c, h], bk, NT, P), 0.0)

        def squares_and_values():
            for c, h in pieces:
                rows = slice(c * T, (c + 1) * T)
                sl = slice(h * HEAD_A, (h + 1) * HEAD_A)
                lab = d["pm"][c, h][0:T, 0:T]
                d["x"][c, h] = eye_t + lab
                d["pw"][c, h] = _dotp(lab, lab, NN, P)
                d["lv"][c, h] = _dotp(d["pm"][c, h][:, T:2 * T], d["vh"][c, h], NN, P)
                g_last = jnp.exp(f["cum"][(c + 1) * T - 1:(c + 1) * T, sl])
                d["gcol"][c, h] = jnp.sum(eye_h * g_last, axis=1, keepdims=True)
                d["bkg"][c, h] = jnp.concatenate([f["btg"][rows, sl], f["ktg"][rows, sl]], axis=0)

        def inverse_round(last):
            def run():
                for c, h in pieces:
                    x_next = d["x"][c, h] + _dotp(d["pw"][c, h], d["x"][c, h], NN, P)
                    if not last:
                        d["pw"][c, h] = _dotp(d["pw"][c, h], d["pw"][c, h], NN, P)
                    d["x"][c, h] = x_next
            return run

        def fold_inverse():
            for c, h in pieces:
                xa = _dotp(d["x"][c, h], d["ar"][c, h][0:T], NN, P)
                xl = _dotp(d["x"][c, h], d["lv"][c, h][0:T], NN, P)
                d["xar"][c, h] = jnp.concatenate([xa, d["ar"][c, h][T:2 * T]], axis=0)
                d["lv"][c, h] = jnp.concatenate([xl, d["lv"][c, h][T:2 * T]], axis=0)

        stages = [products, squares_and_values] + [inverse_round(it == 4) for it in range(5)] + [fold_inverse]
        return stages, d

    def dependent_stages(q, f, d):
        st = {}
        base = {}

        def load():
            for h in range(H_A):
                st[h] = st_scr[q, h]

        def read(c):
            def run():
                if c == 0:
                    load()
                for h in range(H_A):
                    base[h] = _dotp(d["xar"][c, h], st[h], NN, P) + d["lv"][c, h]
            return run

        def update(c):
            def run():
                for h in range(H_A):
                    st[h] = d["gcol"][c, h] * st[h] + _dotp(
                        d["bkg"][c, h], jnp.concatenate([base[h][0:T], d["vh"][c, h]], axis=0), TN, P)
                for h in range(H_A):
                    o_scr[q, c * T:(c + 1) * T, h * HEAD_A:(h + 1) * HEAD_A] = (
                        base[h][T:2 * T] + _dotp(d["pm"][c, h][T:2 * T, 0:T], base[h][0:T], NN, P))
                if c == NC - 1:
                    for h in range(H_A):
                        st_scr[q, h] = st[h]
                    y_ref[q] = _rwkv_finish(o_scr[q], f["r"], f["kmod"], f["v"], rk_ref[...], lng_ref[...],
                                            lnb_ref[...], ones)
            return run

        return [stage for c in range(NC) for stage in (read(c), update(c))]

    feats = {q: features(q) for q in range(min(2, R))}
    pending = []
    for q in range(R):
        stages, d = independent_stages(feats[q])
        if q + 2 < R:
            feats[q + 2] = features(q + 2)
        for k in range(max(len(stages), len(pending))):
            if k < len(stages):
                stages[k]()
            if k < len(pending):
                pending[k]()
        pending = dependent_stages(q, feats[q], d)
    for stage in pending:
        stage()

    @pl.when(step == pl.num_programs(1) - 1)
    def _():
        s_ref[...] = st_scr[...]


RWKV_PARAMS = ("mu", "w0", "wup", "a0", "aup", "kkw", "kaw", "rk", "lng", "lnb")
GLA_PARAMS = ("gup", "gbias", "gng")
LRU_PARAMS = ("cw", "cb", "lwa", "lba", "lwx", "lbx", "lam")


def _rwkv_prompt(pa, prm, layer):
    b, l, _ = pa.shape
    T = RWKV_STEP_ROWS
    R = RWKV_STEP_BATCH
    return pl.pallas_call(
        _rwkv_prompt_kernel,
        grid=(b // R, l // T),
        in_specs=[pl.BlockSpec((R, T, A_PAD), lambda i, c: (i, c, 0))]
        + [_lspec(prm[k], layer) for k in RWKV_PARAMS],
        out_specs=[pl.BlockSpec((R, T, W_BR), lambda i, c: (i, c, 0)),
                   pl.BlockSpec((R, H_A, HEAD_A, HEAD_A), lambda i, c: (i, 0, 0, 0))],
        out_shape=[jax.ShapeDtypeStruct((b, l, W_BR), f32),
                   jax.ShapeDtypeStruct((b, H_A, HEAD_A, HEAD_A), f32)],
        scratch_shapes=[pltpu.VMEM((R, H_A, HEAD_A, HEAD_A), f32), pltpu.VMEM((R, 1, A_PAD), f32),
                        pltpu.VMEM((R, T, W_BR), f32)],
        compiler_params=pltpu.CompilerParams(dimension_semantics=("arbitrary", "arbitrary"),
                                             vmem_limit_bytes=VMEM_LIMIT),
        name="rwkv_prompt",
    )(pa, *[prm[k] for k in RWKV_PARAMS])


def _gla_prompt_kernel(f_ref, up_ref, bias_ref, ng_ref, y_ref, s_ref, s_scr, o_scr):
    step = pl.program_id(1)
    T = GLA_CHUNK
    TT = GLA_STEP_ROWS
    NC = TT // T

    @pl.when(step == 0)
    def _():
        s_scr[...] = jnp.zeros_like(s_scr)

    f = f_ref[0]
    hk = H_B * DK_B
    q = f[:, 0:hk] * (DK_B ** -0.5)
    k = f[:, hk:2 * hk]
    v = f[:, 2 * hk:2 * hk + W_BR]
    gl = f[:, 2 * hk + W_BR:B_PAD]
    P = GLA_PASSES
    g = _log_sigmoid(_bdot(gl, up_ref[...]) + bias_ref[...]) * (1.0 / GLA_TAU)
    bcum = _cumsum_rows(g, T)
    b_last = jnp.concatenate(
        [jnp.broadcast_to(bcum[(c + 1) * T - 1:(c + 1) * T, :], (T, hk)) for c in range(NC)], axis=0)
    qe = q * jnp.exp(bcum)
    ke = k * jnp.exp(-bcum)
    kl = k * jnp.exp(b_last - bcum)
    causal = _iota((T, T), 0) >= _iota((T, T), 1)
    eye_k = _eye(DK_B)
    pieces = [(c, h) for c in range(NC) for h in range(H_B)]
    av, kv, ecol = {}, {}, {}
    for c, h in pieces:
        rows = slice(c * T, (c + 1) * T)
        ks = slice(h * DK_B, (h + 1) * DK_B)
        vs = slice(h * DV_B, (h + 1) * DV_B)
        att = jnp.where(causal, _dotp(qe[rows, ks], ke[rows, ks], NT, P), 0.0)
        av[c, h] = _dotp(att, v[rows, vs], NN, P)
        kv[c, h] = _dotp(kl[rows, ks], v[rows, vs], TN, P)
        e_last = jnp.exp(bcum[(c + 1) * T - 1:(c + 1) * T, ks])
        ecol[c, h] = jnp.sum(eye_k * e_last, axis=1, keepdims=True)
    s = [s_scr[h] for h in range(H_B)]
    for c in range(NC):
        rows = slice(c * T, (c + 1) * T)
        for h in range(H_B):
            ks = slice(h * DK_B, (h + 1) * DK_B)
            o_scr[rows, h * DV_B:(h + 1) * DV_B] = av[c, h] + _dotp(qe[rows, ks], s[h], NN, P)
            s[h] = ecol[c, h] * s[h] + kv[c, h]
    for h in range(H_B):
        s_scr[h] = s[h]
    o = o_scr[...]
    ms = _ones_dot(o * o, _block_ones(W_BR, DV_B).astype(bf16)) * (1.0 / DV_B)
    y_ref[0] = o * lax.rsqrt(ms + NORM_EPS) * ng_ref[...]

    @pl.when(step == pl.num_programs(1) - 1)
    def _():
        s_ref[0] = s_scr[...]


def _gla_prompt(pb, prm, layer):
    b, l, _ = pb.shape
    T = GLA_STEP_ROWS
    return pl.pallas_call(
        _gla_prompt_kernel,
        grid=(b, l // T),
        in_specs=[pl.BlockSpec((1, T, B_PAD), lambda i, c: (i, c, 0))]
        + [_lspec(prm[k], layer) for k in GLA_PARAMS],
        out_specs=[pl.BlockSpec((1, T, W_BR), lambda i, c: (i, c, 0)),
                   pl.BlockSpec((1, H_B, DK_B, DV_B), lambda i, c: (i, 0, 0, 0))],
        out_shape=[jax.ShapeDtypeStruct((b, l, W_BR), f32),
                   jax.ShapeDtypeStruct((b, H_B, DK_B, DV_B), f32)],
        scratch_shapes=[pltpu.VMEM((H_B, DK_B, DV_B), f32), pltpu.VMEM((T, W_BR), f32)],
        compiler_params=pltpu.CompilerParams(dimension_semantics=("arbitrary", "arbitrary"),
                                             vmem_limit_bytes=VMEM_LIMIT),
        name="gla_prompt",
    )(pb, *[prm[k] for k in GLA_PARAMS])


def _swa_prompt_kernel(cur_ref, prev_ref, sink_ref, y_ref):
    step = pl.program_id(1)
    W = WINDOW
    NB = SWA_STEP_BLOCKS
    qo, ko, vo = 0, H_C * HD_C, H_C * HD_C + KV_C * HD_C
    assert G_C == 2
    row = _iota((G_C * W, 2 * W), 0)
    s = _iota((G_C * W, 2 * W), 1)
    t = jnp.where(row >= W, row - W, row)
    dist = W + t - s
    ok = (dist >= 0) & (dist <= W)
    ok_first = ok & ((s >= W) | (step > 0))
    distf = dist.astype(f32)
    second = _iota((G_C * W, 1), 0) >= W
    scale = HD_C ** -0.5

    def band(col, j):
        if j == 0:
            return jnp.concatenate([prev_ref[0, :, col:col + HD_C], cur_ref[0, 0:W, col:col + HD_C]], axis=0)
        return cur_ref[0, (j - 1) * W:(j + 1) * W, col:col + HD_C]

    pieces = [(j, g) for j in range(NB) for g in range(KV_C)]
    scores, sinks = {}, {}
    for g in range(KV_C):
        h0, h1 = g * G_C, g * G_C + 1
        sinks[g] = jnp.where(second, sink_ref[:, h1:h1 + 1], sink_ref[:, h0:h0 + 1])
    for j, g in pieces:
        h0, h1 = g * G_C, g * G_C + 1
        q2 = jnp.concatenate([cur_ref[0, j * W:(j + 1) * W, qo + h0 * HD_C:qo + (h0 + 1) * HD_C],
                              cur_ref[0, j * W:(j + 1) * W, qo + h1 * HD_C:qo + (h1 + 1) * HD_C]], axis=0)
        slope = jnp.where(second, ALIBI_SLOPES[h1], ALIBI_SLOPES[h0])
        raw = _bdot_nt(q2, band(ko + g * HD_C, j)) * scale - slope * distf
        scores[j, g] = jnp.where(ok_first if j == 0 else ok, raw, NEG_BIG)
    probs, dens = {}, {}
    for j, g in pieces:
        m = jnp.maximum(jnp.max(scores[j, g], -1, keepdims=True), sinks[g])
        p = jnp.exp(scores[j, g] - m)
        probs[j, g] = p
        dens[j, g] = jnp.sum(p, -1, keepdims=True) + jnp.exp(sinks[g] - m)
    for j, g in pieces:
        out = _bdot(probs[j, g], band(vo + g * HD_C, j)) / dens[j, g]
        for jj in range(G_C):
            h = g * G_C + jj
            y_ref[0, j * W:(j + 1) * W, h * HD_C:(h + 1) * HD_C] = out[jj * W:(jj + 1) * W]


def _swa_prompt(pc, prm, layer):
    b, l, _ = pc.shape
    W = WINDOW
    NB = SWA_STEP_BLOCKS
    return pl.pallas_call(
        _swa_prompt_kernel,
        grid=(b, l // (NB * W)),
        in_specs=[pl.BlockSpec((1, NB * W, C_COLS), lambda i, c: (i, c, 0)),
                  pl.BlockSpec((1, W, C_COLS), lambda i, c: (i, jnp.maximum(NB * c - 1, 0), 0)),
                  _lspec(prm["sinks"], layer)],
        out_specs=pl.BlockSpec((1, NB * W, W_BR), lambda i, c: (i, c, 0)),
        out_shape=jax.ShapeDtypeStruct((b, l, W_BR), f32),
        compiler_params=pltpu.CompilerParams(dimension_semantics=("arbitrary", "arbitrary"),
                                             vmem_limit_bytes=VMEM_LIMIT),
        name="swa_prompt",
    )(pc, pc, prm["sinks"])


def _lru_gates(xc, wa, ba, wx, bx, lam):
    r = _sigmoid(_bdot(xc, wa) + ba)
    i = _sigmoid(_bdot(xc, wx) + bx)
    log_a = C_RG * r * _log_sigmoid(lam)
    a = jnp.exp(log_a)
    bterm = jnp.sqrt(1.0 - jnp.exp(2.0 * log_a)) * (i * xc)
    return a, bterm


def _lru_prompt_kernel(x_ref, cw_ref, cb_ref, wa_ref, ba_ref, wx_ref, bx_ref, lam_ref, y_ref,
                       xbuf_scr, h_scr):
    c = pl.program_id(1)
    T = LRU_CHUNK
    PADR = 8

    @pl.when(c == 0)
    def _():
        xbuf_scr[0:PADR, :] = jnp.zeros((PADR, W_BR), f32)
        h_scr[...] = jnp.zeros_like(h_scr)

    x = x_ref[0]
    xbuf_scr[PADR:PADR + T, :] = x
    xc = cb_ref[...] + x * cw_ref[CONV_W - 1:CONV_W, :]
    for j in range(1, CONV_W):
        xc = xc + xbuf_scr[PADR - j:PADR - j + T, :] * cw_ref[CONV_W - 1 - j:CONV_W - j, :]
    xbuf_scr[0:PADR, :] = x[T - PADR:T, :]
    a, bv = _lru_gates(xc, wa_ref[...], ba_ref[...], wx_ref[...], bx_ref[...], lam_ref[...])
    row = _iota((T, W_BR), 0)
    d = 1
    while d < T:
        keep = row >= d
        a_sh = jnp.where(keep, pltpu.roll(a, d, 0), 1.0)
        b_sh = jnp.where(keep, pltpu.roll(bv, d, 0), 0.0)
        bv = a * b_sh + bv
        a = a * a_sh
        d *= 2
    h = a * h_scr[...] + bv
    y_ref[0] = h
    h_scr[...] = h[T - 1:T, :]


def _lru_prompt(pd, prm, layer):
    b, l, _ = pd.shape
    T = LRU_CHUNK
    return pl.pallas_call(
        _lru_prompt_kernel,
        grid=(b, l // T),
        in_specs=[pl.BlockSpec((1, T, W_BR), lambda i, c: (i, c, 0))]
        + [_lspec(prm[k], layer) for k in LRU_PARAMS],
        out_specs=pl.BlockSpec((1, T, W_BR), lambda i, c: (i, c, 0)),
        out_shape=jax.ShapeDtypeStruct((b, l, W_BR), f32),
        scratch_shapes=[pltpu.VMEM((T + 8, W_BR), f32), pltpu.VMEM((1, W_BR), f32)],
        compiler_params=pltpu.CompilerParams(dimension_semantics=("arbitrary", "arbitrary"),
                                             vmem_limit_bytes=VMEM_LIMIT),
        name="lru_prompt",
    )(pd, *[prm[k] for k in LRU_PARAMS])


def _colbcast(row, n_out, eye_bf16):
    c = row.shape[1]
    hi, lo = _split_bf16(row)
    return (_dg(eye_bf16, jnp.broadcast_to(hi, (n_out, c)), NT)
            + _dg(eye_bf16, jnp.broadcast_to(lo, (n_out, c)), NT))


def _decode_kernel(pa_ref, pb_ref, q8_ref, kvn_ref, pd_ref, shift_ref, swkv_ref, sgla_ref, kbuf_ref, vbuf_ref,
                   c0_ref, c1_ref, c2_ref, h0_ref,
                   mu_ref, w0_ref, wup_ref, a0_ref, aup_ref, kkw_ref, kaw_ref, rk_ref, lng_ref, lnb_ref,
                   gup_ref, gbias_ref, gng_ref, sink8_ref, slope8_ref,
                   cw_ref, cb_ref, wa_ref, ba_ref, wx_ref, bx_ref, lam_ref, *rest, n_prev):
    earlier, rest = rest[:4 if n_prev else 0], rest[4 if n_prev else 0:]
    ya_ref, yb_ref, yc8_ref, yd_ref, swkv_all, sgla_all, kout_all, vout_all, oa_scr, ob_scr = rest
    for src, dst in zip(earlier, (swkv_all, sgla_all, kout_all, vout_all)):
        dst[0:n_prev] = src[...]
    swkv_out, sgla_out = swkv_all.at[n_prev], sgla_all.at[n_prev]
    kout_ref, vout_ref = kout_all.at[n_prev], vout_all.at[n_prev]
    bt = pa_ref.shape[0]
    samples = range(bt)
    u = pa_ref[...]
    us = u + (shift_ref[...] - u) * mu_ref[...]
    ones = _head_ones()
    r, kmod, v, ld, kk, a = _rwkv_features(us, w0_ref[...], wup_ref[...], a0_ref[...], aup_ref[...],
                                           kkw_ref[...], kaw_ref[...], ones)
    decay = jnp.exp(ld)
    kka = kk * a
    fb = pb_ref[...]
    hk = H_B * DK_B
    qb = fb[:, 0:hk] * (DK_B ** -0.5)
    kb = fb[:, hk:2 * hk]
    vb = fb[:, 2 * hk:2 * hk + W_BR]
    gb = _log_sigmoid(_bdot(fb[:, 2 * hk + W_BR:B_PAD], gup_ref[...]) + gbias_ref[...]) * (1.0 / GLA_TAU)
    eg = jnp.exp(gb)
    eye_a = _eye(HEAD_A).astype(bf16)
    eye_b = _eye(DK_B).astype(bf16)
    row_of = lambda t, b: t[b:b + 1, :]

    v_col, eg_col, k_col, sa = {}, {}, {}, {}
    for b in samples:
        for h in range(H_A):
            sl = slice(h * HEAD_A, (h + 1) * HEAD_A)
            v_col[b, h] = _colbcast(row_of(v, b)[:, sl], HEAD_A, eye_a)
            kk_rows = jnp.broadcast_to(row_of(kk, b)[:, sl], (HEAD_A, HEAD_A))
            sa[b, h] = -_dotp(swkv_ref[b, h], kk_rows, NT, DECODE_PASSES)
        for h in range(H_B):
            ks = slice(h * DK_B, (h + 1) * DK_B)
            eg_col[b, h] = _colbcast(row_of(eg, b)[:, ks], DV_B, eye_b)
            k_col[b, h] = _colbcast(row_of(kb, b)[:, ks], DV_B, eye_b)
    for b in samples:
        for h in range(H_A):
            sl = slice(h * HEAD_A, (h + 1) * HEAD_A)
            s = swkv_ref[b, h]
            swkv_out[b, h] = (s * row_of(decay, b)[:, sl] + sa[b, h] * row_of(kka, b)[:, sl]
                              + v_col[b, h] * row_of(kmod, b)[:, sl])
        for h in range(H_B):
            vs = slice(h * DV_B, (h + 1) * DV_B)
            sgla_out[b, h] = eg_col[b, h] * sgla_ref[b, h] + k_col[b, h] * row_of(vb, b)[:, vs]
    for b in samples:
        for h in range(H_A):
            sl = slice(h * HEAD_A, (h + 1) * HEAD_A)
            r_rows = jnp.broadcast_to(row_of(r, b)[:, sl], (8, HEAD_A))
            oa_scr[b:b + 1, sl] = _dotp(r_rows, swkv_out[b, h], NT, DECODE_PASSES)[0:1, :]
        for h in range(H_B):
            ks = slice(h * DK_B, (h + 1) * DK_B)
            q_rows = jnp.broadcast_to(row_of(qb, b)[:, ks], (8, DK_B))
            ob_scr[b:b + 1, h * DV_B:(h + 1) * DV_B] = _dotp(q_rows, sgla_out[b, h], NN, DECODE_PASSES)[0:1, :]
    wdist = (WINDOW - _iota((1, WINDOW), 1)).astype(f32)
    scale = HD_C ** -0.5
    sink8 = sink8_ref[...]
    slope8 = slope8_ref[...]
    half = KV_C * HD_C
    kn = lambda b: kvn_ref[b:b + 1, 0:half]
    vn = lambda b: kvn_ref[b:b + 1, half:2 * half]
    scores = [_bdot_nt(q8_ref[b], kbuf_ref[b]) * scale - slope8 * wdist for b in samples]
    probs, tails = [], []
    for b in samples:
        sn = jnp.sum(q8_ref[b] * kn(b), axis=1, keepdims=True) * scale
        m = jnp.maximum(jnp.maximum(jnp.max(scores[b], axis=1, keepdims=True), sn), sink8)
        p = jnp.exp(scores[b] - m)
        pn = jnp.exp(sn - m)
        probs.append(p)
        tails.append((pn, jnp.sum(p, axis=1, keepdims=True) + pn + jnp.exp(sink8 - m)))
    for b in samples:
        pn, den = tails[b]
        yc8_ref[b] = (_bdot(probs[b], vbuf_ref[b]) + pn * vn(b)) / den
    for b in samples:
        kout_ref[b, 0:WINDOW - 1, :] = kbuf_ref[b, 1:WINDOW, :]
        kout_ref[b, WINDOW - 1:WINDOW, :] = kn(b)
        vout_ref[b, 0:WINDOW - 1, :] = vbuf_ref[b, 1:WINDOW, :]
        vout_ref[b, WINDOW - 1:WINDOW, :] = vn(b)

    ya_ref[...] = _rwkv_finish(oa_scr[...], r, kmod, v, rk_ref[...], lng_ref[...], lnb_ref[...], ones)
    ob = ob_scr[...]
    assert DV_B == HEAD_A
    ms = _ones_dot(ob * ob, ones) * (1.0 / DV_B)
    yb_ref[...] = ob * lax.rsqrt(ms + NORM_EPS) * gng_ref[...]
    xd = pd_ref[...]
    xc = (cb_ref[...] + c0_ref[...] * cw_ref[0:1, :] + c1_ref[...] * cw_ref[1:2, :]
          + c2_ref[...] * cw_ref[2:3, :] + xd * cw_ref[3:4, :])
    al, bterm = _lru_gates(xc, wa_ref[...], ba_ref[...], wx_ref[...], bx_ref[...], lam_ref[...])
    yd_ref[...] = al * h0_ref[...] + bterm


def _decode(pa, pb, pc, pd, shift0, swkv_all, sgla_all, kbuf, vbuf, c0, c1, c2, h0, prm, layer, earlier, bt):
    n = pa.shape[0]
    n_prev = layer
    half = KV_C * HD_C
    q4 = pc[:, 0:W_BR].reshape(n, H_C, HD_C)
    q8 = jnp.concatenate(
        [jnp.pad(q4[:, h:h + 1], ((0, 0), (0, 0), ((h // G_C) * HD_C, half - HD_C - (h // G_C) * HD_C)))
         for h in range(H_C)] + [jnp.zeros((n, 8 - H_C, half), f32)], axis=1)
    kvn = pc[:, W_BR:]
    slope8 = jnp.asarray(ALIBI_SLOPES + (0.0,) * (8 - H_C), f32).reshape(8, 1)
    rows = lambda w: pl.BlockSpec((bt, w), lambda i: (i, 0))
    cube = lambda d1, d2: pl.BlockSpec((bt, d1, d2), lambda i: (i, 0, 0))
    wkv_dims, gla_dims, kv_dims = (H_A, HEAD_A, HEAD_A), (H_B, DK_B, DV_B), (WINDOW, half)
    of_layer = lambda dims: pl.BlockSpec((None, bt) + dims, lambda i: (layer, i) + (0,) * len(dims))
    stacked = lambda nl, dims: pl.BlockSpec((nl, bt) + dims, lambda i: (0, i) + (0,) * len(dims))
    state_dims = (wkv_dims, gla_dims, kv_dims, kv_dims)
    in_specs = [rows(A_PAD), rows(B_PAD), cube(8, half), rows(2 * half), rows(D_COLS), rows(A_PAD),
                of_layer(wkv_dims), of_layer(gla_dims), cube(WINDOW, half), cube(WINDOW, half),
                rows(W_BR), rows(W_BR), rows(W_BR), rows(W_BR)]
    in_specs += [_lspec(prm[k], layer) for k in RWKV_PARAMS + GLA_PARAMS]
    in_specs += [_lspec(prm["sink8"], layer), _const_spec((8, 1))]
    in_specs += [_lspec(prm[k], layer) for k in LRU_PARAMS]
    if n_prev:
        in_specs += [stacked(n_prev, dims) for dims in state_dims]
    out_specs = [rows(W_BR), rows(W_BR), cube(8, half), rows(W_BR)] + [stacked(n_prev + 1, d) for d in state_dims]
    out_shape = [jax.ShapeDtypeStruct((n, W_BR), f32), jax.ShapeDtypeStruct((n, W_BR), f32),
                 jax.ShapeDtypeStruct((n, 8, half), f32), jax.ShapeDtypeStruct((n, W_BR), f32)]
    out_shape += [jax.ShapeDtypeStruct((n_prev + 1, n) + dims, f32) for dims in state_dims]
    ya, yb, yc8, yd, wkv1, gla1, k1, v1 = pl.pallas_call(
        functools.partial(_decode_kernel, n_prev=n_prev),
        grid=(n // bt,),
        in_specs=in_specs,
        out_specs=out_specs,
        out_shape=out_shape,
        scratch_shapes=[pltpu.VMEM((bt, W_BR), f32), pltpu.VMEM((bt, W_BR), f32)],
        compiler_params=pltpu.CompilerParams(dimension_semantics=("arbitrary",),
                                             vmem_limit_bytes=VMEM_LIMIT),
        name="decode_mixers",
    )(pa, pb, q8, kvn, pd, shift0, swkv_all, sgla_all, kbuf, vbuf, c0, c1, c2, h0,
      *[prm[k] for k in RWKV_PARAMS + GLA_PARAMS], prm["sink8"], slope8, *[prm[k] for k in LRU_PARAMS],
      *(earlier if n_prev else ()))
    yc = jnp.concatenate([yc8[:, h, (h // G_C) * HD_C:(h // G_C + 1) * HD_C] for h in range(H_C)], axis=1)
    return ya, yb, yc, yd, wkv1, gla1, k1, v1


def _merge_kernel(x_ref, ya_ref, yb_ref, yc_ref, yd_ref, g_ref, w_ref, wbr_ref, wout_ref,
                  fg_ref, o_ref, *, final):
    x = x_ref[...]
    hn = _rms(x, g_ref[...]).astype(bf16)
    ys = (ya_ref, yb_ref, yc_ref, yd_ref)
    merged = None
    for n in range(N_BRANCH):
        z = _dg(hn, w_ref[0, n * W_BR:(n + 1) * W_BR, :], NT)
        yz = ys[n][...] * (z * _sigmoid(z))
        br = jnp.dot(yz.astype(bf16), wbr_ref[n], preferred_element_type=f32)
        gate = _sigmoid(_dg(hn, w_ref[0, Z_COLS + n * D_MODEL:Z_COLS + (n + 1) * D_MODEL, :], NT))
        merged = gate * br if merged is None else merged + gate * br
    out = x + jnp.dot(merged.astype(bf16), wout_ref[...], preferred_element_type=f32)
    if final:
        out = _rms(out, fg_ref[...])
    o_ref[...] = out


def _merge(x, ya, yb, yc, yd, g, wt, wbr, wout, fg, layer, tm, final):
    m = x.shape[0]
    tile = lambda w: pl.BlockSpec((tm, w), lambda i: (i, 0))
    per_layer = lambda shape: pl.BlockSpec((None,) + shape, lambda *_: (layer,) + (0,) * len(shape),
                                           pipeline_mode=pl.Buffered(1))
    return pl.pallas_call(
        functools.partial(_merge_kernel, final=final),
        grid=(m // tm,),
        in_specs=[tile(D_MODEL), tile(W_BR), tile(W_BR), tile(W_BR), tile(W_BR), _lspec(g, layer),
                  _wt_spec(layer, MIX_ROWS, GATE_ROWS, pipeline_mode=pl.Buffered(1)),
                  per_layer((N_BRANCH, W_BR, D_MODEL)),
                  per_layer((D_MODEL, D_MODEL)), _const_spec((1, D_MODEL))],
        out_specs=tile(D_MODEL),
        out_shape=jax.ShapeDtypeStruct((m, D_MODEL), f32),
        compiler_params=pltpu.CompilerParams(dimension_semantics=("arbitrary",),
                                             vmem_limit_bytes=VMEM_LIMIT),
        name="merge_final" if final else "merge",
    )(x, ya, yb, yc, yd, g, wt, wbr, wout, fg)


def _pad_cols(w, width):
    return jnp.pad(w, ((0, 0), (0, width - w.shape[1])))


def _pad_rows_at(w, start, total):
    return jnp.pad(w, ((start, total - start - w.shape[0]), (0, 0)))


def _block_diag(w):
    nb, bs, _ = w.shape
    out = jnp.zeros((nb * bs, nb * bs), w.dtype)
    for n in range(nb):
        out = out.at[n * bs:(n + 1) * bs, n * bs:(n + 1) * bs].set(w[n])
    return out


def _lspec(arr, layer):
    return pl.BlockSpec((None,) + arr.shape[1:], lambda *_: (layer,) + (0,) * (arr.ndim - 1))


def _stacked_params(norm_g, mu_shift, w0, w_decay_up, a0, a_icl_up, k_k, k_a, r_k, ln_x_g, ln_x_b,
                    gla_gate_up, gla_gate_b, gla_norm_g, swa_sinks, lru_conv_w, lru_conv_b, lru_wa, lru_ba,
                    lru_wx, lru_bx, lru_lambda):
    depth = norm_g.shape[0]
    row = lambda t: t.reshape(depth, 1, -1)
    pad_rows = lambda w, start: jnp.pad(w, ((0, 0), (start, LANE - start - w.shape[1]), (0, 0)))
    dense = lambda w: jnp.einsum("lnij,nm->lnimj", w, jnp.eye(w.shape[1], dtype=w.dtype)).reshape(
        depth, W_BR, W_BR).astype(bf16)
    return dict(
        g=row(norm_g),
        mu=row(jnp.pad(mu_shift, ((0, 0), (0, A_PAD - A_COLS)))),
        w0=row(w0), wup=pad_rows(w_decay_up, 0), a0=row(a0), aup=pad_rows(a_icl_up, R_DECAY),
        kkw=row(k_k), kaw=row(k_a), rk=row(r_k), lng=row(ln_x_g), lnb=row(ln_x_b),
        gup=pad_rows(gla_gate_up, 0), gbias=row(gla_gate_b), gng=row(jnp.tile(gla_norm_g, (1, H_B))),
        sinks=row(swa_sinks),
        sink8=jnp.pad(swa_sinks.reshape(depth, H_C, 1), ((0, 0), (0, 8 - H_C), (0, 0))),
        cw=lru_conv_w, cb=row(lru_conv_b),
        lwa=dense(lru_wa), lba=row(lru_ba), lwx=dense(lru_wx), lbx=row(lru_bx), lam=row(lru_lambda),
    )


def kernel(x_prompt, x_sample, state_wkv, state_shift, state_gla, cache_swa_k, cache_swa_v, state_lru_conv, state_lru_h, norm_g, w_in, mu_shift, w0, w_decay_up, a0, a_icl_up, k_k, k_a, r_k, ln_x_g, ln_x_b, gla_gate_up, gla_gate_b, gla_norm_g, swa_sinks, lru_conv_w, lru_conv_b, lru_wa, lru_ba, lru_wx, lru_bx, lru_lambda, w_branch, w_out, final_norm_g):
    bp, lp, _ = x_prompt.shape
    bs = x_sample.shape[0]
    depth = w_in.shape[0]
    fg = final_norm_g.reshape(1, -1)
    xp = x_prompt.reshape(bp * lp, D_MODEL)
    xs = x_sample.reshape(bs, D_MODEL)
    outs_p = [[] for _ in range(7)]
    outs_s = [[] for _ in range(7)]
    wt_all = jnp.swapaxes(w_in, 1, 2).astype(bf16)
    wbr_all = w_branch.astype(bf16)
    wout_all = w_out.astype(bf16)
    stacked_s = None
    prm = _stacked_params(norm_g, mu_shift, w0, w_decay_up, a0, a_icl_up, k_k, k_a, r_k, ln_x_g, ln_x_b,
                          gla_gate_up, gla_gate_b, gla_norm_g, swa_sinks, lru_conv_w, lru_conv_b, lru_wa,
                          lru_ba, lru_wx, lru_bx, lru_lambda)
    shift_pad = jnp.pad(state_shift, ((0, 0), (0, 0), (0, A_PAD - A_COLS)))
    for l in range(depth):
        final = l == depth - 1
        pa, pb, pc, pd = _inproj(xp, prm["g"], wt_all, l, tm=512)
        pa3, pb3 = pa.reshape(bp, lp, A_PAD), pb.reshape(bp, lp, B_PAD)
        pc3, pd3 = pc.reshape(bp, lp, C_COLS), pd.reshape(bp, lp, D_COLS)
        ya, st_t = _rwkv_prompt(pa3, prm, l)
        yb, sgla = _gla_prompt(pb3, prm, l)
        yc = _swa_prompt(pc3, prm, l)
        yd = _lru_prompt(pd3, prm, l)
        flat = lambda t: t.reshape(bp * lp, W_BR)
        xp = _merge(xp, flat(ya), flat(yb), flat(yc), flat(yd), prm["g"], wt_all, wbr_all, wout_all, fg, l,
                    tm=512, final=final)
        kv = pc3[:, lp - WINDOW:, H_C * HD_C:]
        outs_p[0].append(jnp.swapaxes(st_t, -1, -2))
        outs_p[1].append(pa3[:, lp - 1, :A_COLS])
        outs_p[2].append(sgla)
        outs_p[3].append(kv[:, :, :KV_C * HD_C].reshape(bp, WINDOW, KV_C, HD_C))
        outs_p[4].append(kv[:, :, KV_C * HD_C:].reshape(bp, WINDOW, KV_C, HD_C))
        outs_p[5].append(pd3[:, lp - (CONV_W - 1):, :])
        outs_p[6].append(yd[:, lp - 1, :])
        sa, sb, sc, sd = _inproj(xs, prm["g"], wt_all, l, tm=bs)
        kbuf = cache_swa_k[l].reshape(bs, WINDOW, KV_C * HD_C)
        vbuf = cache_swa_v[l].reshape(bs, WINDOW, KV_C * HD_C)
        conv0 = state_lru_conv[l]
        ya_s, yb_s, yc_s, yd_s, *stacked_s = _decode(
            sa, sb, sc, sd, shift_pad[l], state_wkv, state_gla, kbuf, vbuf,
            conv0[:, 0], conv0[:, 1], conv0[:, 2], state_lru_h[l], prm, l, stacked_s, bt=8)
        xs = _merge(xs, ya_s, yb_s, yc_s, yd_s, prm["g"], wt_all, wbr_all, wout_all, fg, l, tm=bs, final=final)
        outs_s[1].append(sa[:, :A_COLS])
        outs_s[5].append(jnp.stack([conv0[:, 1], conv0[:, 2], sd], axis=1))
        outs_s[6].append(yd_s)
    y_prompt = xp.reshape(bp, lp, D_MODEL)
    y_sample = xs.reshape(bs, 1, D_MODEL)
    sp = [jnp.stack(t) for t in outs_p]
    wkv_s, gla_s, k_s, v_s = stacked_s
    k_s = k_s.reshape(depth, bs, WINDOW, KV_C, HD_C)
    v_s = v_s.reshape(depth, bs, WINDOW, KV_C, HD_C)
    shift_s, conv_s, h_s = (jnp.stack(outs_s[i]) for i in (1, 5, 6))
    return (y_prompt, y_sample, sp[0], wkv_s, sp[1], shift_s, sp[2], gla_s, sp[3], k_s, sp[4], v_s,
            sp[5], conv_s, sp[6], h_s)
```

```python
import functools

import jax
import jax.numpy as jnp
from jax import lax
from jax.experimental import pallas as pl
from jax.experimental.pallas import tpu as pltpu

f32 = jnp.float32
bf16 = jnp.bfloat16
HI = lax.Precision.HIGHEST

D_MODEL = 1024
N_BRANCH = 4
W_BR = 256
HEAD_A = 64
H_A = 4
R_DECAY = 32
R_ICL = 32
GN_EPS_A = 64e-5
H_B = 4
DK_B = 32
DV_B = 64
R_GATE_B = 16
GLA_TAU = 16.0
GLA_CHUNK = 64
GLA_STEP_ROWS = 256
H_C = 4
KV_C = 2
HD_C = 64
G_C = 2
WINDOW = 128
SWA_STEP_BLOCKS = 4
CONV_W = 4
C_RG = 8.0
NORM_EPS = 1e-6

A_COLS = 3 * W_BR + R_DECAY + R_ICL
B_COLS = 2 * H_B * DK_B + W_BR + R_GATE_B
C_COLS = H_C * HD_C + 2 * KV_C * HD_C
D_COLS = W_BR
Z_COLS = N_BRANCH * W_BR
G_COLS = N_BRANCH * D_MODEL

LANE = 128
A_PAD = 7 * LANE
B_PAD = 5 * LANE
RWKV_CHUNK = 64
RWKV_STEP_ROWS = 256
RWKV_STEP_BATCH = 4
RWKV_PASSES = 1
GLA_PASSES = 1
DECODE_PASSES = 1
LRU_CHUNK = 256
VMEM_LIMIT = 56 * 1024 * 1024
NEG_BIG = -1e30

ALIBI_SLOPES = tuple(2.0 ** (-8.0 * (h + 1) / H_C) for h in range(H_C))


def _mm(a, b):
    return jnp.dot(a, b, precision=HI, preferred_element_type=f32)


def _mm_nt(a, b):
    return lax.dot_general(a, b, (((1,), (1,)), ((), ())), precision=HI, preferred_element_type=f32)


def _mm_tn(a, b):
    return lax.dot_general(a, b, (((0,), (0,)), ((), ())), precision=HI, preferred_element_type=f32)


def _bdot(a, b):
    return jnp.dot(a.astype(bf16), b.astype(bf16), preferred_element_type=f32)


def _bdot_nt(a, b):
    return lax.dot_general(a.astype(bf16), b.astype(bf16), (((1,), (1,)), ((), ())),
                           preferred_element_type=f32)


NN = ((1,), (0,))
NT = ((1,), (1,))
TN = ((0,), (0,))


def _dg(a, b, dims):
    return lax.dot_general(a, b, (dims, ((), ())), preferred_element_type=f32)


def _split_bf16(a):
    hi = a.astype(bf16)
    return hi, (a - hi.astype(f32)).astype(bf16)


def _dotp(a, b, dims, passes):
    if passes == 1:
        return _dg(a.astype(bf16), b.astype(bf16), dims)
    ah, al = _split_bf16(a)
    bh, bl = _split_bf16(b)
    return _dg(ah, bh, dims) + (_dg(ah, bl, dims) + _dg(al, bh, dims))


def _ones_dot(a, ones_bf16):
    ah, al = _split_bf16(a)
    return _dg(ah, ones_bf16, NN) + _dg(al, ones_bf16, NN)


def _cumsum_rows(x, seg, pos=None):
    if pos is None:
        pos = _iota(x.shape, 0) % seg
    d = 1
    while d < seg:
        x = x + jnp.where(pos >= d, pltpu.roll(x, d, 0), 0.0)
        d *= 2
    return x


def _iota(shape, dim):
    return lax.broadcasted_iota(jnp.int32, shape, dim)


def _eye(n):
    return (_iota((n, n), 0) == _iota((n, n), 1)).astype(f32)


def _block_ones(n, blk):
    return ((_iota((n, n), 0) // blk) == (_iota((n, n), 1) // blk)).astype(f32)


def _tril_ones(n):
    return (_iota((n, n), 0) >= _iota((n, n), 1)).astype(f32)


def _softplus(x):
    return jnp.maximum(x, 0.0) + jnp.log(1.0 + jnp.exp(-jnp.abs(x)))


def _log_sigmoid(x):
    return -_softplus(-x)


def _sigmoid(x):
    return 1.0 / (1.0 + jnp.exp(-x))


def _rms(x, g):
    return x * lax.rsqrt(jnp.mean(x * x, -1, keepdims=True) + NORM_EPS) * g


MIX_COLS = (A_COLS, B_COLS, C_COLS, D_COLS)
MIX_WIDTHS = (A_PAD, B_PAD, C_COLS, D_COLS)
MIX_ROWS = sum(MIX_COLS)
GATE_ROWS = Z_COLS + G_COLS


def _wt_spec(layer, start, rows, **kw):
    return pl.BlockSpec((pl.Element(1), pl.Element(rows), pl.Element(D_MODEL)), lambda *_: (layer, start, 0), **kw)


def _inproj_kernel(x_ref, g_ref, wt_ref, oa_ref, ob_ref, oc_ref, od_ref):
    hn = _rms(x_ref[...], g_ref[...]).astype(bf16)
    start = 0
    for o_ref, cols, width in zip((oa_ref, ob_ref, oc_ref, od_ref), MIX_COLS, MIX_WIDTHS):
        o_ref[:, 0:cols] = _dg(hn, wt_ref[0, start:start + cols, :], NT)
        if width > cols:
            o_ref[:, cols:width] = jnp.zeros((o_ref.shape[0], width - cols), f32)
        start += cols


def _const_spec(shape):
    return pl.BlockSpec(shape, lambda *_: (0,) * len(shape))


def _inproj(x, g, wt, layer, tm):
    m = x.shape[0]
    return pl.pallas_call(
        _inproj_kernel,
        grid=(m // tm,),
        in_specs=[pl.BlockSpec((tm, D_MODEL), lambda i: (i, 0)), _lspec(g, layer), _wt_spec(layer, 0, MIX_ROWS)],
        out_specs=[pl.BlockSpec((tm, w), lambda i: (i, 0)) for w in MIX_WIDTHS],
        out_shape=[jax.ShapeDtypeStruct((m, w), f32) for w in MIX_WIDTHS],
        compiler_params=pltpu.CompilerParams(dimension_semantics=("arbitrary",),
                                             vmem_limit_bytes=VMEM_LIMIT),
        name="inproj",
    )(x, g, wt)


def _head_ones():
    return _block_ones(W_BR, HEAD_A).astype(bf16)


def _rwkv_features(us, w0, wup, a0, aup, kk_w, ka_w, ones):
    r = us[:, 0:W_BR]
    k = us[:, W_BR:2 * W_BR]
    v = us[:, 2 * W_BR:3 * W_BR]
    lora = us[:, 3 * W_BR:A_PAD]
    w = -_softplus(-(w0 + _bdot(jnp.tanh(lora), wup))) - 0.5
    logdecay = -jnp.exp(w)
    a = _sigmoid(a0 + _bdot(lora, aup))
    kk = k * kk_w
    ss = _ones_dot(kk * kk, ones)
    kk = kk / jnp.maximum(jnp.sqrt(ss), 1e-12)
    kmod = k * (1.0 + (a - 1.0) * ka_w)
    return r, kmod, v, logdecay, kk, a


def _rwkv_finish(o, r, kmod, v, rk, lng, lnb, ones):
    mean = _ones_dot(o, ones) * (1.0 / HEAD_A)
    cen = o - mean
    var = _ones_dot(cen * cen, ones) * (1.0 / HEAD_A)
    o = cen * lax.rsqrt(var + GN_EPS_A) * lng + lnb
    bonus = _ones_dot(r * kmod * rk, ones) * v
    return o + bonus


def _rwkv_prompt_kernel(u_ref, mu_ref, w0_ref, wup_ref, a0_ref, aup_ref, kkw_ref, kaw_ref, rk_ref,
                        lng_ref, lnb_ref, y_ref, s_ref, st_scr, prev_scr, o_scr):
    step = pl.program_id(1)
    T = RWKV_CHUNK
    TT = RWKV_STEP_ROWS

    @pl.when(step == 0)
    def _():
        st_scr[...] = jnp.zeros_like(st_scr)
        prev_scr[...] = jnp.zeros_like(prev_scr)

    R = RWKV_STEP_BATCH
    P = RWKV_PASSES
    NC = TT // T
    pieces = [(c, h) for c in range(NC) for h in range(H_A)]
    row0 = _iota((TT, A_PAD), 0) == 0
    pos = _iota((TT, W_BR), 0) % T
    ones = _head_ones()

    def features(q):
        u = u_ref[q]
        u_prev = jnp.where(row0, prev_scr[q], pltpu.roll(u, 1, 0))
        prev_scr[q] = u[TT - 1:TT, :]
        us = u + (u_prev - u) * mu_ref[...]
        r, kmod, v, ld, kk, a = _rwkv_features(us, w0_ref[...], wup_ref[...], a0_ref[...], aup_ref[...],
                                               kkw_ref[...], kaw_ref[...], ones)
        cum = _cumsum_rows(ld, T, pos)
        cum_last = jnp.concatenate(
            [jnp.broadcast_to(cum[(c + 1) * T - 1:(c + 1) * T, :], (T, W_BR)) for c in range(NC)], axis=0)
        g_inv = jnp.exp(-cum)
        g_tail = jnp.exp(cum_last - cum)
        kka = kk * a
        return dict(r=r, kmod=kmod, v=v, cum=cum, at=-kk * jnp.exp(cum - ld), rt=r * jnp.exp(cum),
                    bt=kka * g_inv, kt=kmod * g_inv, btg=kka * g_tail, ktg=kmod * g_tail)

    ri = _iota((2 * T, 2 * T), 0)
    ci = _iota((2 * T, 2 * T), 1)
    ti = jnp.where(ri >= T, ri - T, ri)
    si = jnp.where(ci >= T, ci - T, ci)
    keep = (ti > si) | ((ri >= T) & (ti == si))
    eye_t = _eye(T)
    eye_h = _eye(HEAD_A)

    def independent_stages(f):
        d = dict(ar={}, vh={}, pm={}, x={}, pw={}, lv={}, gcol={}, bkg={}, xar={})

        def products():
            for c, h in pieces:
                rows = slice(c * T, (c + 1) * T)
                sl = slice(h * HEAD_A, (h + 1) * HEAD_A)
                d["vh"][c, h] = f["v"][rows, sl]
                d["ar"][c, h] = jnp.concatenate([f["at"][rows, sl], f["rt"][rows, sl]], axis=0)
                bk = jnp.concatenate([f["bt"][rows, sl], f["kt"][rows, sl]], axis=0)
                d["pm"][c, h] = jnp.where(keep, _dotp(d["ar"][c, h], bk, NT, P), 0.0)

        def squares_and_values():
            for c, h in pieces:
                rows = slice(c * T, (c + 1) * T)
                sl = slice(h * HEAD_A, (h + 1) * HEAD_A)
                lab = d["pm"][c, h][0:T, 0:T]
                d["x"][c, h] = eye_t + lab
                d["pw"][c, h] = _dotp(lab, lab, NN, P)
                d["lv"][c, h] = _dotp(d["pm"][c, h][:, T:2 * T], d["vh"][c, h], NN, P)
                g_last = jnp.exp(f["cum"][(c + 1) * T - 1:(c + 1) * T, sl])
                d["gcol"][c, h] = jnp.sum(eye_h * g_last, axis=1, keepdims=True)
                d["bkg"][c, h] = jnp.concatenate([f["btg"][rows, sl], f["ktg"][rows, sl]], axis=0)

        def inverse_round(last):
            def run():
                for c, h in pieces:
                    x_next = d["x"][c, h] + _dotp(d["pw"][c, h], d["x"][c, h], NN, P)
                    if not last:
                        d["pw"][c, h] = _dotp(d["pw"][c, h], d["pw"][c, h], NN, P)
                    d["x"][c, h] = x_next
            return run

        def fold_inverse():
            for c, h in pieces:
                xa = _dotp(d["x"][c, h], d["ar"][c, h][0:T], NN, P)
                xl = _dotp(d["x"][c, h], d["lv"][c, h][0:T], NN, P)
                d["xar"][c, h] = jnp.concatenate([xa, d["ar"][c, h][T:2 * T]], axis=0)
                d["lv"][c, h] = jnp.concatenate([xl, d["lv"][c, h][T:2 * T]], axis=0)

        stages = [products, squares_and_values] + [inverse_round(it == 4) for it in range(5)] + [fold_inverse]
        return stages, d

    def dependent_stages(q, f, d):
        st = {}
        base = {}

        def load():
            for h in range(H_A):
                st[h] = st_scr[q, h]

        def read(c):
            def run():
                if c == 0:
                    load()
                for h in range(H_A):
                    base[h] = _dotp(d["xar"][c, h], st[h], NN, P) + d["lv"][c, h]
            return run

        def update(c):
            def run():
                for h in range(H_A):
                    st[h] = d["gcol"][c, h] * st[h] + _dotp(
                        d["bkg"][c, h], jnp.concatenate([base[h][0:T], d["vh"][c, h]], axis=0), TN, P)
                for h in range(H_A):
                    o_scr[q, c * T:(c + 1) * T, h * HEAD_A:(h + 1) * HEAD_A] = (
                        base[h][T:2 * T] + _dotp(d["pm"][c, h][T:2 * T, 0:T], base[h][0:T], NN, P))
                if c == NC - 1:
                    for h in range(H_A):
                        st_scr[q, h] = st[h]
                    y_ref[q] = _rwkv_finish(o_scr[q], f["r"], f["kmod"], f["v"], rk_ref[...], lng_ref[...],
                                            lnb_ref[...], ones)
            return run

        return [stage for c in range(NC) for stage in (read(c), update(c))]

    feats = {q: features(q) for q in range(min(2, R))}
    pending = []
    for q in range(R):
        stages, d = independent_stages(feats[q])
        if q + 2 < R:
            feats[q + 2] = features(q + 2)
        for k in range(max(len(stages), len(pending))):
            if k < len(stages):
                stages[k]()
            if k < len(pending):
                pending[k]()
        pending = dependent_stages(q, feats[q], d)
    for stage in pending:
        stage()

    @pl.when(step == pl.num_programs(1) - 1)
    def _():
        s_ref[...] = st_scr[...]


RWKV_PARAMS = ("mu", "w0", "wup", "a0", "aup", "kkw", "kaw", "rk", "lng", "lnb")
GLA_PARAMS = ("gup", "gbias", "gng")
LRU_PARAMS = ("cw", "cb", "lwa", "lba", "lwx", "lbx", "lam")


def _rwkv_prompt(pa, prm, layer):
    b, l, _ = pa.shape
    T = RWKV_STEP_ROWS
    R = RWKV_STEP_BATCH
    return pl.pallas_call(
        _rwkv_prompt_kernel,
        grid=(b // R, l // T),
        in_specs=[pl.BlockSpec((R, T, A_PAD), lambda i, c: (i, c, 0))]
        + [_lspec(prm[k], layer) for k in RWKV_PARAMS],
        out_specs=[pl.BlockSpec((R, T, W_BR), lambda i, c: (i, c, 0)),
                   pl.BlockSpec((R, H_A, HEAD_A, HEAD_A), lambda i, c: (i, 0, 0, 0))],
        out_shape=[jax.ShapeDtypeStruct((b, l, W_BR), f32),
                   jax.ShapeDtypeStruct((b, H_A, HEAD_A, HEAD_A), f32)],
        scratch_shapes=[pltpu.VMEM((R, H_A, HEAD_A, HEAD_A), f32), pltpu.VMEM((R, 1, A_PAD), f32),
                        pltpu.VMEM((R, T, W_BR), f32)],
        compiler_params=pltpu.CompilerParams(dimension_semantics=("arbitrary", "arbitrary"),
                                             vmem_limit_bytes=VMEM_LIMIT),
        name="rwkv_prompt",
    )(pa, *[prm[k] for k in RWKV_PARAMS])


def _gla_prompt_kernel(f_ref, up_ref, bias_ref, ng_ref, y_ref, s_ref, s_scr, o_scr):
    step = pl.program_id(1)
    T = GLA_CHUNK
    TT = GLA_STEP_ROWS
    NC = TT // T

    @pl.when(step == 0)
    def _():
        s_scr[...] = jnp.zeros_like(s_scr)

    f = f_ref[0]
    hk = H_B * DK_B
    q = f[:, 0:hk] * (DK_B ** -0.5)
    k = f[:, hk:2 * hk]
    v = f[:, 2 * hk:2 * hk + W_BR]
    gl = f[:, 2 * hk + W_BR:B_PAD]
    P = GLA_PASSES
    g = _log_sigmoid(_bdot(gl, up_ref[...]) + bias_ref[...]) * (1.0 / GLA_TAU)
    bcum = _cumsum_rows(g, T)
    b_last = jnp.concatenate(
        [jnp.broadcast_to(bcum[(c + 1) * T - 1:(c + 1) * T, :], (T, hk)) for c in range(NC)], axis=0)
    qe = q * jnp.exp(bcum)
    ke = k * jnp.exp(-bcum)
    kl = k * jnp.exp(b_last - bcum)
    causal = _iota((T, T), 0) >= _iota((T, T), 1)
    eye_k = _eye(DK_B)
    pieces = [(c, h) for c in range(NC) for h in range(H_B)]
    av, kv, ecol = {}, {}, {}
    for c, h in pieces:
        rows = slice(c * T, (c + 1) * T)
        ks = slice(h * DK_B, (h + 1) * DK_B)
        vs = slice(h * DV_B, (h + 1) * DV_B)
        att = jnp.where(causal, _dotp(qe[rows, ks], ke[rows, ks], NT, P), 0.0)
        av[c, h] = _dotp(att, v[rows, vs], NN, P)
        kv[c, h] = _dotp(kl[rows, ks], v[rows, vs], TN, P)
        e_last = jnp.exp(bcum[(c + 1) * T - 1:(c + 1) * T, ks])
        ecol[c, h] = jnp.sum(eye_k * e_last, axis=1, keepdims=True)
    s = [s_scr[h] for h in range(H_B)]
    for c in range(NC):
        rows = slice(c * T, (c + 1) * T)
        for h in range(H_B):
            ks = slice(h * DK_B, (h + 1) * DK_B)
            o_scr[rows, h * DV_B:(h + 1) * DV_B] = av[c, h] + _dotp(qe[rows, ks], s[h], NN, P)
            s[h] = ecol[c, h] * s[h] + kv[c, h]
    for h in range(H_B):
        s_scr[h] = s[h]
    o = o_scr[...]
    ms = _ones_dot(o * o, _block_ones(W_BR, DV_B).astype(bf16)) * (1.0 / DV_B)
    y_ref[0] = o * lax.rsqrt(ms + NORM_EPS) * ng_ref[...]

    @pl.when(step == pl.num_programs(1) - 1)
    def _():
        s_ref[0] = s_scr[...]


def _gla_prompt(pb, prm, layer):
    b, l, _ = pb.shape
    T = GLA_STEP_ROWS
    return pl.pallas_call(
        _gla_prompt_kernel,
        grid=(b, l // T),
        in_specs=[pl.BlockSpec((1, T, B_PAD), lambda i, c: (i, c, 0))]
        + [_lspec(prm[k], layer) for k in GLA_PARAMS],
        out_specs=[pl.BlockSpec((1, T, W_BR), lambda i, c: (i, c, 0)),
                   pl.BlockSpec((1, H_B, DK_B, DV_B), lambda i, c: (i, 0, 0, 0))],
        out_shape=[jax.ShapeDtypeStruct((b, l, W_BR), f32),
                   jax.ShapeDtypeStruct((b, H_B, DK_B, DV_B), f32)],
        scratch_shapes=[pltpu.VMEM((H_B, DK_B, DV_B), f32), pltpu.VMEM((T, W_BR), f32)],
        compiler_params=pltpu.CompilerParams(dimension_semantics=("arbitrary", "arbitrary"),
                                             vmem_limit_bytes=VMEM_LIMIT),
        name="gla_prompt",
    )(pb, *[prm[k] for k in GLA_PARAMS])


def _swa_prompt_kernel(cur_ref, prev_ref, sink_ref, y_ref):
    step = pl.program_id(1)
    W = WINDOW
    NB = SWA_STEP_BLOCKS
    qo, ko, vo = 0, H_C * HD_C, H_C * HD_C + KV_C * HD_C
    assert G_C == 2
    row = _iota((G_C * W, 2 * W), 0)
    s = _iota((G_C * W, 2 * W), 1)
    t = jnp.where(row >= W, row - W, row)
    dist = W + t - s
    ok = (dist >= 0) & (dist <= W)
    ok_first = ok & ((s >= W) | (step > 0))
    distf = dist.astype(f32)
    second = _iota((G_C * W, 1), 0) >= W
    scale = HD_C ** -0.5

    def band(col, j):
        if j == 0:
            return jnp.concatenate([prev_ref[0, :, col:col + HD_C], cur_ref[0, 0:W, col:col + HD_C]], axis=0)
        return cur_ref[0, (j - 1) * W:(j + 1) * W, col:col + HD_C]

    pieces = [(j, g) for j in range(NB) for g in range(KV_C)]
    scores, sinks = {}, {}
    for g in range(KV_C):
        h0, h1 = g * G_C, g * G_C + 1
        sinks[g] = jnp.where(second, sink_ref[:, h1:h1 + 1], sink_ref[:, h0:h0 + 1])
    for j, g in pieces:
        h0, h1 = g * G_C, g * G_C + 1
        q2 = jnp.concatenate([cur_ref[0, j * W:(j + 1) * W, qo + h0 * HD_C:qo + (h0 + 1) * HD_C],
                              cur_ref[0, j * W:(j + 1) * W, qo + h1 * HD_C:qo + (h1 + 1) * HD_C]], axis=0)
        slope = jnp.where(second, ALIBI_SLOPES[h1], ALIBI_SLOPES[h0])
        raw = _bdot_nt(q2, band(ko + g * HD_C, j)) * scale - slope * distf
        scores[j, g] = jnp.where(ok_first if j == 0 else ok, raw, NEG_BIG)
    probs, dens = {}, {}
    for j, g in pieces:
        m = jnp.maximum(jnp.max(scores[j, g], -1, keepdims=True), sinks[g])
        p = jnp.exp(scores[j, g] - m)
        probs[j, g] = p
        dens[j, g] = jnp.sum(p, -1, keepdims=True) + jnp.exp(sinks[g] - m)
    for j, g in pieces:
        out = _bdot(probs[j, g], band(vo + g * HD_C, j)) / dens[j, g]
        for jj in range(G_C):
            h = g * G_C + jj
            y_ref[0, j * W:(j + 1) * W, h * HD_C:(h + 1) * HD_C] = out[jj * W:(jj + 1) * W]


def _swa_prompt(pc, prm, layer):
    b, l, _ = pc.shape
    W = WINDOW
    NB = SWA_STEP_BLOCKS
    return pl.pallas_call(
        _swa_prompt_kernel,
        grid=(b, l // (NB * W)),
        in_specs=[pl.BlockSpec((1, NB * W, C_COLS), lambda i, c: (i, c, 0)),
                  pl.BlockSpec((1, W, C_COLS), lambda i, c: (i, jnp.maximum(NB * c - 1, 0), 0)),
                  _lspec(prm["sinks"], layer)],
        out_specs=pl.BlockSpec((1, NB * W, W_BR), lambda i, c: (i, c, 0)),
        out_shape=jax.ShapeDtypeStruct((b, l, W_BR), f32),
        compiler_params=pltpu.CompilerParams(dimension_semantics=("arbitrary", "arbitrary"),
                                             vmem_limit_bytes=VMEM_LIMIT),
        name="swa_prompt",
    )(pc, pc, prm["sinks"])


def _lru_gates(xc, wa, ba, wx, bx, lam):
    r = _sigmoid(_bdot(xc, wa) + ba)
    i = _sigmoid(_bdot(xc, wx) + bx)
    log_a = C_RG * r * _log_sigmoid(lam)
    a = jnp.exp(log_a)
    bterm = jnp.sqrt(1.0 - jnp.exp(2.0 * log_a)) * (i * xc)
    return a, bterm


def _lru_prompt_kernel(x_ref, cw_ref, cb_ref, wa_ref, ba_ref, wx_ref, bx_ref, lam_ref, y_ref,
                       xbuf_scr, h_scr):
    c = pl.program_id(1)
    T = LRU_CHUNK
    PADR = 8

    @pl.when(c == 0)
    def _():
        xbuf_scr[0:PADR, :] = jnp.zeros((PADR, W_BR), f32)
        h_scr[...] = jnp.zeros_like(h_scr)

    x = x_ref[0]
    xbuf_scr[PADR:PADR + T, :] = x
    xc = cb_ref[...] + x * cw_ref[CONV_W - 1:CONV_W, :]
    for j in range(1, CONV_W):
        xc = xc + xbuf_scr[PADR - j:PADR - j + T, :] * cw_ref[CONV_W - 1 - j:CONV_W - j, :]
    xbuf_scr[0:PADR, :] = x[T - PADR:T, :]
    a, bv = _lru_gates(xc, wa_ref[...], ba_ref[...], wx_ref[...], bx_ref[...], lam_ref[...])
    row = _iota((T, W_BR), 0)
    d = 1
    while d < T:
        keep = row >= d
        a_sh = jnp.where(keep, pltpu.roll(a, d, 0), 1.0)
        b_sh = jnp.where(keep, pltpu.roll(bv, d, 0), 0.0)
        bv = a * b_sh + bv
        a = a * a_sh
        d *= 2
    h = a * h_scr[...] + bv
    y_ref[0] = h
    h_scr[...] = h[T - 1:T, :]


def _lru_prompt(pd, prm, layer):
    b, l, _ = pd.shape
    T = LRU_CHUNK
    return pl.pallas_call(
        _lru_prompt_kernel,
        grid=(b, l // T),
        in_specs=[pl.BlockSpec((1, T, W_BR), lambda i, c: (i, c, 0))]
        + [_lspec(prm[k], layer) for k in LRU_PARAMS],
        out_specs=pl.BlockSpec((1, T, W_BR), lambda i, c: (i, c, 0)),
        out_shape=jax.ShapeDtypeStruct((b, l, W_BR), f32),
        scratch_shapes=[pltpu.VMEM((T + 8, W_BR), f32), pltpu.VMEM((1, W_BR), f32)],
        compiler_params=pltpu.CompilerParams(dimension_semantics=("arbitrary", "arbitrary"),
                                             vmem_limit_bytes=VMEM_LIMIT),
        name="lru_prompt",
    )(pd, *[prm[k] for k in LRU_PARAMS])


def _colbcast(row, n_out, eye_bf16):
    c = row.shape[1]
    hi, lo = _split_bf16(row)
    return (_dg(eye_bf16, jnp.broadcast_to(hi, (n_out, c)), NT)
            + _dg(eye_bf16, jnp.broadcast_to(lo, (n_out, c)), NT))


def _decode_states_kernel(pa_ref, pb_ref, shift_ref, swkv_ref, sgla_ref,
                          mu_ref, w0_ref, wup_ref, a0_ref, aup_ref, kkw_ref, kaw_ref, rk_ref, lng_ref, lnb_ref,
                          gup_ref, gbias_ref, gng_ref, *rest, n_prev):
    earlier, rest = rest[:2 if n_prev else 0], rest[2 if n_prev else 0:]
    oa_ref, ob_ref, swkv_all, sgla_all, fa_scr, fb_scr, vb_scr, oa_scr, ob_scr = rest
    for src, dst in zip(earlier, (swkv_all, sgla_all)):
        dst[0:n_prev] = src[...]
    swkv_out, sgla_out = swkv_all.at[n_prev], sgla_all.at[n_prev]
    h = pl.program_id(0)
    hk = H_B * DK_B

    @pl.when(h == 0)
    def _():
        u = pa_ref[...]
        us = u + (shift_ref[...] - u) * mu_ref[...]
        r, kmod, v, ld, kk, a = _rwkv_features(us, w0_ref[...], wup_ref[...], a0_ref[...], aup_ref[...],
                                               kkw_ref[...], kaw_ref[...], _head_ones())
        for i, t in enumerate((r, kmod, v, jnp.exp(ld), kk, kk * a)):
            fa_scr[i] = t.T
        fb = pb_ref[...]
        gb = _log_sigmoid(_bdot(fb[:, 2 * hk + W_BR:B_PAD], gup_ref[...]) + gbias_ref[...]) * (1.0 / GLA_TAU)
        for i, t in enumerate((fb[:, 0:hk] * (DK_B ** -0.5), fb[:, hk:2 * hk], jnp.exp(gb))):
            fb_scr[i] = t.T
        vb_scr[...] = fb[:, 2 * hk:2 * hk + W_BR].T

    hs = pl.ds(pl.multiple_of(h * HEAD_A, HEAD_A), HEAD_A)
    r_h, km_h, v_h, w_h, kk_h, kka_h = (fa_scr[i, hs, :] for i in range(6))
    sub = _iota((8, r_h.shape[1]), 0)
    for g in range(HEAD_A // 8):
        rows8 = jnp.zeros((8, r_h.shape[1]), f32)
        for j in range(8):
            vi = g * 8 + j
            s = swkv_ref[vi]
            sa = -jnp.sum(s * kk_h, axis=0, keepdims=True)
            s_new = s * w_h + sa * kka_h + v_h[vi:vi + 1, :] * km_h
            swkv_out[vi] = s_new
            rows8 = jnp.where(sub == j, jnp.sum(s_new * r_h, axis=0, keepdims=True), rows8)
        oa_scr[pl.ds(pl.multiple_of(h * HEAD_A + g * 8, 8), 8), :] = rows8
    ds_ = pl.ds(pl.multiple_of(h * DK_B, DK_B), DK_B)
    q_h, k_h, eg_h = (fb_scr[i, ds_, :] for i in range(3))
    vb_h = vb_scr[hs, :]
    acc = jnp.zeros_like(vb_h)
    for d in range(DK_B):
        s_new = eg_h[d:d + 1, :] * sgla_ref[d] + k_h[d:d + 1, :] * vb_h
        sgla_out[d] = s_new
        acc = acc + q_h[d:d + 1, :] * s_new
    ob_scr[hs, :] = acc

    @pl.when(h == pl.num_programs(0) - 1)
    def _():
        oa_ref[...] = oa_scr[...].T
        ob_ref[...] = ob_scr[...].T


def _decode_states(pa, pb, shift0, swkv_t, sgla_t, prm, layer, earlier):
    assert H_A == H_B and DV_B == HEAD_A
    n = pa.shape[0]
    n_prev = layer
    wkv_dims, gla_dims = (HEAD_A, HEAD_A, n), (DK_B, DV_B, n)
    of_head = lambda dims: pl.BlockSpec((None, None) + dims, lambda h: (layer, h) + (0,) * len(dims))
    stacked = lambda nl, dims: pl.BlockSpec((nl, None) + dims, lambda h: (0, h) + (0,) * len(dims))
    keys = RWKV_PARAMS + GLA_PARAMS
    in_specs = [_const_spec((n, A_PAD)), _const_spec((n, B_PAD)), _const_spec((n, A_PAD)),
                of_head(wkv_dims), of_head(gla_dims)] + [_lspec(prm[k], layer) for k in keys]
    if n_prev:
        in_specs += [stacked(n_prev, wkv_dims), stacked(n_prev, gla_dims)]
    return pl.pallas_call(
        functools.partial(_decode_states_kernel, n_prev=n_prev),
        grid=(H_A,),
        in_specs=in_specs,
        out_specs=[_const_spec((n, W_BR)), _const_spec((n, W_BR)),
                   stacked(n_prev + 1, wkv_dims), stacked(n_prev + 1, gla_dims)],
        out_shape=[jax.ShapeDtypeStruct((n, W_BR), f32), jax.ShapeDtypeStruct((n, W_BR), f32),
                   jax.ShapeDtypeStruct((n_prev + 1, H_A) + wkv_dims, f32),
                   jax.ShapeDtypeStruct((n_prev + 1, H_B) + gla_dims, f32)],
        scratch_shapes=[pltpu.VMEM((6, W_BR, n), f32), pltpu.VMEM((3, H_B * DK_B, n), f32),
                        pltpu.VMEM((W_BR, n), f32), pltpu.VMEM((W_BR, n), f32), pltpu.VMEM((W_BR, n), f32)],
        compiler_params=pltpu.CompilerParams(dimension_semantics=("arbitrary",),
                                             vmem_limit_bytes=VMEM_LIMIT),
        name="decode_states",
    )(pa, pb, shift0, swkv_t, sgla_t, *[prm[k] for k in keys], *(earlier if n_prev else ()))


def _decode_kernel(pa_ref, oa_ref, ob_ref, q8_ref, kvn_ref, pd_ref, shift_ref, kbuf_ref, vbuf_ref,
                   c0_ref, c1_ref, c2_ref, h0_ref,
                   mu_ref, w0_ref, wup_ref, a0_ref, aup_ref, kkw_ref, kaw_ref, rk_ref, lng_ref, lnb_ref,
                   gup_ref, gbias_ref, gng_ref, sink8_ref, slope8_ref,
                   cw_ref, cb_ref, wa_ref, ba_ref, wx_ref, bx_ref, lam_ref, *rest, n_prev):
    earlier, rest = rest[:2 if n_prev else 0], rest[2 if n_prev else 0:]
    ya_ref, yb_ref, yc8_ref, yd_ref, kout_all, vout_all = rest
    for src, dst in zip(earlier, (kout_all, vout_all)):
        dst[0:n_prev] = src[...]
    kout_ref, vout_ref = kout_all.at[n_prev], vout_all.at[n_prev]
    bt = pa_ref.shape[0]
    samples = range(bt)
    u = pa_ref[...]
    us = u + (shift_ref[...] - u) * mu_ref[...]
    ones = _head_ones()
    r, kmod, v, _, _, _ = _rwkv_features(us, w0_ref[...], wup_ref[...], a0_ref[...], aup_ref[...],
                                         kkw_ref[...], kaw_ref[...], ones)
    wdist = (WINDOW - _iota((1, WINDOW), 1)).astype(f32)
    scale = HD_C ** -0.5
    sink8 = sink8_ref[...]
    slope8 = slope8_ref[...]
    half = KV_C * HD_C
    kn = lambda b: kvn_ref[b:b + 1, 0:half]
    vn = lambda b: kvn_ref[b:b + 1, half:2 * half]
    scores = [_bdot_nt(q8_ref[b], kbuf_ref[b]) * scale - slope8 * wdist for b in samples]
    probs, tails = [], []
    for b in samples:
        sn = jnp.sum(q8_ref[b] * kn(b), axis=1, keepdims=True) * scale
        m = jnp.maximum(jnp.maximum(jnp.max(scores[b], axis=1, keepdims=True), sn), sink8)
        p = jnp.exp(scores[b] - m)
        pn = jnp.exp(sn - m)
        probs.append(p)
        tails.append((pn, jnp.sum(p, axis=1, keepdims=True) + pn + jnp.exp(sink8 - m)))
    for b in samples:
        pn, den = tails[b]
        yc8_ref[b] = (_bdot(probs[b], vbuf_ref[b]) + pn * vn(b)) / den
    for b in samples:
        kout_ref[b, 0:WINDOW - 1, :] = kbuf_ref[b, 1:WINDOW, :]
        kout_ref[b, WINDOW - 1:WINDOW, :] = kn(b)
        vout_ref[b, 0:WINDOW - 1, :] = vbuf_ref[b, 1:WINDOW, :]
        vout_ref[b, WINDOW - 1:WINDOW, :] = vn(b)

    ya_ref[...] = _rwkv_finish(oa_ref[...], r, kmod, v, rk_ref[...], lng_ref[...], lnb_ref[...], ones)
    ob = ob_ref[...]
    assert DV_B == HEAD_A
    ms = _ones_dot(ob * ob, ones) * (1.0 / DV_B)
    yb_ref[...] = ob * lax.rsqrt(ms + NORM_EPS) * gng_ref[...]
    xd = pd_ref[...]
    xc = (cb_ref[...] + c0_ref[...] * cw_ref[0:1, :] + c1_ref[...] * cw_ref[1:2, :]
          + c2_ref[...] * cw_ref[2:3, :] + xd * cw_ref[3:4, :])
    al, bterm = _lru_gates(xc, wa_ref[...], ba_ref[...], wx_ref[...], bx_ref[...], lam_ref[...])
    yd_ref[...] = al * h0_ref[...] + bterm


def _decode(pa, oa, ob, pc, pd, shift0, kbuf, vbuf, c0, c1, c2, h0, prm, layer, earlier, bt):
    n = pa.shape[0]
    n_prev = layer
    half = KV_C * HD_C
    q4 = pc[:, 0:W_BR].reshape(n, H_C, HD_C)
    q8 = jnp.concatenate(
        [jnp.pad(q4[:, h:h + 1], ((0, 0), (0, 0), ((h // G_C) * HD_C, half - HD_C - (h // G_C) * HD_C)))
         for h in range(H_C)] + [jnp.zeros((n, 8 - H_C, half), f32)], axis=1)
    kvn = pc[:, W_BR:]
    slope8 = jnp.asarray(ALIBI_SLOPES + (0.0,) * (8 - H_C), f32).reshape(8, 1)
    rows = lambda w: pl.BlockSpec((bt, w), lambda i: (i, 0))
    cube = lambda d1, d2: pl.BlockSpec((bt, d1, d2), lambda i: (i, 0, 0))
    stacked = lambda nl: pl.BlockSpec((nl, bt, WINDOW, half), lambda i: (0, i, 0, 0))
    in_specs = [rows(A_PAD), rows(W_BR), rows(W_BR), cube(8, half), rows(2 * half), rows(D_COLS), rows(A_PAD),
                cube(WINDOW, half), cube(WINDOW, half), rows(W_BR), rows(W_BR), rows(W_BR), rows(W_BR)]
    in_specs += [_lspec(prm[k], layer) for k in RWKV_PARAMS + GLA_PARAMS]
    in_specs += [_lspec(prm["sink8"], layer), _const_spec((8, 1))]
    in_specs += [_lspec(prm[k], layer) for k in LRU_PARAMS]
    if n_prev:
        in_specs += [stacked(n_prev)] * 2
    out_specs = [rows(W_BR), rows(W_BR), cube(8, half), rows(W_BR)] + [stacked(n_prev + 1)] * 2
    out_shape = [jax.ShapeDtypeStruct((n, W_BR), f32), jax.ShapeDtypeStruct((n, W_BR), f32),
                 jax.ShapeDtypeStruct((n, 8, half), f32), jax.ShapeDtypeStruct((n, W_BR), f32)]
    out_shape += [jax.ShapeDtypeStruct((n_prev + 1, n, WINDOW, half), f32)] * 2
    ya, yb, yc8, yd, k1, v1 = pl.pallas_call(
        functools.partial(_decode_kernel, n_prev=n_prev),
        grid=(n // bt,),
        in_specs=in_specs,
        out_specs=out_specs,
        out_shape=out_shape,
        compiler_params=pltpu.CompilerParams(dimension_semantics=("arbitrary",),
                                             vmem_limit_bytes=VMEM_LIMIT),
        name="decode_mixers",
    )(pa, oa, ob, q8, kvn, pd, shift0, kbuf, vbuf, c0, c1, c2, h0,
      *[prm[k] for k in RWKV_PARAMS + GLA_PARAMS], prm["sink8"], slope8, *[prm[k] for k in LRU_PARAMS],
      *(earlier if n_prev else ()))
    yc = jnp.concatenate([yc8[:, h, (h // G_C) * HD_C:(h // G_C + 1) * HD_C] for h in range(H_C)], axis=1)
    return ya, yb, yc, yd, k1, v1


def _merge_kernel(x_ref, ya_ref, yb_ref, yc_ref, yd_ref, g_ref, w_ref, wbr_ref, wout_ref,
                  fg_ref, o_ref, *, final):
    x = x_ref[...]
    hn = _rms(x, g_ref[...]).astype(bf16)
    ys = (ya_ref, yb_ref, yc_ref, yd_ref)
    merged = None
    for n in range(N_BRANCH):
        z = _dg(hn, w_ref[0, n * W_BR:(n + 1) * W_BR, :], NT)
        yz = ys[n][...] * (z * _sigmoid(z))
        br = jnp.dot(yz.astype(bf16), wbr_ref[n], preferred_element_type=f32)
        gate = _sigmoid(_dg(hn, w_ref[0, Z_COLS + n * D_MODEL:Z_COLS + (n + 1) * D_MODEL, :], NT))
        merged = gate * br if merged is None else merged + gate * br
    out = x + jnp.dot(merged.astype(bf16), wout_ref[...], preferred_element_type=f32)
    if final:
        out = _rms(out, fg_ref[...])
    o_ref[...] = out


def _merge(x, ya, yb, yc, yd, g, wt, wbr, wout, fg, layer, tm, final):
    m = x.shape[0]
    tile = lambda w: pl.BlockSpec((tm, w), lambda i: (i, 0))
    per_layer = lambda shape: pl.BlockSpec((None,) + shape, lambda *_: (layer,) + (0,) * len(shape),
                                           pipeline_mode=pl.Buffered(1))
    return pl.pallas_call(
        functools.partial(_merge_kernel, final=final),
        grid=(m // tm,),
        in_specs=[tile(D_MODEL), tile(W_BR), tile(W_BR), tile(W_BR), tile(W_BR), _lspec(g, layer),
                  _wt_spec(layer, MIX_ROWS, GATE_ROWS, pipeline_mode=pl.Buffered(1)),
                  per_layer((N_BRANCH, W_BR, D_MODEL)),
                  per_layer((D_MODEL, D_MODEL)), _const_spec((1, D_MODEL))],
        out_specs=tile(D_MODEL),
        out_shape=jax.ShapeDtypeStruct((m, D_MODEL), f32),
        compiler_params=pltpu.CompilerParams(dimension_semantics=("arbitrary",),
                                             vmem_limit_bytes=VMEM_LIMIT),
        name="merge_final" if final else "merge",
    )(x, ya, yb, yc, yd, g, wt, wbr, wout, fg)


def _pad_cols(w, width):
    return jnp.pad(w, ((0, 0), (0, width - w.shape[1])))


def _pad_rows_at(w, start, total):
    return jnp.pad(w, ((start, total - start - w.shape[0]), (0, 0)))


def _block_diag(w):
    nb, bs, _ = w.shape
    out = jnp.zeros((nb * bs, nb * bs), w.dtype)
    for n in range(nb):
        out = out.at[n * bs:(n + 1) * bs, n * bs:(n + 1) * bs].set(w[n])
    return out


def _lspec(arr, layer):
    return pl.BlockSpec((None,) + arr.shape[1:], lambda *_: (layer,) + (0,) * (arr.ndim - 1))


def _stacked_params(norm_g, mu_shift, w0, w_decay_up, a0, a_icl_up, k_k, k_a, r_k, ln_x_g, ln_x_b,
                    gla_gate_up, gla_gate_b, gla_norm_g, swa_sinks, lru_conv_w, lru_conv_b, lru_wa, lru_ba,
                    lru_wx, lru_bx, lru_lambda):
    depth = norm_g.shape[0]
    row = lambda t: t.reshape(depth, 1, -1)
    pad_rows = lambda w, start: jnp.pad(w, ((0, 0), (start, LANE - start - w.shape[1]), (0, 0)))
    dense = lambda w: jnp.einsum("lnij,nm->lnimj", w, jnp.eye(w.shape[1], dtype=w.dtype)).reshape(
        depth, W_BR, W_BR).astype(bf16)
    return dict(
        g=row(norm_g),
        mu=row(jnp.pad(mu_shift, ((0, 0), (0, A_PAD - A_COLS)))),
        w0=row(w0), wup=pad_rows(w_decay_up, 0), a0=row(a0), aup=pad_rows(a_icl_up, R_DECAY),
        kkw=row(k_k), kaw=row(k_a), rk=row(r_k), lng=row(ln_x_g), lnb=row(ln_x_b),
        gup=pad_rows(gla_gate_up, 0), gbias=row(gla_gate_b), gng=row(jnp.tile(gla_norm_g, (1, H_B))),
        sinks=row(swa_sinks),
        sink8=jnp.pad(swa_sinks.reshape(depth, H_C, 1), ((0, 0), (0, 8 - H_C), (0, 0))),
        cw=lru_conv_w, cb=row(lru_conv_b),
        lwa=dense(lru_wa), lba=row(lru_ba), lwx=dense(lru_wx), lbx=row(lru_bx), lam=row(lru_lambda),
    )


def kernel(x_prompt, x_sample, state_wkv, state_shift, state_gla, cache_swa_k, cache_swa_v, state_lru_conv, state_lru_h, norm_g, w_in, mu_shift, w0, w_decay_up, a0, a_icl_up, k_k, k_a, r_k, ln_x_g, ln_x_b, gla_gate_up, gla_gate_b, gla_norm_g, swa_sinks, lru_conv_w, lru_conv_b, lru_wa, lru_ba, lru_wx, lru_bx, lru_lambda, w_branch, w_out, final_norm_g):
    bp, lp, _ = x_prompt.shape
    bs = x_sample.shape[0]
    depth = w_in.shape[0]
    fg = final_norm_g.reshape(1, -1)
    xp = x_prompt.reshape(bp * lp, D_MODEL)
    xs = x_sample.reshape(bs, D_MODEL)
    outs_p = [[] for _ in range(7)]
    outs_s = [[] for _ in range(7)]
    wt_all = jnp.swapaxes(w_in, 1, 2).astype(bf16)
    wbr_all = w_branch.astype(bf16)
    wout_all = w_out.astype(bf16)
    stacked_s = None
    prm = _stacked_params(norm_g, mu_shift, w0, w_decay_up, a0, a_icl_up, k_k, k_a, r_k, ln_x_g, ln_x_b,
                          gla_gate_up, gla_gate_b, gla_norm_g, swa_sinks, lru_conv_w, lru_conv_b, lru_wa,
                          lru_ba, lru_wx, lru_bx, lru_lambda)
    shift_pad = jnp.pad(state_shift, ((0, 0), (0, 0), (0, A_PAD - A_COLS)))
    wkv_t = jnp.transpose(state_wkv, (0, 2, 3, 4, 1))
    gla_t = jnp.transpose(state_gla, (0, 2, 3, 4, 1))
    stacked_states = None
    for l in range(depth):
        final = l == depth - 1
        pa, pb, pc, pd = _inproj(xp, prm["g"], wt_all, l, tm=512)
        pa3, pb3 = pa.reshape(bp, lp, A_PAD), pb.reshape(bp, lp, B_PAD)
        pc3, pd3 = pc.reshape(bp, lp, C_COLS), pd.reshape(bp, lp, D_COLS)
        ya, st_t = _rwkv_prompt(pa3, prm, l)
        yb, sgla = _gla_prompt(pb3, prm, l)
        yc = _swa_prompt(pc3, prm, l)
        yd = _lru_prompt(pd3, prm, l)
        flat = lambda t: t.reshape(bp * lp, W_BR)
        xp = _merge(xp, flat(ya), flat(yb), flat(yc), flat(yd), prm["g"], wt_all, wbr_all, wout_all, fg, l,
                    tm=512, final=final)
        kv = pc3[:, lp - WINDOW:, H_C * HD_C:]
        outs_p[0].append(jnp.swapaxes(st_t, -1, -2))
        outs_p[1].append(pa3[:, lp - 1, :A_COLS])
        outs_p[2].append(sgla)
        outs_p[3].append(kv[:, :, :KV_C * HD_C].reshape(bp, WINDOW, KV_C, HD_C))
        outs_p[4].append(kv[:, :, KV_C * HD_C:].reshape(bp, WINDOW, KV_C, HD_C))
        outs_p[5].append(pd3[:, lp - (CONV_W - 1):, :])
        outs_p[6].append(yd[:, lp - 1, :])
        sa, sb, sc, sd = _inproj(xs, prm["g"], wt_all, l, tm=bs)
        kbuf = cache_swa_k[l].reshape(bs, WINDOW, KV_C * HD_C)
        vbuf = cache_swa_v[l].reshape(bs, WINDOW, KV_C * HD_C)
        conv0 = state_lru_conv[l]
        oa_s, ob_s, *stacked_states = _decode_states(sa, sb, shift_pad[l], wkv_t, gla_t, prm, l, stacked_states)
        ya_s, yb_s, yc_s, yd_s, *stacked_s = _decode(
            sa, oa_s, ob_s, sc, sd, shift_pad[l], kbuf, vbuf,
            conv0[:, 0], conv0[:, 1], conv0[:, 2], state_lru_h[l], prm, l, stacked_s, bt=8)
        xs = _merge(xs, ya_s, yb_s, yc_s, yd_s, prm["g"], wt_all, wbr_all, wout_all, fg, l, tm=bs, final=final)
        outs_s[1].append(sa[:, :A_COLS])
        outs_s[5].append(jnp.stack([conv0[:, 1], conv0[:, 2], sd], axis=1))
        outs_s[6].append(yd_s)
    y_prompt = xp.reshape(bp, lp, D_MODEL)
    y_sample = xs.reshape(bs, 1, D_MODEL)
    sp = [jnp.stack(t) for t in outs_p]
    wkv_s, gla_s = (jnp.transpose(t, (0, 4, 1, 2, 3)) for t in stacked_states)
    k_s, v_s = stacked_s
    k_s =k_s.reshape(depth, bs, WINDOW, KV_C, HD_C)
    v_s = v_s.reshape(depth, bs, WINDOW, KV_C, HD_C)
    shift_s, conv_s, h_s = (jnp.stack(outs_s[i]) for i in (1, 5, 6))
    return (y_prompt, y_sample, sp[0], wkv_s, sp[1], shift_s, sp[2], gla_s, sp[3], k_s, sp[4], v_s,
            sp[5], conv_s, sp[6], h_s)
```

```python
import functools

import jax
import jax.numpy as jnp
from jax import lax
from jax.experimental import pallas as pl
from jax.experimental.pallas import tpu as pltpu

f32 = jnp.float32
bf16 = jnp.bfloat16
HI = lax.Precision.HIGHEST

D_MODEL = 1024
N_BRANCH = 4
W_BR = 256
HEAD_A = 64
H_A = 4
R_DECAY = 32
R_ICL = 32
GN_EPS_A = 64e-5
H_B = 4
DK_B = 32
DV_B = 64
R_GATE_B = 16
GLA_TAU = 16.0
GLA_CHUNK = 64
GLA_STEP_ROWS = 256
H_C = 4
KV_C = 2
HD_C = 64
G_C = 2
WINDOW = 128
SWA_STEP_BLOCKS = 4
CONV_W = 4
C_RG = 8.0
NORM_EPS = 1e-6

A_COLS = 3 * W_BR + R_DECAY + R_ICL
B_COLS = 2 * H_B * DK_B + W_BR + R_GATE_B
C_COLS = H_C * HD_C + 2 * KV_C * HD_C
D_COLS = W_BR
Z_COLS = N_BRANCH * W_BR
G_COLS = N_BRANCH * D_MODEL

LANE = 128
A_PAD = 7 * LANE
B_PAD = 5 * LANE
RWKV_CHUNK = 64
RWKV_STEP_ROWS = 256
RWKV_STEP_BATCH = 4
RWKV_PASSES = 1
GLA_PASSES = 1
DECODE_PASSES = 1
LRU_CHUNK = 256
VMEM_LIMIT = 56 * 1024 * 1024
NEG_BIG = -1e30

ALIBI_SLOPES = tuple(2.0 ** (-8.0 * (h + 1) / H_C) for h in range(H_C))


def _mm(a, b):
    return jnp.dot(a, b, precision=HI, preferred_element_type=f32)


def _mm_nt(a, b):
    return lax.dot_general(a, b, (((1,), (1,)), ((), ())), precision=HI, preferred_element_type=f32)


def _mm_tn(a, b):
    return lax.dot_general(a, b, (((0,), (0,)), ((), ())), precision=HI, preferred_element_type=f32)


def _bdot(a, b):
    return jnp.dot(a.astype(bf16), b.astype(bf16), preferred_element_type=f32)


def _bdot_nt(a, b):
    return lax.dot_general(a.astype(bf16), b.astype(bf16), (((1,), (1,)), ((), ())),
                           preferred_element_type=f32)


NN = ((1,), (0,))
NT = ((1,), (1,))
TN = ((0,), (0,))


def _dg(a, b, dims):
    return lax.dot_general(a, b, (dims, ((), ())), preferred_element_type=f32)


def _split_bf16(a):
    hi = a.astype(bf16)
    return hi, (a - hi.astype(f32)).astype(bf16)


def _dotp(a, b, dims, passes):
    if passes == 1:
        return _dg(a.astype(bf16), b.astype(bf16), dims)
    ah, al = _split_bf16(a)
    bh, bl = _split_bf16(b)
    return _dg(ah, bh, dims) + (_dg(ah, bl, dims) + _dg(al, bh, dims))


def _ones_dot(a, ones_bf16):
    ah, al = _split_bf16(a)
    return _dg(ah, ones_bf16, NN) + _dg(al, ones_bf16, NN)


def _cumsum_rows(x, seg, pos=None):
    if pos is None:
        pos = _iota(x.shape, 0) % seg
    d = 1
    while d < seg:
        x = x + jnp.where(pos >= d, pltpu.roll(x, d, 0), 0.0)
        d *= 2
    return x


def _iota(shape, dim):
    return lax.broadcasted_iota(jnp.int32, shape, dim)


def _eye(n):
    return (_iota((n, n), 0) == _iota((n, n), 1)).astype(f32)


def _block_ones(n, blk):
    return ((_iota((n, n), 0) // blk) == (_iota((n, n), 1) // blk)).astype(f32)


def _tril_ones(n):
    return (_iota((n, n), 0) >= _iota((n, n), 1)).astype(f32)


def _softplus(x):
    return jnp.maximum(x, 0.0) + jnp.log(1.0 + jnp.exp(-jnp.abs(x)))


def _log_sigmoid(x):
    return -_softplus(-x)


def _sigmoid(x):
    return 1.0 / (1.0 + jnp.exp(-x))


def _rms(x, g):
    return x * lax.rsqrt(jnp.mean(x * x, -1, keepdims=True) + NORM_EPS) * g


MIX_COLS = (A_COLS, B_COLS, C_COLS, D_COLS)
MIX_WIDTHS = (A_PAD, B_PAD, C_COLS, D_COLS)
MIX_ROWS = sum(MIX_COLS)
GATE_ROWS = Z_COLS + G_COLS


def _wt_spec(layer, start, rows, **kw):
    return pl.BlockSpec((pl.Element(1), pl.Element(rows), pl.Element(D_MODEL)), lambda *_: (layer, start, 0), **kw)


def _inproj_kernel(x_ref, g_ref, wt_ref, oa_ref, ob_ref, oc_ref, od_ref):
    hn = _rms(x_ref[...], g_ref[...]).astype(bf16)
    start = 0
    for o_ref, cols, width in zip((oa_ref, ob_ref, oc_ref, od_ref), MIX_COLS, MIX_WIDTHS):
        o_ref[:, 0:cols] = _dg(hn, wt_ref[0, start:start + cols, :], NT)
        if width > cols:
            o_ref[:, cols:width] = jnp.zeros((o_ref.shape[0], width - cols), f32)
        start += cols


def _const_spec(shape):
    return pl.BlockSpec(shape, lambda *_: (0,) * len(shape))


def _inproj(x, g, wt, layer, tm):
    m = x.shape[0]
    return pl.pallas_call(
        _inproj_kernel,
        grid=(m // tm,),
        in_specs=[pl.BlockSpec((tm, D_MODEL), lambda i: (i, 0)), _lspec(g, layer), _wt_spec(layer, 0, MIX_ROWS)],
        out_specs=[pl.BlockSpec((tm, w), lambda i: (i, 0)) for w in MIX_WIDTHS],
        out_shape=[jax.ShapeDtypeStruct((m, w), f32) for w in MIX_WIDTHS],
        compiler_params=pltpu.CompilerParams(dimension_semantics=("arbitrary",),
                                             vmem_limit_bytes=VMEM_LIMIT),
        name="inproj",
    )(x, g, wt)


def _head_ones():
    return _block_ones(W_BR, HEAD_A).astype(bf16)


def _rwkv_features(us, w0, wup, a0, aup, kk_w, ka_w, ones):
    r = us[:, 0:W_BR]
    k = us[:, W_BR:2 * W_BR]
    v = us[:, 2 * W_BR:3 * W_BR]
    lora = us[:, 3 * W_BR:A_PAD]
    w = -_softplus(-(w0 + _bdot(jnp.tanh(lora), wup))) - 0.5
    logdecay = -jnp.exp(w)
    a = _sigmoid(a0 + _bdot(lora, aup))
    kk = k * kk_w
    ss = _ones_dot(kk * kk, ones)
    kk = kk / jnp.maximum(jnp.sqrt(ss), 1e-12)
    kmod = k * (1.0 + (a - 1.0) * ka_w)
    return r, kmod, v, logdecay, kk, a


def _rwkv_finish(o, r, kmod, v, rk, lng, lnb, ones):
    mean = _ones_dot(o, ones) * (1.0 / HEAD_A)
    cen = o - mean
    var = _ones_dot(cen * cen, ones) * (1.0 / HEAD_A)
    o = cen * lax.rsqrt(var + GN_EPS_A) * lng + lnb
    bonus = _ones_dot(r * kmod * rk, ones) * v
    return o + bonus


def _rwkv_prompt_kernel(u_ref, mu_ref, w0_ref, wup_ref, a0_ref, aup_ref, kkw_ref, kaw_ref, rk_ref,
                        lng_ref, lnb_ref, y_ref, s_ref, st_scr, prev_scr, o_scr):
    step = pl.program_id(1)
    T = RWKV_CHUNK
    TT = RWKV_STEP_ROWS

    @pl.when(step == 0)
    def _():
        st_scr[...] = jnp.zeros_like(st_scr)
        prev_scr[...] = jnp.zeros_like(prev_scr)

    R = RWKV_STEP_BATCH
    P = RWKV_PASSES
    NC = TT // T
    pieces = [(c, h) for c in range(NC) for h in range(H_A)]
    row0 = _iota((TT, A_PAD), 0) == 0
    pos = _iota((TT, W_BR), 0) % T
    ones = _head_ones()

    def features(q):
        u = u_ref[q]
        u_prev = jnp.where(row0, prev_scr[q], pltpu.roll(u, 1, 0))
        prev_scr[q] = u[TT - 1:TT, :]
        us = u + (u_prev - u) * mu_ref[...]
        r, kmod, v, ld, kk, a = _rwkv_features(us, w0_ref[...], wup_ref[...], a0_ref[...], aup_ref[...],
                                               kkw_ref[...], kaw_ref[...], ones)
        cum = _cumsum_rows(ld, T, pos)
        cum_last = jnp.concatenate(
            [jnp.broadcast_to(cum[(c + 1) * T - 1:(c + 1) * T, :], (T, W_BR)) for c in range(NC)], axis=0)
        g_inv = jnp.exp(-cum)
        g_tail = jnp.exp(cum_last - cum)
        kka = kk * a
        return dict(r=r, kmod=kmod, v=v, cum=cum, at=-kk * jnp.exp(cum - ld), rt=r * jnp.exp(cum),
                    bt=kka * g_inv, kt=kmod * g_inv, btg=kka * g_tail, ktg=kmod * g_tail)

    ri = _iota((2 * T, 2 * T), 0)
    ci = _iota((2 * T, 2 * T), 1)
    ti = jnp.where(ri >= T, ri - T, ri)
    si = jnp.where(ci >= T, ci - T, ci)
    keep = (ti > si) | ((ri >= T) & (ti == si))
    eye_t = _eye(T)
    eye_h = _eye(HEAD_A)

    def independent_stages(f):
        d = dict(ar={}, vh={}, pm={}, x={}, pw={}, lv={}, gcol={}, bkg={}, xar={})

        def products():
            for c, h in pieces:
                rows = slice(c * T, (c + 1) * T)
                sl = slice(h * HEAD_A, (h + 1) * HEAD_A)
                d["vh"][c, h] = f["v"][rows, sl]
                d["ar"][c, h] = jnp.concatenate([f["at"][rows, sl], f["rt"][rows, sl]], axis=0)
                bk = jnp.concatenate([f["bt"][rows, sl], f["kt"][rows, sl]], axis=0)
                d["pm"][c, h] = jnp.where(keep, _dotp(d["ar"][c, h], bk, NT, P), 0.0)

        def squares_and_values():
            for c, h in pieces:
                rows = slice(c * T, (c + 1) * T)
                sl = slice(h * HEAD_A, (h + 1) * HEAD_A)
                lab = d["pm"][c, h][0:T, 0:T]
                d["x"][c, h] = eye_t + lab
                d["pw"][c, h] = _dotp(lab, lab, NN, P)
                d["lv"][c, h] = _dotp(d["pm"][c, h][:, T:2 * T], d["vh"][c, h], NN, P)
                g_last = jnp.exp(f["cum"][(c + 1) * T - 1:(c + 1) * T, sl])
                d["gcol"][c, h] = jnp.sum(eye_h * g_last, axis=1, keepdims=True)
                d["bkg"][c, h] = jnp.concatenate([f["btg"][rows, sl], f["ktg"][rows, sl]], axis=0)

        def inverse_round(last):
            def run():
                for c, h in pieces:
                    x_next = d["x"][c, h] + _dotp(d["pw"][c, h], d["x"][c, h], NN, P)
                    if not last:
                        d["pw"][c, h] = _dotp(d["pw"][c, h], d["pw"][c, h], NN, P)
                    d["x"][c, h] = x_next
            return run

        def fold_inverse():
            for c, h in pieces:
                xa = _dotp(d["x"][c, h], d["ar"][c, h][0:T], NN, P)
                xl = _dotp(d["x"][c, h], d["lv"][c, h][0:T], NN, P)
                d["xar"][c, h] = jnp.concatenate([xa, d["ar"][c, h][T:2 * T]], axis=0)
                d["lv"][c, h] = jnp.concatenate([xl, d["lv"][c, h][T:2 * T]], axis=0)

        stages = [products, squares_and_values] + [inverse_round(it == 4) for it in range(5)] + [fold_inverse]
        return stages, d

    def dependent_stages(q, f, d):
        st = {}
        base = {}

        def load():
            for h in range(H_A):
                st[h] = st_scr[q, h]

        def read(c):
            def run():
                if c == 0:
                    load()
                for h in range(H_A):
                    base[h] = _dotp(d["xar"][c, h], st[h], NN, P) + d["lv"][c, h]
            return run

        def update(c):
            def run():
                for h in range(H_A):
                    st[h] = d["gcol"][c, h] * st[h] + _dotp(
                        d["bkg"][c, h], jnp.concatenate([base[h][0:T], d["vh"][c, h]], axis=0), TN, P)
                for h in range(H_A):
                    o_scr[q, c * T:(c + 1) * T, h * HEAD_A:(h + 1) * HEAD_A] = (
                        base[h][T:2 * T] + _dotp(d["pm"][c, h][T:2 * T, 0:T], base[h][0:T], NN, P))
                if c == NC - 1:
                    for h in range(H_A):
                        st_scr[q, h] = st[h]
                    y_ref[q] = _rwkv_finish(o_scr[q], f["r"], f["kmod"], f["v"], rk_ref[...], lng_ref[...],
                                            lnb_ref[...], ones)
            return run

        return [stage for c in range(NC) for stage in (read(c), update(c))]

    feats = {q: features(q) for q in range(min(2, R))}
    pending = []
    for q in range(R):
        stages, d = independent_stages(feats[q])
        if q + 2 < R:
            feats[q + 2] = features(q + 2)
        for k in range(max(len(stages), len(pending))):
            if k < len(stages):
                stages[k]()
            if k < len(pending):
                pending[k]()
        pending = dependent_stages(q, feats[q], d)
    for stage in pending:
        stage()

    @pl.when(step == pl.num_programs(1) - 1)
    def _():
        s_ref[...] = st_scr[...]


RWKV_PARAMS = ("mu", "w0", "wup", "a0", "aup", "kkw", "kaw", "rk", "lng", "lnb")
GLA_PARAMS = ("gup", "gbias", "gng")
LRU_PARAMS = ("cw", "cb", "lwa", "lba", "lwx", "lbx", "lam")


def _rwkv_prompt(pa, prm, layer):
    b, l, _ = pa.shape
    T = RWKV_STEP_ROWS
    R = RWKV_STEP_BATCH
    return pl.pallas_call(
        _rwkv_prompt_kernel,
        grid=(b // R, l // T),
        in_specs=[pl.BlockSpec((R, T, A_PAD), lambda i, c: (i, c, 0))]
        + [_lspec(prm[k], layer) for k in RWKV_PARAMS],
        out_specs=[pl.BlockSpec((R, T, W_BR), lambda i, c: (i, c, 0)),
                   pl.BlockSpec((R, H_A, HEAD_A, HEAD_A), lambda i, c: (i, 0, 0, 0))],
        out_shape=[jax.ShapeDtypeStruct((b, l, W_BR), f32),
                   jax.ShapeDtypeStruct((b, H_A, HEAD_A, HEAD_A), f32)],
        scratch_shapes=[pltpu.VMEM((R, H_A, HEAD_A, HEAD_A), f32), pltpu.VMEM((R, 1, A_PAD), f32),
                        pltpu.VMEM((R, T, W_BR), f32)],
        compiler_params=pltpu.CompilerParams(dimension_semantics=("arbitrary", "arbitrary"),
                                             vmem_limit_bytes=VMEM_LIMIT),
        name="rwkv_prompt",
    )(pa, *[prm[k] for k in RWKV_PARAMS])


def _gla_prompt_kernel(f_ref, up_ref, bias_ref, ng_ref, y_ref, s_ref, s_scr, o_scr):
    step = pl.program_id(1)
    T = GLA_CHUNK
    TT = GLA_STEP_ROWS
    NC = TT // T

    @pl.when(step == 0)
    def _():
        s_scr[...] = jnp.zeros_like(s_scr)

    f = f_ref[0]
    hk = H_B * DK_B
    q = f[:, 0:hk] * (DK_B ** -0.5)
    k = f[:, hk:2 * hk]
    v = f[:, 2 * hk:2 * hk + W_BR]
    gl = f[:, 2 * hk + W_BR:B_PAD]
    P = GLA_PASSES
    g = _log_sigmoid(_bdot(gl, up_ref[...]) + bias_ref[...]) * (1.0 / GLA_TAU)
    bcum = _cumsum_rows(g, T)
    b_last = jnp.concatenate(
        [jnp.broadcast_to(bcum[(c + 1) * T - 1:(c + 1) * T, :], (T, hk)) for c in range(NC)], axis=0)
    qe = q * jnp.exp(bcum)
    ke = k * jnp.exp(-bcum)
    kl = k * jnp.exp(b_last - bcum)
    causal = _iota((T, T), 0) >= _iota((T, T), 1)
    eye_k = _eye(DK_B)
    pieces = [(c, h) for c in range(NC) for h in range(H_B)]
    av, kv, ecol = {}, {}, {}
    for c, h in pieces:
        rows = slice(c * T, (c + 1) * T)
        ks = slice(h * DK_B, (h + 1) * DK_B)
        vs = slice(h * DV_B, (h + 1) * DV_B)
        att = jnp.where(causal, _dotp(qe[rows, ks], ke[rows, ks], NT, P), 0.0)
        av[c, h] = _dotp(att, v[rows, vs], NN, P)
        kv[c, h] = _dotp(kl[rows, ks], v[rows, vs], TN, P)
        e_last = jnp.exp(bcum[(c + 1) * T - 1:(c + 1) * T, ks])
        ecol[c, h] = jnp.sum(eye_k * e_last, axis=1, keepdims=True)
    s = [s_scr[h] for h in range(H_B)]
    for c in range(NC):
        rows = slice(c * T, (c + 1) * T)
        for h in range(H_B):
            ks = slice(h * DK_B, (h + 1) * DK_B)
            o_scr[rows, h * DV_B:(h + 1) * DV_B] = av[c, h] + _dotp(qe[rows, ks], s[h], NN, P)
            s[h] = ecol[c, h] * s[h] + kv[c, h]
    for h in range(H_B):
        s_scr[h] = s[h]
    o = o_scr[...]
    ms = _ones_dot(o * o, _block_ones(W_BR, DV_B).astype(bf16)) * (1.0 / DV_B)
    y_ref[0] = o * lax.rsqrt(ms + NORM_EPS) * ng_ref[...]

    @pl.when(step == pl.num_programs(1) - 1)
    def _():
        s_ref[0] = s_scr[...]


def _gla_prompt(pb, prm, layer):
    b, l, _ = pb.shape
    T = GLA_STEP_ROWS
    return pl.pallas_call(
        _gla_prompt_kernel,
        grid=(b, l // T),
        in_specs=[pl.BlockSpec((1, T, B_PAD), lambda i, c: (i, c, 0))]
        + [_lspec(prm[k], layer) for k in GLA_PARAMS],
        out_specs=[pl.BlockSpec((1, T, W_BR), lambda i, c: (i, c, 0)),
                   pl.BlockSpec((1, H_B, DK_B, DV_B), lambda i, c: (i, 0, 0, 0))],
        out_shape=[jax.ShapeDtypeStruct((b, l, W_BR), f32),
                   jax.ShapeDtypeStruct((b, H_B, DK_B, DV_B), f32)],
        scratch_shapes=[pltpu.VMEM((H_B, DK_B, DV_B), f32), pltpu.VMEM((T, W_BR), f32)],
        compiler_params=pltpu.CompilerParams(dimension_semantics=("arbitrary", "arbitrary"),
                                             vmem_limit_bytes=VMEM_LIMIT),
        name="gla_prompt",
    )(pb, *[prm[k] for k in GLA_PARAMS])


def _swa_prompt_kernel(cur_ref, prev_ref, sink_ref, y_ref):
    step = pl.program_id(1)
    W = WINDOW
    NB = SWA_STEP_BLOCKS
    qo, ko, vo = 0, H_C * HD_C, H_C * HD_C + KV_C * HD_C
    assert G_C == 2
    row = _iota((G_C * W, 2 * W), 0)
    s = _iota((G_C * W, 2 * W), 1)
    t = jnp.where(row >= W, row - W, row)
    dist = W + t - s
    ok = (dist >= 0) & (dist <= W)
    ok_first = ok & ((s >= W) | (step > 0))
    distf = dist.astype(f32)
    second = _iota((G_C * W, 1), 0) >= W
    scale = HD_C ** -0.5

    def band(col, j):
        if j == 0:
            return jnp.concatenate([prev_ref[0, :, col:col + HD_C], cur_ref[0, 0:W, col:col + HD_C]], axis=0)
        return cur_ref[0, (j - 1) * W:(j + 1) * W, col:col + HD_C]

    pieces = [(j, g) for j in range(NB) for g in range(KV_C)]
    scores, sinks = {}, {}
    for g in range(KV_C):
        h0, h1 = g * G_C, g * G_C + 1
        sinks[g] = jnp.where(second, sink_ref[:, h1:h1 + 1], sink_ref[:, h0:h0 + 1])
    for j, g in pieces:
        h0, h1 = g * G_C, g * G_C + 1
        q2 = jnp.concatenate([cur_ref[0, j * W:(j + 1) * W, qo + h0 * HD_C:qo + (h0 + 1) * HD_C],
                              cur_ref[0, j * W:(j + 1) * W, qo + h1 * HD_C:qo + (h1 + 1) * HD_C]], axis=0)
        slope = jnp.where(second, ALIBI_SLOPES[h1], ALIBI_SLOPES[h0])
        raw = _bdot_nt(q2, band(ko + g * HD_C, j)) * scale - slope * distf
        scores[j, g] = jnp.where(ok_first if j == 0 else ok, raw, NEG_BIG)
    probs, dens = {}, {}
    for j, g in pieces:
        m = jnp.maximum(jnp.max(scores[j, g], -1, keepdims=True), sinks[g])
        p = jnp.exp(scores[j, g] - m)
        probs[j, g] = p
        dens[j, g] = jnp.sum(p, -1, keepdims=True) + jnp.exp(sinks[g] - m)
    for j, g in pieces:
        out = _bdot(probs[j, g], band(vo + g * HD_C, j)) / dens[j, g]
        for jj in range(G_C):
            h = g * G_C + jj
            y_ref[0, j * W:(j + 1) * W, h * HD_C:(h + 1) * HD_C] = out[jj * W:(jj + 1) * W]


def _swa_prompt(pc, prm, layer):
    b, l, _ = pc.shape
    W = WINDOW
    NB = SWA_STEP_BLOCKS
    return pl.pallas_call(
        _swa_prompt_kernel,
        grid=(b, l // (NB * W)),
        in_specs=[pl.BlockSpec((1, NB * W, C_COLS), lambda i, c: (i, c, 0)),
                  pl.BlockSpec((1, W, C_COLS), lambda i, c: (i, jnp.maximum(NB * c - 1, 0), 0)),
                  _lspec(prm["sinks"], layer)],
        out_specs=pl.BlockSpec((1, NB * W, W_BR), lambda i, c: (i, c, 0)),
        out_shape=jax.ShapeDtypeStruct((b, l, W_BR), f32),
        compiler_params=pltpu.CompilerParams(dimension_semantics=("arbitrary", "arbitrary"),
                                             vmem_limit_bytes=VMEM_LIMIT),
        name="swa_prompt",
    )(pc, pc, prm["sinks"])


def _lru_gates(xc, wa, ba, wx, bx, lam):
    r = _sigmoid(_bdot(xc, wa) + ba)
    i = _sigmoid(_bdot(xc, wx) + bx)
    log_a = C_RG * r * _log_sigmoid(lam)
    a = jnp.exp(log_a)
    bterm = jnp.sqrt(1.0 - jnp.exp(2.0 * log_a)) * (i * xc)
    return a, bterm


def _lru_prompt_kernel(x_ref, cw_ref, cb_ref, wa_ref, ba_ref, wx_ref, bx_ref, lam_ref, y_ref,
                       xbuf_scr, h_scr):
    c = pl.program_id(1)
    T = LRU_CHUNK
    PADR = 8

    @pl.when(c == 0)
    def _():
        xbuf_scr[0:PADR, :] = jnp.zeros((PADR, W_BR), f32)
        h_scr[...] = jnp.zeros_like(h_scr)

    x = x_ref[0]
    xbuf_scr[PADR:PADR + T, :] = x
    xc = cb_ref[...] + x * cw_ref[CONV_W - 1:CONV_W, :]
    for j in range(1, CONV_W):
        xc = xc + xbuf_scr[PADR - j:PADR - j + T, :] * cw_ref[CONV_W - 1 - j:CONV_W - j, :]
    xbuf_scr[0:PADR, :] = x[T - PADR:T, :]
    a, bv = _lru_gates(xc, wa_ref[...], ba_ref[...], wx_ref[...], bx_ref[...], lam_ref[...])
    row = _iota((T, W_BR), 0)
    d = 1
    while d < T:
        keep = row >= d
        a_sh = jnp.where(keep, pltpu.roll(a, d, 0), 1.0)
        b_sh = jnp.where(keep, pltpu.roll(bv, d, 0), 0.0)
        bv = a * b_sh + bv
        a = a * a_sh
        d *= 2
    h = a * h_scr[...] + bv
    y_ref[0] = h
    h_scr[...] = h[T - 1:T, :]


def _lru_prompt(pd, prm, layer):
    b, l, _ = pd.shape
    T = LRU_CHUNK
    return pl.pallas_call(
        _lru_prompt_kernel,
        grid=(b, l // T),
        in_specs=[pl.BlockSpec((1, T, W_BR), lambda i, c: (i, c, 0))]
        + [_lspec(prm[k], layer) for k in LRU_PARAMS],
        out_specs=pl.BlockSpec((1, T, W_BR), lambda i, c: (i, c, 0)),
        out_shape=jax.ShapeDtypeStruct((b, l, W_BR), f32),
        scratch_shapes=[pltpu.VMEM((T + 8, W_BR), f32), pltpu.VMEM((1, W_BR), f32)],
        compiler_params=pltpu.CompilerParams(dimension_semantics=("arbitrary", "arbitrary"),
                                             vmem_limit_bytes=VMEM_LIMIT),
        name="lru_prompt",
    )(pd, *[prm[k] for k in LRU_PARAMS])


def _colbcast(row, n_out, eye_bf16):
    c = row.shape[1]
    hi, lo = _split_bf16(row)
    return (_dg(eye_bf16, jnp.broadcast_to(hi, (n_out, c)), NT)
            + _dg(eye_bf16, jnp.broadcast_to(lo, (n_out, c)), NT))


def _decode_states_kernel(pa_ref, pb_ref, shift_ref, swkv_ref, sgla_ref,
                          mu_ref, w0_ref, wup_ref, a0_ref, aup_ref, kkw_ref, kaw_ref, rk_ref, lng_ref, lnb_ref,
                          gup_ref, gbias_ref, gng_ref, *rest, n_prev):
    earlier, rest = rest[:2 if n_prev else 0], rest[2 if n_prev else 0:]
    oa_ref, ob_ref, swkv_all, sgla_all, fa_scr, fb_scr, vb_scr, oa_scr, ob_scr = rest
    for src, dst in zip(earlier, (swkv_all, sgla_all)):
        dst[0:n_prev] = src[...]
    swkv_out, sgla_out = swkv_all.at[n_prev], sgla_all.at[n_prev]
    h = pl.program_id(0)
    hk = H_B * DK_B

    @pl.when(h == 0)
    def _():
        u = pa_ref[...]
        us = u + (shift_ref[...] - u) * mu_ref[...]
        r, kmod, v, ld, kk, a = _rwkv_features(us, w0_ref[...], wup_ref[...], a0_ref[...], aup_ref[...],
                                               kkw_ref[...], kaw_ref[...], _head_ones())
        for i, t in enumerate((r, kmod, v, jnp.exp(ld), kk, kk * a)):
            fa_scr[i] = t.T
        fb = pb_ref[...]
        gb = _log_sigmoid(_bdot(fb[:, 2 * hk + W_BR:B_PAD], gup_ref[...]) + gbias_ref[...]) * (1.0 / GLA_TAU)
        for i, t in enumerate((fb[:, 0:hk] * (DK_B ** -0.5), fb[:, hk:2 * hk], jnp.exp(gb))):
            fb_scr[i] = t.T
        vb_scr[...] = fb[:, 2 * hk:2 * hk + W_BR].T

    hs = pl.ds(pl.multiple_of(h * HEAD_A, HEAD_A), HEAD_A)
    r_h, km_h, v_h, w_h, kk_h, kka_h = (fa_scr[i, hs, :] for i in range(6))
    sub = _iota((8, r_h.shape[1]), 0)
    for g in range(HEAD_A // 8):
        rows8 = jnp.zeros((8, r_h.shape[1]), f32)
        for j in range(8):
            vi = g * 8 + j
            s = swkv_ref[vi]
            sa = -jnp.sum(s * kk_h, axis=0, keepdims=True)
            s_new = s * w_h + sa * kka_h + v_h[vi:vi + 1, :] * km_h
            swkv_out[vi] = s_new
            rows8 = jnp.where(sub == j, jnp.sum(s_new * r_h, axis=0, keepdims=True), rows8)
        oa_scr[pl.ds(pl.multiple_of(h * HEAD_A + g * 8, 8), 8), :] = rows8
    ds_ = pl.ds(pl.multiple_of(h * DK_B, DK_B), DK_B)
    q_h, k_h, eg_h = (fb_scr[i, ds_, :] for i in range(3))
    vb_h = vb_scr[hs, :]
    acc = jnp.zeros_like(vb_h)
    for d in range(DK_B):
        s_new = eg_h[d:d + 1, :] * sgla_ref[d] + k_h[d:d + 1, :] * vb_h
        sgla_out[d] = s_new
        acc = acc + q_h[d:d + 1, :] * s_new
    ob_scr[hs, :] = acc

    @pl.when(h == pl.num_programs(0) - 1)
    def _():
        oa_ref[...] = oa_scr[...].T
        ob_ref[...] = ob_scr[...].T


def _decode_states(pa, pb, shift0, swkv_t, sgla_t, prm, layer, earlier):
    assert H_A == H_B and DV_B == HEAD_A
    n = pa.shape[0]
    n_prev = layer
    wkv_dims, gla_dims = (HEAD_A, HEAD_A, n), (DK_B, DV_B, n)
    of_head = lambda dims: pl.BlockSpec((None, None) + dims, lambda h: (layer, h) + (0,) * len(dims))
    stacked = lambda nl, dims: pl.BlockSpec((nl, None) + dims, lambda h: (0, h) + (0,) * len(dims))
    keys = RWKV_PARAMS + GLA_PARAMS
    in_specs = [_const_spec((n, A_PAD)), _const_spec((n, B_PAD)), _const_spec((n, A_PAD)),
                of_head(wkv_dims), of_head(gla_dims)] + [_lspec(prm[k], layer) for k in keys]
    if n_prev:
        in_specs += [stacked(n_prev, wkv_dims), stacked(n_prev, gla_dims)]
    return pl.pallas_call(
        functools.partial(_decode_states_kernel, n_prev=n_prev),
        grid=(H_A,),
        in_specs=in_specs,
        out_specs=[_const_spec((n, W_BR)), _const_spec((n, W_BR)),
                   stacked(n_prev + 1, wkv_dims), stacked(n_prev + 1, gla_dims)],
        out_shape=[jax.ShapeDtypeStruct((n, W_BR), f32), jax.ShapeDtypeStruct((n, W_BR), f32),
                   jax.ShapeDtypeStruct((n_prev + 1, H_A) + wkv_dims, f32),
                   jax.ShapeDtypeStruct((n_prev + 1, H_B) + gla_dims, f32)],
        scratch_shapes=[pltpu.VMEM((6, W_BR, n), f32), pltpu.VMEM((3, H_B * DK_B, n), f32),
                        pltpu.VMEM((W_BR, n), f32), pltpu.VMEM((W_BR, n), f32), pltpu.VMEM((W_BR, n), f32)],
        compiler_params=pltpu.CompilerParams(dimension_semantics=("arbitrary",),
                                             vmem_limit_bytes=VMEM_LIMIT),
        name="decode_states",
    )(pa, pb, shift0, swkv_t, sgla_t, *[prm[k] for k in keys], *(earlier if n_prev else ()))


def _decode_kernel(pa_ref, oa_ref, ob_ref, q8_ref, kvn_ref, pd_ref, shift_ref, kbuf_ref, vbuf_ref,
                   c0_ref, c1_ref, c2_ref, h0_ref,
                   mu_ref, w0_ref, wup_ref, a0_ref, aup_ref, kkw_ref, kaw_ref, rk_ref, lng_ref, lnb_ref,
                   gup_ref, gbias_ref, gng_ref, sink8_ref, slope8_ref,
                   cw_ref, cb_ref, wa_ref, ba_ref, wx_ref, bx_ref, lam_ref, *rest, n_prev):
    earlier, rest = rest[:2 if n_prev else 0], rest[2 if n_prev else 0:]
    ya_ref, yb_ref, yc8_ref, yd_ref, kout_all, vout_all = rest
    for src, dst in zip(earlier, (kout_all, vout_all)):
        dst[0:n_prev] = src[...]
    kout_ref, vout_ref = kout_all.at[n_prev], vout_all.at[n_prev]
    bt = pa_ref.shape[0]
    samples = range(bt)
    u = pa_ref[...]
    us = u + (shift_ref[...] - u) * mu_ref[...]
    ones = _head_ones()
    r, kmod, v, _, _, _ = _rwkv_features(us, w0_ref[...], wup_ref[...], a0_ref[...], aup_ref[...],
                                         kkw_ref[...], kaw_ref[...], ones)
    wdist = (WINDOW - _iota((1, WINDOW), 1)).astype(f32)
    last = _iota((HD_C, WINDOW), 1) == WINDOW - 1
    scale = HD_C ** -0.5
    eye_c = _eye(HD_C).astype(bf16)
    half = KV_C * HD_C
    pieces = [(b, g) for b in samples for g in range(KV_C)]
    kn = lambda b, g: kvn_ref[b:b + 1, g * HD_C:(g + 1) * HD_C]
    vn = lambda b, g: kvn_ref[b:b + 1, half + g * HD_C:half + (g + 1) * HD_C]
    scores = {(b, g): _bdot(q8_ref[b, g], kbuf_ref[b, g]) * scale - slope8_ref[g] * wdist
              for b, g in pieces}
    probs, tails = {}, {}
    for b, g in pieces:
        sink = sink8_ref[g]
        sn = jnp.sum(q8_ref[b, g] * kn(b, g), axis=1, keepdims=True) * scale
        m = jnp.maximum(jnp.maximum(jnp.max(scores[b, g], axis=1, keepdims=True), sn), sink)
        p = jnp.exp(scores[b, g] - m)
        pn = jnp.exp(sn - m)
        probs[b, g] = p
        tails[b, g] = (pn, jnp.sum(p, axis=1, keepdims=True) + pn + jnp.exp(sink - m))
    for b, g in pieces:
        pn, den = tails[b, g]
        yc8_ref[b, g] = (_bdot_nt(probs[b, g], vbuf_ref[b, g]) + pn * vn(b, g)) / den
    for b, g in pieces:
        for src, dst, new in ((kbuf_ref, kout_ref, kn(b, g)), (vbuf_ref, vout_ref, vn(b, g))):
            new_col = _colbcast(new, WINDOW, eye_c)
            dst[b, g] = jnp.where(last, new_col, pltpu.roll(src[b, g], WINDOW - 1, 1))

    ya_ref[...] = _rwkv_finish(oa_ref[...], r, kmod, v, rk_ref[...], lng_ref[...], lnb_ref[...], ones)
    ob = ob_ref[...]
    assert DV_B == HEAD_A
    ms = _ones_dot(ob * ob, ones) * (1.0 / DV_B)
    yb_ref[...] = ob * lax.rsqrt(ms + NORM_EPS) * gng_ref[...]
    xd = pd_ref[...]
    xc = (cb_ref[...] + c0_ref[...] * cw_ref[0:1, :] + c1_ref[...] * cw_ref[1:2, :]
          + c2_ref[...] * cw_ref[2:3, :] + xd * cw_ref[3:4, :])
    al, bterm = _lru_gates(xc, wa_ref[...], ba_ref[...], wx_ref[...], bx_ref[...], lam_ref[...])
    yd_ref[...] = al * h0_ref[...] + bterm


def _decode(pa, oa, ob, pc, pd, shift0, kbuf, vbuf, c0, c1, c2, h0, prm, layer, earlier, bt):
    n = pa.shape[0]
    n_prev = layer
    half = KV_C * HD_C
    q8 = jnp.pad(pc[:, 0:W_BR].reshape(n, KV_C, G_C, HD_C), ((0, 0), (0, 0), (0, 8 - G_C), (0, 0)))
    kvn = pc[:, W_BR:]
    slope8 = jnp.pad(jnp.asarray(ALIBI_SLOPES, f32).reshape(KV_C, G_C, 1), ((0, 0), (0, 8 - G_C), (0, 0)))
    rows = lambda w: pl.BlockSpec((bt, w), lambda i: (i, 0))
    qspec = pl.BlockSpec((bt, KV_C, 8, HD_C), lambda i: (i, 0, 0, 0))
    cache_dims = (KV_C, HD_C, WINDOW)
    cache_in = pl.BlockSpec((None, bt) + cache_dims, lambda i: (layer, i, 0, 0, 0))
    stacked = lambda nl: pl.BlockSpec((nl, bt) + cache_dims, lambda i: (0, i, 0, 0, 0))
    in_specs = [rows(A_PAD), rows(W_BR), rows(W_BR), qspec, rows(2 * half), rows(D_COLS), rows(A_PAD),
                cache_in, cache_in, rows(W_BR), rows(W_BR), rows(W_BR), rows(W_BR)]
    in_specs += [_lspec(prm[k], layer) for k in RWKV_PARAMS + GLA_PARAMS]
    in_specs += [_lspec(prm["sink8"], layer), _const_spec((KV_C, 8, 1))]
    in_specs += [_lspec(prm[k], layer) for k in LRU_PARAMS]
    if n_prev:
        in_specs += [stacked(n_prev)] * 2
    out_specs = [rows(W_BR), rows(W_BR), qspec, rows(W_BR)] + [stacked(n_prev + 1)] * 2
    out_shape = [jax.ShapeDtypeStruct((n, W_BR), f32), jax.ShapeDtypeStruct((n, W_BR), f32),
                 jax.ShapeDtypeStruct((n, KV_C, 8, HD_C), f32), jax.ShapeDtypeStruct((n, W_BR), f32)]
    out_shape += [jax.ShapeDtypeStruct((n_prev + 1, n) + cache_dims, f32)] * 2
    ya, yb, yc8, yd, k1, v1 = pl.pallas_call(
        functools.partial(_decode_kernel, n_prev=n_prev),
        grid=(n // bt,),
        in_specs=in_specs,
        out_specs=out_specs,
        out_shape=out_shape,
        compiler_params=pltpu.CompilerParams(dimension_semantics=("arbitrary",),
                                             vmem_limit_bytes=VMEM_LIMIT),
        name="decode_mixers",
    )(pa, oa, ob, q8, kvn, pd, shift0, kbuf, vbuf, c0, c1, c2, h0,
      *[prm[k] for k in RWKV_PARAMS + GLA_PARAMS], prm["sink8"], slope8, *[prm[k] for k in LRU_PARAMS],
      *(earlier if n_prev else ()))
    return ya, yb, yc8[:, :, 0:G_C, :].reshape(n, W_BR), yd, k1, v1


def _merge_kernel(x_ref, ya_ref, yb_ref, yc_ref, yd_ref, g_ref, w_ref, wbr_ref, wout_ref,
                  fg_ref, o_ref, *, final):
    x = x_ref[...]
    hn = _rms(x, g_ref[...]).astype(bf16)
    ys = (ya_ref, yb_ref, yc_ref, yd_ref)
    merged = None
    for n in range(N_BRANCH):
        z = _dg(hn, w_ref[0, n * W_BR:(n + 1) * W_BR, :], NT)
        yz = ys[n][...] * (z * _sigmoid(z))
        br = jnp.dot(yz.astype(bf16), wbr_ref[n], preferred_element_type=f32)
        gate = _sigmoid(_dg(hn, w_ref[0, Z_COLS + n * D_MODEL:Z_COLS + (n + 1) * D_MODEL, :], NT))
        merged = gate * br if merged is None else merged + gate * br
    out = x + jnp.dot(merged.astype(bf16), wout_ref[...], preferred_element_type=f32)
    if final:
        out = _rms(out, fg_ref[...])
    o_ref[...] = out


def _merge(x, ya, yb, yc, yd, g, wt, wbr, wout, fg, layer, tm, final):
    m = x.shape[0]
    tile = lambda w: pl.BlockSpec((tm, w), lambda i: (i, 0))
    per_layer = lambda shape: pl.BlockSpec((None,) + shape, lambda *_: (layer,) + (0,) * len(shape),
                                           pipeline_mode=pl.Buffered(1))
    return pl.pallas_call(
        functools.partial(_merge_kernel, final=final),
        grid=(m // tm,),
        in_specs=[tile(D_MODEL), tile(W_BR), tile(W_BR), tile(W_BR), tile(W_BR), _lspec(g, layer),
                  _wt_spec(layer, MIX_ROWS, GATE_ROWS, pipeline_mode=pl.Buffered(1)),
                  per_layer((N_BRANCH, W_BR, D_MODEL)),
                  per_layer((D_MODEL, D_MODEL)), _const_spec((1, D_MODEL))],
        out_specs=tile(D_MODEL),
        out_shape=jax.ShapeDtypeStruct((m, D_MODEL), f32),
        compiler_params=pltpu.CompilerParams(dimension_semantics=("arbitrary",),
                                             vmem_limit_bytes=VMEM_LIMIT),
        name="merge_final" if final else "merge",
    )(x, ya, yb, yc, yd, g, wt, wbr, wout, fg)


def _pad_cols(w, width):
    return jnp.pad(w, ((0, 0), (0, width - w.shape[1])))


def _pad_rows_at(w, start, total):
    return jnp.pad(w, ((start, total - start - w.shape[0]), (0, 0)))


def _block_diag(w):
    nb, bs, _ = w.shape
    out = jnp.zeros((nb * bs, nb * bs), w.dtype)
    for n in range(nb):
        out = out.at[n * bs:(n + 1) * bs, n * bs:(n + 1) * bs].set(w[n])
    return out


def _lspec(arr, layer):
    return pl.BlockSpec((None,) + arr.shape[1:], lambda *_: (layer,) + (0,) * (arr.ndim - 1))


def _stacked_params(norm_g, mu_shift, w0, w_decay_up, a0, a_icl_up, k_k, k_a, r_k, ln_x_g, ln_x_b,
                    gla_gate_up, gla_gate_b, gla_norm_g, swa_sinks, lru_conv_w, lru_conv_b, lru_wa, lru_ba,
                    lru_wx, lru_bx, lru_lambda):
    depth = norm_g.shape[0]
    row = lambda t: t.reshape(depth, 1, -1)
    pad_rows = lambda w, start: jnp.pad(w, ((0, 0), (start, LANE - start - w.shape[1]), (0, 0)))
    dense = lambda w: jnp.einsum("lnij,nm->lnimj", w, jnp.eye(w.shape[1], dtype=w.dtype)).reshape(
        depth, W_BR, W_BR).astype(bf16)
    return dict(
        g=row(norm_g),
        mu=row(jnp.pad(mu_shift, ((0, 0), (0, A_PAD - A_COLS)))),
        w0=row(w0), wup=pad_rows(w_decay_up, 0), a0=row(a0), aup=pad_rows(a_icl_up, R_DECAY),
        kkw=row(k_k), kaw=row(k_a), rk=row(r_k), lng=row(ln_x_g), lnb=row(ln_x_b),
        gup=pad_rows(gla_gate_up, 0), gbias=row(gla_gate_b), gng=row(jnp.tile(gla_norm_g, (1, H_B))),
        sinks=row(swa_sinks),
        sink8=jnp.pad(swa_sinks.reshape(depth, KV_C, G_C, 1), ((0, 0), (0, 0), (0, 8 - G_C), (0, 0))),
        cw=lru_conv_w, cb=row(lru_conv_b),
        lwa=dense(lru_wa), lba=row(lru_ba), lwx=dense(lru_wx), lbx=row(lru_bx), lam=row(lru_lambda),
    )


def kernel(x_prompt, x_sample, state_wkv, state_shift, state_gla, cache_swa_k, cache_swa_v, state_lru_conv, state_lru_h, norm_g, w_in, mu_shift, w0, w_decay_up, a0, a_icl_up, k_k, k_a, r_k, ln_x_g, ln_x_b, gla_gate_up, gla_gate_b, gla_norm_g, swa_sinks, lru_conv_w, lru_conv_b, lru_wa, lru_ba, lru_wx, lru_bx, lru_lambda, w_branch, w_out, final_norm_g):
    bp, lp, _ = x_prompt.shape
    bs = x_sample.shape[0]
    depth = w_in.shape[0]
    fg = final_norm_g.reshape(1, -1)
    xp = x_prompt.reshape(bp * lp, D_MODEL)
    xs = x_sample.reshape(bs, D_MODEL)
    outs_p = [[] for _ in range(7)]
    outs_s = [[] for _ in range(7)]
    wt_all = jnp.swapaxes(w_in, 1, 2).astype(bf16)
    wbr_all = w_branch.astype(bf16)
    wout_all = w_out.astype(bf16)
    stacked_s = None
    prm = _stacked_params(norm_g, mu_shift, w0, w_decay_up, a0, a_icl_up, k_k, k_a, r_k, ln_x_g, ln_x_b,
                          gla_gate_up, gla_gate_b, gla_norm_g, swa_sinks, lru_conv_w, lru_conv_b, lru_wa,
                          lru_ba, lru_wx, lru_bx, lru_lambda)
    shift_pad = jnp.pad(state_shift, ((0, 0), (0, 0), (0, A_PAD - A_COLS)))
    wkv_t = jnp.transpose(state_wkv, (0, 2, 3, 4, 1))
    gla_t = jnp.transpose(state_gla, (0, 2, 3, 4, 1))
    stacked_states = None
    kbuf_t = jnp.transpose(cache_swa_k, (0, 1, 3, 4, 2))
    vbuf_t = jnp.transpose(cache_swa_v, (0, 1, 3, 4, 2))
    for l in range(depth):
        final = l == depth - 1
        pa, pb, pc, pd = _inproj(xp, prm["g"], wt_all, l, tm=512)
        pa3, pb3 = pa.reshape(bp, lp, A_PAD), pb.reshape(bp, lp, B_PAD)
        pc3, pd3 = pc.reshape(bp, lp, C_COLS), pd.reshape(bp, lp, D_COLS)
        ya, st_t = _rwkv_prompt(pa3, prm, l)
        yb, sgla = _gla_prompt(pb3, prm, l)
        yc = _swa_prompt(pc3, prm, l)
        yd = _lru_prompt(pd3, prm, l)
        flat = lambda t: t.reshape(bp * lp, W_BR)
        xp = _merge(xp, flat(ya), flat(yb), flat(yc), flat(yd), prm["g"], wt_all, wbr_all, wout_all, fg, l,
                    tm=512, final=final)
        kv = pc3[:, lp - WINDOW:, H_C * HD_C:]
        outs_p[0].append(jnp.swapaxes(st_t, -1, -2))
        outs_p[1].append(pa3[:, lp - 1, :A_COLS])
        outs_p[2].append(sgla)
        outs_p[3].append(kv[:, :, :KV_C * HD_C].reshape(bp, WINDOW, KV_C, HD_C))
        outs_p[4].append(kv[:, :, KV_C * HD_C:].reshape(bp, WINDOW, KV_C, HD_C))
        outs_p[5].append(pd3[:, lp - (CONV_W - 1):, :])
        outs_p[6].append(yd[:, lp - 1, :])
        sa, sb, sc, sd = _inproj(xs, prm["g"], wt_all, l, tm=bs)
        conv0 = state_lru_conv[l]
        oa_s, ob_s, *stacked_states = _decode_states(sa, sb, shift_pad[l], wkv_t, gla_t, prm, l, stacked_states)
        ya_s, yb_s, yc_s, yd_s, *stacked_s = _decode(
            sa, oa_s, ob_s, sc, sd, shift_pad[l], kbuf_t, vbuf_t,
            conv0[:, 0], conv0[:, 1], conv0[:, 2], state_lru_h[l], prm, l, stacked_s, bt=8)
        xs = _merge(xs, ya_s, yb_s, yc_s, yd_s, prm["g"], wt_all, wbr_all, wout_all, fg, l, tm=bs, final=final)
        outs_s[1].append(sa[:, :A_COLS])
        outs_s[5].append(jnp.stack([conv0[:, 1], conv0[:, 2], sd], axis=1))
        outs_s[6].append(yd_s)
    y_prompt = xp.reshape(bp, lp, D_MODEL)
    y_sample = xs.reshape(bs, 1, D_MODEL)
    sp = [jnp.stack(t) for t in outs_p]
    wkv_s, gla_s = (jnp.transpose(t, (0, 4, 1, 2, 3)) for t in stacked_states)
    k_s, v_s = (jnp.transpose(t, (0, 1, 4, 2, 3)) for t in stacked_s)
    shift_s, conv_s, h_s = (jnp.stack(outs_s[i]) for i in (1, 5, 6))
    return (y_prompt, y_sample, sp[0], wkv_s, sp[1], shift_s, sp[2], gla_s, sp[3], k_s, sp[4], v_s,
            sp[5], conv_s, sp[6], h_s)
```

```python
import functools

import jax
import jax.numpy as jnp
from jax import lax
from jax.experimental import pallas as pl
from jax.experimental.pallas import tpu as pltpu

f32 = jnp.float32
bf16 = jnp.bfloat16

D_MODEL = 1024
N_BRANCH = 4
W_BR = 256
HEAD_A = 64
H_A = 4
R_DECAY = 32
R_ICL = 32
GN_EPS_A = 64e-5
H_B = 4
DK_B = 32
DV_B = 64
R_GATE_B = 16
GLA_TAU = 16.0
GLA_CHUNK = 64
GLA_STEP_ROWS = 256
GLA_STEP_BATCH = 2
H_C = 4
KV_C = 2
HD_C = 64
G_C = 2
WINDOW = 128
SWA_STEP_BLOCKS = 8
CONV_W = 4
C_RG = 8.0
NORM_EPS = 1e-6

A_COLS = 3 * W_BR + R_DECAY + R_ICL
B_COLS = 2 * H_B * DK_B + W_BR + R_GATE_B
C_COLS = H_C * HD_C + 2 * KV_C * HD_C
D_COLS = W_BR
Z_COLS = N_BRANCH * W_BR
G_COLS = N_BRANCH * D_MODEL

LANE = 128
A_PAD = 7 * LANE
B_PAD = 5 * LANE
RWKV_CHUNK = 64
RWKV_STEP_ROWS = 256
RWKV_STEP_BATCH = 4
GLA_PASSES = 1
LRU_CHUNK = 256
VMEM_LIMIT = 56 * 1024 * 1024
NEG_BIG = -1e30

ALIBI_SLOPES = tuple(2.0 ** (-8.0 * (h + 1) / H_C) for h in range(H_C))


def _bdot(a, b):
    return jnp.dot(a.astype(bf16), b.astype(bf16), preferred_element_type=f32)


def _bdot_nt(a, b):
    return lax.dot_general(a.astype(bf16), b.astype(bf16), (((1,), (1,)), ((), ())),
                           preferred_element_type=f32)


NN = ((1,), (0,))
NT = ((1,), (1,))
TN = ((0,), (0,))


def _dg(a, b, dims):
    return lax.dot_general(a, b, (dims, ((), ())), preferred_element_type=f32)


def _split_bf16(a):
    hi = a.astype(bf16)
    return hi, (a - hi.astype(f32)).astype(bf16)


def _dotp(a, b, dims, passes):
    if passes == 1:
        return _dg(a.astype(bf16), b.astype(bf16), dims)
    ah, al = _split_bf16(a)
    bh, bl = _split_bf16(b)
    return _dg(ah, bh, dims) + (_dg(ah, bl, dims) + _dg(al, bh, dims))


def _ones_dot(a, ones_bf16):
    ah, al = _split_bf16(a)
    return _dg(ah, ones_bf16, NN) + _dg(al, ones_bf16, NN)


def _cumsum_rows(x, seg, pos=None):
    if pos is None:
        pos = _iota(x.shape, 0) % seg
    d = 1
    while d < seg:
        x = x + jnp.where(pos >= d, pltpu.roll(x, d, 0), 0.0)
        d *= 2
    return x


def _iota(shape, dim):
    return lax.broadcasted_iota(jnp.int32, shape, dim)


def _eye(n):
    return (_iota((n, n), 0) == _iota((n, n), 1)).astype(f32)


def _block_ones(n, blk):
    return ((_iota((n, n), 0) // blk) == (_iota((n, n), 1) // blk)).astype(f32)


def _softplus(x):
    return jnp.maximum(x, 0.0) + jnp.log(1.0 + jnp.exp(-jnp.abs(x)))


def _log_sigmoid(x):
    return -_softplus(-x)


def _sigmoid(x):
    return 1.0 / (1.0 + jnp.exp(-x))


def _rms(x, g):
    return x * lax.rsqrt(jnp.mean(x * x, -1, keepdims=True) + NORM_EPS) * g


MIX_COLS = (A_COLS, B_COLS, C_COLS, D_COLS)
MIX_WIDTHS = (A_PAD, B_PAD, C_COLS, D_COLS)
MIX_ROWS = sum(MIX_COLS)
GATE_ROWS = Z_COLS + G_COLS


def _wt_spec(layer, start, rows, **kw):
    return pl.BlockSpec((pl.Element(1), pl.Element(rows), pl.Element(D_MODEL)), lambda *_: (layer, start, 0), **kw)


def _inproj_kernel(x_ref, g_ref, wt_ref, oa_ref, ob_ref, oc_ref, od_ref):
    hn = _rms(x_ref[...], g_ref[...]).astype(bf16)
    start = 0
    for o_ref, cols, width in zip((oa_ref, ob_ref, oc_ref, od_ref), MIX_COLS, MIX_WIDTHS):
        o_ref[:, 0:cols] = _dg(hn, wt_ref[0, start:start + cols, :], NT)
        if width > cols:
            o_ref[:, cols:width] = jnp.zeros((o_ref.shape[0], width - cols), f32)
        start += cols


def _const_spec(shape):
    return pl.BlockSpec(shape, lambda *_: (0,) * len(shape))


def _inproj(x, g, wt, layer, tm):
    m = x.shape[0]
    return pl.pallas_call(
        _inproj_kernel,
        grid=(m // tm,),
        in_specs=[pl.BlockSpec((tm, D_MODEL), lambda i: (i, 0)), _lspec(g, layer), _wt_spec(layer, 0, MIX_ROWS)],
        out_specs=[pl.BlockSpec((tm, w), lambda i: (i, 0)) for w in MIX_WIDTHS],
        out_shape=[jax.ShapeDtypeStruct((m, w), f32) for w in MIX_WIDTHS],
        compiler_params=pltpu.CompilerParams(dimension_semantics=("arbitrary",),
                                             vmem_limit_bytes=VMEM_LIMIT),
        name="inproj",
    )(x, g, wt)


def _head_ones():
    return _block_ones(W_BR, HEAD_A).astype(bf16)


def _rwkv_features(us, w0, wup, a0, aup, kk_w, ka_w, ones):
    r = us[:, 0:W_BR]
    k = us[:, W_BR:2 * W_BR]
    v = us[:, 2 * W_BR:3 * W_BR]
    lora = us[:, 3 * W_BR:A_PAD]
    w = -_softplus(-(w0 + _bdot(jnp.tanh(lora), wup))) - 0.5
    logdecay = -jnp.exp(w)
    a = _sigmoid(a0 + _bdot(lora, aup))
    kk = k * kk_w
    ss = _ones_dot(kk * kk, ones)
    kk = kk / jnp.maximum(jnp.sqrt(ss), 1e-12)
    kmod = k * (1.0 + (a - 1.0) * ka_w)
    return r, kmod, v, logdecay, kk, a


def _rwkv_finish(o, r, kmod, v, rk, lng, lnb, ones):
    mean = _ones_dot(o, ones) * (1.0 / HEAD_A)
    cen = o - mean
    var = _ones_dot(cen * cen, ones) * (1.0 / HEAD_A)
    o = cen * lax.rsqrt(var + GN_EPS_A) * lng + lnb
    bonus = _ones_dot(r * kmod * rk, ones) * v
    return o + bonus


def _rwkv_prompt_kernel(u_ref, mu_ref, w0_ref, wup_ref, a0_ref, aup_ref, kkw_ref, kaw_ref, rk_ref,
                        lng_ref, lnb_ref, y_ref, s_ref, st_scr, prev_scr, o_scr):
    step = pl.program_id(1)
    T = RWKV_CHUNK
    TT = RWKV_STEP_ROWS

    @pl.when(step == 0)
    def _():
        st_scr[...] = jnp.zeros_like(st_scr)
        prev_scr[...] = jnp.zeros_like(prev_scr)

    R = RWKV_STEP_BATCH
    NC = TT // T
    pieces = [(c, h) for c in range(NC) for h in range(H_A)]
    row0 = _iota((TT, A_PAD), 0) == 0
    pos = _iota((TT, W_BR), 0) % T
    ones = _head_ones()

    def features(q):
        u = u_ref[q]
        u_prev = jnp.where(row0, prev_scr[q], pltpu.roll(u, 1, 0))
        prev_scr[q] = u[TT - 1:TT, :]
        us = u + (u_prev - u) * mu_ref[...]
        r, kmod, v, ld, kk, a = _rwkv_features(us, w0_ref[...], wup_ref[...], a0_ref[...], aup_ref[...],
                                               kkw_ref[...], kaw_ref[...], ones)
        cum = _cumsum_rows(ld, T, pos)
        cum_last = jnp.concatenate(
            [jnp.broadcast_to(cum[(c + 1) * T - 1:(c + 1) * T, :], (T, W_BR)) for c in range(NC)], axis=0)
        g_inv = jnp.exp(-cum)
        g_tail = jnp.exp(cum_last - cum)
        kka = kk * a
        return dict(r=r, kmod=kmod, v=v, cum=cum, at=-kk * jnp.exp(cum - ld), rt=r * jnp.exp(cum),
                    bt=kka * g_inv, kt=kmod * g_inv, btg=kka * g_tail, ktg=kmod * g_tail)

    ri = _iota((2 * T, 2 * T), 0)
    ci = _iota((2 * T, 2 * T), 1)
    ti = jnp.where(ri >= T, ri - T, ri)
    si = jnp.where(ci >= T, ci - T, ci)
    keep = (ti > si) | ((ri >= T) & (ti == si))
    eye_t = _eye(T)
    eye_h = _eye(HEAD_A)

    def independent_stages(f):
        d = dict(ar={}, vh={}, pm={}, x={}, pw={}, lv={}, gcol={}, bkg={}, xar={})

        def products():
            for c, h in pieces:
                rows = slice(c * T, (c + 1) * T)
                sl = slice(h * HEAD_A, (h + 1) * HEAD_A)
                d["vh"][c, h] = f["v"][rows, sl]
                d["ar"][c, h] = jnp.concatenate([f["at"][rows, sl], f["rt"][rows, sl]], axis=0)
                bk = jnp.concatenate([f["bt"][rows, sl], f["kt"][rows, sl]], axis=0)
                d["pm"][c, h] = jnp.where(keep, _bdot_nt(d["ar"][c, h], bk), 0.0)

        def squares_and_values():
            for c, h in pieces:
                rows = slice(c * T, (c + 1) * T)
                sl = slice(h * HEAD_A, (h + 1) * HEAD_A)
                lab = d["pm"][c, h][0:T, 0:T]
                d["x"][c, h] = eye_t + lab
                d["pw"][c, h] = _bdot(lab, lab)
                d["lv"][c, h] = _bdot(d["pm"][c, h][:, T:2 * T], d["vh"][c, h])
                g_last = jnp.exp(f["cum"][(c + 1) * T - 1:(c + 1) * T, sl])
                d["gcol"][c, h] = jnp.sum(eye_h * g_last, axis=1, keepdims=True)
                d["bkg"][c, h] = jnp.concatenate([f["btg"][rows, sl], f["ktg"][rows, sl]], axis=0)

        def inverse_round(last):
            def run():
                for c, h in pieces:
                    x_next = d["x"][c, h] + _bdot(d["pw"][c, h], d["x"][c, h])
                    if not last:
                        d["pw"][c, h] = _bdot(d["pw"][c, h], d["pw"][c, h])
                    d["x"][c, h] = x_next
            return run

        def fold_inverse():
            for c, h in pieces:
                xa = _bdot(d["x"][c, h], d["ar"][c, h][0:T])
                xl = _bdot(d["x"][c, h], d["lv"][c, h][0:T])
                d["xar"][c, h] = jnp.concatenate([xa, d["ar"][c, h][T:2 * T]], axis=0)
                d["lv"][c, h] = jnp.concatenate([xl, d["lv"][c, h][T:2 * T]], axis=0)

        stages = [products, squares_and_values] + [inverse_round(it == 4) for it in range(5)] + [fold_inverse]
        return stages, d

    def dependent_stages(q, f, d):
        st = {}
        base = {}

        def load():
            for h in range(H_A):
                st[h] = st_scr[q, h]

        def read(c):
            def run():
                if c == 0:
                    load()
                for h in range(H_A):
                    base[h] = _bdot(d["xar"][c, h], st[h]) + d["lv"][c, h]
            return run

        def update(c):
            def run():
                for h in range(H_A):
                    st[h] = d["gcol"][c, h] * st[h] + _dotp(
                        d["bkg"][c, h], jnp.concatenate([base[h][0:T], d["vh"][c, h]], axis=0), TN, 1)
                for h in range(H_A):
                    o_scr[q, c * T:(c + 1) * T, h * HEAD_A:(h + 1) * HEAD_A] = (
                        base[h][T:2 * T] + _bdot(d["pm"][c, h][T:2 * T, 0:T], base[h][0:T]))
                if c == NC - 1:
                    for h in range(H_A):
                        st_scr[q, h] = st[h]
                    y_ref[q] = _rwkv_finish(o_scr[q], f["r"], f["kmod"], f["v"], rk_ref[...], lng_ref[...],
                                            lnb_ref[...], ones)
            return run

        return [stage for c in range(NC) for stage in (read(c), update(c))]

    feats = {q: features(q) for q in range(min(2, R))}
    pending = []
    for q in range(R):
        stages, d = independent_stages(feats[q])
        if q + 2 < R:
            feats[q + 2] = features(q + 2)
        for k in range(max(len(stages), len(pending))):
            if k < len(stages):
                stages[k]()
            if k < len(pending):
                pending[k]()
        pending = dependent_stages(q, feats[q], d)
    for stage in pending:
        stage()

    @pl.when(step == pl.num_programs(1) - 1)
    def _():
        s_ref[...] = st_scr[...]


RWKV_PARAMS = ("mu", "w0", "wup", "a0", "aup", "kkw", "kaw", "rk", "lng", "lnb")
GLA_PARAMS = ("gup", "gbias", "gng")
LRU_PARAMS = ("cw", "cb", "lwa", "lba", "lwx", "lbx", "lam")


def _rwkv_prompt(pa, prm, layer):
    b, l, _ = pa.shape
    T = RWKV_STEP_ROWS
    R = RWKV_STEP_BATCH
    return pl.pallas_call(
        _rwkv_prompt_kernel,
        grid=(b // R, l // T),
        in_specs=[pl.BlockSpec((R, T, A_PAD), lambda i, c: (i, c, 0))]
        + [_lspec(prm[k], layer) for k in RWKV_PARAMS],
        out_specs=[pl.BlockSpec((R, T, W_BR), lambda i, c: (i, c, 0)),
                   pl.BlockSpec((R, H_A, HEAD_A, HEAD_A), lambda i, c: (i, 0, 0, 0))],
        out_shape=[jax.ShapeDtypeStruct((b, l, W_BR), f32),
                   jax.ShapeDtypeStruct((b, H_A, HEAD_A, HEAD_A), f32)],
        scratch_shapes=[pltpu.VMEM((R, H_A, HEAD_A, HEAD_A), f32), pltpu.VMEM((R, 1, A_PAD), f32),
                        pltpu.VMEM((R, T, W_BR), f32)],
        compiler_params=pltpu.CompilerParams(dimension_semantics=("arbitrary", "arbitrary"),
                                             vmem_limit_bytes=VMEM_LIMIT),
        name="rwkv_prompt",
    )(pa, *[prm[k] for k in RWKV_PARAMS])


def _gla_prompt_kernel(f_ref, up_ref, bias_ref, ng_ref, y_ref, s_ref, s_scr, o_scr):
    step = pl.program_id(1)
    T = GLA_CHUNK
    TT = GLA_STEP_ROWS
    R = GLA_STEP_BATCH
    NCR = TT // T
    NC = R * NCR

    @pl.when(step == 0)
    def _():
        s_scr[...] = jnp.zeros_like(s_scr)

    f = f_ref[...].reshape(R * TT, B_PAD)
    hk = H_B * DK_B
    q = f[:, 0:hk] * (DK_B ** -0.5)
    k = f[:, hk:2 * hk]
    v = f[:, 2 * hk:2 * hk + W_BR]
    gl = f[:, 2 * hk + W_BR:B_PAD]
    P = GLA_PASSES
    g = _log_sigmoid(_bdot(gl, up_ref[...]) + bias_ref[...]) * (1.0 / GLA_TAU)
    bcum = _cumsum_rows(g, T)
    b_last = jnp.concatenate(
        [jnp.broadcast_to(bcum[(c + 1) * T - 1:(c + 1) * T, :], (T, hk)) for c in range(NC)], axis=0)
    qe = q * jnp.exp(bcum)
    ke = k * jnp.exp(-bcum)
    kl = k * jnp.exp(b_last - bcum)
    causal = _iota((T, T), 0) >= _iota((T, T), 1)
    eye_k = _eye(DK_B)
    pieces = [(c, h) for c in range(NC) for h in range(H_B)]
    av, kv, ecol = {}, {}, {}
    for c, h in pieces:
        rows = slice(c * T, (c + 1) * T)
        ks = slice(h * DK_B, (h + 1) * DK_B)
        vs = slice(h * DV_B, (h + 1) * DV_B)
        att = jnp.where(causal, _dotp(qe[rows, ks], ke[rows, ks], NT, P), 0.0)
        av[c, h] = _dotp(att, v[rows, vs], NN, P)
        kv[c, h] = _dotp(kl[rows, ks], v[rows, vs], TN, P)
        e_last = jnp.exp(bcum[(c + 1) * T - 1:(c + 1) * T, ks])
        ecol[c, h] = jnp.sum(eye_k * e_last, axis=1, keepdims=True)
    chains = [(q_, h) for q_ in range(R) for h in range(H_B)]
    s = {qh: s_scr[qh] for qh in chains}
    for j in range(NCR):
        for q_, h in chains:
            c = q_ * NCR + j
            rows = slice(c * T, (c + 1) * T)
            ks = slice(h * DK_B, (h + 1) * DK_B)
            o_scr[rows, h * DV_B:(h + 1) * DV_B] = av[c, h] + _dotp(qe[rows, ks], s[q_, h], NN, P)
            s[q_, h] = ecol[c, h] * s[q_, h] + kv[c, h]
    for qh in chains:
        s_scr[qh] = s[qh]
    o = o_scr[...]
    ms = _ones_dot(o * o, _block_ones(W_BR, DV_B).astype(bf16)) * (1.0 / DV_B)
    y_ref[...] = (o * lax.rsqrt(ms + NORM_EPS) * ng_ref[...]).reshape(R, TT, W_BR)

    @pl.when(step == pl.num_programs(1) - 1)
    def _():
        s_ref[...] = s_scr[...]


def _gla_prompt(pb, prm, layer):
    b, l, _ = pb.shape
    T = GLA_STEP_ROWS
    R = GLA_STEP_BATCH
    return pl.pallas_call(
        _gla_prompt_kernel,
        grid=(b // R, l // T),
        in_specs=[pl.BlockSpec((R, T, B_PAD), lambda i, c: (i, c, 0))]
        + [_lspec(prm[k], layer) for k in GLA_PARAMS],
        out_specs=[pl.BlockSpec((R, T, W_BR), lambda i, c: (i, c, 0)),
                   pl.BlockSpec((R, H_B, DK_B, DV_B), lambda i, c: (i, 0, 0, 0))],
        out_shape=[jax.ShapeDtypeStruct((b, l, W_BR), f32),
                   jax.ShapeDtypeStruct((b, H_B, DK_B, DV_B), f32)],
        scratch_shapes=[pltpu.VMEM((R, H_B, DK_B, DV_B), f32), pltpu.VMEM((R * T, W_BR), f32)],
        compiler_params=pltpu.CompilerParams(dimension_semantics=("arbitrary", "arbitrary"),
                                             vmem_limit_bytes=VMEM_LIMIT),
        name="gla_prompt",
    )(pb, *[prm[k] for k in GLA_PARAMS])


def _swa_prompt_kernel(cur_ref, prev_ref, sink_ref, y_ref):
    step = pl.program_id(1)
    W = WINDOW
    NB = SWA_STEP_BLOCKS
    qo, ko, vo = 0, H_C * HD_C, H_C * HD_C + KV_C * HD_C
    assert G_C == 2
    row = _iota((G_C * W, 2 * W), 0)
    s = _iota((G_C * W, 2 * W), 1)
    t = jnp.where(row >= W, row - W, row)
    dist = W + t - s
    ok = (dist >= 0) & (dist <= W)
    ok_first = ok & ((s >= W) | (step > 0))
    distf = dist.astype(f32)
    second = _iota((G_C * W, 1), 0) >= W
    scale = HD_C ** -0.5

    def band(col, j):
        if j == 0:
            return jnp.concatenate([prev_ref[0, :, col:col + HD_C], cur_ref[0, 0:W, col:col + HD_C]], axis=0)
        return cur_ref[0, (j - 1) * W:(j + 1) * W, col:col + HD_C]

    pieces = [(j, g) for j in range(NB) for g in range(KV_C)]
    scores, sinks = {}, {}
    for g in range(KV_C):
        h0, h1 = g * G_C, g * G_C + 1
        sinks[g] = jnp.where(second, sink_ref[:, h1:h1 + 1], sink_ref[:, h0:h0 + 1])
    for j, g in pieces:
        h0, h1 = g * G_C, g * G_C + 1
        q2 = jnp.concatenate([cur_ref[0, j * W:(j + 1) * W, qo + h0 * HD_C:qo + (h0 + 1) * HD_C],
                              cur_ref[0, j * W:(j + 1) * W, qo + h1 * HD_C:qo + (h1 + 1) * HD_C]], axis=0)
        slope = jnp.where(second, ALIBI_SLOPES[h1], ALIBI_SLOPES[h0])
        raw = _bdot_nt(q2, band(ko + g * HD_C, j)) * scale - slope * distf
        scores[j, g] = jnp.where(ok_first if j == 0 else ok, raw, NEG_BIG)
    probs, dens = {}, {}
    for j, g in pieces:
        m = jnp.maximum(jnp.max(scores[j, g], -1, keepdims=True), sinks[g])
        p = jnp.exp(scores[j, g] - m)
        probs[j, g] = p
        dens[j, g] = jnp.sum(p, -1, keepdims=True) + jnp.exp(sinks[g] - m)
    for j, g in pieces:
        out = _bdot(probs[j, g], band(vo + g * HD_C, j)) / dens[j, g]
        for jj in range(G_C):
            h = g * G_C + jj
            y_ref[0, j * W:(j + 1) * W, h * HD_C:(h + 1) * HD_C] = out[jj * W:(jj + 1) * W]


def _swa_prompt(pc, prm, layer):
    b, l, _ = pc.shape
    W = WINDOW
    NB = SWA_STEP_BLOCKS
    return pl.pallas_call(
        _swa_prompt_kernel,
        grid=(b, l // (NB * W)),
        in_specs=[pl.BlockSpec((1, NB * W, C_COLS), lambda i, c: (i, c, 0)),
                  pl.BlockSpec((1, W, C_COLS), lambda i, c: (i, jnp.maximum(NB * c - 1, 0), 0)),
                  _lspec(prm["sinks"], layer)],
        out_specs=pl.BlockSpec((1, NB * W, W_BR), lambda i, c: (i, c, 0)),
        out_shape=jax.ShapeDtypeStruct((b, l, W_BR), f32),
        compiler_params=pltpu.CompilerParams(dimension_semantics=("arbitrary", "arbitrary"),
                                             vmem_limit_bytes=VMEM_LIMIT),
        name="swa_prompt",
    )(pc, pc, prm["sinks"])


def _lru_gates(xc, wa, ba, wx, bx, lam):
    r = _sigmoid(_bdot(xc, wa) + ba)
    i = _sigmoid(_bdot(xc, wx) + bx)
    log_a = C_RG * r * _log_sigmoid(lam)
    a = jnp.exp(log_a)
    bterm = jnp.sqrt(1.0 - jnp.exp(2.0 * log_a)) * (i * xc)
    return a, bterm


def _lru_prompt_kernel(x_ref, cw_ref, cb_ref, wa_ref, ba_ref, wx_ref, bx_ref, lam_ref, y_ref,
                       xbuf_scr, h_scr):
    c = pl.program_id(1)
    T = LRU_CHUNK
    PADR = 8

    @pl.when(c == 0)
    def _():
        xbuf_scr[0:PADR, :] = jnp.zeros((PADR, W_BR), f32)
        h_scr[...] = jnp.zeros_like(h_scr)

    x = x_ref[0]
    xbuf_scr[PADR:PADR + T, :] = x
    xc = cb_ref[...] + x * cw_ref[CONV_W - 1:CONV_W, :]
    for j in range(1, CONV_W):
        xc = xc + xbuf_scr[PADR - j:PADR - j + T, :] * cw_ref[CONV_W - 1 - j:CONV_W - j, :]
    xbuf_scr[0:PADR, :] = x[T - PADR:T, :]
    a, bv = _lru_gates(xc, wa_ref[...], ba_ref[...], wx_ref[...], bx_ref[...], lam_ref[...])
    row = _iota((T, W_BR), 0)
    d = 1
    while d < T:
        keep = row >= d
        a_sh = jnp.where(keep, pltpu.roll(a, d, 0), 1.0)
        b_sh = jnp.where(keep, pltpu.roll(bv, d, 0), 0.0)
        bv = a * b_sh + bv
        a = a * a_sh
        d *= 2
    h = a * h_scr[...] + bv
    y_ref[0] = h
    h_scr[...] = h[T - 1:T, :]


def _lru_prompt(pd, prm, layer):
    b, l, _ = pd.shape
    T = LRU_CHUNK
    return pl.pallas_call(
        _lru_prompt_kernel,
        grid=(b, l // T),
        in_specs=[pl.BlockSpec((1, T, W_BR), lambda i, c: (i, c, 0))]
        + [_lspec(prm[k], layer) for k in LRU_PARAMS],
        out_specs=pl.BlockSpec((1, T, W_BR), lambda i, c: (i, c, 0)),
        out_shape=jax.ShapeDtypeStruct((b, l, W_BR), f32),
        scratch_shapes=[pltpu.VMEM((T + 8, W_BR), f32), pltpu.VMEM((1, W_BR), f32)],
        compiler_params=pltpu.CompilerParams(dimension_semantics=("arbitrary", "arbitrary"),
                                             vmem_limit_bytes=VMEM_LIMIT),
        name="lru_prompt",
    )(pd, *[prm[k] for k in LRU_PARAMS])


def _colbcast(row, n_out, eye_bf16):
    c = row.shape[1]
    hi, lo = _split_bf16(row)
    return (_dg(eye_bf16, jnp.broadcast_to(hi, (n_out, c)), NT)
            + _dg(eye_bf16, jnp.broadcast_to(lo, (n_out, c)), NT))


def _decode_states_kernel(pa_ref, pb_ref, shift_ref, swkv_ref, sgla_ref,
                          mu_ref, w0_ref, wup_ref, a0_ref, aup_ref, kkw_ref, kaw_ref, rk_ref, lng_ref, lnb_ref,
                          gup_ref, gbias_ref, gng_ref, *rest, n_prev):
    earlier, rest = rest[:2 if n_prev else 0], rest[2 if n_prev else 0:]
    oa_ref, ob_ref, swkv_all, sgla_all, fa_scr, fb_scr, vb_scr, oa_scr, ob_scr = rest
    for src, dst in zip(earlier, (swkv_all, sgla_all)):
        dst[0:n_prev] = src[...]
    swkv_out, sgla_out = swkv_all.at[n_prev], sgla_all.at[n_prev]
    h = pl.program_id(0)
    hk = H_B * DK_B

    @pl.when(h == 0)
    def _():
        u = pa_ref[...]
        us = u + (shift_ref[...] - u) * mu_ref[...]
        r, kmod, v, ld, kk, a = _rwkv_features(us, w0_ref[...], wup_ref[...], a0_ref[...], aup_ref[...],
                                               kkw_ref[...], kaw_ref[...], _head_ones())
        for i, t in enumerate((r, kmod, v, jnp.exp(ld), kk, kk * a)):
            fa_scr[i] = t.T
        fb = pb_ref[...]
        gb = _log_sigmoid(_bdot(fb[:, 2 * hk + W_BR:B_PAD], gup_ref[...]) + gbias_ref[...]) * (1.0 / GLA_TAU)
        for i, t in enumerate((fb[:, 0:hk] * (DK_B ** -0.5), fb[:, hk:2 * hk], jnp.exp(gb))):
            fb_scr[i] = t.T
        vb_scr[...] = fb[:, 2 * hk:2 * hk + W_BR].T

    hs = pl.ds(pl.multiple_of(h * HEAD_A, HEAD_A), HEAD_A)
    r_h, km_h, v_h, w_h, kk_h, kka_h = (fa_scr[i, hs, :] for i in range(6))
    sub = _iota((8, r_h.shape[1]), 0)
    for g in range(HEAD_A // 8):
        rows8 = jnp.zeros((8, r_h.shape[1]), f32)
        for j in range(8):
            vi = g * 8 + j
            s = swkv_ref[vi]
            sa = -jnp.sum(s * kk_h, axis=0, keepdims=True)
            s_new = s * w_h + sa * kka_h + v_h[vi:vi + 1, :] * km_h
            swkv_out[vi] = s_new
            rows8 = jnp.where(sub == j, jnp.sum(s_new * r_h, axis=0, keepdims=True), rows8)
        oa_scr[pl.ds(pl.multiple_of(h * HEAD_A + g * 8, 8), 8), :] = rows8
    ds_ = pl.ds(pl.multiple_of(h * DK_B, DK_B), DK_B)
    q_h, k_h, eg_h = (fb_scr[i, ds_, :] for i in range(3))
    vb_h = vb_scr[hs, :]
    acc = jnp.zeros_like(vb_h)
    for d in range(DK_B):
        s_new = eg_h[d:d + 1, :] * sgla_ref[d] + k_h[d:d + 1, :] * vb_h
        sgla_out[d] = s_new
        acc = acc + q_h[d:d + 1, :] * s_new
    ob_scr[hs, :] = acc

    @pl.when(h == pl.num_programs(0) - 1)
    def _():
        oa_ref[...] = oa_scr[...].T
        ob_ref[...] = ob_scr[...].T


def _decode_states(pa, pb, shift0, swkv_t, sgla_t, prm, layer, earlier):
    assert H_A == H_B and DV_B == HEAD_A
    n = pa.shape[0]
    n_prev = layer
    wkv_dims, gla_dims = (HEAD_A, HEAD_A, n), (DK_B, DV_B, n)
    of_head = lambda dims: pl.BlockSpec((None, None) + dims, lambda h: (layer, h) + (0,) * len(dims))
    stacked = lambda nl, dims: pl.BlockSpec((nl, None) + dims, lambda h: (0, h) + (0,) * len(dims))
    keys = RWKV_PARAMS + GLA_PARAMS
    in_specs = [_const_spec((n, A_PAD)), _const_spec((n, B_PAD)), _const_spec((n, A_PAD)),
                of_head(wkv_dims), of_head(gla_dims)] + [_lspec(prm[k], layer) for k in keys]
    if n_prev:
        in_specs += [stacked(n_prev, wkv_dims), stacked(n_prev, gla_dims)]
    return pl.pallas_call(
        functools.partial(_decode_states_kernel, n_prev=n_prev),
        grid=(H_A,),
        in_specs=in_specs,
        out_specs=[_const_spec((n, W_BR)), _const_spec((n, W_BR)),
                   stacked(n_prev + 1, wkv_dims), stacked(n_prev + 1, gla_dims)],
        out_shape=[jax.ShapeDtypeStruct((n, W_BR), f32), jax.ShapeDtypeStruct((n, W_BR), f32),
                   jax.ShapeDtypeStruct((n_prev + 1, H_A) + wkv_dims, f32),
                   jax.ShapeDtypeStruct((n_prev + 1, H_B) + gla_dims, f32)],
        scratch_shapes=[pltpu.VMEM((6, W_BR, n), f32), pltpu.VMEM((3, H_B * DK_B, n), f32),
                        pltpu.VMEM((W_BR, n), f32), pltpu.VMEM((W_BR, n), f32), pltpu.VMEM((W_BR, n), f32)],
        compiler_params=pltpu.CompilerParams(dimension_semantics=("arbitrary",),
                                             vmem_limit_bytes=VMEM_LIMIT),
        name="decode_states",
    )(pa, pb, shift0, swkv_t, sgla_t, *[prm[k] for k in keys], *(earlier if n_prev else ()))


def _decode_kernel(pa_ref, oa_ref, ob_ref, q8_ref, kvn_ref, pd_ref, shift_ref, kbuf_ref, vbuf_ref,
                   c0_ref, c1_ref, c2_ref, h0_ref,
                   mu_ref, w0_ref, wup_ref, a0_ref, aup_ref, kkw_ref, kaw_ref, rk_ref, lng_ref, lnb_ref,
                   gup_ref, gbias_ref, gng_ref, sink8_ref, slope8_ref,
                   cw_ref, cb_ref, wa_ref, ba_ref, wx_ref, bx_ref, lam_ref, *rest, n_prev):
    earlier, rest = rest[:2 if n_prev else 0], rest[2 if n_prev else 0:]
    ya_ref, yb_ref, yc8_ref, yd_ref, kout_all, vout_all = rest
    for src, dst in zip(earlier, (kout_all, vout_all)):
        dst[0:n_prev] = src[...]
    kout_ref, vout_ref = kout_all.at[n_prev], vout_all.at[n_prev]
    bt = pa_ref.shape[0]
    samples = range(bt)
    u = pa_ref[...]
    us = u + (shift_ref[...] - u) * mu_ref[...]
    ones = _head_ones()
    r, kmod, v, _, _, _ = _rwkv_features(us, w0_ref[...], wup_ref[...], a0_ref[...], aup_ref[...],
                                         kkw_ref[...], kaw_ref[...], ones)
    wdist = (WINDOW - _iota((1, WINDOW), 1)).astype(f32)
    last = _iota((HD_C, WINDOW), 1) == WINDOW - 1
    scale = HD_C ** -0.5
    eye_c = _eye(HD_C).astype(bf16)
    half = KV_C * HD_C
    pieces = [(b, g) for b in samples for g in range(KV_C)]
    kn = lambda b, g: kvn_ref[b:b + 1, g * HD_C:(g + 1) * HD_C]
    vn = lambda b, g: kvn_ref[b:b + 1, half + g * HD_C:half + (g + 1) * HD_C]
    scores = {(b, g): _bdot(q8_ref[b, g], kbuf_ref[b, g]) * scale - slope8_ref[g] * wdist
              for b, g in pieces}
    probs, tails = {}, {}
    for b, g in pieces:
        sink = sink8_ref[g]
        sn = jnp.sum(q8_ref[b, g] * kn(b, g), axis=1, keepdims=True) * scale
        m = jnp.maximum(jnp.maximum(jnp.max(scores[b, g], axis=1, keepdims=True), sn), sink)
        p = jnp.exp(scores[b, g] - m)
        pn = jnp.exp(sn - m)
        probs[b, g] = p
        tails[b, g] = (pn, jnp.sum(p, axis=1, keepdims=True) + pn + jnp.exp(sink - m))
    for b, g in pieces:
        pn, den = tails[b, g]
        yc8_ref[b, g] = (_bdot_nt(probs[b, g], vbuf_ref[b, g]) + pn * vn(b, g)) / den
    for b, g in pieces:
        for src, dst, new in ((kbuf_ref, kout_ref, kn(b, g)), (vbuf_ref, vout_ref, vn(b, g))):
            new_col = _colbcast(new, WINDOW, eye_c)
            dst[b, g] = jnp.where(last, new_col, pltpu.roll(src[b, g], WINDOW - 1, 1))

    ya_ref[...] = _rwkv_finish(oa_ref[...], r, kmod, v, rk_ref[...], lng_ref[...], lnb_ref[...], ones)
    ob = ob_ref[...]
    assert DV_B == HEAD_A
    ms = _ones_dot(ob * ob, ones) * (1.0 / DV_B)
    yb_ref[...] = ob * lax.rsqrt(ms + NORM_EPS) * gng_ref[...]
    xd = pd_ref[...]
    xc = (cb_ref[...] + c0_ref[...] * cw_ref[0:1, :] + c1_ref[...] * cw_ref[1:2, :]
          + c2_ref[...] * cw_ref[2:3, :] + xd * cw_ref[3:4, :])
    al, bterm = _lru_gates(xc, wa_ref[...], ba_ref[...], wx_ref[...], bx_ref[...], lam_ref[...])
    yd_ref[...] = al * h0_ref[...] + bterm


def _decode(pa, oa, ob, pc, pd, shift0, kbuf, vbuf, c0, c1, c2, h0, prm, layer, earlier, bt):
    n = pa.shape[0]
    n_prev = layer
    half = KV_C * HD_C
    q8 = jnp.pad(pc[:, 0:W_BR].reshape(n, KV_C, G_C, HD_C), ((0, 0), (0, 0), (0, 8 - G_C), (0, 0)))
    kvn = pc[:, W_BR:]
    slope8 = jnp.pad(jnp.asarray(ALIBI_SLOPES, f32).reshape(KV_C, G_C, 1), ((0, 0), (0, 8 - G_C), (0, 0)))
    rows = lambda w: pl.BlockSpec((bt, w), lambda i: (i, 0))
    qspec = pl.BlockSpec((bt, KV_C, 8, HD_C), lambda i: (i, 0, 0, 0))
    cache_dims = (KV_C, HD_C, WINDOW)
    cache_in = pl.BlockSpec((None, bt) + cache_dims, lambda i: (layer, i, 0, 0, 0))
    stacked = lambda nl: pl.BlockSpec((nl, bt) + cache_dims, lambda i: (0, i, 0, 0, 0))
    in_specs = [rows(A_PAD), rows(W_BR), rows(W_BR), qspec, rows(2 * half), rows(D_COLS), rows(A_PAD),
                cache_in, cache_in, rows(W_BR), rows(W_BR), rows(W_BR), rows(W_BR)]
    in_specs += [_lspec(prm[k], layer) for k in RWKV_PARAMS + GLA_PARAMS]
    in_specs += [_lspec(prm["sink8"], layer), _const_spec((KV_C, 8, 1))]
    in_specs += [_lspec(prm[k], layer) for k in LRU_PARAMS]
    if n_prev:
        in_specs += [stacked(n_prev)] * 2
    out_specs = [rows(W_BR), rows(W_BR), qspec, rows(W_BR)] + [stacked(n_prev + 1)] * 2
    out_shape = [jax.ShapeDtypeStruct((n, W_BR), f32), jax.ShapeDtypeStruct((n, W_BR), f32),
                 jax.ShapeDtypeStruct((n, KV_C, 8, HD_C), f32), jax.ShapeDtypeStruct((n, W_BR), f32)]
    out_shape += [jax.ShapeDtypeStruct((n_prev + 1, n) + cache_dims, f32)] * 2
    ya, yb, yc8, yd, k1, v1 = pl.pallas_call(
        functools.partial(_decode_kernel, n_prev=n_prev),
        grid=(n // bt,),
        in_specs=in_specs,
        out_specs=out_specs,
        out_shape=out_shape,
        compiler_params=pltpu.CompilerParams(dimension_semantics=("arbitrary",),
                                             vmem_limit_bytes=VMEM_LIMIT),
        name="decode_mixers",
    )(pa, oa, ob, q8, kvn, pd, shift0, kbuf, vbuf, c0, c1, c2, h0,
      *[prm[k] for k in RWKV_PARAMS + GLA_PARAMS], prm["sink8"], slope8, *[prm[k] for k in LRU_PARAMS],
      *(earlier if n_prev else ()))
    return ya, yb, yc8[:, :, 0:G_C, :].reshape(n, W_BR), yd, k1, v1


def _merge_kernel(x_ref, ya_ref, yb_ref, yc_ref, yd_ref, g_ref, w_ref, wbr_ref, wout_ref,
                  fg_ref, o_ref, *, final):
    x = x_ref[...]
    hn = _rms(x, g_ref[...]).astype(bf16)
    ys = (ya_ref, yb_ref, yc_ref, yd_ref)
    merged = None
    for n in range(N_BRANCH):
        z = _dg(hn, w_ref[0, n * W_BR:(n + 1) * W_BR, :], NT)
        yz = ys[n][...] * (z * _sigmoid(z))
        br = jnp.dot(yz.astype(bf16), wbr_ref[n], preferred_element_type=f32)
        gate = _sigmoid(_dg(hn, w_ref[0, Z_COLS + n * D_MODEL:Z_COLS + (n + 1) * D_MODEL, :], NT))
        merged = gate * br if merged is None else merged + gate * br
    out = x + jnp.dot(merged.astype(bf16), wout_ref[...], preferred_element_type=f32)
    if final:
        out = _rms(out, fg_ref[...])
    o_ref[...] = out


def _merge(x, ya, yb, yc, yd, g, wt, wbr, wout, fg, layer, tm, final):
    m = x.shape[0]
    tile = lambda w: pl.BlockSpec((tm, w), lambda i: (i, 0))
    per_layer = lambda shape: pl.BlockSpec((None,) + shape, lambda *_: (layer,) + (0,) * len(shape),
                                           pipeline_mode=pl.Buffered(1))
    return pl.pallas_call(
        functools.partial(_merge_kernel, final=final),
        grid=(m // tm,),
        in_specs=[tile(D_MODEL), tile(W_BR), tile(W_BR), tile(W_BR), tile(W_BR), _lspec(g, layer),
                  _wt_spec(layer, MIX_ROWS, GATE_ROWS, pipeline_mode=pl.Buffered(1)),
                  per_layer((N_BRANCH, W_BR, D_MODEL)),
                  per_layer((D_MODEL, D_MODEL)), _const_spec((1, D_MODEL))],
        out_specs=tile(D_MODEL),
        out_shape=jax.ShapeDtypeStruct((m, D_MODEL), f32),
        compiler_params=pltpu.CompilerParams(dimension_semantics=("arbitrary",),
                                             vmem_limit_bytes=VMEM_LIMIT),
        name="merge_final" if final else "merge",
    )(x, ya, yb, yc, yd, g, wt, wbr, wout, fg)


def _lspec(arr, layer):
    return pl.BlockSpec((None,) + arr.shape[1:], lambda *_: (layer,) + (0,) * (arr.ndim - 1))


def _stacked_params(norm_g, mu_shift, w0, w_decay_up, a0, a_icl_up, k_k, k_a, r_k, ln_x_g, ln_x_b,
                    gla_gate_up, gla_gate_b, gla_norm_g, swa_sinks, lru_conv_w, lru_conv_b, lru_wa, lru_ba,
                    lru_wx, lru_bx, lru_lambda):
    depth = norm_g.shape[0]
    row = lambda t: t.reshape(depth, 1, -1)
    pad_rows = lambda w, start: jnp.pad(w, ((0, 0), (start, LANE - start - w.shape[1]), (0, 0)))
    dense = lambda w: jnp.einsum("lnij,nm->lnimj", w, jnp.eye(w.shape[1], dtype=w.dtype)).reshape(
        depth, W_BR, W_BR).astype(bf16)
    return dict(
        g=row(norm_g),
        mu=row(jnp.pad(mu_shift, ((0, 0), (0, A_PAD - A_COLS)))),
        w0=row(w0), wup=pad_rows(w_decay_up, 0), a0=row(a0), aup=pad_rows(a_icl_up, R_DECAY),
        kkw=row(k_k), kaw=row(k_a), rk=row(r_k), lng=row(ln_x_g), lnb=row(ln_x_b),
        gup=pad_rows(gla_gate_up, 0), gbias=row(gla_gate_b), gng=row(jnp.tile(gla_norm_g, (1, H_B))),
        sinks=row(swa_sinks),
        sink8=jnp.pad(swa_sinks.reshape(depth, KV_C, G_C, 1), ((0, 0), (0, 0), (0, 8 - G_C), (0, 0))),
        cw=lru_conv_w, cb=row(lru_conv_b),
        lwa=dense(lru_wa), lba=row(lru_ba), lwx=dense(lru_wx), lbx=row(lru_bx), lam=row(lru_lambda),
    )


def kernel(x_prompt, x_sample, state_wkv, state_shift, state_gla, cache_swa_k, cache_swa_v, state_lru_conv, state_lru_h, norm_g, w_in, mu_shift, w0, w_decay_up, a0, a_icl_up, k_k, k_a, r_k, ln_x_g, ln_x_b, gla_gate_up, gla_gate_b, gla_norm_g, swa_sinks, lru_conv_w, lru_conv_b, lru_wa, lru_ba, lru_wx, lru_bx, lru_lambda, w_branch, w_out, final_norm_g):
    bp, lp, _ = x_prompt.shape
    bs = x_sample.shape[0]
    depth = w_in.shape[0]
    fg = final_norm_g.reshape(1, -1)
    xp = x_prompt.reshape(bp * lp, D_MODEL)
    xs = x_sample.reshape(bs, D_MODEL)
    outs_p = [[] for _ in range(7)]
    outs_s = [[] for _ in range(7)]
    wt_all = jnp.swapaxes(w_in, 1, 2).astype(bf16)
    wbr_all = w_branch.astype(bf16)
    wout_all = w_out.astype(bf16)
    stacked_s = None
    prm = _stacked_params(norm_g, mu_shift, w0, w_decay_up, a0, a_icl_up, k_k, k_a, r_k, ln_x_g, ln_x_b,
                          gla_gate_up, gla_gate_b, gla_norm_g, swa_sinks, lru_conv_w, lru_conv_b, lru_wa,
                          lru_ba, lru_wx, lru_bx, lru_lambda)
    shift_pad = jnp.pad(state_shift, ((0, 0), (0, 0), (0, A_PAD - A_COLS)))
    wkv_t = jnp.transpose(state_wkv, (0, 2, 3, 4, 1))
    gla_t = jnp.transpose(state_gla, (0, 2, 3, 4, 1))
    stacked_states = None
    kbuf_t = jnp.transpose(cache_swa_k, (0, 1, 3, 4, 2))
    vbuf_t = jnp.transpose(cache_swa_v, (0, 1, 3, 4, 2))
    for l in range(depth):
        final = l == depth - 1
        pa, pb, pc, pd = _inproj(xp, prm["g"], wt_all, l, tm=512)
        pa3, pb3 = pa.reshape(bp, lp, A_PAD), pb.reshape(bp, lp, B_PAD)
        pc3, pd3 = pc.reshape(bp, lp, C_COLS), pd.reshape(bp, lp, D_COLS)
        ya, st_t = _rwkv_prompt(pa3, prm, l)
        yb, sgla = _gla_prompt(pb3, prm, l)
        yc = _swa_prompt(pc3, prm, l)
        yd = _lru_prompt(pd3, prm, l)
        flat = lambda t: t.reshape(bp * lp, W_BR)
        xp = _merge(xp, flat(ya), flat(yb), flat(yc), flat(yd), prm["g"], wt_all, wbr_all, wout_all, fg, l,
                    tm=512, final=final)
        kv = pc3[:, lp - WINDOW:, H_C * HD_C:]
        outs_p[0].append(jnp.swapaxes(st_t, -1, -2))
        outs_p[1].append(pa3[:, lp - 1, :A_COLS])
        outs_p[2].append(sgla)
        outs_p[3].append(kv[:, :, :KV_C * HD_C].reshape(bp, WINDOW, KV_C, HD_C))
        outs_p[4].append(kv[:, :, KV_C * HD_C:].reshape(bp, WINDOW, KV_C, HD_C))
        outs_p[5].append(pd3[:, lp - (CONV_W - 1):, :])
        outs_p[6].append(yd[:, lp - 1, :])
        sa, sb, sc, sd = _inproj(xs, prm["g"], wt_all, l, tm=bs)
        conv0 = state_lru_conv[l]
        oa_s, ob_s, *stacked_states = _decode_states(sa, sb, shift_pad[l], wkv_t, gla_t, prm, l, stacked_states)
        ya_s, yb_s, yc_s, yd_s, *stacked_s = _decode(
            sa, oa_s, ob_s, sc, sd, shift_pad[l], kbuf_t, vbuf_t,
            conv0[:, 0], conv0[:, 1], conv0[:, 2], state_lru_h[l], prm, l, stacked_s, bt=8)
        xs = _merge(xs, ya_s, yb_s, yc_s, yd_s, prm["g"], wt_all, wbr_all, wout_all, fg, l, tm=bs, final=final)
        outs_s[1].append(sa[:, :A_COLS])
        outs_s[5].append(jnp.stack([conv0[:, 1], conv0[:, 2], sd], axis=1))
        outs_s[6].append(yd_s)
    y_prompt = xp.reshape(bp, lp, D_MODEL)
    y_sample = xs.reshape(bs, 1, D_MODEL)
    sp = [jnp.stack(t) for t in outs_p]
    wkv_s, gla_s = (jnp.transpose(t, (0, 4, 1, 2, 3)) for t in stacked_states)
    k_s, v_s = (jnp.transpose(t, (0, 1, 4, 2, 3)) for t in stacked_s)
    shift_s, conv_s, h_s = (jnp.stack(outs_s[i]) for i in (1, 5, 6))
    return (y_prompt, y_sample, sp[0], wkv_s, sp[1], shift_s, sp[2], gla_s, sp[3], k_s, sp[4], v_s,
            sp[5], conv_s, sp[6], h_s)
```

```python
import functools

import jax
import jax.numpy as jnp
from jax import lax
from jax.experimental import pallas as pl
from jax.experimental.pallas import tpu as pltpu

f32 = jnp.float32
bf16 = jnp.bfloat16

D_MODEL = 1024
N_BRANCH = 4
W_BR = 256
HEAD_A = 64
H_A = 4
R_DECAY = 32
R_ICL = 32
GN_EPS_A = 64e-5
H_B = 4
DK_B = 32
DV_B = 64
R_GATE_B = 16
GLA_TAU = 16.0
GLA_CHUNK = 64
GLA_STEP_ROWS = 256
GLA_STEP_BATCH = 2
H_C = 4
KV_C = 2
HD_C = 64
G_C = 2
WINDOW = 128
SWA_STEP_BLOCKS = 8
CONV_W = 4
C_RG = 8.0
NORM_EPS = 1e-6

A_COLS = 3 * W_BR + R_DECAY + R_ICL
B_COLS = 2 * H_B * DK_B + W_BR + R_GATE_B
C_COLS = H_C * HD_C + 2 * KV_C * HD_C
D_COLS = W_BR
Z_COLS = N_BRANCH * W_BR
G_COLS = N_BRANCH * D_MODEL

LANE = 128
A_PAD = 7 * LANE
B_PAD = 5 * LANE
RWKV_CHUNK = 64
RWKV_STEP_ROWS = 256
RWKV_STEP_BATCH = 4
GLA_PASSES = 1
LRU_CHUNK = 256
VMEM_LIMIT = 56 * 1024 * 1024
NEG_BIG = -1e30

ALIBI_SLOPES = tuple(2.0 ** (-8.0 * (h + 1) / H_C) for h in range(H_C))


def _bdot(a, b):
    return jnp.dot(a.astype(bf16), b.astype(bf16), preferred_element_type=f32)


def _bdot_nt(a, b):
    return lax.dot_general(a.astype(bf16), b.astype(bf16), (((1,), (1,)), ((), ())),
                           preferred_element_type=f32)


NN = ((1,), (0,))
NT = ((1,), (1,))
TN = ((0,), (0,))


def _dg(a, b, dims):
    return lax.dot_general(a, b, (dims, ((), ())), preferred_element_type=f32)


def _split_bf16(a):
    hi = a.astype(bf16)
    return hi, (a - hi.astype(f32)).astype(bf16)


def _dotp(a, b, dims, passes):
    if passes == 1:
        return _dg(a.astype(bf16), b.astype(bf16), dims)
    ah, al = _split_bf16(a)
    bh, bl = _split_bf16(b)
    return _dg(ah, bh, dims) + (_dg(ah, bl, dims) + _dg(al, bh, dims))


def _ones_dot(a, ones_bf16):
    ah, al = _split_bf16(a)
    return _dg(ah, ones_bf16, NN) + _dg(al, ones_bf16, NN)


def _cumsum_rows(x, seg, pos=None):
    if pos is None:
        pos = _iota(x.shape, 0) % seg
    d = 1
    while d < seg:
        x = x + jnp.where(pos >= d, pltpu.roll(x, d, 0), 0.0)
        d *= 2
    return x


def _iota(shape, dim):
    return lax.broadcasted_iota(jnp.int32, shape, dim)


def _eye(n):
    return (_iota((n, n), 0) == _iota((n, n), 1)).astype(f32)


def _block_ones(n, blk):
    return ((_iota((n, n), 0) // blk) == (_iota((n, n), 1) // blk)).astype(f32)


def _softplus(x):
    return jnp.maximum(x, 0.0) + jnp.log(1.0 + jnp.exp(-jnp.abs(x)))


def _log_sigmoid(x):
    return -_softplus(-x)


def _sigmoid(x):
    return 1.0 / (1.0 + jnp.exp(-x))


def _rms(x, g):
    return x * lax.rsqrt(jnp.mean(x * x, -1, keepdims=True) + NORM_EPS) * g


MIX_COLS = (A_COLS, B_COLS, C_COLS, D_COLS)
MIX_WIDTHS = (A_PAD, B_PAD, C_COLS, D_COLS)
MIX_ROWS = sum(MIX_COLS)
GATE_ROWS = Z_COLS + G_COLS


PACK_MIX_ROWS = MIX_ROWS // 7
PACK_GATE_ROWS = 256


def _cast_kernel(w_ref, o_ref):
    o_ref[...] = w_ref[...].astype(bf16)


def _transpose_cast_kernel(w_ref, o_ref):
    o_ref[...] = w_ref[0].T.astype(bf16)


def _pack_w_in(w_in):
    depth = w_in.shape[0]
    wt = jnp.swapaxes(w_in, 1, 2)
    assert PACK_MIX_ROWS * 7 == MIX_ROWS and PACK_MIX_ROWS % 16 == 0 and GATE_ROWS % PACK_GATE_ROWS == 0
    params = pltpu.CompilerParams(dimension_semantics=("arbitrary", "arbitrary"), vmem_limit_bytes=VMEM_LIMIT)
    wt_mix = pl.pallas_call(
        _cast_kernel,
        grid=(depth, MIX_ROWS // PACK_MIX_ROWS),
        in_specs=[pl.BlockSpec((None, PACK_MIX_ROWS, D_MODEL), lambda l, i: (l, i, 0))],
        out_specs=pl.BlockSpec((None, PACK_MIX_ROWS, D_MODEL), lambda l, i: (l, i, 0)),
        out_shape=jax.ShapeDtypeStruct((depth, MIX_ROWS, D_MODEL), bf16),
        compiler_params=params,
        name="pack_mix",
    )(wt)
    rows = PACK_GATE_ROWS
    w_gate = pl.pallas_call(
        _transpose_cast_kernel,
        grid=(depth, GATE_ROWS // rows),
        in_specs=[pl.BlockSpec((pl.Element(1), pl.Element(rows), pl.Element(D_MODEL)),
                               lambda l, j: (l, pl.multiple_of(MIX_ROWS + rows * j, 8), 0))],
        out_specs=pl.BlockSpec((None, D_MODEL, rows), lambda l, j: (l, 0, j)),
        out_shape=jax.ShapeDtypeStruct((depth, D_MODEL, GATE_ROWS), bf16),
        compiler_params=params,
        name="pack_gate",
    )(wt)
    return wt_mix, w_gate


def _inproj_kernel(x_ref, g_ref, wt_ref, oa_ref, ob_ref, oc_ref, od_ref):
    hn = _rms(x_ref[...], g_ref[...]).astype(bf16)
    start = 0
    for o_ref, cols, width in zip((oa_ref, ob_ref, oc_ref, od_ref), MIX_COLS, MIX_WIDTHS):
        o_ref[:, 0:cols] = _dg(hn, wt_ref[start:start + cols, :], NT)
        if width > cols:
            o_ref[:, cols:width] = jnp.zeros((o_ref.shape[0], width - cols), f32)
        start += cols


def _const_spec(shape):
    return pl.BlockSpec(shape, lambda *_: (0,) * len(shape))


def _inproj(x, g, wt, layer, tm):
    m = x.shape[0]
    return pl.pallas_call(
        _inproj_kernel,
        grid=(m // tm,),
        in_specs=[pl.BlockSpec((tm, D_MODEL), lambda i: (i, 0)), _lspec(g, layer), _lspec(wt, layer)],
        out_specs=[pl.BlockSpec((tm, w), lambda i: (i, 0)) for w in MIX_WIDTHS],
        out_shape=[jax.ShapeDtypeStruct((m, w), f32) for w in MIX_WIDTHS],
        compiler_params=pltpu.CompilerParams(dimension_semantics=("arbitrary",),
                                             vmem_limit_bytes=VMEM_LIMIT),
        name="inproj",
    )(x, g, wt)


def _head_ones():
    return _block_ones(W_BR, HEAD_A).astype(bf16)


def _rwkv_features(us, w0, wup, a0, aup, kk_w, ka_w, ones):
    r = us[:, 0:W_BR]
    k = us[:, W_BR:2 * W_BR]
    v = us[:, 2 * W_BR:3 * W_BR]
    lora = us[:, 3 * W_BR:A_PAD]
    w = -_softplus(-(w0 + _bdot(jnp.tanh(lora), wup))) - 0.5
    logdecay = -jnp.exp(w)
    a = _sigmoid(a0 + _bdot(lora, aup))
    kk = k * kk_w
    ss = _ones_dot(kk * kk, ones)
    kk = kk / jnp.maximum(jnp.sqrt(ss), 1e-12)
    kmod = k * (1.0 + (a - 1.0) * ka_w)
    return r, kmod, v, logdecay, kk, a


def _rwkv_finish(o, r, kmod, v, rk, lng, lnb, ones):
    mean = _ones_dot(o, ones) * (1.0 / HEAD_A)
    cen = o - mean
    var = _ones_dot(cen * cen, ones) * (1.0 / HEAD_A)
    o = cen * lax.rsqrt(var + GN_EPS_A) * lng + lnb
    bonus = _ones_dot(r * kmod * rk, ones) * v
    return o + bonus


def _rwkv_prompt_kernel(u_ref, mu_ref, w0_ref, wup_ref, a0_ref, aup_ref, kkw_ref, kaw_ref, rk_ref,
                        lng_ref, lnb_ref, y_ref, s_ref, st_scr, prev_scr, o_scr):
    step = pl.program_id(1)
    T = RWKV_CHUNK
    TT = RWKV_STEP_ROWS

    @pl.when(step == 0)
    def _():
        st_scr[...] = jnp.zeros_like(st_scr)
        prev_scr[...] = jnp.zeros_like(prev_scr)

    R = RWKV_STEP_BATCH
    NC = TT // T
    pieces = [(c, h) for c in range(NC) for h in range(H_A)]
    row0 = _iota((TT, A_PAD), 0) == 0
    pos = _iota((TT, W_BR), 0) % T
    ones = _head_ones()

    def features(q):
        u = u_ref[q]
        u_prev = jnp.where(row0, prev_scr[q], pltpu.roll(u, 1, 0))
        prev_scr[q] = u[TT - 1:TT, :]
        us = u + (u_prev - u) * mu_ref[...]
        r, kmod, v, ld, kk, a = _rwkv_features(us, w0_ref[...], wup_ref[...], a0_ref[...], aup_ref[...],
                                               kkw_ref[...], kaw_ref[...], ones)
        cum = _cumsum_rows(ld, T, pos)
        cum_last = jnp.concatenate(
            [jnp.broadcast_to(cum[(c + 1) * T - 1:(c + 1) * T, :], (T, W_BR)) for c in range(NC)], axis=0)
        g_inv = jnp.exp(-cum)
        g_tail = jnp.exp(cum_last - cum)
        kka = kk * a
        return dict(r=r, kmod=kmod, v=v, cum=cum, at=-kk * jnp.exp(cum - ld), rt=r * jnp.exp(cum),
                    bt=kka * g_inv, kt=kmod * g_inv, btg=kka * g_tail, ktg=kmod * g_tail)

    ri = _iota((2 * T, 2 * T), 0)
    ci = _iota((2 * T, 2 * T), 1)
    ti = jnp.where(ri >= T, ri - T, ri)
    si = jnp.where(ci >= T, ci - T, ci)
    keep = (ti > si) | ((ri >= T) & (ti == si))
    eye_t = _eye(T)
    eye_h = _eye(HEAD_A)

    def independent_stages(f):
        d = dict(ar={}, vh={}, pm={}, x={}, pw={}, lv={}, gcol={}, bkg={}, xar={})

        def products():
            for c, h in pieces:
                rows = slice(c * T, (c + 1) * T)
                sl = slice(h * HEAD_A, (h + 1) * HEAD_A)
                d["vh"][c, h] = f["v"][rows, sl]
                d["ar"][c, h] = jnp.concatenate([f["at"][rows, sl], f["rt"][rows, sl]], axis=0)
                bk = jnp.concatenate([f["bt"][rows, sl], f["kt"][rows, sl]], axis=0)
                d["pm"][c, h] = jnp.where(keep, _bdot_nt(d["ar"][c, h], bk), 0.0)

        def squares_and_values():
            for c, h in pieces:
                rows = slice(c * T, (c + 1) * T)
                sl = slice(h * HEAD_A, (h + 1) * HEAD_A)
                lab = d["pm"][c, h][0:T, 0:T]
                d["x"][c, h] = eye_t + lab
                d["pw"][c, h] = _bdot(lab, lab)
                d["lv"][c, h] = _bdot(d["pm"][c, h][:, T:2 * T], d["vh"][c, h])
                g_last = jnp.exp(f["cum"][(c + 1) * T - 1:(c + 1) * T, sl])
                d["gcol"][c, h] = jnp.sum(eye_h * g_last, axis=1, keepdims=True)
                d["bkg"][c, h] = jnp.concatenate([f["btg"][rows, sl], f["ktg"][rows, sl]], axis=0)

        def inverse_round(last):
            def run():
                for c, h in pieces:
                    x_next = d["x"][c, h] + _bdot(d["pw"][c, h], d["x"][c, h])
                    if not last:
                        d["pw"][c, h] = _bdot(d["pw"][c, h], d["pw"][c, h])
                    d["x"][c, h] = x_next
            return run

        def fold_inverse():
            for c, h in pieces:
                xa = _bdot(d["x"][c, h], d["ar"][c, h][0:T])
                xl = _bdot(d["x"][c, h], d["lv"][c, h][0:T])
                d["xar"][c, h] = jnp.concatenate([xa, d["ar"][c, h][T:2 * T]], axis=0)
                d["lv"][c, h] = jnp.concatenate([xl, d["lv"][c, h][T:2 * T]], axis=0)

        stages = [products, squares_and_values] + [inverse_round(it == 4) for it in range(5)] + [fold_inverse]
        return stages, d

    def dependent_stages(q, f, d):
        st = {}
        base = {}

        def load():
            for h in range(H_A):
                st[h] = st_scr[q, h]

        def read(c):
            def run():
                if c == 0:
                    load()
                for h in range(H_A):
                    base[h] = _bdot(d["xar"][c, h], st[h]) + d["lv"][c, h]
            return run

        def update(c):
            def run():
                for h in range(H_A):
                    st[h] = d["gcol"][c, h] * st[h] + _dotp(
                        d["bkg"][c, h], jnp.concatenate([base[h][0:T], d["vh"][c, h]], axis=0), TN, 1)
                for h in range(H_A):
                    o_scr[q, c * T:(c + 1) * T, h * HEAD_A:(h + 1) * HEAD_A] = (
                        base[h][T:2 * T] + _bdot(d["pm"][c, h][T:2 * T, 0:T], base[h][0:T]))
                if c == NC - 1:
                    for h in range(H_A):
                        st_scr[q, h] = st[h]
                    y_ref[q] = _rwkv_finish(o_scr[q], f["r"], f["kmod"], f["v"], rk_ref[...], lng_ref[...],
                                            lnb_ref[...], ones)
            return run

        return [stage for c in range(NC) for stage in (read(c), update(c))]

    feats = {q: features(q) for q in range(min(2, R))}
    pending = []
    for q in range(R):
        stages, d = independent_stages(feats[q])
        if q + 2 < R:
            feats[q + 2] = features(q + 2)
        for k in range(max(len(stages), len(pending))):
            if k < len(stages):
                stages[k]()
            if k < len(pending):
                pending[k]()
        pending = dependent_stages(q, feats[q], d)
    for stage in pending:
        stage()

    @pl.when(step == pl.num_programs(1) - 1)
    def _():
        s_ref[...] = st_scr[...]


RWKV_PARAMS = ("mu", "w0", "wup", "a0", "aup", "kkw", "kaw", "rk", "lng", "lnb")
GLA_PARAMS = ("gup", "gbias", "gng")
LRU_PARAMS = ("cw", "cb", "lwa", "lba", "lwx", "lbx", "lam")


def _rwkv_prompt(pa, prm, layer):
    b, l, _ = pa.shape
    T = RWKV_STEP_ROWS
    R = RWKV_STEP_BATCH
    return pl.pallas_call(
        _rwkv_prompt_kernel,
        grid=(b // R, l // T),
        in_specs=[pl.BlockSpec((R, T, A_PAD), lambda i, c: (i, c, 0))]
        + [_lspec(prm[k], layer) for k in RWKV_PARAMS],
        out_specs=[pl.BlockSpec((R, T, W_BR), lambda i, c: (i, c, 0)),
                   pl.BlockSpec((R, H_A, HEAD_A, HEAD_A), lambda i, c: (i, 0, 0, 0))],
        out_shape=[jax.ShapeDtypeStruct((b, l, W_BR), f32),
                   jax.ShapeDtypeStruct((b, H_A, HEAD_A, HEAD_A), f32)],
        scratch_shapes=[pltpu.VMEM((R, H_A, HEAD_A, HEAD_A), f32), pltpu.VMEM((R, 1, A_PAD), f32),
                        pltpu.VMEM((R, T, W_BR), f32)],
        compiler_params=pltpu.CompilerParams(dimension_semantics=("arbitrary", "arbitrary"),
                                             vmem_limit_bytes=VMEM_LIMIT),
        name="rwkv_prompt",
    )(pa, *[prm[k] for k in RWKV_PARAMS])


def _gla_prompt_kernel(f_ref, up_ref, bias_ref, ng_ref, y_ref, s_ref, s_scr, o_scr):
    step = pl.program_id(1)
    T = GLA_CHUNK
    TT = GLA_STEP_ROWS
    R = GLA_STEP_BATCH
    NCR = TT // T
    NC = R * NCR

    @pl.when(step == 0)
    def _():
        s_scr[...] = jnp.zeros_like(s_scr)

    f = f_ref[...].reshape(R * TT, B_PAD)
    hk = H_B * DK_B
    q = f[:, 0:hk] * (DK_B ** -0.5)
    k = f[:, hk:2 * hk]
    v = f[:, 2 * hk:2 * hk + W_BR]
    gl = f[:, 2 * hk + W_BR:B_PAD]
    P = GLA_PASSES
    g = _log_sigmoid(_bdot(gl, up_ref[...]) + bias_ref[...]) * (1.0 / GLA_TAU)
    bcum = _cumsum_rows(g, T)
    b_last = jnp.concatenate(
        [jnp.broadcast_to(bcum[(c + 1) * T - 1:(c + 1) * T, :], (T, hk)) for c in range(NC)], axis=0)
    qe = q * jnp.exp(bcum)
    ke = k * jnp.exp(-bcum)
    kl = k * jnp.exp(b_last - bcum)
    causal = _iota((T, T), 0) >= _iota((T, T), 1)
    eye_k = _eye(DK_B)
    pieces = [(c, h) for c in range(NC) for h in range(H_B)]
    av, kv, ecol = {}, {}, {}
    for c, h in pieces:
        rows = slice(c * T, (c + 1) * T)
        ks = slice(h * DK_B, (h + 1) * DK_B)
        vs = slice(h * DV_B, (h + 1) * DV_B)
        att = jnp.where(causal, _dotp(qe[rows, ks], ke[rows, ks], NT, P), 0.0)
        av[c, h] = _dotp(att, v[rows, vs], NN, P)
        kv[c, h] = _dotp(kl[rows, ks], v[rows, vs], TN, P)
        e_last = jnp.exp(bcum[(c + 1) * T - 1:(c + 1) * T, ks])
        ecol[c, h] = jnp.sum(eye_k * e_last, axis=1, keepdims=True)
    chains = [(q_, h) for q_ in range(R) for h in range(H_B)]
    s = {qh: s_scr[qh] for qh in chains}
    for j in range(NCR):
        for q_, h in chains:
            c = q_ * NCR + j
            rows = slice(c * T, (c + 1) * T)
            ks = slice(h * DK_B, (h + 1) * DK_B)
            o_scr[rows, h * DV_B:(h + 1) * DV_B] = av[c, h] + _dotp(qe[rows, ks], s[q_, h], NN, P)
            s[q_, h] = ecol[c, h] * s[q_, h] + kv[c, h]
    for qh in chains:
        s_scr[qh] = s[qh]
    o = o_scr[...]
    ms = _ones_dot(o * o, _block_ones(W_BR, DV_B).astype(bf16)) * (1.0 / DV_B)
    y_ref[...] = (o * lax.rsqrt(ms + NORM_EPS) * ng_ref[...]).reshape(R, TT, W_BR)

    @pl.when(step == pl.num_programs(1) - 1)
    def _():
        s_ref[...] = s_scr[...]


def _gla_prompt(pb, prm, layer):
    b, l, _ = pb.shape
    T = GLA_STEP_ROWS
    R = GLA_STEP_BATCH
    return pl.pallas_call(
        _gla_prompt_kernel,
        grid=(b // R, l // T),
        in_specs=[pl.BlockSpec((R, T, B_PAD), lambda i, c: (i, c, 0))]
        + [_lspec(prm[k], layer) for k in GLA_PARAMS],
        out_specs=[pl.BlockSpec((R, T, W_BR), lambda i, c: (i, c, 0)),
                   pl.BlockSpec((R, H_B, DK_B, DV_B), lambda i, c: (i, 0, 0, 0))],
        out_shape=[jax.ShapeDtypeStruct((b, l, W_BR), f32),
                   jax.ShapeDtypeStruct((b, H_B, DK_B, DV_B), f32)],
        scratch_shapes=[pltpu.VMEM((R, H_B, DK_B, DV_B), f32), pltpu.VMEM((R * T, W_BR), f32)],
        compiler_params=pltpu.CompilerParams(dimension_semantics=("arbitrary", "arbitrary"),
                                             vmem_limit_bytes=VMEM_LIMIT),
        name="gla_prompt",
    )(pb, *[prm[k] for k in GLA_PARAMS])


def _swa_prompt_kernel(cur_ref, prev_ref, sink_ref, y_ref):
    step = pl.program_id(1)
    W = WINDOW
    NB = SWA_STEP_BLOCKS
    qo, ko, vo = 0, H_C * HD_C, H_C * HD_C + KV_C * HD_C
    assert G_C == 2
    row = _iota((G_C * W, 2 * W), 0)
    s = _iota((G_C * W, 2 * W), 1)
    t = jnp.where(row >= W, row - W, row)
    dist = W + t - s
    ok = (dist >= 0) & (dist <= W)
    ok_first = ok & ((s >= W) | (step > 0))
    distf = dist.astype(f32)
    second = _iota((G_C * W, 1), 0) >= W
    scale = HD_C ** -0.5

    def band(col, j):
        if j == 0:
            return jnp.concatenate([prev_ref[0, :, col:col + HD_C], cur_ref[0, 0:W, col:col + HD_C]], axis=0)
        return cur_ref[0, (j - 1) * W:(j + 1) * W, col:col + HD_C]

    pieces = [(j, g) for j in range(NB) for g in range(KV_C)]
    scores, sinks = {}, {}
    for g in range(KV_C):
        h0, h1 = g * G_C, g * G_C + 1
        sinks[g] = jnp.where(second, sink_ref[:, h1:h1 + 1], sink_ref[:, h0:h0 + 1])
    for j, g in pieces:
        h0, h1 = g * G_C, g * G_C + 1
        q2 = jnp.concatenate([cur_ref[0, j * W:(j + 1) * W, qo + h0 * HD_C:qo + (h0 + 1) * HD_C],
                              cur_ref[0, j * W:(j + 1) * W, qo + h1 * HD_C:qo + (h1 + 1) * HD_C]], axis=0)
        slope = jnp.where(second, ALIBI_SLOPES[h1], ALIBI_SLOPES[h0])
        raw = _bdot_nt(q2, band(ko + g * HD_C, j)) * scale - slope * distf
        scores[j, g] = jnp.where(ok_first if j == 0 else ok, raw, NEG_BIG)
    probs, dens = {}, {}
    for j, g in pieces:
        m = jnp.maximum(jnp.max(scores[j, g], -1, keepdims=True), sinks[g])
        p = jnp.exp(scores[j, g] - m)
        probs[j, g] = p
        dens[j, g] = jnp.sum(p, -1, keepdims=True) + jnp.exp(sinks[g] - m)
    for j, g in pieces:
        out = _bdot(probs[j, g], band(vo + g * HD_C, j)) / dens[j, g]
        for jj in range(G_C):
            h = g * G_C + jj
            y_ref[0, j * W:(j + 1) * W, h * HD_C:(h + 1) * HD_C] = out[jj * W:(jj + 1) * W]


def _swa_prompt(pc, prm, layer):
    b, l, _ = pc.shape
    W = WINDOW
    NB = SWA_STEP_BLOCKS
    return pl.pallas_call(
        _swa_prompt_kernel,
        grid=(b, l // (NB * W)),
        in_specs=[pl.BlockSpec((1, NB * W, C_COLS), lambda i, c: (i, c, 0)),
                  pl.BlockSpec((1, W, C_COLS), lambda i, c: (i, jnp.maximum(NB * c - 1, 0), 0)),
                  _lspec(prm["sinks"], layer)],
        out_specs=pl.BlockSpec((1, NB * W, W_BR), lambda i, c: (i, c, 0)),
        out_shape=jax.ShapeDtypeStruct((b, l, W_BR), f32),
        compiler_params=pltpu.CompilerParams(dimension_semantics=("arbitrary", "arbitrary"),
                                             vmem_limit_bytes=VMEM_LIMIT),
        name="swa_prompt",
    )(pc, pc, prm["sinks"])


def _lru_gates(xc, wa, ba, wx, bx, lam):
    r = _sigmoid(_bdot(xc, wa) + ba)
    i = _sigmoid(_bdot(xc, wx) + bx)
    log_a = C_RG * r * _log_sigmoid(lam)
    a = jnp.exp(log_a)
    bterm = jnp.sqrt(1.0 - jnp.exp(2.0 * log_a)) * (i * xc)
    return a, bterm


def _lru_prompt_kernel(x_ref, cw_ref, cb_ref, wa_ref, ba_ref, wx_ref, bx_ref, lam_ref, y_ref,
                       xbuf_scr, h_scr):
    c = pl.program_id(1)
    T = LRU_CHUNK
    PADR = 8

    @pl.when(c == 0)
    def _():
        xbuf_scr[0:PADR, :] = jnp.zeros((PADR, W_BR), f32)
        h_scr[...] = jnp.zeros_like(h_scr)

    x = x_ref[0]
    xbuf_scr[PADR:PADR + T, :] = x
    xc = cb_ref[...] + x * cw_ref[CONV_W - 1:CONV_W, :]
    for j in range(1, CONV_W):
        xc = xc + xbuf_scr[PADR - j:PADR - j + T, :] * cw_ref[CONV_W - 1 - j:CONV_W - j, :]
    xbuf_scr[0:PADR, :] = x[T - PADR:T, :]
    a, bv = _lru_gates(xc, wa_ref[...], ba_ref[...], wx_ref[...], bx_ref[...], lam_ref[...])
    row = _iota((T, W_BR), 0)
    d = 1
    while d < T:
        keep = row >= d
        a_sh = jnp.where(keep, pltpu.roll(a, d, 0), 1.0)
        b_sh = jnp.where(keep, pltpu.roll(bv, d, 0), 0.0)
        bv = a * b_sh + bv
        a = a * a_sh
        d *= 2
    h = a * h_scr[...] + bv
    y_ref[0] = h
    h_scr[...] = h[T - 1:T, :]


def _lru_prompt(pd, prm, layer):
    b, l, _ = pd.shape
    T = LRU_CHUNK
    return pl.pallas_call(
        _lru_prompt_kernel,
        grid=(b, l // T),
        in_specs=[pl.BlockSpec((1, T, W_BR), lambda i, c: (i, c, 0))]
        + [_lspec(prm[k], layer) for k in LRU_PARAMS],
        out_specs=pl.BlockSpec((1, T, W_BR), lambda i, c: (i, c, 0)),
        out_shape=jax.ShapeDtypeStruct((b, l, W_BR), f32),
        scratch_shapes=[pltpu.VMEM((T + 8, W_BR), f32), pltpu.VMEM((1, W_BR), f32)],
        compiler_params=pltpu.CompilerParams(dimension_semantics=("arbitrary", "arbitrary"),
                                             vmem_limit_bytes=VMEM_LIMIT),
        name="lru_prompt",
    )(pd, *[prm[k] for k in LRU_PARAMS])


def _colbcast(row, n_out, eye_bf16):
    c = row.shape[1]
    hi, lo = _split_bf16(row)
    return (_dg(eye_bf16, jnp.broadcast_to(hi, (n_out, c)), NT)
            + _dg(eye_bf16, jnp.broadcast_to(lo, (n_out, c)), NT))


def _decode_states_kernel(pa_ref, pb_ref, shift_ref, swkv_ref, sgla_ref,
                          mu_ref, w0_ref, wup_ref, a0_ref, aup_ref, kkw_ref, kaw_ref, rk_ref, lng_ref, lnb_ref,
                          gup_ref, gbias_ref, gng_ref, *rest, n_prev):
    earlier, rest = rest[:2 if n_prev else 0], rest[2 if n_prev else 0:]
    oa_ref, ob_ref, swkv_all, sgla_all, fa_scr, fb_scr, vb_scr, oa_scr, ob_scr = rest
    for src, dst in zip(earlier, (swkv_all, sgla_all)):
        dst[0:n_prev] = src[...]
    swkv_out, sgla_out = swkv_all.at[n_prev], sgla_all.at[n_prev]
    h = pl.program_id(0)
    hk = H_B * DK_B

    @pl.when(h == 0)
    def _():
        u = pa_ref[...]
        us = u + (shift_ref[...] - u) * mu_ref[...]
        r, kmod, v, ld, kk, a = _rwkv_features(us, w0_ref[...], wup_ref[...], a0_ref[...], aup_ref[...],
                                               kkw_ref[...], kaw_ref[...], _head_ones())
        for i, t in enumerate((r, kmod, v, jnp.exp(ld), kk, kk * a)):
            fa_scr[i] = t.T
        fb = pb_ref[...]
        gb = _log_sigmoid(_bdot(fb[:, 2 * hk + W_BR:B_PAD], gup_ref[...]) + gbias_ref[...]) * (1.0 / GLA_TAU)
        for i, t in enumerate((fb[:, 0:hk] * (DK_B ** -0.5), fb[:, hk:2 * hk], jnp.exp(gb))):
            fb_scr[i] = t.T
        vb_scr[...] = fb[:, 2 * hk:2 * hk + W_BR].T

    hs = pl.ds(pl.multiple_of(h * HEAD_A, HEAD_A), HEAD_A)
    r_h, km_h, v_h, w_h, kk_h, kka_h = (fa_scr[i, hs, :] for i in range(6))
    sub = _iota((8, r_h.shape[1]), 0)
    for g in range(HEAD_A // 8):
        rows8 = jnp.zeros((8, r_h.shape[1]), f32)
        for j in range(8):
            vi = g * 8 + j
            s = swkv_ref[vi]
            sa = -jnp.sum(s * kk_h, axis=0, keepdims=True)
            s_new = s * w_h + sa * kka_h + v_h[vi:vi + 1, :] * km_h
            swkv_out[vi] = s_new
            rows8 = jnp.where(sub == j, jnp.sum(s_new * r_h, axis=0, keepdims=True), rows8)
        oa_scr[pl.ds(pl.multiple_of(h * HEAD_A + g * 8, 8), 8), :] = rows8
    ds_ = pl.ds(pl.multiple_of(h * DK_B, DK_B), DK_B)
    q_h, k_h, eg_h = (fb_scr[i, ds_, :] for i in range(3))
    vb_h = vb_scr[hs, :]
    acc = jnp.zeros_like(vb_h)
    for d in range(DK_B):
        s_new = eg_h[d:d + 1, :] * sgla_ref[d] + k_h[d:d + 1, :] * vb_h
        sgla_out[d] = s_new
        acc = acc + q_h[d:d + 1, :] * s_new
    ob_scr[hs, :] = acc

    @pl.when(h == pl.num_programs(0) - 1)
    def _():
        oa_ref[...] = oa_scr[...].T
        ob_ref[...] = ob_scr[...].T


def _decode_states(pa, pb, shift0, swkv_t, sgla_t, prm, layer, earlier):
    assert H_A == H_B and DV_B == HEAD_A
    n = pa.shape[0]
    n_prev = layer
    wkv_dims, gla_dims = (HEAD_A, HEAD_A, n), (DK_B, DV_B, n)
    of_head = lambda dims: pl.BlockSpec((None, None) + dims, lambda h: (layer, h) + (0,) * len(dims))
    stacked = lambda nl, dims: pl.BlockSpec((nl, None) + dims, lambda h: (0, h) + (0,) * len(dims))
    keys = RWKV_PARAMS + GLA_PARAMS
    in_specs = [_const_spec((n, A_PAD)), _const_spec((n, B_PAD)), _const_spec((n, A_PAD)),
                of_head(wkv_dims), of_head(gla_dims)] + [_lspec(prm[k], layer) for k in keys]
    if n_prev:
        in_specs += [stacked(n_prev, wkv_dims), stacked(n_prev, gla_dims)]
    return pl.pallas_call(
        functools.partial(_decode_states_kernel, n_prev=n_prev),
        grid=(H_A,),
        in_specs=in_specs,
        out_specs=[_const_spec((n, W_BR)), _const_spec((n, W_BR)),
                   stacked(n_prev + 1, wkv_dims), stacked(n_prev + 1, gla_dims)],
        out_shape=[jax.ShapeDtypeStruct((n, W_BR), f32), jax.ShapeDtypeStruct((n, W_BR), f32),
                   jax.ShapeDtypeStruct((n_prev + 1, H_A) + wkv_dims, f32),
                   jax.ShapeDtypeStruct((n_prev + 1, H_B) + gla_dims, f32)],
        scratch_shapes=[pltpu.VMEM((6, W_BR, n), f32), pltpu.VMEM((3, H_B * DK_B, n), f32),
                        pltpu.VMEM((W_BR, n), f32), pltpu.VMEM((W_BR, n), f32), pltpu.VMEM((W_BR, n), f32)],
        compiler_params=pltpu.CompilerParams(dimension_semantics=("arbitrary",),
                                             vmem_limit_bytes=VMEM_LIMIT),
        name="decode_states",
    )(pa, pb, shift0, swkv_t, sgla_t, *[prm[k] for k in keys], *(earlier if n_prev else ()))


def _decode_kernel(pa_ref, oa_ref, ob_ref, q8_ref, kvn_ref, pd_ref, shift_ref, kbuf_ref, vbuf_ref,
                   c0_ref, c1_ref, c2_ref, h0_ref,
                   mu_ref, w0_ref, wup_ref, a0_ref, aup_ref, kkw_ref, kaw_ref, rk_ref, lng_ref, lnb_ref,
                   gup_ref, gbias_ref, gng_ref, sink8_ref, slope8_ref,
                   cw_ref, cb_ref, wa_ref, ba_ref, wx_ref, bx_ref, lam_ref, *rest, n_prev):
    earlier, rest = rest[:2 if n_prev else 0], rest[2 if n_prev else 0:]
    ya_ref, yb_ref, yc8_ref, yd_ref, kout_all, vout_all = rest
    for src, dst in zip(earlier, (kout_all, vout_all)):
        dst[0:n_prev] = src[...]
    kout_ref, vout_ref = kout_all.at[n_prev], vout_all.at[n_prev]
    bt = pa_ref.shape[0]
    samples = range(bt)
    u = pa_ref[...]
    us = u + (shift_ref[...] - u) * mu_ref[...]
    ones = _head_ones()
    r, kmod, v, _, _, _ = _rwkv_features(us, w0_ref[...], wup_ref[...], a0_ref[...], aup_ref[...],
                                         kkw_ref[...], kaw_ref[...], ones)
    wdist = (WINDOW - _iota((1, WINDOW), 1)).astype(f32)
    last = _iota((HD_C, WINDOW), 1) == WINDOW - 1
    scale = HD_C ** -0.5
    eye_c = _eye(HD_C).astype(bf16)
    half = KV_C * HD_C
    pieces = [(b, g) for b in samples for g in range(KV_C)]
    kn = lambda b, g: kvn_ref[b:b + 1, g * HD_C:(g + 1) * HD_C]
    vn = lambda b, g: kvn_ref[b:b + 1, half + g * HD_C:half + (g + 1) * HD_C]
    scores = {(b, g): _bdot(q8_ref[b, g], kbuf_ref[b, g]) * scale - slope8_ref[g] * wdist
              for b, g in pieces}
    probs, tails = {}, {}
    for b, g in pieces:
        sink = sink8_ref[g]
        sn = jnp.sum(q8_ref[b, g] * kn(b, g), axis=1, keepdims=True) * scale
        m = jnp.maximum(jnp.maximum(jnp.max(scores[b, g], axis=1, keepdims=True), sn), sink)
        p = jnp.exp(scores[b, g] - m)
        pn = jnp.exp(sn - m)
        probs[b, g] = p
        tails[b, g] = (pn, jnp.sum(p, axis=1, keepdims=True) + pn + jnp.exp(sink - m))
    for b, g in pieces:
        pn, den = tails[b, g]
        yc8_ref[b, g] = (_bdot_nt(probs[b, g], vbuf_ref[b, g]) + pn * vn(b, g)) / den
    for b, g in pieces:
        for src, dst, new in ((kbuf_ref, kout_ref, kn(b, g)), (vbuf_ref, vout_ref, vn(b, g))):
            new_col = _colbcast(new, WINDOW, eye_c)
            dst[b, g] = jnp.where(last, new_col, pltpu.roll(src[b, g], WINDOW - 1, 1))

    ya_ref[...] = _rwkv_finish(oa_ref[...], r, kmod, v, rk_ref[...], lng_ref[...], lnb_ref[...], ones)
    ob = ob_ref[...]
    assert DV_B == HEAD_A
    ms = _ones_dot(ob * ob, ones) * (1.0 / DV_B)
    yb_ref[...] = ob * lax.rsqrt(ms + NORM_EPS) * gng_ref[...]
    xd = pd_ref[...]
    xc = (cb_ref[...] + c0_ref[...] * cw_ref[0:1, :] + c1_ref[...] * cw_ref[1:2, :]
          + c2_ref[...] * cw_ref[2:3, :] + xd * cw_ref[3:4, :])
    al, bterm = _lru_gates(xc, wa_ref[...], ba_ref[...], wx_ref[...], bx_ref[...], lam_ref[...])
    yd_ref[...] = al * h0_ref[...] + bterm


def _decode(pa, oa, ob, pc, pd, shift0, kbuf, vbuf, c0, c1, c2, h0, prm, layer, earlier, bt):
    n = pa.shape[0]
    n_prev = layer
    half = KV_C * HD_C
    q8 = jnp.pad(pc[:, 0:W_BR].reshape(n, KV_C, G_C, HD_C), ((0, 0), (0, 0), (0, 8 - G_C), (0, 0)))
    kvn = pc[:, W_BR:]
    slope8 = jnp.pad(jnp.asarray(ALIBI_SLOPES, f32).reshape(KV_C, G_C, 1), ((0, 0), (0, 8 - G_C), (0, 0)))
    rows = lambda w: pl.BlockSpec((bt, w), lambda i: (i, 0))
    qspec = pl.BlockSpec((bt, KV_C, 8, HD_C), lambda i: (i, 0, 0, 0))
    cache_dims = (KV_C, HD_C, WINDOW)
    cache_in = pl.BlockSpec((None, bt) + cache_dims, lambda i: (layer, i, 0, 0, 0))
    stacked = lambda nl: pl.BlockSpec((nl, bt) + cache_dims, lambda i: (0, i, 0, 0, 0))
    in_specs = [rows(A_PAD), rows(W_BR), rows(W_BR), qspec, rows(2 * half), rows(D_COLS), rows(A_PAD),
                cache_in, cache_in, rows(W_BR), rows(W_BR), rows(W_BR), rows(W_BR)]
    in_specs += [_lspec(prm[k], layer) for k in RWKV_PARAMS + GLA_PARAMS]
    in_specs += [_lspec(prm["sink8"], layer), _const_spec((KV_C, 8, 1))]
    in_specs += [_lspec(prm[k], layer) for k in LRU_PARAMS]
    if n_prev:
        in_specs += [stacked(n_prev)] * 2
    out_specs = [rows(W_BR), rows(W_BR), qspec, rows(W_BR)] + [stacked(n_prev + 1)] * 2
    out_shape = [jax.ShapeDtypeStruct((n, W_BR), f32), jax.ShapeDtypeStruct((n, W_BR), f32),
                 jax.ShapeDtypeStruct((n, KV_C, 8, HD_C), f32), jax.ShapeDtypeStruct((n, W_BR), f32)]
    out_shape += [jax.ShapeDtypeStruct((n_prev + 1, n) + cache_dims, f32)] * 2
    ya, yb, yc8, yd, k1, v1 = pl.pallas_call(
        functools.partial(_decode_kernel, n_prev=n_prev),
        grid=(n // bt,),
        in_specs=in_specs,
        out_specs=out_specs,
        out_shape=out_shape,
        compiler_params=pltpu.CompilerParams(dimension_semantics=("arbitrary",),
                                             vmem_limit_bytes=VMEM_LIMIT),
        name="decode_mixers",
    )(pa, oa, ob, q8, kvn, pd, shift0, kbuf, vbuf, c0, c1, c2, h0,
      *[prm[k] for k in RWKV_PARAMS + GLA_PARAMS], prm["sink8"], slope8, *[prm[k] for k in LRU_PARAMS],
      *(earlier if n_prev else ()))
    return ya, yb, yc8[:, :, 0:G_C, :].reshape(n, W_BR), yd, k1, v1


def _merge_kernel(x_ref, ya_ref, yb_ref, yc_ref, yd_ref, g_ref, w_ref, wbr_ref, wout_ref,
                  fg_ref, o_ref, *, final):
    x = x_ref[...]
    hn = _rms(x, g_ref[...]).astype(bf16)
    ys = (ya_ref, yb_ref, yc_ref, yd_ref)
    merged = None
    for n in range(N_BRANCH):
        z = _dg(hn, w_ref[:, n * W_BR:(n + 1) * W_BR], NN)
        yz = ys[n][...] * (z * _sigmoid(z))
        br = jnp.dot(yz.astype(bf16), wbr_ref[n], preferred_element_type=f32)
        gate = _sigmoid(_dg(hn, w_ref[:, Z_COLS + n * D_MODEL:Z_COLS + (n + 1) * D_MODEL], NN))
        merged = gate * br if merged is None else merged + gate * br
    out = x + jnp.dot(merged.astype(bf16), wout_ref[...], preferred_element_type=f32)
    if final:
        out = _rms(out, fg_ref[...])
    o_ref[...] = out


def _merge(x, ya, yb, yc, yd, g, wt, wbr, wout, fg, layer, tm, final):
    m = x.shape[0]
    tile = lambda w: pl.BlockSpec((tm, w), lambda i: (i, 0))
    per_layer = lambda shape: pl.BlockSpec((None,) + shape, lambda *_: (layer,) + (0,) * len(shape),
                                           pipeline_mode=pl.Buffered(1))
    return pl.pallas_call(
        functools.partial(_merge_kernel, final=final),
        grid=(m // tm,),
        in_specs=[tile(D_MODEL), tile(W_BR), tile(W_BR), tile(W_BR), tile(W_BR), _lspec(g, layer),
                  per_layer((D_MODEL, GATE_ROWS)),
                  per_layer((N_BRANCH, W_BR, D_MODEL)),
                  per_layer((D_MODEL, D_MODEL)), _const_spec((1, D_MODEL))],
        out_specs=tile(D_MODEL),
        out_shape=jax.ShapeDtypeStruct((m, D_MODEL), f32),
        compiler_params=pltpu.CompilerParams(dimension_semantics=("arbitrary",),
                                             vmem_limit_bytes=VMEM_LIMIT),
        name="merge_final" if final else "merge",
    )(x, ya, yb, yc, yd, g, wt, wbr, wout, fg)


def _lspec(arr, layer):
    return pl.BlockSpec((None,) + arr.shape[1:], lambda *_: (layer,) + (0,) * (arr.ndim - 1))


def _stacked_params(norm_g, mu_shift, w0, w_decay_up, a0, a_icl_up, k_k, k_a, r_k, ln_x_g, ln_x_b,
                    gla_gate_up, gla_gate_b, gla_norm_g, swa_sinks, lru_conv_w, lru_conv_b, lru_wa, lru_ba,
                    lru_wx, lru_bx, lru_lambda):
    depth = norm_g.shape[0]
    row = lambda t: t.reshape(depth, 1, -1)
    pad_rows = lambda w, start: jnp.pad(w, ((0, 0), (start, LANE - start - w.shape[1]), (0, 0)))
    dense = lambda w: jnp.einsum("lnij,nm->lnimj", w, jnp.eye(w.shape[1], dtype=w.dtype)).reshape(
        depth, W_BR, W_BR).astype(bf16)
    return dict(
        g=row(norm_g),
        mu=row(jnp.pad(mu_shift, ((0, 0), (0, A_PAD - A_COLS)))),
        w0=row(w0), wup=pad_rows(w_decay_up, 0), a0=row(a0), aup=pad_rows(a_icl_up, R_DECAY),
        kkw=row(k_k), kaw=row(k_a), rk=row(r_k), lng=row(ln_x_g), lnb=row(ln_x_b),
        gup=pad_rows(gla_gate_up, 0), gbias=row(gla_gate_b), gng=row(jnp.tile(gla_norm_g, (1, H_B))),
        sinks=row(swa_sinks),
        sink8=jnp.pad(swa_sinks.reshape(depth, KV_C, G_C, 1), ((0, 0), (0, 0), (0, 8 - G_C), (0, 0))),
        cw=lru_conv_w, cb=row(lru_conv_b),
        lwa=dense(lru_wa), lba=row(lru_ba), lwx=dense(lru_wx), lbx=row(lru_bx), lam=row(lru_lambda),
    )


def kernel(x_prompt, x_sample, state_wkv, state_shift, state_gla, cache_swa_k, cache_swa_v, state_lru_conv, state_lru_h, norm_g, w_in, mu_shift, w0, w_decay_up, a0, a_icl_up, k_k, k_a, r_k, ln_x_g, ln_x_b, gla_gate_up, gla_gate_b, gla_norm_g, swa_sinks, lru_conv_w, lru_conv_b, lru_wa, lru_ba, lru_wx, lru_bx, lru_lambda, w_branch, w_out, final_norm_g):
    bp, lp, _ = x_prompt.shape
    bs = x_sample.shape[0]
    depth = w_in.shape[0]
    fg = final_norm_g.reshape(1, -1)
    xp = x_prompt.reshape(bp * lp, D_MODEL)
    xs = x_sample.reshape(bs, D_MODEL)
    outs_p = [[] for _ in range(7)]
    outs_s = [[] for _ in range(7)]
    wt_mix, w_gate = _pack_w_in(w_in)
    wbr_all = w_branch.astype(bf16)
    wout_all = w_out.astype(bf16)
    stacked_s = None
    prm = _stacked_params(norm_g, mu_shift, w0, w_decay_up, a0, a_icl_up, k_k, k_a, r_k, ln_x_g, ln_x_b,
                          gla_gate_up, gla_gate_b, gla_norm_g, swa_sinks, lru_conv_w, lru_conv_b, lru_wa,
                          lru_ba, lru_wx, lru_bx, lru_lambda)
    shift_pad = jnp.pad(state_shift, ((0, 0), (0, 0), (0, A_PAD - A_COLS)))
    wkv_t = jnp.transpose(state_wkv, (0, 2, 3, 4, 1))
    gla_t = jnp.transpose(state_gla, (0, 2, 3, 4, 1))
    stacked_states = None
    kbuf_t = jnp.transpose(cache_swa_k, (0, 1, 3, 4, 2))
    vbuf_t = jnp.transpose(cache_swa_v, (0, 1, 3, 4, 2))
    for l in range(depth):
        final = l == depth - 1
        pa, pb, pc, pd = _inproj(xp, prm["g"], wt_mix, l, tm=1024)
        pa3, pb3 = pa.reshape(bp, lp, A_PAD), pb.reshape(bp, lp, B_PAD)
        pc3, pd3 = pc.reshape(bp, lp, C_COLS), pd.reshape(bp, lp, D_COLS)
        ya, st_t = _rwkv_prompt(pa3, prm, l)
        yb, sgla = _gla_prompt(pb3, prm, l)
        yc = _swa_prompt(pc3, prm, l)
        yd = _lru_prompt(pd3, prm, l)
        flat = lambda t: t.reshape(bp * lp, W_BR)
        xp = _merge(xp, flat(ya), flat(yb), flat(yc), flat(yd), prm["g"], w_gate, wbr_all, wout_all, fg, l,
                    tm=512, final=final)
        kv = pc3[:, lp - WINDOW:, H_C * HD_C:]
        outs_p[0].append(jnp.swapaxes(st_t, -1, -2))
        outs_p[1].append(pa3[:, lp - 1, :A_COLS])
        outs_p[2].append(sgla)
        outs_p[3].append(kv[:, :, :KV_C * HD_C].reshape(bp, WINDOW, KV_C, HD_C))
        outs_p[4].append(kv[:, :, KV_C * HD_C:].reshape(bp, WINDOW, KV_C, HD_C))
        outs_p[5].append(pd3[:, lp - (CONV_W - 1):, :])
        outs_p[6].append(yd[:, lp - 1, :])
        sa, sb, sc, sd = _inproj(xs, prm["g"], wt_mix, l, tm=bs)
        conv0 = state_lru_conv[l]
        oa_s, ob_s, *stacked_states = _decode_states(sa, sb, shift_pad[l], wkv_t, gla_t, prm, l, stacked_states)
        ya_s, yb_s, yc_s, yd_s, *stacked_s = _decode(
            sa, oa_s, ob_s, sc, sd, shift_pad[l], kbuf_t, vbuf_t,
            conv0[:, 0], conv0[:, 1], conv0[:, 2], state_lru_h[l], prm, l, stacked_s, bt=8)
        xs = _merge(xs, ya_s, yb_s, yc_s, yd_s, prm["g"], w_gate, wbr_all, wout_all, fg, l, tm=bs, final=final)
        outs_s[1].append(sa[:, :A_COLS])
        outs_s[5].append(jnp.stack([conv0[:, 1], conv0[:, 2], sd], axis=1))
        outs_s[6].append(yd_s)
    y_prompt = xp.reshape(bp, lp, D_MODEL)
    y_sample = xs.reshape(bs, 1, D_MODEL)
    sp = [jnp.stack(t) for t in outs_p]
    wkv_s, gla_s = (jnp.transpose(t, (0, 4, 1, 2, 3)) for t in stacked_states)
    k_s, v_s = (jnp.transpose(t, (0, 1, 4, 2, 3)) for t in stacked_s)
    shift_s, conv_s, h_s = (jnp.stack(outs_s[i]) for i in (1, 5, 6))
    return (y_prompt, y_sample, sp[0], wkv_s, sp[1], shift_s, sp[2], gla_s, sp[3], k_s, sp[4], v_s,
            sp[5], conv_s, sp[6], h_s)
```

```python
import functools

import jax
import jax.numpy as jnp
from jax import lax
from jax.experimental import pallas as pl
from jax.experimental.pallas import tpu as pltpu

f32 = jnp.float32
bf16 = jnp.bfloat16

D_MODEL = 1024
N_BRANCH = 4
W_BR = 256
HEAD_A = 64
H_A = 4
R_DECAY = 32
R_ICL = 32
GN_EPS_A = 64e-5
H_B = 4
DK_B = 32
DV_B = 64
R_GATE_B = 16
GLA_TAU = 16.0
GLA_CHUNK = 64
GLA_STEP_ROWS = 256
GLA_STEP_BATCH = 2
H_C = 4
KV_C = 2
HD_C = 64
G_C = 2
WINDOW = 128
SWA_STEP_BLOCKS = 8
CONV_W = 4
C_RG = 8.0
NORM_EPS = 1e-6

A_COLS = 3 * W_BR + R_DECAY + R_ICL
B_COLS = 2 * H_B * DK_B + W_BR + R_GATE_B
C_COLS = H_C * HD_C + 2 * KV_C * HD_C
D_COLS = W_BR
Z_COLS = N_BRANCH * W_BR
G_COLS = N_BRANCH * D_MODEL

LANE = 128
A_PAD = 7 * LANE
B_PAD = 5 * LANE
RWKV_CHUNK = 64
RWKV_STEP_ROWS = 256
RWKV_STEP_BATCH = 4
GLA_PASSES = 1
LRU_CHUNK = 256
VMEM_LIMIT = 56 * 1024 * 1024
NEG_BIG = -1e30

ALIBI_SLOPES = tuple(2.0 ** (-8.0 * (h + 1) / H_C) for h in range(H_C))


def _bdot(a, b):
    return jnp.dot(a.astype(bf16), b.astype(bf16), preferred_element_type=f32)


def _bdot_nt(a, b):
    return lax.dot_general(a.astype(bf16), b.astype(bf16), (((1,), (1,)), ((), ())),
                           preferred_element_type=f32)


NN = ((1,), (0,))
NT = ((1,), (1,))
TN = ((0,), (0,))


def _dg(a, b, dims):
    return lax.dot_general(a, b, (dims, ((), ())), preferred_element_type=f32)


def _split_bf16(a):
    hi = a.astype(bf16)
    return hi, (a - hi.astype(f32)).astype(bf16)


def _dotp(a, b, dims, passes):
    if passes == 1:
        return _dg(a.astype(bf16), b.astype(bf16), dims)
    ah, al = _split_bf16(a)
    bh, bl = _split_bf16(b)
    return _dg(ah, bh, dims) + (_dg(ah, bl, dims) + _dg(al, bh, dims))


def _ones_dot(a, ones_bf16):
    ah, al = _split_bf16(a)
    return _dg(ah, ones_bf16, NN) + _dg(al, ones_bf16, NN)


def _cumsum_rows(x, seg, pos=None):
    if pos is None:
        pos = _iota(x.shape, 0) % seg
    d = 1
    while d < seg:
        x = x + jnp.where(pos >= d, pltpu.roll(x, d, 0), 0.0)
        d *= 2
    return x


def _iota(shape, dim):
    return lax.broadcasted_iota(jnp.int32, shape, dim)


def _eye(n):
    return (_iota((n, n), 0) == _iota((n, n), 1)).astype(f32)


def _block_ones(n, blk):
    return ((_iota((n, n), 0) // blk) == (_iota((n, n), 1) // blk)).astype(f32)


def _softplus(x):
    return jnp.maximum(x, 0.0) + jnp.log(1.0 + jnp.exp(-jnp.abs(x)))


def _log_sigmoid(x):
    return -_softplus(-x)


def _sigmoid(x):
    return 1.0 / (1.0 + jnp.exp(-x))


def _rms(x, g):
    return x * lax.rsqrt(jnp.mean(x * x, -1, keepdims=True) + NORM_EPS) * g


MIX_COLS = (A_COLS, B_COLS, C_COLS, D_COLS)
MIX_WIDTHS = (A_PAD, B_PAD, C_COLS, D_COLS)
MIX_ROWS = sum(MIX_COLS)
GATE_ROWS = Z_COLS + G_COLS


PACK_MIX_ROWS = MIX_ROWS
PACK_GATE_ROWS = 512


def _cast_kernel(w_ref, o_ref):
    o_ref[...] = w_ref[...].astype(bf16)


def _transpose_cast_kernel(w_ref, o_ref):
    o_ref[...] = w_ref[0].T.astype(bf16)


def _pack_w_in(w_in):
    depth = w_in.shape[0]
    wt = jnp.swapaxes(w_in, 1, 2)
    assert MIX_ROWS % PACK_MIX_ROWS == 0 and PACK_MIX_ROWS % 16 == 0 and GATE_ROWS % PACK_GATE_ROWS == 0
    params = pltpu.CompilerParams(dimension_semantics=("arbitrary", "arbitrary"), vmem_limit_bytes=VMEM_LIMIT)
    wt_mix = pl.pallas_call(
        _cast_kernel,
        grid=(depth, MIX_ROWS // PACK_MIX_ROWS),
        in_specs=[pl.BlockSpec((None, PACK_MIX_ROWS, D_MODEL), lambda l, i: (l, i, 0))],
        out_specs=pl.BlockSpec((None, PACK_MIX_ROWS, D_MODEL), lambda l, i: (l, i, 0)),
        out_shape=jax.ShapeDtypeStruct((depth, MIX_ROWS, D_MODEL), bf16),
        compiler_params=params,
        name="pack_mix",
    )(wt)
    rows = PACK_GATE_ROWS
    w_gate = pl.pallas_call(
        _transpose_cast_kernel,
        grid=(depth, GATE_ROWS // rows),
        in_specs=[pl.BlockSpec((pl.Element(1), pl.Element(rows), pl.Element(D_MODEL)),
                               lambda l, j: (l, pl.multiple_of(MIX_ROWS + rows * j, 8), 0))],
        out_specs=pl.BlockSpec((None, D_MODEL, rows), lambda l, j: (l, 0, j)),
        out_shape=jax.ShapeDtypeStruct((depth, D_MODEL, GATE_ROWS), bf16),
        compiler_params=params,
        name="pack_gate",
    )(wt)
    return wt_mix, w_gate


def _inproj_kernel(x_ref, g_ref, wt_ref, oa_ref, ob_ref, oc_ref, od_ref):
    hn = _rms(x_ref[...], g_ref[...]).astype(bf16)
    start = 0
    for o_ref, cols, width in zip((oa_ref, ob_ref, oc_ref, od_ref), MIX_COLS, MIX_WIDTHS):
        o_ref[:, 0:cols] = _dg(hn, wt_ref[start:start + cols, :], NT)
        if width > cols:
            o_ref[:, cols:width] = jnp.zeros((o_ref.shape[0], width - cols), f32)
        start += cols


def _const_spec(shape):
    return pl.BlockSpec(shape, lambda *_: (0,) * len(shape))


def _inproj(x, g, wt, layer, tm):
    m = x.shape[0]
    return pl.pallas_call(
        _inproj_kernel,
        grid=(m // tm,),
        in_specs=[pl.BlockSpec((tm, D_MODEL), lambda i: (i, 0)), _lspec(g, layer), _lspec(wt, layer)],
        out_specs=[pl.BlockSpec((tm, w), lambda i: (i, 0)) for w in MIX_WIDTHS],
        out_shape=[jax.ShapeDtypeStruct((m, w), f32) for w in MIX_WIDTHS],
        compiler_params=pltpu.CompilerParams(dimension_semantics=("arbitrary",),
                                             vmem_limit_bytes=VMEM_LIMIT),
        name="inproj",
    )(x, g, wt)


def _head_ones():
    return _block_ones(W_BR, HEAD_A).astype(bf16)


def _rwkv_features(us, w0, wup, a0, aup, kk_w, ka_w, ones):
    r = us[:, 0:W_BR]
    k = us[:, W_BR:2 * W_BR]
    v = us[:, 2 * W_BR:3 * W_BR]
    lora = us[:, 3 * W_BR:A_PAD]
    w = -_softplus(-(w0 + _bdot(jnp.tanh(lora), wup))) - 0.5
    logdecay = -jnp.exp(w)
    a = _sigmoid(a0 + _bdot(lora, aup))
    kk = k * kk_w
    ss = _ones_dot(kk * kk, ones)
    kk = kk / jnp.maximum(jnp.sqrt(ss), 1e-12)
    kmod = k * (1.0 + (a - 1.0) * ka_w)
    return r, kmod, v, logdecay, kk, a


def _rwkv_finish(o, r, kmod, v, rk, lng, lnb, ones):
    mean = _ones_dot(o, ones) * (1.0 / HEAD_A)
    cen = o - mean
    var = _ones_dot(cen * cen, ones) * (1.0 / HEAD_A)
    o = cen * lax.rsqrt(var + GN_EPS_A) * lng + lnb
    bonus = _ones_dot(r * kmod * rk, ones) * v
    return o + bonus


def _rwkv_prompt_kernel(u_ref, mu_ref, w0_ref, wup_ref, a0_ref, aup_ref, kkw_ref, kaw_ref, rk_ref,
                        lng_ref, lnb_ref, y_ref, s_ref, st_scr, prev_scr, o_scr):
    step = pl.program_id(1)
    T = RWKV_CHUNK
    TT = RWKV_STEP_ROWS

    @pl.when(step == 0)
    def _():
        st_scr[...] = jnp.zeros_like(st_scr)
        prev_scr[...] = jnp.zeros_like(prev_scr)

    R = RWKV_STEP_BATCH
    NC = TT // T
    pieces = [(c, h) for c in range(NC) for h in range(H_A)]
    row0 = _iota((TT, A_PAD), 0) == 0
    pos = _iota((TT, W_BR), 0) % T
    ones = _head_ones()

    def features(q):
        u = u_ref[q]
        u_prev = jnp.where(row0, prev_scr[q], pltpu.roll(u, 1, 0))
        prev_scr[q] = u[TT - 1:TT, :]
        us = u + (u_prev - u) * mu_ref[...]
        r, kmod, v, ld, kk, a = _rwkv_features(us, w0_ref[...], wup_ref[...], a0_ref[...], aup_ref[...],
                                               kkw_ref[...], kaw_ref[...], ones)
        cum = _cumsum_rows(ld, T, pos)
        cum_last = jnp.concatenate(
            [jnp.broadcast_to(cum[(c + 1) * T - 1:(c + 1) * T, :], (T, W_BR)) for c in range(NC)], axis=0)
        g_inv = jnp.exp(-cum)
        g_tail = jnp.exp(cum_last - cum)
        kka = kk * a
        return dict(r=r, kmod=kmod, v=v, cum=cum, at=-kk * jnp.exp(cum - ld), rt=r * jnp.exp(cum),
                    bt=kka * g_inv, kt=kmod * g_inv, btg=kka * g_tail, ktg=kmod * g_tail)

    ri = _iota((2 * T, 2 * T), 0)
    ci = _iota((2 * T, 2 * T), 1)
    ti = jnp.where(ri >= T, ri - T, ri)
    si = jnp.where(ci >= T, ci - T, ci)
    keep = (ti > si) | ((ri >= T) & (ti == si))
    eye_t = _eye(T)
    eye_h = _eye(HEAD_A)

    def independent_stages(f):
        d = dict(ar={}, vh={}, pm={}, x={}, pw={}, lv={}, gcol={}, bkg={}, xar={})

        def products():
            for c, h in pieces:
                rows = slice(c * T, (c + 1) * T)
                sl = slice(h * HEAD_A, (h + 1) * HEAD_A)
                d["vh"][c, h] = f["v"][rows, sl]
                d["ar"][c, h] = jnp.concatenate([f["at"][rows, sl], f["rt"][rows, sl]], axis=0)
                bk = jnp.concatenate([f["bt"][rows, sl], f["kt"][rows, sl]], axis=0)
                d["pm"][c, h] = jnp.where(keep, _bdot_nt(d["ar"][c, h], bk), 0.0)

        def squares_and_values():
            for c, h in pieces:
                rows = slice(c * T, (c + 1) * T)
                sl = slice(h * HEAD_A, (h + 1) * HEAD_A)
                lab = d["pm"][c, h][0:T, 0:T]
                d["x"][c, h] = eye_t + lab
                d["pw"][c, h] = _bdot(lab, lab)
                d["lv"][c, h] = _bdot(d["pm"][c, h][:, T:2 * T], d["vh"][c, h])
                g_last = jnp.exp(f["cum"][(c + 1) * T - 1:(c + 1) * T, sl])
                d["gcol"][c, h] = jnp.sum(eye_h * g_last, axis=1, keepdims=True)
                d["bkg"][c, h] = jnp.concatenate([f["btg"][rows, sl], f["ktg"][rows, sl]], axis=0)

        def inverse_round(last):
            def run():
                for c, h in pieces:
                    x_next = d["x"][c, h] + _bdot(d["pw"][c, h], d["x"][c, h])
                    if not last:
                        d["pw"][c, h] = _bdot(d["pw"][c, h], d["pw"][c, h])
                    d["x"][c, h] = x_next
            return run

        def fold_inverse():
            for c, h in pieces:
                xa = _bdot(d["x"][c, h], d["ar"][c, h][0:T])
                xl = _bdot(d["x"][c, h], d["lv"][c, h][0:T])
                d["xar"][c, h] = jnp.concatenate([xa, d["ar"][c, h][T:2 * T]], axis=0)
                d["lv"][c, h] = jnp.concatenate([xl, d["lv"][c, h][T:2 * T]], axis=0)

        stages = [products, squares_and_values] + [inverse_round(it == 4) for it in range(5)] + [fold_inverse]
        return stages, d

    def dependent_stages(q, f, d):
        st = {}
        base = {}

        def load():
            for h in range(H_A):
                st[h] = st_scr[q, h]

        def read(c):
            def run():
                if c == 0:
                    load()
                for h in range(H_A):
                    base[h] = _bdot(d["xar"][c, h], st[h]) + d["lv"][c, h]
            return run

        def update(c):
            def run():
                for h in range(H_A):
                    st[h] = d["gcol"][c, h] * st[h] + _dotp(
                        d["bkg"][c, h], jnp.concatenate([base[h][0:T], d["vh"][c, h]], axis=0), TN, 1)
                for h in range(H_A):
                    o_scr[q, c * T:(c + 1) * T, h * HEAD_A:(h + 1) * HEAD_A] = (
                        base[h][T:2 * T] + _bdot(d["pm"][c, h][T:2 * T, 0:T], base[h][0:T]))
                if c == NC - 1:
                    for h in range(H_A):
                        st_scr[q, h] = st[h]
                    y_ref[q] = _rwkv_finish(o_scr[q], f["r"], f["kmod"], f["v"], rk_ref[...], lng_ref[...],
                                            lnb_ref[...], ones)
            return run

        return [stage for c in range(NC) for stage in (read(c), update(c))]

    feats = {q: features(q) for q in range(min(2, R))}
    pending = []
    for q in range(R):
        stages, d = independent_stages(feats[q])
        if q + 2 < R:
            feats[q + 2] = features(q + 2)
        for k in range(max(len(stages), len(pending))):
            if k < len(stages):
                stages[k]()
            if k < len(pending):
                pending[k]()
        pending = dependent_stages(q, feats[q], d)
    for stage in pending:
        stage()

    @pl.when(step == pl.num_programs(1) - 1)
    def _():
        s_ref[...] = st_scr[...]


RWKV_PARAMS = ("mu", "w0", "wup", "a0", "aup", "kkw", "kaw", "rk", "lng", "lnb")
GLA_PARAMS = ("gup", "gbias", "gng")
LRU_PARAMS = ("cw", "cb", "lwa", "lba", "lwx", "lbx", "lam")


def _rwkv_prompt(pa, prm, layer):
    b, l, _ = pa.shape
    T = RWKV_STEP_ROWS
    R = RWKV_STEP_BATCH
    return pl.pallas_call(
        _rwkv_prompt_kernel,
        grid=(b // R, l // T),
        in_specs=[pl.BlockSpec((R, T, A_PAD), lambda i, c: (i, c, 0))]
        + [_lspec(prm[k], layer) for k in RWKV_PARAMS],
        out_specs=[pl.BlockSpec((R, T, W_BR), lambda i, c: (i, c, 0)),
                   pl.BlockSpec((R, H_A, HEAD_A, HEAD_A), lambda i, c: (i, 0, 0, 0))],
        out_shape=[jax.ShapeDtypeStruct((b, l, W_BR), f32),
                   jax.ShapeDtypeStruct((b, H_A, HEAD_A, HEAD_A), f32)],
        scratch_shapes=[pltpu.VMEM((R, H_A, HEAD_A, HEAD_A), f32), pltpu.VMEM((R, 1, A_PAD), f32),
                        pltpu.VMEM((R, T, W_BR), f32)],
        compiler_params=pltpu.CompilerParams(dimension_semantics=("arbitrary", "arbitrary"),
                                             vmem_limit_bytes=VMEM_LIMIT),
        name="rwkv_prompt",
    )(pa, *[prm[k] for k in RWKV_PARAMS])


def _gla_prompt_kernel(f_ref, up_ref, bias_ref, ng_ref, y_ref, s_ref, s_scr, o_scr):
    step = pl.program_id(1)
    T = GLA_CHUNK
    TT = GLA_STEP_ROWS
    R = GLA_STEP_BATCH
    NCR = TT // T
    NC = R * NCR

    @pl.when(step == 0)
    def _():
        s_scr[...] = jnp.zeros_like(s_scr)

    f = f_ref[...].reshape(R * TT, B_PAD)
    hk = H_B * DK_B
    q = f[:, 0:hk] * (DK_B ** -0.5)
    k = f[:, hk:2 * hk]
    v = f[:, 2 * hk:2 * hk + W_BR]
    gl = f[:, 2 * hk + W_BR:B_PAD]
    P = GLA_PASSES
    g = _log_sigmoid(_bdot(gl, up_ref[...]) + bias_ref[...]) * (1.0 / GLA_TAU)
    bcum = _cumsum_rows(g, T)
    b_last = jnp.concatenate(
        [jnp.broadcast_to(bcum[(c + 1) * T - 1:(c + 1) * T, :], (T, hk)) for c in range(NC)], axis=0)
    qe = q * jnp.exp(bcum)
    ke = k * jnp.exp(-bcum)
    kl = k * jnp.exp(b_last - bcum)
    causal = _iota((T, T), 0) >= _iota((T, T), 1)
    eye_k = _eye(DK_B)
    pieces = [(c, h) for c in range(NC) for h in range(H_B)]
    av, kv, ecol = {}, {}, {}
    for c, h in pieces:
        rows = slice(c * T, (c + 1) * T)
        ks = slice(h * DK_B, (h + 1) * DK_B)
        vs = slice(h * DV_B, (h + 1) * DV_B)
        att = jnp.where(causal, _dotp(qe[rows, ks], ke[rows, ks], NT, P), 0.0)
        av[c, h] = _dotp(att, v[rows, vs], NN, P)
        kv[c, h] = _dotp(kl[rows, ks], v[rows, vs], TN, P)
        e_last = jnp.exp(bcum[(c + 1) * T - 1:(c + 1) * T, ks])
        ecol[c, h] = jnp.sum(eye_k * e_last, axis=1, keepdims=True)
    chains = [(q_, h) for q_ in range(R) for h in range(H_B)]
    s = {qh: s_scr[qh] for qh in chains}
    for j in range(NCR):
        for q_, h in chains:
            c = q_ * NCR + j
            rows = slice(c * T, (c + 1) * T)
            ks = slice(h * DK_B, (h + 1) * DK_B)
            o_scr[rows, h * DV_B:(h + 1) * DV_B] = av[c, h] + _dotp(qe[rows, ks], s[q_, h], NN, P)
            s[q_, h] = ecol[c, h] * s[q_, h] + kv[c, h]
    for qh in chains:
        s_scr[qh] = s[qh]
    o = o_scr[...]
    ms = _ones_dot(o * o, _block_ones(W_BR, DV_B).astype(bf16)) * (1.0 / DV_B)
    y_ref[...] = (o * lax.rsqrt(ms + NORM_EPS) * ng_ref[...]).reshape(R, TT, W_BR)

    @pl.when(step == pl.num_programs(1) - 1)
    def _():
        s_ref[...] = s_scr[...]


def _gla_prompt(pb, prm, layer):
    b, l, _ = pb.shape
    T = GLA_STEP_ROWS
    R = GLA_STEP_BATCH
    return pl.pallas_call(
        _gla_prompt_kernel,
        grid=(b // R, l // T),
        in_specs=[pl.BlockSpec((R, T, B_PAD), lambda i, c: (i, c, 0))]
        + [_lspec(prm[k], layer) for k in GLA_PARAMS],
        out_specs=[pl.BlockSpec((R, T, W_BR), lambda i, c: (i, c, 0)),
                   pl.BlockSpec((R, H_B, DK_B, DV_B), lambda i, c: (i, 0, 0, 0))],
        out_shape=[jax.ShapeDtypeStruct((b, l, W_BR), f32),
                   jax.ShapeDtypeStruct((b, H_B, DK_B, DV_B), f32)],
        scratch_shapes=[pltpu.VMEM((R, H_B, DK_B, DV_B), f32), pltpu.VMEM((R * T, W_BR), f32)],
        compiler_params=pltpu.CompilerParams(dimension_semantics=("arbitrary", "arbitrary"),
                                             vmem_limit_bytes=VMEM_LIMIT),
        name="gla_prompt",
    )(pb, *[prm[k] for k in GLA_PARAMS])


def _swa_prompt_kernel(cur_ref, prev_ref, sink_ref, y_ref):
    step = pl.program_id(1)
    W = WINDOW
    NB = SWA_STEP_BLOCKS
    qo, ko, vo = 0, H_C * HD_C, H_C * HD_C + KV_C * HD_C
    assert G_C == 2
    row = _iota((G_C * W, 2 * W), 0)
    s = _iota((G_C * W, 2 * W), 1)
    t = jnp.where(row >= W, row - W, row)
    dist = W + t - s
    ok = (dist >= 0) & (dist <= W)
    ok_first = ok & ((s >= W) | (step > 0))
    distf = dist.astype(f32)
    second = _iota((G_C * W, 1), 0) >= W
    scale = HD_C ** -0.5

    def band(col, j):
        if j == 0:
            return jnp.concatenate([prev_ref[0, :, col:col + HD_C], cur_ref[0, 0:W, col:col + HD_C]], axis=0)
        return cur_ref[0, (j - 1) * W:(j + 1) * W, col:col + HD_C]

    pieces = [(j, g) for j in range(NB) for g in range(KV_C)]
    scores, sinks = {}, {}
    for g in range(KV_C):
        h0, h1 = g * G_C, g * G_C + 1
        sinks[g] = jnp.where(second, sink_ref[:, h1:h1 + 1], sink_ref[:, h0:h0 + 1])
    for j, g in pieces:
        h0, h1 = g * G_C, g * G_C + 1
        q2 = jnp.concatenate([cur_ref[0, j * W:(j + 1) * W, qo + h0 * HD_C:qo + (h0 + 1) * HD_C],
                              cur_ref[0, j * W:(j + 1) * W, qo + h1 * HD_C:qo + (h1 + 1) * HD_C]], axis=0)
        slope = jnp.where(second, ALIBI_SLOPES[h1], ALIBI_SLOPES[h0])
        raw = _bdot_nt(q2, band(ko + g * HD_C, j)) * scale - slope * distf
        scores[j, g] = jnp.where(ok_first if j == 0 else ok, raw, NEG_BIG)
    probs, dens = {}, {}
    for j, g in pieces:
        m = jnp.maximum(jnp.max(scores[j, g], -1, keepdims=True), sinks[g])
        p = jnp.exp(scores[j, g] - m)
        probs[j, g] = p
        dens[j, g] = jnp.sum(p, -1, keepdims=True) + jnp.exp(sinks[g] - m)
    for j, g in pieces:
        out = _bdot(probs[j, g], band(vo + g * HD_C, j)) / dens[j, g]
        for jj in range(G_C):
            h = g * G_C + jj
            y_ref[0, j * W:(j + 1) * W, h * HD_C:(h + 1) * HD_C] = out[jj * W:(jj + 1) * W]


def _swa_prompt(pc, prm, layer):
    b, l, _ = pc.shape
    W = WINDOW
    NB = SWA_STEP_BLOCKS
    return pl.pallas_call(
        _swa_prompt_kernel,
        grid=(b, l // (NB * W)),
        in_specs=[pl.BlockSpec((1, NB * W, C_COLS), lambda i, c: (i, c, 0)),
                  pl.BlockSpec((1, W, C_COLS), lambda i, c: (i, jnp.maximum(NB * c - 1, 0), 0)),
                  _lspec(prm["sinks"], layer)],
        out_specs=pl.BlockSpec((1, NB * W, W_BR), lambda i, c: (i, c, 0)),
        out_shape=jax.ShapeDtypeStruct((b, l, W_BR), f32),
        compiler_params=pltpu.CompilerParams(dimension_semantics=("arbitrary", "arbitrary"),
                                             vmem_limit_bytes=VMEM_LIMIT),
        name="swa_prompt",
    )(pc, pc, prm["sinks"])


def _lru_gates(xc, wa, ba, wx, bx, lam):
    r = _sigmoid(_bdot(xc, wa) + ba)
    i = _sigmoid(_bdot(xc, wx) + bx)
    log_a = C_RG * r * _log_sigmoid(lam)
    a = jnp.exp(log_a)
    bterm = jnp.sqrt(1.0 - jnp.exp(2.0 * log_a)) * (i * xc)
    return a, bterm


def _lru_prompt_kernel(x_ref, cw_ref, cb_ref, wa_ref, ba_ref, wx_ref, bx_ref, lam_ref, y_ref,
                       xbuf_scr, h_scr):
    c = pl.program_id(1)
    T = LRU_CHUNK
    PADR = 8

    @pl.when(c == 0)
    def _():
        xbuf_scr[0:PADR, :] = jnp.zeros((PADR, W_BR), f32)
        h_scr[...] = jnp.zeros_like(h_scr)

    x = x_ref[0]
    xbuf_scr[PADR:PADR + T, :] = x
    xc = cb_ref[...] + x * cw_ref[CONV_W - 1:CONV_W, :]
    for j in range(1, CONV_W):
        xc = xc + xbuf_scr[PADR - j:PADR - j + T, :] * cw_ref[CONV_W - 1 - j:CONV_W - j, :]
    xbuf_scr[0:PADR, :] = x[T - PADR:T, :]
    a, bv = _lru_gates(xc, wa_ref[...], ba_ref[...], wx_ref[...], bx_ref[...], lam_ref[...])
    row = _iota((T, W_BR), 0)
    d = 1
    while d < T:
        keep = row >= d
        a_sh = jnp.where(keep, pltpu.roll(a, d, 0), 1.0)
        b_sh = jnp.where(keep, pltpu.roll(bv, d, 0), 0.0)
        bv = a * b_sh + bv
        a = a * a_sh
        d *= 2
    h = a * h_scr[...] + bv
    y_ref[0] = h
    h_scr[...] = h[T - 1:T, :]


def _lru_prompt(pd, prm, layer):
    b, l, _ = pd.shape
    T = LRU_CHUNK
    return pl.pallas_call(
        _lru_prompt_kernel,
        grid=(b, l // T),
        in_specs=[pl.BlockSpec((1, T, W_BR), lambda i, c: (i, c, 0))]
        + [_lspec(prm[k], layer) for k in LRU_PARAMS],
        out_specs=pl.BlockSpec((1, T, W_BR), lambda i, c: (i, c, 0)),
        out_shape=jax.ShapeDtypeStruct((b, l, W_BR), f32),
        scratch_shapes=[pltpu.VMEM((T + 8, W_BR), f32), pltpu.VMEM((1, W_BR), f32)],
        compiler_params=pltpu.CompilerParams(dimension_semantics=("arbitrary", "arbitrary"),
                                             vmem_limit_bytes=VMEM_LIMIT),
        name="lru_prompt",
    )(pd, *[prm[k] for k in LRU_PARAMS])


def _colbcast(row, n_out, eye_bf16):
    c = row.shape[1]
    hi, lo = _split_bf16(row)
    return (_dg(eye_bf16, jnp.broadcast_to(hi, (n_out, c)), NT)
            + _dg(eye_bf16, jnp.broadcast_to(lo, (n_out, c)), NT))


def _decode_states_kernel(pa_ref, pb_ref, shift_ref, swkv_ref, sgla_ref,
                          mu_ref, w0_ref, wup_ref, a0_ref, aup_ref, kkw_ref, kaw_ref, rk_ref, lng_ref, lnb_ref,
                          gup_ref, gbias_ref, gng_ref, *rest, n_prev):
    earlier, rest = rest[:2 if n_prev else 0], rest[2 if n_prev else 0:]
    oa_ref, ob_ref, swkv_all, sgla_all, fa_scr, fb_scr, vb_scr, oa_scr, ob_scr = rest
    for src, dst in zip(earlier, (swkv_all, sgla_all)):
        dst[0:n_prev] = src[...]
    swkv_out, sgla_out = swkv_all.at[n_prev], sgla_all.at[n_prev]
    h = pl.program_id(0)
    hk = H_B * DK_B

    @pl.when(h == 0)
    def _():
        u = pa_ref[...]
        us = u + (shift_ref[...] - u) * mu_ref[...]
        r, kmod, v, ld, kk, a = _rwkv_features(us, w0_ref[...], wup_ref[...], a0_ref[...], aup_ref[...],
                                               kkw_ref[...], kaw_ref[...], _head_ones())
        for i, t in enumerate((r, kmod, v, jnp.exp(ld), kk, kk * a)):
            fa_scr[i] = t.T
        fb = pb_ref[...]
        gb = _log_sigmoid(_bdot(fb[:, 2 * hk + W_BR:B_PAD], gup_ref[...]) + gbias_ref[...]) * (1.0 / GLA_TAU)
        for i, t in enumerate((fb[:, 0:hk] * (DK_B ** -0.5), fb[:, hk:2 * hk], jnp.exp(gb))):
            fb_scr[i] = t.T
        vb_scr[...] = fb[:, 2 * hk:2 * hk + W_BR].T

    hs = pl.ds(pl.multiple_of(h * HEAD_A, HEAD_A), HEAD_A)
    r_h, km_h, v_h, w_h, kk_h, kka_h = (fa_scr[i, hs, :] for i in range(6))
    sub = _iota((8, r_h.shape[1]), 0)
    for g in range(HEAD_A // 8):
        rows8 = jnp.zeros((8, r_h.shape[1]), f32)
        for j in range(8):
            vi = g * 8 + j
            s = swkv_ref[vi]
            sa = -jnp.sum(s * kk_h, axis=0, keepdims=True)
            s_new = s * w_h + sa * kka_h + v_h[vi:vi + 1, :] * km_h
            swkv_out[vi] = s_new
            rows8 = jnp.where(sub == j, jnp.sum(s_new * r_h, axis=0, keepdims=True), rows8)
        oa_scr[pl.ds(pl.multiple_of(h * HEAD_A + g * 8, 8), 8), :] = rows8
    ds_ = pl.ds(pl.multiple_of(h * DK_B, DK_B), DK_B)
    q_h, k_h, eg_h = (fb_scr[i, ds_, :] for i in range(3))
    vb_h = vb_scr[hs, :]
    acc = jnp.zeros_like(vb_h)
    for d in range(DK_B):
        s_new = eg_h[d:d + 1, :] * sgla_ref[d] + k_h[d:d + 1, :] * vb_h
        sgla_out[d] = s_new
        acc = acc + q_h[d:d + 1, :] * s_new
    ob_scr[hs, :] = acc

    @pl.when(h == pl.num_programs(0) - 1)
    def _():
        oa_ref[...] = oa_scr[...].T
        ob_ref[...] = ob_scr[...].T


def _decode_states(pa, pb, shift0, swkv_t, sgla_t, prm, layer, earlier):
    assert H_A == H_B and DV_B == HEAD_A
    n = pa.shape[0]
    n_prev = layer
    wkv_dims, gla_dims = (HEAD_A, HEAD_A, n), (DK_B, DV_B, n)
    of_head = lambda dims: pl.BlockSpec((None, None) + dims, lambda h: (layer, h) + (0,) * len(dims))
    stacked = lambda nl, dims: pl.BlockSpec((nl, None) + dims, lambda h: (0, h) + (0,) * len(dims))
    keys = RWKV_PARAMS + GLA_PARAMS
    in_specs = [_const_spec((n, A_PAD)), _const_spec((n, B_PAD)), _const_spec((n, A_PAD)),
                of_head(wkv_dims), of_head(gla_dims)] + [_lspec(prm[k], layer) for k in keys]
    if n_prev:
        in_specs += [stacked(n_prev, wkv_dims), stacked(n_prev, gla_dims)]
    return pl.pallas_call(
        functools.partial(_decode_states_kernel, n_prev=n_prev),
        grid=(H_A,),
        in_specs=in_specs,
        out_specs=[_const_spec((n, W_BR)), _const_spec((n, W_BR)),
                   stacked(n_prev + 1, wkv_dims), stacked(n_prev + 1, gla_dims)],
        out_shape=[jax.ShapeDtypeStruct((n, W_BR), f32), jax.ShapeDtypeStruct((n, W_BR), f32),
                   jax.ShapeDtypeStruct((n_prev + 1, H_A) + wkv_dims, f32),
                   jax.ShapeDtypeStruct((n_prev + 1, H_B) + gla_dims, f32)],
        scratch_shapes=[pltpu.VMEM((6, W_BR, n), f32), pltpu.VMEM((3, H_B * DK_B, n), f32),
                        pltpu.VMEM((W_BR, n), f32), pltpu.VMEM((W_BR, n), f32), pltpu.VMEM((W_BR, n), f32)],
        compiler_params=pltpu.CompilerParams(dimension_semantics=("arbitrary",),
                                             vmem_limit_bytes=VMEM_LIMIT),
        name="decode_states",
    )(pa, pb, shift0, swkv_t, sgla_t, *[prm[k] for k in keys], *(earlier if n_prev else ()))


def _decode_kernel(pa_ref, oa_ref, ob_ref, q8_ref, kvn_ref, pd_ref, shift_ref, kbuf_ref, vbuf_ref,
                   c0_ref, c1_ref, c2_ref, h0_ref,
                   mu_ref, w0_ref, wup_ref, a0_ref, aup_ref, kkw_ref, kaw_ref, rk_ref, lng_ref, lnb_ref,
                   gup_ref, gbias_ref, gng_ref, sink8_ref, slope8_ref,
                   cw_ref, cb_ref, wa_ref, ba_ref, wx_ref, bx_ref, lam_ref, *rest, n_prev):
    earlier, rest = rest[:2 if n_prev else 0], rest[2 if n_prev else 0:]
    ya_ref, yb_ref, yc8_ref, yd_ref, kout_all, vout_all = rest
    for src, dst in zip(earlier, (kout_all, vout_all)):
        dst[0:n_prev] = src[...]
    kout_ref, vout_ref = kout_all.at[n_prev], vout_all.at[n_prev]
    bt = pa_ref.shape[0]
    samples = range(bt)
    u = pa_ref[...]
    us = u + (shift_ref[...] - u) * mu_ref[...]
    ones = _head_ones()
    r, kmod, v, _, _, _ = _rwkv_features(us, w0_ref[...], wup_ref[...], a0_ref[...], aup_ref[...],
                                         kkw_ref[...], kaw_ref[...], ones)
    wdist = (WINDOW - _iota((1, WINDOW), 1)).astype(f32)
    last = _iota((HD_C, WINDOW), 1) == WINDOW - 1
    scale = HD_C ** -0.5
    eye_c = _eye(HD_C).astype(bf16)
    half = KV_C * HD_C
    pieces = [(b, g) for b in samples for g in range(KV_C)]
    kn = lambda b, g: kvn_ref[b:b + 1, g * HD_C:(g + 1) * HD_C]
    vn = lambda b, g: kvn_ref[b:b + 1, half + g * HD_C:half + (g + 1) * HD_C]
    scores = {(b, g): _bdot(q8_ref[b, g], kbuf_ref[b, g]) * scale - slope8_ref[g] * wdist
              for b, g in pieces}
    probs, tails = {}, {}
    for b, g in pieces:
        sink = sink8_ref[g]
        sn = jnp.sum(q8_ref[b, g] * kn(b, g), axis=1, keepdims=True) * scale
        m = jnp.maximum(jnp.maximum(jnp.max(scores[b, g], axis=1, keepdims=True), sn), sink)
        p = jnp.exp(scores[b, g] - m)
        pn = jnp.exp(sn - m)
        probs[b, g] = p
        tails[b, g] = (pn, jnp.sum(p, axis=1, keepdims=True) + pn + jnp.exp(sink - m))
    for b, g in pieces:
        pn, den = tails[b, g]
        yc8_ref[b, g] = (_bdot_nt(probs[b, g], vbuf_ref[b, g]) + pn * vn(b, g)) / den
    for b, g in pieces:
        for src, dst, new in ((kbuf_ref, kout_ref, kn(b, g)), (vbuf_ref, vout_ref, vn(b, g))):
            new_col = _colbcast(new, WINDOW, eye_c)
            dst[b, g] = jnp.where(last, new_col, pltpu.roll(src[b, g], WINDOW - 1, 1))

    ya_ref[...] = _rwkv_finish(oa_ref[...], r, kmod, v, rk_ref[...], lng_ref[...], lnb_ref[...], ones)
    ob = ob_ref[...]
    assert DV_B == HEAD_A
    ms = _ones_dot(ob * ob, ones) * (1.0 / DV_B)
    yb_ref[...] = ob * lax.rsqrt(ms + NORM_EPS) * gng_ref[...]
    xd = pd_ref[...]
    xc = (cb_ref[...] + c0_ref[...] * cw_ref[0:1, :] + c1_ref[...] * cw_ref[1:2, :]
          + c2_ref[...] * cw_ref[2:3, :] + xd * cw_ref[3:4, :])
    al, bterm = _lru_gates(xc, wa_ref[...], ba_ref[...], wx_ref[...], bx_ref[...], lam_ref[...])
    yd_ref[...] = al * h0_ref[...] + bterm


def _decode(pa, oa, ob, pc, pd, shift0, kbuf, vbuf, c0, c1, c2, h0, prm, layer, earlier, bt):
    n = pa.shape[0]
    n_prev = layer
    half = KV_C * HD_C
    q8 = jnp.pad(pc[:, 0:W_BR].reshape(n, KV_C, G_C, HD_C), ((0, 0), (0, 0), (0, 8 - G_C), (0, 0)))
    kvn = pc[:, W_BR:]
    slope8 = jnp.pad(jnp.asarray(ALIBI_SLOPES, f32).reshape(KV_C, G_C, 1), ((0, 0), (0, 8 - G_C), (0, 0)))
    rows = lambda w: pl.BlockSpec((bt, w), lambda i: (i, 0))
    qspec = pl.BlockSpec((bt, KV_C, 8, HD_C), lambda i: (i, 0, 0, 0))
    cache_dims = (KV_C, HD_C, WINDOW)
    cache_in = pl.BlockSpec((None, bt) + cache_dims, lambda i: (layer, i, 0, 0, 0))
    stacked = lambda nl: pl.BlockSpec((nl, bt) + cache_dims, lambda i: (0, i, 0, 0, 0))
    in_specs = [rows(A_PAD), rows(W_BR), rows(W_BR), qspec, rows(2 * half), rows(D_COLS), rows(A_PAD),
                cache_in, cache_in, rows(W_BR), rows(W_BR), rows(W_BR), rows(W_BR)]
    in_specs += [_lspec(prm[k], layer) for k in RWKV_PARAMS + GLA_PARAMS]
    in_specs += [_lspec(prm["sink8"], layer), _const_spec((KV_C, 8, 1))]
    in_specs += [_lspec(prm[k], layer) for k in LRU_PARAMS]
    if n_prev:
        in_specs += [stacked(n_prev)] * 2
    out_specs = [rows(W_BR), rows(W_BR), qspec, rows(W_BR)] + [stacked(n_prev + 1)] * 2
    out_shape = [jax.ShapeDtypeStruct((n, W_BR), f32), jax.ShapeDtypeStruct((n, W_BR), f32),
                 jax.ShapeDtypeStruct((n, KV_C, 8, HD_C), f32), jax.ShapeDtypeStruct((n, W_BR), f32)]
    out_shape += [jax.ShapeDtypeStruct((n_prev + 1, n) + cache_dims, f32)] * 2
    ya, yb, yc8, yd, k1, v1 = pl.pallas_call(
        functools.partial(_decode_kernel, n_prev=n_prev),
        grid=(n // bt,),
        in_specs=in_specs,
        out_specs=out_specs,
        out_shape=out_shape,
        compiler_params=pltpu.CompilerParams(dimension_semantics=("arbitrary",),
                                             vmem_limit_bytes=VMEM_LIMIT),
        name="decode_mixers",
    )(pa, oa, ob, q8, kvn, pd, shift0, kbuf, vbuf, c0, c1, c2, h0,
      *[prm[k] for k in RWKV_PARAMS + GLA_PARAMS], prm["sink8"], slope8, *[prm[k] for k in LRU_PARAMS],
      *(earlier if n_prev else ()))
    return ya, yb, yc8[:, :, 0:G_C, :].reshape(n, W_BR), yd, k1, v1


def _merge_kernel(x_ref, ya_ref, yb_ref, yc_ref, yd_ref, g_ref, w_ref, wbr_ref, wout_ref,
                  fg_ref, o_ref, *, final):
    x = x_ref[...]
    hn = _rms(x, g_ref[...]).astype(bf16)
    ys = (ya_ref, yb_ref, yc_ref, yd_ref)
    merged = None
    for n in range(N_BRANCH):
        z = _dg(hn, w_ref[:, n * W_BR:(n + 1) * W_BR], NN)
        yz = ys[n][...] * (z * _sigmoid(z))
        br = jnp.dot(yz.astype(bf16), wbr_ref[n], preferred_element_type=f32)
        gate = _sigmoid(_dg(hn, w_ref[:, Z_COLS + n * D_MODEL:Z_COLS + (n + 1) * D_MODEL], NN))
        merged = gate * br if merged is None else merged + gate * br
    out = x + jnp.dot(merged.astype(bf16), wout_ref[...], preferred_element_type=f32)
    if final:
        out = _rms(out, fg_ref[...])
    o_ref[...] = out


def _merge(x, ya, yb, yc, yd, g, wt, wbr, wout, fg, layer, tm, final):
    m = x.shape[0]
    tile = lambda w: pl.BlockSpec((tm, w), lambda i: (i, 0))
    per_layer = lambda shape: pl.BlockSpec((None,) + shape, lambda *_: (layer,) + (0,) * len(shape),
                                           pipeline_mode=pl.Buffered(1))
    return pl.pallas_call(
        functools.partial(_merge_kernel, final=final),
        grid=(m // tm,),
        in_specs=[tile(D_MODEL), tile(W_BR), tile(W_BR), tile(W_BR), tile(W_BR), _lspec(g, layer),
                  per_layer((D_MODEL, GATE_ROWS)),
                  per_layer((N_BRANCH, W_BR, D_MODEL)),
                  per_layer((D_MODEL, D_MODEL)), _const_spec((1, D_MODEL))],
        out_specs=tile(D_MODEL),
        out_shape=jax.ShapeDtypeStruct((m, D_MODEL), f32),
        compiler_params=pltpu.CompilerParams(dimension_semantics=("arbitrary",),
                                             vmem_limit_bytes=VMEM_LIMIT),
        name="merge_final" if final else "merge",
    )(x, ya, yb, yc, yd, g, wt, wbr, wout, fg)


def _lspec(arr, layer):
    return pl.BlockSpec((None,) + arr.shape[1:], lambda *_: (layer,) + (0,) * (arr.ndim - 1))


def _stacked_params(norm_g, mu_shift, w0, w_decay_up, a0, a_icl_up, k_k, k_a, r_k, ln_x_g, ln_x_b,
                    gla_gate_up, gla_gate_b, gla_norm_g, swa_sinks, lru_conv_w, lru_conv_b, lru_wa, lru_ba,
                    lru_wx, lru_bx, lru_lambda):
    depth = norm_g.shape[0]
    row = lambda t: t.reshape(depth, 1, -1)
    pad_rows = lambda w, start: jnp.pad(w, ((0, 0), (start, LANE - start - w.shape[1]), (0, 0)))
    dense = lambda w: jnp.einsum("lnij,nm->lnimj", w, jnp.eye(w.shape[1], dtype=w.dtype)).reshape(
        depth, W_BR, W_BR).astype(bf16)
    return dict(
        g=row(norm_g),
        mu=row(jnp.pad(mu_shift, ((0, 0), (0, A_PAD - A_COLS)))),
        w0=row(w0), wup=pad_rows(w_decay_up, 0), a0=row(a0), aup=pad_rows(a_icl_up, R_DECAY),
        kkw=row(k_k), kaw=row(k_a), rk=row(r_k), lng=row(ln_x_g), lnb=row(ln_x_b),
        gup=pad_rows(gla_gate_up, 0), gbias=row(gla_gate_b), gng=row(jnp.tile(gla_norm_g, (1, H_B))),
        sinks=row(swa_sinks),
        sink8=jnp.pad(swa_sinks.reshape(depth, KV_C, G_C, 1), ((0, 0), (0, 0), (0, 8 - G_C), (0, 0))),
        cw=lru_conv_w, cb=row(lru_conv_b),
        lwa=dense(lru_wa), lba=row(lru_ba), lwx=dense(lru_wx), lbx=row(lru_bx), lam=row(lru_lambda),
    )


def kernel(x_prompt, x_sample, state_wkv, state_shift, state_gla, cache_swa_k, cache_swa_v, state_lru_conv, state_lru_h, norm_g, w_in, mu_shift, w0, w_decay_up, a0, a_icl_up, k_k, k_a, r_k, ln_x_g, ln_x_b, gla_gate_up, gla_gate_b, gla_norm_g, swa_sinks, lru_conv_w, lru_conv_b, lru_wa, lru_ba, lru_wx, lru_bx, lru_lambda, w_branch, w_out, final_norm_g):
    bp, lp, _ = x_prompt.shape
    bs = x_sample.shape[0]
    depth = w_in.shape[0]
    fg = final_norm_g.reshape(1, -1)
    xp = x_prompt.reshape(bp * lp, D_MODEL)
    xs = x_sample.reshape(bs, D_MODEL)
    outs_p = [[] for _ in range(7)]
    outs_s = [[] for _ in range(7)]
    wt_mix, w_gate = _pack_w_in(w_in)
    wbr_all = w_branch.astype(bf16)
    wout_all = w_out.astype(bf16)
    stacked_s = None
    prm = _stacked_params(norm_g, mu_shift, w0, w_decay_up, a0, a_icl_up, k_k, k_a, r_k, ln_x_g, ln_x_b,
                          gla_gate_up, gla_gate_b, gla_norm_g, swa_sinks, lru_conv_w, lru_conv_b, lru_wa,
                          lru_ba, lru_wx, lru_bx, lru_lambda)
    shift_pad = jnp.pad(state_shift, ((0, 0), (0, 0), (0, A_PAD - A_COLS)))
    wkv_t = jnp.transpose(state_wkv, (0, 2, 3, 4, 1))
    gla_t = jnp.transpose(state_gla, (0, 2, 3, 4, 1))
    stacked_states = None
    kbuf_t = jnp.transpose(cache_swa_k, (0, 1, 3, 4, 2))
    vbuf_t = jnp.transpose(cache_swa_v, (0, 1, 3, 4, 2))
    for l in range(depth):
        final = l == depth - 1
        pa, pb, pc, pd = _inproj(xp, prm["g"], wt_mix, l, tm=1024)
        pa3, pb3 = pa.reshape(bp, lp, A_PAD), pb.reshape(bp, lp, B_PAD)
        pc3, pd3 = pc.reshape(bp, lp, C_COLS), pd.reshape(bp, lp, D_COLS)
        ya, st_t = _rwkv_prompt(pa3, prm, l)
        yb, sgla = _gla_prompt(pb3, prm, l)
        yc = _swa_prompt(pc3, prm, l)
        yd = _lru_prompt(pd3, prm, l)
        flat = lambda t: t.reshape(bp * lp, W_BR)
        xp = _merge(xp, flat(ya), flat(yb), flat(yc), flat(yd), prm["g"], w_gate, wbr_all, wout_all, fg, l,
                    tm=512, final=final)
        kv = pc3[:, lp - WINDOW:, H_C * HD_C:]
        outs_p[0].append(jnp.swapaxes(st_t, -1, -2))
        outs_p[1].append(pa3[:, lp - 1, :A_COLS])
        outs_p[2].append(sgla)
        outs_p[3].append(kv[:, :, :KV_C * HD_C].reshape(bp, WINDOW, KV_C, HD_C))
        outs_p[4].append(kv[:, :, KV_C * HD_C:].reshape(bp, WINDOW, KV_C, HD_C))
        outs_p[5].append(pd3[:, lp - (CONV_W - 1):, :])
        outs_p[6].append(yd[:, lp - 1, :])
        sa, sb, sc, sd = _inproj(xs, prm["g"], wt_mix, l, tm=bs)
        conv0 = state_lru_conv[l]
        oa_s, ob_s, *stacked_states = _decode_states(sa, sb, shift_pad[l], wkv_t, gla_t, prm, l, stacked_states)
        ya_s, yb_s, yc_s, yd_s, *stacked_s = _decode(
            sa, oa_s, ob_s, sc, sd, shift_pad[l], kbuf_t, vbuf_t,
            conv0[:, 0], conv0[:, 1], conv0[:, 2], state_lru_h[l], prm, l, stacked_s, bt=16)
        xs = _merge(xs, ya_s, yb_s, yc_s, yd_s, prm["g"], w_gate, wbr_all, wout_all, fg, l, tm=bs, final=final)
        outs_s[1].append(sa[:, :A_COLS])
        outs_s[5].append(jnp.stack([conv0[:, 1], conv0[:, 2], sd], axis=1))
        outs_s[6].append(yd_s)
    y_prompt = xp.reshape(bp, lp, D_MODEL)
    y_sample = xs.reshape(bs, 1, D_MODEL)
    sp = [jnp.stack(t) for t in outs_p]
    wkv_s, gla_s = (jnp.transpose(t, (0, 4, 1, 2, 3)) for t in stacked_states)
    k_s, v_s = (jnp.transpose(t, (0, 1, 4, 2, 3)) for t in stacked_s)
    shift_s, conv_s, h_s = (jnp.stack(outs_s[i]) for i in (1, 5, 6))
    return (y_prompt, y_sample, sp[0], wkv_s, sp[1], shift_s, sp[2], gla_s, sp[3], k_s, sp[4], v_s,
            sp[5], conv_s, sp[6], h_s)
```

```python
import functools

import jax
import jax.numpy as jnp
from jax import lax
from jax.experimental import pallas as pl
from jax.experimental.pallas import tpu as pltpu

f32 = jnp.float32
bf16 = jnp.bfloat16

D_MODEL = 1024
N_BRANCH = 4
W_BR = 256
HEAD_A = 64
H_A = 4
R_DECAY = 32
R_ICL = 32
GN_EPS_A = 64e-5
H_B = 4
DK_B = 32
DV_B = 64
R_GATE_B = 16
GLA_TAU = 16.0
GLA_CHUNK = 64
GLA_STEP_ROWS = 256
GLA_STEP_BATCH = 4
H_C = 4
KV_C = 2
HD_C = 64
G_C = 2
WINDOW = 128
SWA_STEP_BLOCKS = 8
CONV_W = 4
C_RG = 8.0
NORM_EPS = 1e-6

A_COLS = 3 * W_BR + R_DECAY + R_ICL
B_COLS = 2 * H_B * DK_B + W_BR + R_GATE_B
C_COLS = H_C * HD_C + 2 * KV_C * HD_C
D_COLS = W_BR
Z_COLS = N_BRANCH * W_BR
G_COLS = N_BRANCH * D_MODEL

LANE = 128
A_PAD = 7 * LANE
B_PAD = 5 * LANE
RWKV_CHUNK = 64
RWKV_STEP_ROWS = 256
RWKV_STEP_BATCH = 4
GLA_PASSES = 1
LRU_CHUNK = 256
VMEM_LIMIT = 56 * 1024 * 1024
NEG_BIG = -1e30

ALIBI_SLOPES = tuple(2.0 ** (-8.0 * (h + 1) / H_C) for h in range(H_C))


def _bdot(a, b):
    return jnp.dot(a.astype(bf16), b.astype(bf16), preferred_element_type=f32)


def _bdot_nt(a, b):
    return lax.dot_general(a.astype(bf16), b.astype(bf16), (((1,), (1,)), ((), ())),
                           preferred_element_type=f32)


NN = ((1,), (0,))
NT = ((1,), (1,))
TN = ((0,), (0,))


def _dg(a, b, dims):
    return lax.dot_general(a, b, (dims, ((), ())), preferred_element_type=f32)


def _split_bf16(a):
    hi = a.astype(bf16)
    return hi, (a - hi.astype(f32)).astype(bf16)


def _dotp(a, b, dims, passes):
    if passes == 1:
        return _dg(a.astype(bf16), b.astype(bf16), dims)
    ah, al = _split_bf16(a)
    bh, bl = _split_bf16(b)
    return _dg(ah, bh, dims) + (_dg(ah, bl, dims) + _dg(al, bh, dims))


def _ones_dot(a, ones_bf16):
    ah, al = _split_bf16(a)
    return _dg(ah, ones_bf16, NN) + _dg(al, ones_bf16, NN)


def _cumsum_rows(x, seg, pos=None):
    if pos is None:
        pos = _iota(x.shape, 0) % seg
    d = 1
    while d < seg:
        x = x + jnp.where(pos >= d, pltpu.roll(x, d, 0), 0.0)
        d *= 2
    return x


def _iota(shape, dim):
    return lax.broadcasted_iota(jnp.int32, shape, dim)


def _eye(n):
    return (_iota((n, n), 0) == _iota((n, n), 1)).astype(f32)


def _block_ones(n, blk):
    return ((_iota((n, n), 0) // blk) == (_iota((n, n), 1) // blk)).astype(f32)


def _softplus(x):
    return jnp.maximum(x, 0.0) + jnp.log(1.0 + jnp.exp(-jnp.abs(x)))


def _log_sigmoid(x):
    return -_softplus(-x)


def _sigmoid(x):
    return 1.0 / (1.0 + jnp.exp(-x))


def _rms(x, g):
    return x * lax.rsqrt(jnp.mean(x * x, -1, keepdims=True) + NORM_EPS) * g


MIX_COLS = (A_COLS, B_COLS, C_COLS, D_COLS)
MIX_WIDTHS = (A_PAD, B_PAD, C_COLS, D_COLS)
MIX_ROWS = sum(MIX_COLS)
GATE_ROWS = Z_COLS + G_COLS


PACK_MIX_ROWS = MIX_ROWS
PACK_GATE_ROWS = 512


def _cast_kernel(w_ref, o_ref):
    o_ref[...] = w_ref[...].astype(bf16)


def _transpose_cast_kernel(w_ref, o_ref):
    o_ref[...] = w_ref[0].T.astype(bf16)


def _pack_w_in(w_in):
    depth = w_in.shape[0]
    wt = jnp.swapaxes(w_in, 1, 2)
    assert MIX_ROWS % PACK_MIX_ROWS == 0 and PACK_MIX_ROWS % 16 == 0 and GATE_ROWS % PACK_GATE_ROWS == 0
    params = pltpu.CompilerParams(dimension_semantics=("arbitrary", "arbitrary"), vmem_limit_bytes=VMEM_LIMIT)
    wt_mix = pl.pallas_call(
        _cast_kernel,
        grid=(depth, MIX_ROWS // PACK_MIX_ROWS),
        in_specs=[pl.BlockSpec((None, PACK_MIX_ROWS, D_MODEL), lambda l, i: (l, i, 0))],
        out_specs=pl.BlockSpec((None, PACK_MIX_ROWS, D_MODEL), lambda l, i: (l, i, 0)),
        out_shape=jax.ShapeDtypeStruct((depth, MIX_ROWS, D_MODEL), bf16),
        compiler_params=params,
        name="pack_mix",
    )(wt)
    rows = PACK_GATE_ROWS
    w_gate = pl.pallas_call(
        _transpose_cast_kernel,
        grid=(depth, GATE_ROWS // rows),
        in_specs=[pl.BlockSpec((pl.Element(1), pl.Element(rows), pl.Element(D_MODEL)),
                               lambda l, j: (l, pl.multiple_of(MIX_ROWS + rows * j, 8), 0))],
        out_specs=pl.BlockSpec((None, D_MODEL, rows), lambda l, j: (l, 0, j)),
        out_shape=jax.ShapeDtypeStruct((depth, D_MODEL, GATE_ROWS), bf16),
        compiler_params=params,
        name="pack_gate",
    )(wt)
    return wt_mix, w_gate


def _inproj_kernel(x_ref, g_ref, wt_ref, oa_ref, ob_ref, oc_ref, od_ref):
    hn = _rms(x_ref[...], g_ref[...]).astype(bf16)
    start = 0
    for o_ref, cols, width in zip((oa_ref, ob_ref, oc_ref, od_ref), MIX_COLS, MIX_WIDTHS):
        o_ref[:, 0:cols] = _dg(hn, wt_ref[start:start + cols, :], NT)
        if width > cols:
            o_ref[:, cols:width] = jnp.zeros((o_ref.shape[0], width - cols), f32)
        start += cols


def _const_spec(shape):
    return pl.BlockSpec(shape, lambda *_: (0,) * len(shape))


def _inproj(x, g, wt, layer, tm):
    m = x.shape[0]
    return pl.pallas_call(
        _inproj_kernel,
        grid=(m // tm,),
        in_specs=[pl.BlockSpec((tm, D_MODEL), lambda i: (i, 0)), _lspec(g, layer), _lspec(wt, layer)],
        out_specs=[pl.BlockSpec((tm, w), lambda i: (i, 0)) for w in MIX_WIDTHS],
        out_shape=[jax.ShapeDtypeStruct((m, w), f32) for w in MIX_WIDTHS],
        compiler_params=pltpu.CompilerParams(dimension_semantics=("arbitrary",),
                                             vmem_limit_bytes=VMEM_LIMIT),
        name="inproj",
    )(x, g, wt)


def _head_ones():
    return _block_ones(W_BR, HEAD_A).astype(bf16)


def _rwkv_features(us, w0, wup, a0, aup, kk_w, ka_w, ones):
    r = us[:, 0:W_BR]
    k = us[:, W_BR:2 * W_BR]
    v = us[:, 2 * W_BR:3 * W_BR]
    lora = us[:, 3 * W_BR:A_PAD]
    w = -_softplus(-(w0 + _bdot(jnp.tanh(lora), wup))) - 0.5
    logdecay = -jnp.exp(w)
    a = _sigmoid(a0 + _bdot(lora, aup))
    kk = k * kk_w
    ss = _ones_dot(kk * kk, ones)
    kk = kk / jnp.maximum(jnp.sqrt(ss), 1e-12)
    kmod = k * (1.0 + (a - 1.0) * ka_w)
    return r, kmod, v, logdecay, kk, a


def _rwkv_finish(o, r, kmod, v, rk, lng, lnb, ones):
    mean = _ones_dot(o, ones) * (1.0 / HEAD_A)
    cen = o - mean
    var = _ones_dot(cen * cen, ones) * (1.0 / HEAD_A)
    o = cen * lax.rsqrt(var + GN_EPS_A) * lng + lnb
    bonus = _ones_dot(r * kmod * rk, ones) * v
    return o + bonus


def _rwkv_prompt_kernel(u_ref, mu_ref, w0_ref, wup_ref, a0_ref, aup_ref, kkw_ref, kaw_ref, rk_ref,
                        lng_ref, lnb_ref, y_ref, s_ref, st_scr, prev_scr, o_scr):
    step = pl.program_id(1)
    T = RWKV_CHUNK
    TT = RWKV_STEP_ROWS

    @pl.when(step == 0)
    def _():
        st_scr[...] = jnp.zeros_like(st_scr)
        prev_scr[...] = jnp.zeros_like(prev_scr)

    R = RWKV_STEP_BATCH
    NC = TT // T
    pieces = [(c, h) for c in range(NC) for h in range(H_A)]
    row0 = _iota((TT, A_PAD), 0) == 0
    pos = _iota((TT, W_BR), 0) % T
    ones = _head_ones()

    def features(q):
        u = u_ref[q]
        u_prev = jnp.where(row0, prev_scr[q], pltpu.roll(u, 1, 0))
        prev_scr[q] = u[TT - 1:TT, :]
        us = u + (u_prev - u) * mu_ref[...]
        r, kmod, v, ld, kk, a = _rwkv_features(us, w0_ref[...], wup_ref[...], a0_ref[...], aup_ref[...],
                                               kkw_ref[...], kaw_ref[...], ones)
        cum = _cumsum_rows(ld, T, pos)
        cum_last = jnp.concatenate(
            [jnp.broadcast_to(cum[(c + 1) * T - 1:(c + 1) * T, :], (T, W_BR)) for c in range(NC)], axis=0)
        g_inv = jnp.exp(-cum)
        g_tail = jnp.exp(cum_last - cum)
        kka = kk * a
        return dict(r=r, kmod=kmod, v=v, cum=cum, at=-kk * jnp.exp(cum - ld), rt=r * jnp.exp(cum),
                    bt=kka * g_inv, kt=kmod * g_inv, btg=kka * g_tail, ktg=kmod * g_tail)

    ri = _iota((2 * T, 2 * T), 0)
    ci = _iota((2 * T, 2 * T), 1)
    ti = jnp.where(ri >= T, ri - T, ri)
    si = jnp.where(ci >= T, ci - T, ci)
    keep = (ti > si) | ((ri >= T) & (ti == si))
    eye_t = _eye(T)
    eye_h = _eye(HEAD_A)

    def independent_stages(f):
        d = dict(ar={}, vh={}, pm={}, x={}, pw={}, lv={}, gcol={}, bkg={}, xar={})

        def products():
            for c, h in pieces:
                rows = slice(c * T, (c + 1) * T)
                sl = slice(h * HEAD_A, (h + 1) * HEAD_A)
                d["vh"][c, h] = f["v"][rows, sl]
                d["ar"][c, h] = jnp.concatenate([f["at"][rows, sl], f["rt"][rows, sl]], axis=0)
                bk = jnp.concatenate([f["bt"][rows, sl], f["kt"][rows, sl]], axis=0)
                d["pm"][c, h] = jnp.where(keep, _bdot_nt(d["ar"][c, h], bk), 0.0)

        def squares_and_values():
            for c, h in pieces:
                rows = slice(c * T, (c + 1) * T)
                sl = slice(h * HEAD_A, (h + 1) * HEAD_A)
                lab = d["pm"][c, h][0:T, 0:T]
                d["x"][c, h] = eye_t + lab
                d["pw"][c, h] = _bdot(lab, lab)
                d["lv"][c, h] = _bdot(d["pm"][c, h][:, T:2 * T], d["vh"][c, h])
                g_last = jnp.exp(f["cum"][(c + 1) * T - 1:(c + 1) * T, sl])
                d["gcol"][c, h] = jnp.sum(eye_h * g_last, axis=1, keepdims=True)
                d["bkg"][c, h] = jnp.concatenate([f["btg"][rows, sl], f["ktg"][rows, sl]], axis=0)

        def inverse_round(last):
            def run():
                for c, h in pieces:
                    x_next = d["x"][c, h] + _bdot(d["pw"][c, h], d["x"][c, h])
                    if not last:
                        d["pw"][c, h] = _bdot(d["pw"][c, h], d["pw"][c, h])
                    d["x"][c, h] = x_next
            return run

        def fold_inverse():
            for c, h in pieces:
                xa = _bdot(d["x"][c, h], d["ar"][c, h][0:T])
                xl = _bdot(d["x"][c, h], d["lv"][c, h][0:T])
                d["xar"][c, h] = jnp.concatenate([xa, d["ar"][c, h][T:2 * T]], axis=0)
                d["lv"][c, h] = jnp.concatenate([xl, d["lv"][c, h][T:2 * T]], axis=0)

        stages = [products, squares_and_values] + [inverse_round(it == 4) for it in range(5)] + [fold_inverse]
        return stages, d

    def dependent_stages(q, f, d):
        st = {}
        base = {}

        def load():
            for h in range(H_A):
                st[h] = st_scr[q, h]

        def read(c):
            def run():
                if c == 0:
                    load()
                for h in range(H_A):
                    base[h] = _bdot(d["xar"][c, h], st[h]) + d["lv"][c, h]
            return run

        def update(c):
            def run():
                for h in range(H_A):
                    st[h] = d["gcol"][c, h] * st[h] + _dotp(
                        d["bkg"][c, h], jnp.concatenate([base[h][0:T], d["vh"][c, h]], axis=0), TN, 1)
                for h in range(H_A):
                    o_scr[q, c * T:(c + 1) * T, h * HEAD_A:(h + 1) * HEAD_A] = (
                        base[h][T:2 * T] + _bdot(d["pm"][c, h][T:2 * T, 0:T], base[h][0:T]))
                if c == NC - 1:
                    for h in range(H_A):
                        st_scr[q, h] = st[h]
                    y_ref[q] = _rwkv_finish(o_scr[q], f["r"], f["kmod"], f["v"], rk_ref[...], lng_ref[...],
                                            lnb_ref[...], ones)
            return run

        return [stage for c in range(NC) for stage in (read(c), update(c))]

    feats = {q: features(q) for q in range(min(2, R))}
    pending = []
    for q in range(R):
        stages, d = independent_stages(feats[q])
        if q + 2 < R:
            feats[q + 2] = features(q + 2)
        for k in range(max(len(stages), len(pending))):
            if k < len(stages):
                stages[k]()
            if k < len(pending):
                pending[k]()
        pending = dependent_stages(q, feats[q], d)
    for stage in pending:
        stage()

    @pl.when(step == pl.num_programs(1) - 1)
    def _():
        s_ref[...] = st_scr[...]


RWKV_PARAMS = ("mu", "w0", "wup", "a0", "aup", "kkw", "kaw", "rk", "lng", "lnb")
GLA_PARAMS = ("gup", "gbias", "gng")
LRU_PARAMS = ("cw", "cb", "lwa", "lba", "lwx", "lbx", "lam")


def _rwkv_prompt(pa, prm, layer):
    b, l, _ = pa.shape
    T = RWKV_STEP_ROWS
    R = RWKV_STEP_BATCH
    return pl.pallas_call(
        _rwkv_prompt_kernel,
        grid=(b // R, l // T),
        in_specs=[pl.BlockSpec((R, T, A_PAD), lambda i, c: (i, c, 0))]
        + [_lspec(prm[k], layer) for k in RWKV_PARAMS],
        out_specs=[pl.BlockSpec((R, T, W_BR), lambda i, c: (i, c, 0)),
                   pl.BlockSpec((R, H_A, HEAD_A, HEAD_A), lambda i, c: (i, 0, 0, 0))],
        out_shape=[jax.ShapeDtypeStruct((b, l, W_BR), f32),
                   jax.ShapeDtypeStruct((b, H_A, HEAD_A, HEAD_A), f32)],
        scratch_shapes=[pltpu.VMEM((R, H_A, HEAD_A, HEAD_A), f32), pltpu.VMEM((R, 1, A_PAD), f32),
                        pltpu.VMEM((R, T, W_BR), f32)],
        compiler_params=pltpu.CompilerParams(dimension_semantics=("arbitrary", "arbitrary"),
                                             vmem_limit_bytes=VMEM_LIMIT),
        name="rwkv_prompt",
    )(pa, *[prm[k] for k in RWKV_PARAMS])


def _gla_prompt_kernel(f_ref, up_ref, bias_ref, ng_ref, y_ref, s_ref, s_scr, o_scr):
    step = pl.program_id(1)
    T = GLA_CHUNK
    TT = GLA_STEP_ROWS
    R = GLA_STEP_BATCH
    NCR = TT // T
    NC = R * NCR

    @pl.when(step == 0)
    def _():
        s_scr[...] = jnp.zeros_like(s_scr)

    f = f_ref[...].reshape(R * TT, B_PAD)
    hk = H_B * DK_B
    q = f[:, 0:hk] * (DK_B ** -0.5)
    k = f[:, hk:2 * hk]
    v = f[:, 2 * hk:2 * hk + W_BR]
    gl = f[:, 2 * hk + W_BR:B_PAD]
    P = GLA_PASSES
    g = _log_sigmoid(_bdot(gl, up_ref[...]) + bias_ref[...]) * (1.0 / GLA_TAU)
    bcum = _cumsum_rows(g, T)
    b_last = jnp.concatenate(
        [jnp.broadcast_to(bcum[(c + 1) * T - 1:(c + 1) * T, :], (T, hk)) for c in range(NC)], axis=0)
    qe = q * jnp.exp(bcum)
    ke = k * jnp.exp(-bcum)
    kl = k * jnp.exp(b_last - bcum)
    causal = _iota((T, T), 0) >= _iota((T, T), 1)
    eye_k = _eye(DK_B)
    pieces = [(c, h) for c in range(NC) for h in range(H_B)]
    av, kv, ecol = {}, {}, {}
    for c, h in pieces:
        rows = slice(c * T, (c + 1) * T)
        ks = slice(h * DK_B, (h + 1) * DK_B)
        vs = slice(h * DV_B, (h + 1) * DV_B)
        att = jnp.where(causal, _dotp(qe[rows, ks], ke[rows, ks], NT, P), 0.0)
        av[c, h] = _dotp(att, v[rows, vs], NN, P)
        kv[c, h] = _dotp(kl[rows, ks], v[rows, vs], TN, P)
        e_last = jnp.exp(bcum[(c + 1) * T - 1:(c + 1) * T, ks])
        ecol[c, h] = jnp.sum(eye_k * e_last, axis=1, keepdims=True)
    chains = [(q_, h) for q_ in range(R) for h in range(H_B)]
    s = {qh: s_scr[qh] for qh in chains}
    for j in range(NCR):
        for q_, h in chains:
            c = q_ * NCR + j
            rows = slice(c * T, (c + 1) * T)
            ks = slice(h * DK_B, (h + 1) * DK_B)
            o_scr[rows, h * DV_B:(h + 1) * DV_B] = av[c, h] + _dotp(qe[rows, ks], s[q_, h], NN, P)
            s[q_, h] = ecol[c, h] * s[q_, h] + kv[c, h]
    for qh in chains:
        s_scr[qh] = s[qh]
    o = o_scr[...]
    ms = _ones_dot(o * o, _block_ones(W_BR, DV_B).astype(bf16)) * (1.0 / DV_B)
    y_ref[...] = (o * lax.rsqrt(ms + NORM_EPS) * ng_ref[...]).reshape(R, TT, W_BR)

    @pl.when(step == pl.num_programs(1) - 1)
    def _():
        s_ref[...] = s_scr[...]


def _gla_prompt(pb, prm, layer):
    b, l, _ = pb.shape
    T = GLA_STEP_ROWS
    R = GLA_STEP_BATCH
    return pl.pallas_call(
        _gla_prompt_kernel,
        grid=(b // R, l // T),
        in_specs=[pl.BlockSpec((R, T, B_PAD), lambda i, c: (i, c, 0))]
        + [_lspec(prm[k], layer) for k in GLA_PARAMS],
        out_specs=[pl.BlockSpec((R, T, W_BR), lambda i, c: (i, c, 0)),
                   pl.BlockSpec((R, H_B, DK_B, DV_B), lambda i, c: (i, 0, 0, 0))],
        out_shape=[jax.ShapeDtypeStruct((b, l, W_BR), f32),
                   jax.ShapeDtypeStruct((b, H_B, DK_B, DV_B), f32)],
        scratch_shapes=[pltpu.VMEM((R, H_B, DK_B, DV_B), f32), pltpu.VMEM((R * T, W_BR), f32)],
        compiler_params=pltpu.CompilerParams(dimension_semantics=("arbitrary", "arbitrary"),
                                             vmem_limit_bytes=VMEM_LIMIT),
        name="gla_prompt",
    )(pb, *[prm[k] for k in GLA_PARAMS])


def _swa_prompt_kernel(cur_ref, prev_ref, sink_ref, y_ref):
    step = pl.program_id(1)
    W = WINDOW
    NB = SWA_STEP_BLOCKS
    qo, ko, vo = 0, H_C * HD_C, H_C * HD_C + KV_C * HD_C
    assert G_C == 2
    row = _iota((G_C * W, 2 * W), 0)
    s = _iota((G_C * W, 2 * W), 1)
    t = jnp.where(row >= W, row - W, row)
    dist = W + t - s
    ok = (dist >= 0) & (dist <= W)
    ok_first = ok & ((s >= W) | (step > 0))
    distf = dist.astype(f32)
    second = _iota((G_C * W, 1), 0) >= W
    scale = HD_C ** -0.5

    def band(col, j):
        if j == 0:
            return jnp.concatenate([prev_ref[0, :, col:col + HD_C], cur_ref[0, 0:W, col:col + HD_C]], axis=0)
        return cur_ref[0, (j - 1) * W:(j + 1) * W, col:col + HD_C]

    pieces = [(j, g) for j in range(NB) for g in range(KV_C)]
    scores, sinks = {}, {}
    for g in range(KV_C):
        h0, h1 = g * G_C, g * G_C + 1
        sinks[g] = jnp.where(second, sink_ref[:, h1:h1 + 1], sink_ref[:, h0:h0 + 1])
    for j, g in pieces:
        h0, h1 = g * G_C, g * G_C + 1
        q2 = jnp.concatenate([cur_ref[0, j * W:(j + 1) * W, qo + h0 * HD_C:qo + (h0 + 1) * HD_C],
                              cur_ref[0, j * W:(j + 1) * W, qo + h1 * HD_C:qo + (h1 + 1) * HD_C]], axis=0)
        slope = jnp.where(second, ALIBI_SLOPES[h1], ALIBI_SLOPES[h0])
        raw = _bdot_nt(q2, band(ko + g * HD_C, j)) * scale - slope * distf
        scores[j, g] = jnp.where(ok_first if j == 0 else ok, raw, NEG_BIG)
    probs, dens = {}, {}
    for j, g in pieces:
        m = jnp.maximum(jnp.max(scores[j, g], -1, keepdims=True), sinks[g])
        p = jnp.exp(scores[j, g] - m)
        probs[j, g] = p
        dens[j, g] = jnp.sum(p, -1, keepdims=True) + jnp.exp(sinks[g] - m)
    for j, g in pieces:
        out = _bdot(probs[j, g], band(vo + g * HD_C, j)) / dens[j, g]
        for jj in range(G_C):
            h = g * G_C + jj
            y_ref[0, j * W:(j + 1) * W, h * HD_C:(h + 1) * HD_C] = out[jj * W:(jj + 1) * W]


def _swa_prompt(pc, prm, layer):
    b, l, _ = pc.shape
    W = WINDOW
    NB = SWA_STEP_BLOCKS
    return pl.pallas_call(
        _swa_prompt_kernel,
        grid=(b, l // (NB * W)),
        in_specs=[pl.BlockSpec((1, NB * W, C_COLS), lambda i, c: (i, c, 0)),
                  pl.BlockSpec((1, W, C_COLS), lambda i, c: (i, jnp.maximum(NB * c - 1, 0), 0)),
                  _lspec(prm["sinks"], layer)],
        out_specs=pl.BlockSpec((1, NB * W, W_BR), lambda i, c: (i, c, 0)),
        out_shape=jax.ShapeDtypeStruct((b, l, W_BR), f32),
        compiler_params=pltpu.CompilerParams(dimension_semantics=("arbitrary", "arbitrary"),
                                             vmem_limit_bytes=VMEM_LIMIT),
        name="swa_prompt",
    )(pc, pc, prm["sinks"])


def _lru_gates(xc, wa, ba, wx, bx, lam):
    r = _sigmoid(_bdot(xc, wa) + ba)
    i = _sigmoid(_bdot(xc, wx) + bx)
    log_a = C_RG * r * _log_sigmoid(lam)
    a = jnp.exp(log_a)
    bterm = jnp.sqrt(1.0 - jnp.exp(2.0 * log_a)) * (i * xc)
    return a, bterm


def _lru_prompt_kernel(x_ref, cw_ref, cb_ref, wa_ref, ba_ref, wx_ref, bx_ref, lam_ref, y_ref,
                       xbuf_scr, h_scr):
    c = pl.program_id(1)
    T = LRU_CHUNK
    PADR = 8

    @pl.when(c == 0)
    def _():
        xbuf_scr[0:PADR, :] = jnp.zeros((PADR, W_BR), f32)
        h_scr[...] = jnp.zeros_like(h_scr)

    x = x_ref[0]
    xbuf_scr[PADR:PADR + T, :] = x
    xc = cb_ref[...] + x * cw_ref[CONV_W - 1:CONV_W, :]
    for j in range(1, CONV_W):
        xc = xc + xbuf_scr[PADR - j:PADR - j + T, :] * cw_ref[CONV_W - 1 - j:CONV_W - j, :]
    xbuf_scr[0:PADR, :] = x[T - PADR:T, :]
    a, bv = _lru_gates(xc, wa_ref[...], ba_ref[...], wx_ref[...], bx_ref[...], lam_ref[...])
    row = _iota((T, W_BR), 0)
    d = 1
    while d < T:
        keep = row >= d
        a_sh = jnp.where(keep, pltpu.roll(a, d, 0), 1.0)
        b_sh = jnp.where(keep, pltpu.roll(bv, d, 0), 0.0)
        bv = a * b_sh + bv
        a = a * a_sh
        d *= 2
    h = a * h_scr[...] + bv
    y_ref[0] = h
    h_scr[...] = h[T - 1:T, :]


def _lru_prompt(pd, prm, layer):
    b, l, _ = pd.shape
    T = LRU_CHUNK
    return pl.pallas_call(
        _lru_prompt_kernel,
        grid=(b, l // T),
        in_specs=[pl.BlockSpec((1, T, W_BR), lambda i, c: (i, c, 0))]
        + [_lspec(prm[k], layer) for k in LRU_PARAMS],
        out_specs=pl.BlockSpec((1, T, W_BR), lambda i, c: (i, c, 0)),
        out_shape=jax.ShapeDtypeStruct((b, l, W_BR), f32),
        scratch_shapes=[pltpu.VMEM((T + 8, W_BR), f32), pltpu.VMEM((1, W_BR), f32)],
        compiler_params=pltpu.CompilerParams(dimension_semantics=("arbitrary", "arbitrary"),
                                             vmem_limit_bytes=VMEM_LIMIT),
        name="lru_prompt",
    )(pd, *[prm[k] for k in LRU_PARAMS])


def _colbcast(row, n_out, eye_bf16):
    c = row.shape[1]
    hi, lo = _split_bf16(row)
    return (_dg(eye_bf16, jnp.broadcast_to(hi, (n_out, c)), NT)
            + _dg(eye_bf16, jnp.broadcast_to(lo, (n_out, c)), NT))


def _decode_states_kernel(pa_ref, pb_ref, shift_ref, swkv_ref, sgla_ref,
                          mu_ref, w0_ref, wup_ref, a0_ref, aup_ref, kkw_ref, kaw_ref, rk_ref, lng_ref, lnb_ref,
                          gup_ref, gbias_ref, gng_ref, *rest, n_prev):
    earlier, rest = rest[:2 if n_prev else 0], rest[2 if n_prev else 0:]
    oa_ref, ob_ref, swkv_all, sgla_all, fa_scr, fb_scr, vb_scr, oa_scr, ob_scr = rest
    for src, dst in zip(earlier, (swkv_all, sgla_all)):
        dst[0:n_prev] = src[...]
    swkv_out, sgla_out = swkv_all.at[n_prev], sgla_all.at[n_prev]
    h = pl.program_id(0)
    hk = H_B * DK_B

    @pl.when(h == 0)
    def _():
        u = pa_ref[...]
        us = u + (shift_ref[...] - u) * mu_ref[...]
        r, kmod, v, ld, kk, a = _rwkv_features(us, w0_ref[...], wup_ref[...], a0_ref[...], aup_ref[...],
                                               kkw_ref[...], kaw_ref[...], _head_ones())
        for i, t in enumerate((r, kmod, v, jnp.exp(ld), kk, kk * a)):
            fa_scr[i] = t.T
        fb = pb_ref[...]
        gb = _log_sigmoid(_bdot(fb[:, 2 * hk + W_BR:B_PAD], gup_ref[...]) + gbias_ref[...]) * (1.0 / GLA_TAU)
        for i, t in enumerate((fb[:, 0:hk] * (DK_B ** -0.5), fb[:, hk:2 * hk], jnp.exp(gb))):
            fb_scr[i] = t.T
        vb_scr[...] = fb[:, 2 * hk:2 * hk + W_BR].T

    hs = pl.ds(pl.multiple_of(h * HEAD_A, HEAD_A), HEAD_A)
    r_h, km_h, v_h, w_h, kk_h, kka_h = (fa_scr[i, hs, :] for i in range(6))
    sub = _iota((8, r_h.shape[1]), 0)
    for g in range(HEAD_A // 8):
        rows8 = jnp.zeros((8, r_h.shape[1]), f32)
        for j in range(8):
            vi = g * 8 + j
            s = swkv_ref[vi]
            sa = -jnp.sum(s * kk_h, axis=0, keepdims=True)
            s_new = s * w_h + sa * kka_h + v_h[vi:vi + 1, :] * km_h
            swkv_out[vi] = s_new
            rows8 = jnp.where(sub == j, jnp.sum(s_new * r_h, axis=0, keepdims=True), rows8)
        oa_scr[pl.ds(pl.multiple_of(h * HEAD_A + g * 8, 8), 8), :] = rows8
    ds_ = pl.ds(pl.multiple_of(h * DK_B, DK_B), DK_B)
    q_h, k_h, eg_h = (fb_scr[i, ds_, :] for i in range(3))
    vb_h = vb_scr[hs, :]
    acc = jnp.zeros_like(vb_h)
    for d in range(DK_B):
        s_new = eg_h[d:d + 1, :] * sgla_ref[d] + k_h[d:d + 1, :] * vb_h
        sgla_out[d] = s_new
        acc = acc + q_h[d:d + 1, :] * s_new
    ob_scr[hs, :] = acc

    @pl.when(h == pl.num_programs(0) - 1)
    def _():
        oa_ref[...] = oa_scr[...].T
        ob_ref[...] = ob_scr[...].T


def _decode_states(pa, pb, shift0, swkv_t, sgla_t, prm, layer, earlier):
    assert H_A == H_B and DV_B == HEAD_A
    n = pa.shape[0]
    n_prev = layer
    wkv_dims, gla_dims = (HEAD_A, HEAD_A, n), (DK_B, DV_B, n)
    of_head = lambda dims: pl.BlockSpec((None, None) + dims, lambda h: (layer, h) + (0,) * len(dims))
    stacked = lambda nl, dims: pl.BlockSpec((nl, None) + dims, lambda h: (0, h) + (0,) * len(dims))
    keys = RWKV_PARAMS + GLA_PARAMS
    in_specs = [_const_spec((n, A_PAD)), _const_spec((n, B_PAD)), _const_spec((n, A_PAD)),
                of_head(wkv_dims), of_head(gla_dims)] + [_lspec(prm[k], layer) for k in keys]
    if n_prev:
        in_specs += [stacked(n_prev, wkv_dims), stacked(n_prev, gla_dims)]
    return pl.pallas_call(
        functools.partial(_decode_states_kernel, n_prev=n_prev),
        grid=(H_A,),
        in_specs=in_specs,
        out_specs=[_const_spec((n, W_BR)), _const_spec((n, W_BR)),
                   stacked(n_prev + 1, wkv_dims), stacked(n_prev + 1, gla_dims)],
        out_shape=[jax.ShapeDtypeStruct((n, W_BR), f32), jax.ShapeDtypeStruct((n, W_BR), f32),
                   jax.ShapeDtypeStruct((n_prev + 1, H_A) + wkv_dims, f32),
                   jax.ShapeDtypeStruct((n_prev + 1, H_B) + gla_dims, f32)],
        scratch_shapes=[pltpu.VMEM((6, W_BR, n), f32), pltpu.VMEM((3, H_B * DK_B, n), f32),
                        pltpu.VMEM((W_BR, n), f32), pltpu.VMEM((W_BR, n), f32), pltpu.VMEM((W_BR, n), f32)],
        compiler_params=pltpu.CompilerParams(dimension_semantics=("arbitrary",),
                                             vmem_limit_bytes=VMEM_LIMIT),
        name="decode_states",
    )(pa, pb, shift0, swkv_t, sgla_t, *[prm[k] for k in keys], *(earlier if n_prev else ()))


def _decode_kernel(pa_ref, oa_ref, ob_ref, q8_ref, kvn_ref, pd_ref, shift_ref, kbuf_ref, vbuf_ref,
                   c0_ref, c1_ref, c2_ref, h0_ref,
                   mu_ref, w0_ref, wup_ref, a0_ref, aup_ref, kkw_ref, kaw_ref, rk_ref, lng_ref, lnb_ref,
                   gup_ref, gbias_ref, gng_ref, sink8_ref, slope8_ref,
                   cw_ref, cb_ref, wa_ref, ba_ref, wx_ref, bx_ref, lam_ref, *rest, n_prev):
    earlier, rest = rest[:2 if n_prev else 0], rest[2 if n_prev else 0:]
    ya_ref, yb_ref, yc8_ref, yd_ref, kout_all, vout_all = rest
    for src, dst in zip(earlier, (kout_all, vout_all)):
        dst[0:n_prev] = src[...]
    kout_ref, vout_ref = kout_all.at[n_prev], vout_all.at[n_prev]
    bt = pa_ref.shape[0]
    samples = range(bt)
    u = pa_ref[...]
    us = u + (shift_ref[...] - u) * mu_ref[...]
    ones = _head_ones()
    r, kmod, v, _, _, _ = _rwkv_features(us, w0_ref[...], wup_ref[...], a0_ref[...], aup_ref[...],
                                         kkw_ref[...], kaw_ref[...], ones)
    wdist = (WINDOW - _iota((1, WINDOW), 1)).astype(f32)
    last = _iota((HD_C, WINDOW), 1) == WINDOW - 1
    scale = HD_C ** -0.5
    eye_c = _eye(HD_C).astype(bf16)
    half = KV_C * HD_C
    pieces = [(b, g) for b in samples for g in range(KV_C)]
    kn = lambda b, g: kvn_ref[b:b + 1, g * HD_C:(g + 1) * HD_C]
    vn = lambda b, g: kvn_ref[b:b + 1, half + g * HD_C:half + (g + 1) * HD_C]
    scores = {(b, g): _bdot(q8_ref[b, g], kbuf_ref[b, g]) * scale - slope8_ref[g] * wdist
              for b, g in pieces}
    probs, tails = {}, {}
    for b, g in pieces:
        sink = sink8_ref[g]
        sn = jnp.sum(q8_ref[b, g] * kn(b, g), axis=1, keepdims=True) * scale
        m = jnp.maximum(jnp.maximum(jnp.max(scores[b, g], axis=1, keepdims=True), sn), sink)
        p = jnp.exp(scores[b, g] - m)
        pn = jnp.exp(sn - m)
        probs[b, g] = p
        tails[b, g] = (pn, jnp.sum(p, axis=1, keepdims=True) + pn + jnp.exp(sink - m))
    for b, g in pieces:
        pn, den = tails[b, g]
        yc8_ref[b, g] = (_bdot_nt(probs[b, g], vbuf_ref[b, g]) + pn * vn(b, g)) / den
    for b, g in pieces:
        for src, dst, new in ((kbuf_ref, kout_ref, kn(b, g)), (vbuf_ref, vout_ref, vn(b, g))):
            new_col = _colbcast(new, WINDOW, eye_c)
            dst[b, g] = jnp.where(last, new_col, pltpu.roll(src[b, g], WINDOW - 1, 1))

    ya_ref[...] = _rwkv_finish(oa_ref[...], r, kmod, v, rk_ref[...], lng_ref[...], lnb_ref[...], ones)
    ob = ob_ref[...]
    assert DV_B == HEAD_A
    ms = _ones_dot(ob * ob, ones) * (1.0 / DV_B)
    yb_ref[...] = ob * lax.rsqrt(ms + NORM_EPS) * gng_ref[...]
    xd = pd_ref[...]
    xc = (cb_ref[...] + c0_ref[...] * cw_ref[0:1, :] + c1_ref[...] * cw_ref[1:2, :]
          + c2_ref[...] * cw_ref[2:3, :] + xd * cw_ref[3:4, :])
    al, bterm = _lru_gates(xc, wa_ref[...], ba_ref[...], wx_ref[...], bx_ref[...], lam_ref[...])
    yd_ref[...] = al * h0_ref[...] + bterm


def _decode(pa, oa, ob, pc, pd, shift0, kbuf, vbuf, c0, c1, c2, h0, prm, layer, earlier, bt):
    n = pa.shape[0]
    n_prev = layer
    half = KV_C * HD_C
    q8 = jnp.pad(pc[:, 0:W_BR].reshape(n, KV_C, G_C, HD_C), ((0, 0), (0, 0), (0, 8 - G_C), (0, 0)))
    kvn = pc[:, W_BR:]
    slope8 = jnp.pad(jnp.asarray(ALIBI_SLOPES, f32).reshape(KV_C, G_C, 1), ((0, 0), (0, 8 - G_C), (0, 0)))
    rows = lambda w: pl.BlockSpec((bt, w), lambda i: (i, 0))
    qspec = pl.BlockSpec((bt, KV_C, 8, HD_C), lambda i: (i, 0, 0, 0))
    cache_dims = (KV_C, HD_C, WINDOW)
    cache_in = pl.BlockSpec((None, bt) + cache_dims, lambda i: (layer, i, 0, 0, 0))
    stacked = lambda nl: pl.BlockSpec((nl, bt) + cache_dims, lambda i: (0, i, 0, 0, 0))
    in_specs = [rows(A_PAD), rows(W_BR), rows(W_BR), qspec, rows(2 * half), rows(D_COLS), rows(A_PAD),
                cache_in, cache_in, rows(W_BR), rows(W_BR), rows(W_BR), rows(W_BR)]
    in_specs += [_lspec(prm[k], layer) for k in RWKV_PARAMS + GLA_PARAMS]
    in_specs += [_lspec(prm["sink8"], layer), _const_spec((KV_C, 8, 1))]
    in_specs += [_lspec(prm[k], layer) for k in LRU_PARAMS]
    if n_prev:
        in_specs += [stacked(n_prev)] * 2
    out_specs = [rows(W_BR), rows(W_BR), qspec, rows(W_BR)] + [stacked(n_prev + 1)] * 2
    out_shape = [jax.ShapeDtypeStruct((n, W_BR), f32), jax.ShapeDtypeStruct((n, W_BR), f32),
                 jax.ShapeDtypeStruct((n, KV_C, 8, HD_C), f32), jax.ShapeDtypeStruct((n, W_BR), f32)]
    out_shape += [jax.ShapeDtypeStruct((n_prev + 1, n) + cache_dims, f32)] * 2
    ya, yb, yc8, yd, k1, v1 = pl.pallas_call(
        functools.partial(_decode_kernel, n_prev=n_prev),
        grid=(n // bt,),
        in_specs=in_specs,
        out_specs=out_specs,
        out_shape=out_shape,
        compiler_params=pltpu.CompilerParams(dimension_semantics=("arbitrary",),
                                             vmem_limit_bytes=VMEM_LIMIT),
        name="decode_mixers",
    )(pa, oa, ob, q8, kvn, pd, shift0, kbuf, vbuf, c0, c1, c2, h0,
      *[prm[k] for k in RWKV_PARAMS + GLA_PARAMS], prm["sink8"], slope8, *[prm[k] for k in LRU_PARAMS],
      *(earlier if n_prev else ()))
    return ya, yb, yc8[:, :, 0:G_C, :].reshape(n, W_BR), yd, k1, v1


def _merge_kernel(x_ref, ya_ref, yb_ref, yc_ref, yd_ref, g_ref, w_ref, wbr_ref, wout_ref,
                  fg_ref, o_ref, *, final):
    x = x_ref[...]
    hn = _rms(x, g_ref[...]).astype(bf16)
    ys = (ya_ref, yb_ref, yc_ref, yd_ref)
    merged = None
    for n in range(N_BRANCH):
        z = _dg(hn, w_ref[:, n * W_BR:(n + 1) * W_BR], NN)
        yz = ys[n][...] * (z * _sigmoid(z))
        br = jnp.dot(yz.astype(bf16), wbr_ref[n], preferred_element_type=f32)
        gate = _sigmoid(_dg(hn, w_ref[:, Z_COLS + n * D_MODEL:Z_COLS + (n + 1) * D_MODEL], NN))
        merged = gate * br if merged is None else merged + gate * br
    out = x + jnp.dot(merged.astype(bf16), wout_ref[...], preferred_element_type=f32)
    if final:
        out = _rms(out, fg_ref[...])
    o_ref[...] = out


def _merge(x, ya, yb, yc, yd, g, wt, wbr, wout, fg, layer, tm, final):
    m = x.shape[0]
    tile = lambda w: pl.BlockSpec((tm, w), lambda i: (i, 0))
    per_layer = lambda shape: pl.BlockSpec((None,) + shape, lambda *_: (layer,) + (0,) * len(shape),
                                           pipeline_mode=pl.Buffered(1))
    return pl.pallas_call(
        functools.partial(_merge_kernel, final=final),
        grid=(m // tm,),
        in_specs=[tile(D_MODEL), tile(W_BR), tile(W_BR), tile(W_BR), tile(W_BR), _lspec(g, layer),
                  per_layer((D_MODEL, GATE_ROWS)),
                  per_layer((N_BRANCH, W_BR, D_MODEL)),
                  per_layer((D_MODEL, D_MODEL)), _const_spec((1, D_MODEL))],
        out_specs=tile(D_MODEL),
        out_shape=jax.ShapeDtypeStruct((m, D_MODEL), f32),
        compiler_params=pltpu.CompilerParams(dimension_semantics=("arbitrary",),
                                             vmem_limit_bytes=VMEM_LIMIT),
        name="merge_final" if final else "merge",
    )(x, ya, yb, yc, yd, g, wt, wbr, wout, fg)


def _lspec(arr, layer):
    return pl.BlockSpec((None,) + arr.shape[1:], lambda *_: (layer,) + (0,) * (arr.ndim - 1))


def _stacked_params(norm_g, mu_shift, w0, w_decay_up, a0, a_icl_up, k_k, k_a, r_k, ln_x_g, ln_x_b,
                    gla_gate_up, gla_gate_b, gla_norm_g, swa_sinks, lru_conv_w, lru_conv_b, lru_wa, lru_ba,
                    lru_wx, lru_bx, lru_lambda):
    depth = norm_g.shape[0]
    row = lambda t: t.reshape(depth, 1, -1)
    pad_rows = lambda w, start: jnp.pad(w, ((0, 0), (start, LANE - start - w.shape[1]), (0, 0)))
    dense = lambda w: jnp.einsum("lnij,nm->lnimj", w, jnp.eye(w.shape[1], dtype=w.dtype)).reshape(
        depth, W_BR, W_BR).astype(bf16)
    return dict(
        g=row(norm_g),
        mu=row(jnp.pad(mu_shift, ((0, 0), (0, A_PAD - A_COLS)))),
        w0=row(w0), wup=pad_rows(w_decay_up, 0), a0=row(a0), aup=pad_rows(a_icl_up, R_DECAY),
        kkw=row(k_k), kaw=row(k_a), rk=row(r_k), lng=row(ln_x_g), lnb=row(ln_x_b),
        gup=pad_rows(gla_gate_up, 0), gbias=row(gla_gate_b), gng=row(jnp.tile(gla_norm_g, (1, H_B))),
        sinks=row(swa_sinks),
        sink8=jnp.pad(swa_sinks.reshape(depth, KV_C, G_C, 1), ((0, 0), (0, 0), (0, 8 - G_C), (0, 0))),
        cw=lru_conv_w, cb=row(lru_conv_b),
        lwa=dense(lru_wa), lba=row(lru_ba), lwx=dense(lru_wx), lbx=row(lru_bx), lam=row(lru_lambda),
    )


def kernel(x_prompt, x_sample, state_wkv, state_shift, state_gla, cache_swa_k, cache_swa_v, state_lru_conv, state_lru_h, norm_g, w_in, mu_shift, w0, w_decay_up, a0, a_icl_up, k_k, k_a, r_k, ln_x_g, ln_x_b, gla_gate_up, gla_gate_b, gla_norm_g, swa_sinks, lru_conv_w, lru_conv_b, lru_wa, lru_ba, lru_wx, lru_bx, lru_lambda, w_branch, w_out, final_norm_g):
    bp, lp, _ = x_prompt.shape
    bs = x_sample.shape[0]
    depth = w_in.shape[0]
    fg = final_norm_g.reshape(1, -1)
    xp = x_prompt.reshape(bp * lp, D_MODEL)
    xs = x_sample.reshape(bs, D_MODEL)
    outs_p = [[] for _ in range(7)]
    outs_s = [[] for _ in range(7)]
    wt_mix, w_gate = _pack_w_in(w_in)
    wbr_all = w_branch.astype(bf16)
    wout_all = w_out.astype(bf16)
    stacked_s = None
    prm = _stacked_params(norm_g, mu_shift, w0, w_decay_up, a0, a_icl_up, k_k, k_a, r_k, ln_x_g, ln_x_b,
                          gla_gate_up, gla_gate_b, gla_norm_g, swa_sinks, lru_conv_w, lru_conv_b, lru_wa,
                          lru_ba, lru_wx, lru_bx, lru_lambda)
    shift_pad = jnp.pad(state_shift, ((0, 0), (0, 0), (0, A_PAD - A_COLS)))
    wkv_t = jnp.transpose(state_wkv, (0, 2, 3, 4, 1))
    gla_t = jnp.transpose(state_gla, (0, 2, 3, 4, 1))
    stacked_states = None
    kbuf_t = jnp.transpose(cache_swa_k, (0, 1, 3, 4, 2))
    vbuf_t = jnp.transpose(cache_swa_v, (0, 1, 3, 4, 2))
    for l in range(depth):
        final = l == depth - 1
        pa, pb, pc, pd = _inproj(xp, prm["g"], wt_mix, l, tm=1024)
        pa3, pb3 = pa.reshape(bp, lp, A_PAD), pb.reshape(bp, lp, B_PAD)
        pc3, pd3 = pc.reshape(bp, lp, C_COLS), pd.reshape(bp, lp, D_COLS)
        ya, st_t = _rwkv_prompt(pa3, prm, l)
        yb, sgla = _gla_prompt(pb3, prm, l)
        yc = _swa_prompt(pc3, prm, l)
        yd = _lru_prompt(pd3, prm, l)
        flat = lambda t: t.reshape(bp * lp, W_BR)
        xp = _merge(xp, flat(ya), flat(yb), flat(yc), flat(yd), prm["g"], w_gate, wbr_all, wout_all, fg, l,
                    tm=512, final=final)
        kv = pc3[:, lp - WINDOW:, H_C * HD_C:]
        outs_p[0].append(jnp.swapaxes(st_t, -1, -2))
        outs_p[1].append(pa3[:, lp - 1, :A_COLS])
        outs_p[2].append(sgla)
        outs_p[3].append(kv[:, :, :KV_C * HD_C].reshape(bp, WINDOW, KV_C, HD_C))
        outs_p[4].append(kv[:, :, KV_C * HD_C:].reshape(bp, WINDOW, KV_C, HD_C))
        outs_p[5].append(pd3[:, lp - (CONV_W - 1):, :])
        outs_p[6].append(yd[:, lp - 1, :])
        sa, sb, sc, sd = _inproj(xs, prm["g"], wt_mix, l, tm=bs)
        conv0 = state_lru_conv[l]
        oa_s, ob_s, *stacked_states = _decode_states(sa, sb, shift_pad[l], wkv_t, gla_t, prm, l, stacked_states)
        ya_s, yb_s, yc_s, yd_s, *stacked_s = _decode(
            sa, oa_s, ob_s, sc, sd, shift_pad[l], kbuf_t, vbuf_t,
            conv0[:, 0], conv0[:, 1], conv0[:, 2], state_lru_h[l], prm, l, stacked_s, bt=32)
        xs = _merge(xs, ya_s, yb_s, yc_s, yd_s, prm["g"], w_gate, wbr_all, wout_all, fg, l, tm=bs, final=final)
        outs_s[1].append(sa[:, :A_COLS])
        outs_s[5].append(jnp.stack([conv0[:, 1], conv0[:, 2], sd], axis=1))
        outs_s[6].append(yd_s)
    y_prompt = xp.reshape(bp, lp, D_MODEL)
    y_sample = xs.reshape(bs, 1, D_MODEL)
    sp = [jnp.stack(t) for t in outs_p]
    wkv_s, gla_s = (jnp.transpose(t, (0, 4, 1, 2, 3)) for t in stacked_states)
    k_s, v_s = (jnp.transpose(t, (0, 1, 4, 2, 3)) for t in stacked_s)
    shift_s, conv_s, h_s = (jnp.stack(outs_s[i]) for i in (1, 5, 6))
    return (y_prompt, y_sample, sp[0], wkv_s, sp[1], shift_s, sp[2], gla_s, sp[3], k_s, sp[4], v_s,
            sp[5], conv_s, sp[6], h_s)
```

```python
import functools

import jax
import jax.numpy as jnp
from jax import lax
from jax.experimental import pallas as pl
from jax.experimental.pallas import tpu as pltpu

f32 = jnp.float32
bf16 = jnp.bfloat16

D_MODEL = 1024
N_BRANCH = 4
W_BR = 256
HEAD_A = 64
H_A = 4
R_DECAY = 32
R_ICL = 32
GN_EPS_A = 64e-5
H_B = 4
DK_B = 32
DV_B = 64
R_GATE_B = 16
GLA_TAU = 16.0
GLA_CHUNK = 64
GLA_STEP_ROWS = 256
GLA_STEP_BATCH = 4
H_C = 4
KV_C = 2
HD_C = 64
G_C = 2
WINDOW = 128
SWA_STEP_BLOCKS = 8
CONV_W = 4
C_RG = 8.0
NORM_EPS = 1e-6

A_COLS = 3 * W_BR + R_DECAY + R_ICL
B_COLS = 2 * H_B * DK_B + W_BR + R_GATE_B
C_COLS = H_C * HD_C + 2 * KV_C * HD_C
D_COLS = W_BR
Z_COLS = N_BRANCH * W_BR
G_COLS = N_BRANCH * D_MODEL

LANE = 128
A_PAD = 7 * LANE
B_PAD = 5 * LANE
RWKV_CHUNK = 64
RWKV_STEP_ROWS = 256
RWKV_STEP_BATCH = 4
LRU_CHUNK = 256
VMEM_LIMIT = 56 * 1024 * 1024
NEG_BIG = -1e30

ALIBI_SLOPES = tuple(2.0 ** (-8.0 * (h + 1) / H_C) for h in range(H_C))


def _bdot(a, b):
    return jnp.dot(a.astype(bf16), b.astype(bf16), preferred_element_type=f32)


def _bdot_nt(a, b):
    return lax.dot_general(a.astype(bf16), b.astype(bf16), (((1,), (1,)), ((), ())),
                           preferred_element_type=f32)


NN = ((1,), (0,))
NT = ((1,), (1,))
TN = ((0,), (0,))


def _dg(a, b, dims):
    return lax.dot_general(a, b, (dims, ((), ())), preferred_element_type=f32)


def _split_bf16(a):
    hi = a.astype(bf16)
    return hi, (a - hi.astype(f32)).astype(bf16)


def _bdg(a, b, dims):
    return _dg(a.astype(bf16), b.astype(bf16), dims)


def _ones_dot(a, ones_bf16):
    ah, al = _split_bf16(a)
    return _dg(ah, ones_bf16, NN) + _dg(al, ones_bf16, NN)


def _cumsum_rows(x, seg, pos=None):
    if pos is None:
        pos = _iota(x.shape, 0) % seg
    d = 1
    while d < seg:
        x = x + jnp.where(pos >= d, pltpu.roll(x, d, 0), 0.0)
        d *= 2
    return x


def _iota(shape, dim):
    return lax.broadcasted_iota(jnp.int32, shape, dim)


def _eye(n):
    return (_iota((n, n), 0) == _iota((n, n), 1)).astype(f32)


def _block_ones(n, blk):
    return ((_iota((n, n), 0) // blk) == (_iota((n, n), 1) // blk)).astype(f32)


def _softplus(x):
    return jnp.maximum(x, 0.0) + jnp.log(1.0 + jnp.exp(-jnp.abs(x)))


def _log_sigmoid(x):
    return -_softplus(-x)


def _sigmoid(x):
    return 1.0 / (1.0 + jnp.exp(-x))


def _rms(x, g):
    return x * lax.rsqrt(jnp.mean(x * x, -1, keepdims=True) + NORM_EPS) * g


MIX_COLS = (A_COLS, B_COLS, C_COLS, D_COLS)
MIX_WIDTHS = (A_PAD, B_PAD, C_COLS, D_COLS)
MIX_ROWS = sum(MIX_COLS)
GATE_ROWS = Z_COLS + G_COLS


PACK_MIX_ROWS = MIX_ROWS
PACK_GATE_ROWS = 512


def _cast_kernel(w_ref, o_ref):
    o_ref[...] = w_ref[...].astype(bf16)


def _transpose_cast_kernel(w_ref, o_ref):
    o_ref[...] = w_ref[0].T.astype(bf16)


def _pack_w_in(w_in):
    depth = w_in.shape[0]
    wt = jnp.swapaxes(w_in, 1, 2)
    assert MIX_ROWS % PACK_MIX_ROWS == 0 and PACK_MIX_ROWS % 16 == 0 and GATE_ROWS % PACK_GATE_ROWS == 0
    params = pltpu.CompilerParams(dimension_semantics=("arbitrary", "arbitrary"), vmem_limit_bytes=VMEM_LIMIT)
    wt_mix = pl.pallas_call(
        _cast_kernel,
        grid=(depth, MIX_ROWS // PACK_MIX_ROWS),
        in_specs=[pl.BlockSpec((None, PACK_MIX_ROWS, D_MODEL), lambda l, i: (l, i, 0))],
        out_specs=pl.BlockSpec((None, PACK_MIX_ROWS, D_MODEL), lambda l, i: (l, i, 0)),
        out_shape=jax.ShapeDtypeStruct((depth, MIX_ROWS, D_MODEL), bf16),
        compiler_params=params,
        name="pack_mix",
    )(wt)
    rows = PACK_GATE_ROWS
    w_gate = pl.pallas_call(
        _transpose_cast_kernel,
        grid=(depth, GATE_ROWS // rows),
        in_specs=[pl.BlockSpec((pl.Element(1), pl.Element(rows), pl.Element(D_MODEL)),
                               lambda l, j: (l, pl.multiple_of(MIX_ROWS + rows * j, 8), 0))],
        out_specs=pl.BlockSpec((None, D_MODEL, rows), lambda l, j: (l, 0, j)),
        out_shape=jax.ShapeDtypeStruct((depth, D_MODEL, GATE_ROWS), bf16),
        compiler_params=params,
        name="pack_gate",
    )(wt)
    return wt_mix, w_gate


def _inproj_kernel(x_ref, g_ref, wt_ref, oa_ref, ob_ref, oc_ref, od_ref):
    hn = _rms(x_ref[...], g_ref[...]).astype(bf16)
    start = 0
    for o_ref, cols, width in zip((oa_ref, ob_ref, oc_ref, od_ref), MIX_COLS, MIX_WIDTHS):
        o_ref[:, 0:cols] = _dg(hn, wt_ref[start:start + cols, :], NT)
        if width > cols:
            o_ref[:, cols:width] = jnp.zeros((o_ref.shape[0], width - cols), f32)
        start += cols


def _const_spec(shape):
    return pl.BlockSpec(shape, lambda *_: (0,) * len(shape))


def _inproj(x, g, wt, layer, tm):
    m = x.shape[0]
    return pl.pallas_call(
        _inproj_kernel,
        grid=(m // tm,),
        in_specs=[pl.BlockSpec((tm, D_MODEL), lambda i: (i, 0)), _lspec(g, layer), _lspec(wt, layer)],
        out_specs=[pl.BlockSpec((tm, w), lambda i: (i, 0)) for w in MIX_WIDTHS],
        out_shape=[jax.ShapeDtypeStruct((m, w), f32) for w in MIX_WIDTHS],
        compiler_params=pltpu.CompilerParams(dimension_semantics=("arbitrary",),
                                             vmem_limit_bytes=VMEM_LIMIT),
        name="inproj",
    )(x, g, wt)


def _head_ones():
    return _block_ones(W_BR, HEAD_A).astype(bf16)


def _rwkv_features(us, w0, wup, a0, aup, kk_w, ka_w, ones):
    r = us[:, 0:W_BR]
    k = us[:, W_BR:2 * W_BR]
    v = us[:, 2 * W_BR:3 * W_BR]
    lora = us[:, 3 * W_BR:A_PAD]
    w = -_softplus(-(w0 + _bdot(jnp.tanh(lora), wup))) - 0.5
    logdecay = -jnp.exp(w)
    a = _sigmoid(a0 + _bdot(lora, aup))
    kk = k * kk_w
    ss = _ones_dot(kk * kk, ones)
    kk = kk / jnp.maximum(jnp.sqrt(ss), 1e-12)
    kmod = k * (1.0 + (a - 1.0) * ka_w)
    return r, kmod, v, logdecay, kk, a


def _rwkv_finish(o, r, kmod, v, rk, lng, lnb, ones):
    mean = _ones_dot(o, ones) * (1.0 / HEAD_A)
    cen = o - mean
    var = _ones_dot(cen * cen, ones) * (1.0 / HEAD_A)
    o = cen * lax.rsqrt(var + GN_EPS_A) * lng + lnb
    bonus = _ones_dot(r * kmod * rk, ones) * v
    return o + bonus


def _rwkv_prompt_kernel(u_ref, mu_ref, wup_ref, aup_ref, rows_ref, y_ref, s_ref, st_scr, prev_scr, o_scr):
    w0, a0, kkw, kaw, rk, lng, lnb = _rows(rows_ref, "w0", "a0", "kkw", "kaw", "rk", "lng", "lnb")
    step = pl.program_id(1)
    T = RWKV_CHUNK
    TT = RWKV_STEP_ROWS

    @pl.when(step == 0)
    def _():
        st_scr[...] = jnp.zeros_like(st_scr)
        prev_scr[...] = jnp.zeros_like(prev_scr)

    R = RWKV_STEP_BATCH
    NC = TT // T
    pieces = [(c, h) for c in range(NC) for h in range(H_A)]
    row0 = _iota((TT, A_PAD), 0) == 0
    pos = _iota((TT, W_BR), 0) % T
    ones = _head_ones()

    def features(q):
        u = u_ref[q]
        u_prev = jnp.where(row0, prev_scr[q], pltpu.roll(u, 1, 0))
        prev_scr[q] = u[TT - 1:TT, :]
        us = u + (u_prev - u) * mu_ref[...]
        r, kmod, v, ld, kk, a = _rwkv_features(us, w0, wup_ref[...], a0, aup_ref[...], kkw, kaw, ones)
        cum = _cumsum_rows(ld, T, pos)
        cum_last = jnp.concatenate(
            [jnp.broadcast_to(cum[(c + 1) * T - 1:(c + 1) * T, :], (T, W_BR)) for c in range(NC)], axis=0)
        g_inv = jnp.exp(-cum)
        g_tail = jnp.exp(cum_last - cum)
        kka = kk * a
        return dict(r=r, kmod=kmod, v=v, cum=cum, at=-kk * jnp.exp(cum - ld), rt=r * jnp.exp(cum),
                    bt=kka * g_inv, kt=kmod * g_inv, btg=kka * g_tail, ktg=kmod * g_tail)

    ri = _iota((2 * T, 2 * T), 0)
    ci = _iota((2 * T, 2 * T), 1)
    ti = jnp.where(ri >= T, ri - T, ri)
    si = jnp.where(ci >= T, ci - T, ci)
    keep = (ti > si) | ((ri >= T) & (ti == si))
    eye_t = _eye(T)
    eye_h = _eye(HEAD_A)

    def independent_stages(f):
        d = dict(ar={}, vh={}, pm={}, x={}, pw={}, lv={}, gcol={}, bkg={}, xar={})

        def products():
            for c, h in pieces:
                rows = slice(c * T, (c + 1) * T)
                sl = slice(h * HEAD_A, (h + 1) * HEAD_A)
                d["vh"][c, h] = f["v"][rows, sl]
                d["ar"][c, h] = jnp.concatenate([f["at"][rows, sl], f["rt"][rows, sl]], axis=0)
                bk = jnp.concatenate([f["bt"][rows, sl], f["kt"][rows, sl]], axis=0)
                d["pm"][c, h] = jnp.where(keep, _bdot_nt(d["ar"][c, h], bk), 0.0)

        def squares_and_values():
            for c, h in pieces:
                rows = slice(c * T, (c + 1) * T)
                sl = slice(h * HEAD_A, (h + 1) * HEAD_A)
                lab = d["pm"][c, h][0:T, 0:T]
                d["x"][c, h] = eye_t + lab
                d["pw"][c, h] = _bdot(lab, lab)
                d["lv"][c, h] = _bdot(d["pm"][c, h][:, T:2 * T], d["vh"][c, h])
                g_last = jnp.exp(f["cum"][(c + 1) * T - 1:(c + 1) * T, sl])
                d["gcol"][c, h] = jnp.sum(eye_h * g_last, axis=1, keepdims=True)
                d["bkg"][c, h] = jnp.concatenate([f["btg"][rows, sl], f["ktg"][rows, sl]], axis=0)

        def inverse_round(last):
            def run():
                for c, h in pieces:
                    x_next = d["x"][c, h] + _bdot(d["pw"][c, h], d["x"][c, h])
                    if not last:
                        d["pw"][c, h] = _bdot(d["pw"][c, h], d["pw"][c, h])
                    d["x"][c, h] = x_next
            return run

        def fold_inverse():
            for c, h in pieces:
                xa = _bdot(d["x"][c, h], d["ar"][c, h][0:T])
                xl = _bdot(d["x"][c, h], d["lv"][c, h][0:T])
                d["xar"][c, h] = jnp.concatenate([xa, d["ar"][c, h][T:2 * T]], axis=0)
                d["lv"][c, h] = jnp.concatenate([xl, d["lv"][c, h][T:2 * T]], axis=0)

        stages = [products, squares_and_values] + [inverse_round(it == 4) for it in range(5)] + [fold_inverse]
        return stages, d

    def dependent_stages(q, f, d):
        st = {}
        base = {}

        def load():
            for h in range(H_A):
                st[h] = st_scr[q, h]

        def read(c):
            def run():
                if c == 0:
                    load()
                for h in range(H_A):
                    base[h] = _bdot(d["xar"][c, h], st[h]) + d["lv"][c, h]
            return run

        def update(c):
            def run():
                for h in range(H_A):
                    st[h] = d["gcol"][c, h] * st[h] + _bdg(
                        d["bkg"][c, h], jnp.concatenate([base[h][0:T], d["vh"][c, h]], axis=0), TN)
                for h in range(H_A):
                    o_scr[q, c * T:(c + 1) * T, h * HEAD_A:(h + 1) * HEAD_A] = (
                        base[h][T:2 * T] + _bdot(d["pm"][c, h][T:2 * T, 0:T], base[h][0:T]))
                if c == NC - 1:
                    for h in range(H_A):
                        st_scr[q, h] = st[h]
                    y_ref[q] = _rwkv_finish(o_scr[q], f["r"], f["kmod"], f["v"], rk, lng, lnb, ones)
            return run

        return [stage for c in range(NC) for stage in (read(c), update(c))]

    feats = {q: features(q) for q in range(min(2, R))}
    pending = []
    for q in range(R):
        stages, d = independent_stages(feats[q])
        if q + 2 < R:
            feats[q + 2] = features(q + 2)
        for k in range(max(len(stages), len(pending))):
            if k < len(stages):
                stages[k]()
            if k < len(pending):
                pending[k]()
        pending = dependent_stages(q, feats[q], d)
    for stage in pending:
        stage()

    @pl.when(step == pl.num_programs(1) - 1)
    def _():
        s_ref[...] = st_scr[...]


ROWS = ("w0", "a0", "kkw", "kaw", "rk", "lng", "lnb", "gng", "cb", "lba", "lbx", "lam")
RWKV_PARAMS = ("mu", "wup", "aup", "rows")
GLA_PARAMS = ("gup", "gbias", "rows")
LRU_PARAMS = ("cw", "lwa", "lwx", "rows")
DECODE_STATE_PARAMS = ("mu", "wup", "aup", "gup", "gbias", "rows")


def _rows(rows_ref, *names):
    return [rows_ref[ROWS.index(n):ROWS.index(n) + 1, :] for n in names]


def _rwkv_prompt(pa, prm, layer):
    b, l, _ = pa.shape
    T = RWKV_STEP_ROWS
    R = RWKV_STEP_BATCH
    return pl.pallas_call(
        _rwkv_prompt_kernel,
        grid=(b // R, l // T),
        in_specs=[pl.BlockSpec((R, T, A_PAD), lambda i, c: (i, c, 0))]
        + [_lspec(prm[k], layer) for k in RWKV_PARAMS],
        out_specs=[pl.BlockSpec((R, T, W_BR), lambda i, c: (i, c, 0)),
                   pl.BlockSpec((R, H_A, HEAD_A, HEAD_A), lambda i, c: (i, 0, 0, 0))],
        out_shape=[jax.ShapeDtypeStruct((b, l, W_BR), f32),
                   jax.ShapeDtypeStruct((b, H_A, HEAD_A, HEAD_A), f32)],
        scratch_shapes=[pltpu.VMEM((R, H_A, HEAD_A, HEAD_A), f32), pltpu.VMEM((R, 1, A_PAD), f32),
                        pltpu.VMEM((R, T, W_BR), f32)],
        compiler_params=pltpu.CompilerParams(dimension_semantics=("arbitrary", "arbitrary"),
                                             vmem_limit_bytes=VMEM_LIMIT),
        name="rwkv_prompt",
    )(pa, *[prm[k] for k in RWKV_PARAMS])


def _gla_prompt_kernel(f_ref, up_ref, bias_ref, rows_ref, y_ref, s_ref, s_scr, o_scr):
    step = pl.program_id(1)
    T = GLA_CHUNK
    TT = GLA_STEP_ROWS
    R = GLA_STEP_BATCH
    NCR = TT // T
    NC = R * NCR

    @pl.when(step == 0)
    def _():
        s_scr[...] = jnp.zeros_like(s_scr)

    f = f_ref[...].reshape(R * TT, B_PAD)
    hk = H_B * DK_B
    q = f[:, 0:hk] * (DK_B ** -0.5)
    k = f[:, hk:2 * hk]
    v = f[:, 2 * hk:2 * hk + W_BR]
    gl = f[:, 2 * hk + W_BR:B_PAD]
    g = _log_sigmoid(_bdot(gl, up_ref[...]) + bias_ref[...]) * (1.0 / GLA_TAU)
    bcum = _cumsum_rows(g, T)
    b_last = jnp.concatenate(
        [jnp.broadcast_to(bcum[(c + 1) * T - 1:(c + 1) * T, :], (T, hk)) for c in range(NC)], axis=0)
    qe = q * jnp.exp(bcum)
    ke = k * jnp.exp(-bcum)
    kl = k * jnp.exp(b_last - bcum)
    causal = _iota((T, T), 0) >= _iota((T, T), 1)
    eye_k = _eye(DK_B)
    pieces = [(c, h) for c in range(NC) for h in range(H_B)]
    av, kv, ecol = {}, {}, {}
    for c, h in pieces:
        rows = slice(c * T, (c + 1) * T)
        ks = slice(h * DK_B, (h + 1) * DK_B)
        vs = slice(h * DV_B, (h + 1) * DV_B)
        att = jnp.where(causal, _bdg(qe[rows, ks], ke[rows, ks], NT), 0.0)
        av[c, h] = _bdg(att, v[rows, vs], NN)
        kv[c, h] = _bdg(kl[rows, ks], v[rows, vs], TN)
        e_last = jnp.exp(bcum[(c + 1) * T - 1:(c + 1) * T, ks])
        ecol[c, h] = jnp.sum(eye_k * e_last, axis=1, keepdims=True)
    chains = [(q_, h) for q_ in range(R) for h in range(H_B)]
    s = {qh: s_scr[qh] for qh in chains}
    for j in range(NCR):
        for q_, h in chains:
            c = q_ * NCR + j
            rows = slice(c * T, (c + 1) * T)
            ks = slice(h * DK_B, (h + 1) * DK_B)
            o_scr[rows, h * DV_B:(h + 1) * DV_B] = av[c, h] + _bdg(qe[rows, ks], s[q_, h], NN)
            s[q_, h] = ecol[c, h] * s[q_, h] + kv[c, h]
    for qh in chains:
        s_scr[qh] = s[qh]
    o = o_scr[...]
    ms = _ones_dot(o * o, _block_ones(W_BR, DV_B).astype(bf16)) * (1.0 / DV_B)
    y_ref[...] = (o * lax.rsqrt(ms + NORM_EPS) * _rows(rows_ref, "gng")[0]).reshape(R, TT, W_BR)

    @pl.when(step == pl.num_programs(1) - 1)
    def _():
        s_ref[...] = s_scr[...]


def _gla_prompt(pb, prm, layer):
    b, l, _ = pb.shape
    T = GLA_STEP_ROWS
    R = GLA_STEP_BATCH
    return pl.pallas_call(
        _gla_prompt_kernel,
        grid=(b // R, l // T),
        in_specs=[pl.BlockSpec((R, T, B_PAD), lambda i, c: (i, c, 0))]
        + [_lspec(prm[k], layer) for k in GLA_PARAMS],
        out_specs=[pl.BlockSpec((R, T, W_BR), lambda i, c: (i, c, 0)),
                   pl.BlockSpec((R, H_B, DK_B, DV_B), lambda i, c: (i, 0, 0, 0))],
        out_shape=[jax.ShapeDtypeStruct((b, l, W_BR), f32),
                   jax.ShapeDtypeStruct((b, H_B, DK_B, DV_B), f32)],
        scratch_shapes=[pltpu.VMEM((R, H_B, DK_B, DV_B), f32), pltpu.VMEM((R * T, W_BR), f32)],
        compiler_params=pltpu.CompilerParams(dimension_semantics=("arbitrary", "arbitrary"),
                                             vmem_limit_bytes=VMEM_LIMIT),
        name="gla_prompt",
    )(pb, *[prm[k] for k in GLA_PARAMS])


def _swa_prompt_kernel(cur_ref, prev_ref, sink_ref, y_ref):
    step = pl.program_id(1)
    W = WINDOW
    NB = SWA_STEP_BLOCKS
    qo, ko, vo = 0, H_C * HD_C, H_C * HD_C + KV_C * HD_C
    assert G_C == 2
    row = _iota((G_C * W, 2 * W), 0)
    s = _iota((G_C * W, 2 * W), 1)
    t = jnp.where(row >= W, row - W, row)
    dist = W + t - s
    ok = (dist >= 0) & (dist <= W)
    ok_first = ok & ((s >= W) | (step > 0))
    distf = dist.astype(f32)
    second = _iota((G_C * W, 1), 0) >= W
    scale = HD_C ** -0.5

    def band(col, j):
        if j == 0:
            return jnp.concatenate([prev_ref[0, :, col:col + HD_C], cur_ref[0, 0:W, col:col + HD_C]], axis=0)
        return cur_ref[0, (j - 1) * W:(j + 1) * W, col:col + HD_C]

    pieces = [(j, g) for j in range(NB) for g in range(KV_C)]
    scores, sinks = {}, {}
    for g in range(KV_C):
        h0, h1 = g * G_C, g * G_C + 1
        sinks[g] = jnp.where(second, sink_ref[:, h1:h1 + 1], sink_ref[:, h0:h0 + 1])
    for j, g in pieces:
        h0, h1 = g * G_C, g * G_C + 1
        q2 = jnp.concatenate([cur_ref[0, j * W:(j + 1) * W, qo + h0 * HD_C:qo + (h0 + 1) * HD_C],
                              cur_ref[0, j * W:(j + 1) * W, qo + h1 * HD_C:qo + (h1 + 1) * HD_C]], axis=0)
        slope = jnp.where(second, ALIBI_SLOPES[h1], ALIBI_SLOPES[h0])
        raw = _bdot_nt(q2, band(ko + g * HD_C, j)) * scale - slope * distf
        scores[j, g] = jnp.where(ok_first if j == 0 else ok, raw, NEG_BIG)
    probs, dens = {}, {}
    for j, g in pieces:
        m = jnp.maximum(jnp.max(scores[j, g], -1, keepdims=True), sinks[g])
        p = jnp.exp(scores[j, g] - m)
        probs[j, g] = p
        dens[j, g] = jnp.sum(p, -1, keepdims=True) + jnp.exp(sinks[g] - m)
    for j, g in pieces:
        out = _bdot(probs[j, g], band(vo + g * HD_C, j)) / dens[j, g]
        for jj in range(G_C):
            h = g * G_C + jj
            y_ref[0, j * W:(j + 1) * W, h * HD_C:(h + 1) * HD_C] = out[jj * W:(jj + 1) * W]


def _swa_prompt(pc, prm, layer):
    b, l, _ = pc.shape
    W = WINDOW
    NB = SWA_STEP_BLOCKS
    return pl.pallas_call(
        _swa_prompt_kernel,
        grid=(b, l // (NB * W)),
        in_specs=[pl.BlockSpec((1, NB * W, C_COLS), lambda i, c: (i, c, 0)),
                  pl.BlockSpec((1, W, C_COLS), lambda i, c: (i, jnp.maximum(NB * c - 1, 0), 0)),
                  _lspec(prm["sinks"], layer)],
        out_specs=pl.BlockSpec((1, NB * W, W_BR), lambda i, c: (i, c, 0)),
        out_shape=jax.ShapeDtypeStruct((b, l, W_BR), f32),
        compiler_params=pltpu.CompilerParams(dimension_semantics=("arbitrary", "arbitrary"),
                                             vmem_limit_bytes=VMEM_LIMIT),
        name="swa_prompt",
    )(pc, pc, prm["sinks"])


def _lru_gates(xc, wa, ba, wx, bx, lam):
    r = _sigmoid(_bdot(xc, wa) + ba)
    i = _sigmoid(_bdot(xc, wx) + bx)
    log_a = C_RG * r * _log_sigmoid(lam)
    a = jnp.exp(log_a)
    bterm = jnp.sqrt(1.0 - jnp.exp(2.0 * log_a)) * (i * xc)
    return a, bterm


def _lru_prompt_kernel(x_ref, cw_ref, wa_ref, wx_ref, rows_ref, y_ref, xbuf_scr, h_scr):
    cb, ba, bx, lam = _rows(rows_ref, "cb", "lba", "lbx", "lam")
    c = pl.program_id(1)
    T = LRU_CHUNK
    PADR = 8

    @pl.when(c == 0)
    def _():
        xbuf_scr[0:PADR, :] = jnp.zeros((PADR, W_BR), f32)
        h_scr[...] = jnp.zeros_like(h_scr)

    x = x_ref[0]
    xbuf_scr[PADR:PADR + T, :] = x
    xc = cb + x * cw_ref[CONV_W - 1:CONV_W, :]
    for j in range(1, CONV_W):
        xc = xc + xbuf_scr[PADR - j:PADR - j + T, :] * cw_ref[CONV_W - 1 - j:CONV_W - j, :]
    xbuf_scr[0:PADR, :] = x[T - PADR:T, :]
    a, bv = _lru_gates(xc, wa_ref[...], ba, wx_ref[...], bx, lam)
    row = _iota((T, W_BR), 0)
    d = 1
    while d < T:
        keep = row >= d
        a_sh = jnp.where(keep, pltpu.roll(a, d, 0), 1.0)
        b_sh = jnp.where(keep, pltpu.roll(bv, d, 0), 0.0)
        bv = a * b_sh + bv
        a = a * a_sh
        d *= 2
    h = a * h_scr[...] + bv
    y_ref[0] = h
    h_scr[...] = h[T - 1:T, :]


def _lru_prompt(pd, prm, layer):
    b, l, _ = pd.shape
    T = LRU_CHUNK
    return pl.pallas_call(
        _lru_prompt_kernel,
        grid=(b, l // T),
        in_specs=[pl.BlockSpec((1, T, W_BR), lambda i, c: (i, c, 0))]
        + [_lspec(prm[k], layer) for k in LRU_PARAMS],
        out_specs=pl.BlockSpec((1, T, W_BR), lambda i, c: (i, c, 0)),
        out_shape=jax.ShapeDtypeStruct((b, l, W_BR), f32),
        scratch_shapes=[pltpu.VMEM((T + 8, W_BR), f32), pltpu.VMEM((1, W_BR), f32)],
        compiler_params=pltpu.CompilerParams(dimension_semantics=("arbitrary", "arbitrary"),
                                             vmem_limit_bytes=VMEM_LIMIT),
        name="lru_prompt",
    )(pd, *[prm[k] for k in LRU_PARAMS])


def _colbcast(row, n_out, eye_bf16):
    c = row.shape[1]
    hi, lo = _split_bf16(row)
    return (_dg(eye_bf16, jnp.broadcast_to(hi, (n_out, c)), NT)
            + _dg(eye_bf16, jnp.broadcast_to(lo, (n_out, c)), NT))


def _decode_states_kernel(pa_ref, pb_ref, shift_ref, swkv_ref, sgla_ref,
                          mu_ref, wup_ref, aup_ref, gup_ref, gbias_ref, rows_ref, *rest, n_prev):
    earlier, rest = rest[:2 if n_prev else 0], rest[2 if n_prev else 0:]
    oa_ref, ob_ref, swkv_all, sgla_all, fa_scr, fb_scr, vb_scr, oa_scr, ob_scr = rest
    for src, dst in zip(earlier, (swkv_all, sgla_all)):
        dst[0:n_prev] = src[...]
    swkv_out, sgla_out = swkv_all.at[n_prev], sgla_all.at[n_prev]
    h = pl.program_id(0)
    hk = H_B * DK_B

    @pl.when(h == 0)
    def _():
        u = pa_ref[...]
        us = u + (shift_ref[...] - u) * mu_ref[...]
        w0, a0, kkw, kaw = _rows(rows_ref, "w0", "a0", "kkw", "kaw")
        r, kmod, v, ld, kk, a = _rwkv_features(us, w0, wup_ref[...], a0, aup_ref[...], kkw, kaw, _head_ones())
        for i, t in enumerate((r, kmod, v, jnp.exp(ld), kk, kk * a)):
            fa_scr[i] = t.T
        fb = pb_ref[...]
        gb = _log_sigmoid(_bdot(fb[:, 2 * hk + W_BR:B_PAD], gup_ref[...]) + gbias_ref[...]) * (1.0 / GLA_TAU)
        for i, t in enumerate((fb[:, 0:hk] * (DK_B ** -0.5), fb[:, hk:2 * hk], jnp.exp(gb))):
            fb_scr[i] = t.T
        vb_scr[...] = fb[:, 2 * hk:2 * hk + W_BR].T

    hs = pl.ds(pl.multiple_of(h * HEAD_A, HEAD_A), HEAD_A)
    r_h, km_h, v_h, w_h, kk_h, kka_h = (fa_scr[i, hs, :] for i in range(6))
    sub = _iota((8, r_h.shape[1]), 0)
    for g in range(HEAD_A // 8):
        rows8 = jnp.zeros((8, r_h.shape[1]), f32)
        for j in range(8):
            vi = g * 8 + j
            s = swkv_ref[vi]
            sa = -jnp.sum(s * kk_h, axis=0, keepdims=True)
            s_new = s * w_h + sa * kka_h + v_h[vi:vi + 1, :] * km_h
            swkv_out[vi] = s_new
            rows8 = jnp.where(sub == j, jnp.sum(s_new * r_h, axis=0, keepdims=True), rows8)
        oa_scr[pl.ds(pl.multiple_of(h * HEAD_A + g * 8, 8), 8), :] = rows8
    ds_ = pl.ds(pl.multiple_of(h * DK_B, DK_B), DK_B)
    q_h, k_h, eg_h = (fb_scr[i, ds_, :] for i in range(3))
    vb_h = vb_scr[hs, :]
    acc = jnp.zeros_like(vb_h)
    for d in range(DK_B):
        s_new = eg_h[d:d + 1, :] * sgla_ref[d] + k_h[d:d + 1, :] * vb_h
        sgla_out[d] = s_new
        acc = acc + q_h[d:d + 1, :] * s_new
    ob_scr[hs, :] = acc

    @pl.when(h == pl.num_programs(0) - 1)
    def _():
        oa_ref[...] = oa_scr[...].T
        ob_ref[...] = ob_scr[...].T


def _decode_states(pa, pb, shift0, swkv_t, sgla_t, prm, layer, earlier):
    assert H_A == H_B and DV_B == HEAD_A
    n = pa.shape[0]
    n_prev = layer
    wkv_dims, gla_dims = (HEAD_A, HEAD_A, n), (DK_B, DV_B, n)
    of_head = lambda dims: pl.BlockSpec((None, None) + dims, lambda h: (layer, h) + (0,) * len(dims))
    stacked = lambda nl, dims: pl.BlockSpec((nl, None) + dims, lambda h: (0, h) + (0,) * len(dims))
    keys = DECODE_STATE_PARAMS
    in_specs = [_const_spec((n, A_PAD)), _const_spec((n, B_PAD)), _const_spec((n, A_PAD)),
                of_head(wkv_dims), of_head(gla_dims)] + [_lspec(prm[k], layer) for k in keys]
    if n_prev:
        in_specs += [stacked(n_prev, wkv_dims), stacked(n_prev, gla_dims)]
    return pl.pallas_call(
        functools.partial(_decode_states_kernel, n_prev=n_prev),
        grid=(H_A,),
        in_specs=in_specs,
        out_specs=[_const_spec((n, W_BR)), _const_spec((n, W_BR)),
                   stacked(n_prev + 1, wkv_dims), stacked(n_prev + 1, gla_dims)],
        out_shape=[jax.ShapeDtypeStruct((n, W_BR), f32), jax.ShapeDtypeStruct((n, W_BR), f32),
                   jax.ShapeDtypeStruct((n_prev + 1, H_A) + wkv_dims, f32),
                   jax.ShapeDtypeStruct((n_prev + 1, H_B) + gla_dims, f32)],
        scratch_shapes=[pltpu.VMEM((6, W_BR, n), f32), pltpu.VMEM((3, H_B * DK_B, n), f32),
                        pltpu.VMEM((W_BR, n), f32), pltpu.VMEM((W_BR, n), f32), pltpu.VMEM((W_BR, n), f32)],
        compiler_params=pltpu.CompilerParams(dimension_semantics=("arbitrary",),
                                             vmem_limit_bytes=VMEM_LIMIT),
        name="decode_states",
    )(pa, pb, shift0, swkv_t, sgla_t, *[prm[k] for k in keys], *(earlier if n_prev else ()))


def _decode_kernel(pa_ref, oa_ref, ob_ref, q8_ref, kvn_ref, pd_ref, shift_ref, kbuf_ref, vbuf_ref,
                   c0_ref, c1_ref, c2_ref, h0_ref,
                   mu_ref, wup_ref, aup_ref, rows_ref, sink8_ref, slope8_ref, cw_ref, wa_ref, wx_ref,
                   *rest, n_prev):
    earlier, rest = rest[:2 if n_prev else 0], rest[2 if n_prev else 0:]
    ya_ref, yb_ref, yc8_ref, yd_ref, kout_all, vout_all = rest
    for src, dst in zip(earlier, (kout_all, vout_all)):
        dst[0:n_prev] = src[...]
    kout_ref, vout_ref = kout_all.at[n_prev], vout_all.at[n_prev]
    bt = pa_ref.shape[0]
    samples = range(bt)
    u = pa_ref[...]
    us = u + (shift_ref[...] - u) * mu_ref[...]
    ones = _head_ones()
    w0, a0, kkw, kaw, rk, lng, lnb, gng, cb, ba, bx, lam = _rows(rows_ref, *ROWS)
    r, kmod, v, _, _, _ = _rwkv_features(us, w0, wup_ref[...], a0, aup_ref[...], kkw, kaw, ones)
    wdist = (WINDOW - _iota((1, WINDOW), 1)).astype(f32)
    last = _iota((HD_C, WINDOW), 1) == WINDOW - 1
    scale = HD_C ** -0.5
    eye_c = _eye(HD_C).astype(bf16)
    half = KV_C * HD_C
    pieces = [(b, g) for b in samples for g in range(KV_C)]
    kn = lambda b, g: kvn_ref[b:b + 1, g * HD_C:(g + 1) * HD_C]
    vn = lambda b, g: kvn_ref[b:b + 1, half + g * HD_C:half + (g + 1) * HD_C]
    scores = {(b, g): _bdot(q8_ref[b, g], kbuf_ref[b, g]) * scale - slope8_ref[g] * wdist
              for b, g in pieces}
    probs, tails = {}, {}
    for b, g in pieces:
        sink = sink8_ref[g]
        sn = jnp.sum(q8_ref[b, g] * kn(b, g), axis=1, keepdims=True) * scale
        m = jnp.maximum(jnp.maximum(jnp.max(scores[b, g], axis=1, keepdims=True), sn), sink)
        p = jnp.exp(scores[b, g] - m)
        pn = jnp.exp(sn - m)
        probs[b, g] = p
        tails[b, g] = (pn, jnp.sum(p, axis=1, keepdims=True) + pn + jnp.exp(sink - m))
    for b, g in pieces:
        pn, den = tails[b, g]
        yc8_ref[b, g] = (_bdot_nt(probs[b, g], vbuf_ref[b, g]) + pn * vn(b, g)) / den
    for b, g in pieces:
        for src, dst, new in ((kbuf_ref, kout_ref, kn(b, g)), (vbuf_ref, vout_ref, vn(b, g))):
            new_col = _colbcast(new, WINDOW, eye_c)
            dst[b, g] = jnp.where(last, new_col, pltpu.roll(src[b, g], WINDOW - 1, 1))

    ya_ref[...] = _rwkv_finish(oa_ref[...], r, kmod, v, rk, lng, lnb, ones)
    ob = ob_ref[...]
    assert DV_B == HEAD_A
    ms = _ones_dot(ob * ob, ones) * (1.0 / DV_B)
    yb_ref[...] = ob * lax.rsqrt(ms + NORM_EPS) * gng
    xd = pd_ref[...]
    xc = (cb + c0_ref[...] * cw_ref[0:1, :] + c1_ref[...] * cw_ref[1:2, :]
          + c2_ref[...] * cw_ref[2:3, :] + xd * cw_ref[3:4, :])
    al, bterm = _lru_gates(xc, wa_ref[...], ba, wx_ref[...], bx, lam)
    yd_ref[...] = al * h0_ref[...] + bterm


def _decode(pa, oa, ob, pc, pd, shift0, kbuf, vbuf, c0, c1, c2, h0, prm, layer, earlier, bt):
    n = pa.shape[0]
    n_prev = layer
    half = KV_C * HD_C
    q8 = jnp.pad(pc[:, 0:W_BR].reshape(n, KV_C, G_C, HD_C), ((0, 0), (0, 0), (0, 8 - G_C), (0, 0)))
    kvn = pc[:, W_BR:]
    slope8 = jnp.pad(jnp.asarray(ALIBI_SLOPES, f32).reshape(KV_C, G_C, 1), ((0, 0), (0, 8 - G_C), (0, 0)))
    rows = lambda w: pl.BlockSpec((bt, w), lambda i: (i, 0))
    qspec = pl.BlockSpec((bt, KV_C, 8, HD_C), lambda i: (i, 0, 0, 0))
    cache_dims = (KV_C, HD_C, WINDOW)
    cache_in = pl.BlockSpec((None, bt) + cache_dims, lambda i: (layer, i, 0, 0, 0))
    stacked = lambda nl: pl.BlockSpec((nl, bt) + cache_dims, lambda i: (0, i, 0, 0, 0))
    in_specs = [rows(A_PAD), rows(W_BR), rows(W_BR), qspec, rows(2 * half), rows(D_COLS), rows(A_PAD),
                cache_in, cache_in, rows(W_BR), rows(W_BR), rows(W_BR), rows(W_BR)]
    in_specs += [_lspec(prm[k], layer) for k in RWKV_PARAMS]
    in_specs += [_lspec(prm["sink8"], layer), _const_spec((KV_C, 8, 1))]
    in_specs += [_lspec(prm[k], layer) for k in ("cw", "lwa", "lwx")]
    if n_prev:
        in_specs += [stacked(n_prev)] * 2
    out_specs = [rows(W_BR), rows(W_BR), qspec, rows(W_BR)] + [stacked(n_prev + 1)] * 2
    out_shape = [jax.ShapeDtypeStruct((n, W_BR), f32), jax.ShapeDtypeStruct((n, W_BR), f32),
                 jax.ShapeDtypeStruct((n, KV_C, 8, HD_C), f32), jax.ShapeDtypeStruct((n, W_BR), f32)]
    out_shape += [jax.ShapeDtypeStruct((n_prev + 1, n) + cache_dims, f32)] * 2
    ya, yb, yc8, yd, k1, v1 = pl.pallas_call(
        functools.partial(_decode_kernel, n_prev=n_prev),
        grid=(n // bt,),
        in_specs=in_specs,
        out_specs=out_specs,
        out_shape=out_shape,
        compiler_params=pltpu.CompilerParams(dimension_semantics=("arbitrary",),
                                             vmem_limit_bytes=VMEM_LIMIT),
        name="decode_mixers",
    )(pa, oa, ob, q8, kvn, pd, shift0, kbuf, vbuf, c0, c1, c2, h0,
      *[prm[k] for k in RWKV_PARAMS], prm["sink8"], slope8, prm["cw"], prm["lwa"], prm["lwx"],
      *(earlier if n_prev else ()))
    return ya, yb, yc8[:, :, 0:G_C, :].reshape(n, W_BR), yd, k1, v1


def _merge_kernel(x_ref, ya_ref, yb_ref, yc_ref, yd_ref, g_ref, w_ref, wbr_ref, wout_ref,
                  fg_ref, o_ref, *, final):
    x = x_ref[...]
    hn = _rms(x, g_ref[...]).astype(bf16)
    ys = (ya_ref, yb_ref, yc_ref, yd_ref)
    merged = None
    for n in range(N_BRANCH):
        z = _dg(hn, w_ref[:, n * W_BR:(n + 1) * W_BR], NN)
        yz = ys[n][...] * (z * _sigmoid(z))
        br = jnp.dot(yz.astype(bf16), wbr_ref[n], preferred_element_type=f32)
        gate = _sigmoid(_dg(hn, w_ref[:, Z_COLS + n * D_MODEL:Z_COLS + (n + 1) * D_MODEL], NN))
        merged = gate * br if merged is None else merged + gate * br
    out = x + jnp.dot(merged.astype(bf16), wout_ref[...], preferred_element_type=f32)
    if final:
        out = _rms(out, fg_ref[...])
    o_ref[...] = out


def _merge(x, ya, yb, yc, yd, g, wt, wbr, wout, fg, layer, tm, final):
    m = x.shape[0]
    tile = lambda w: pl.BlockSpec((tm, w), lambda i: (i, 0))
    per_layer = lambda shape: pl.BlockSpec((None,) + shape, lambda *_: (layer,) + (0,) * len(shape),
                                           pipeline_mode=pl.Buffered(1))
    return pl.pallas_call(
        functools.partial(_merge_kernel, final=final),
        grid=(m // tm,),
        in_specs=[tile(D_MODEL), tile(W_BR), tile(W_BR), tile(W_BR), tile(W_BR), _lspec(g, layer),
                  per_layer((D_MODEL, GATE_ROWS)),
                  per_layer((N_BRANCH, W_BR, D_MODEL)),
                  per_layer((D_MODEL, D_MODEL)), _const_spec((1, D_MODEL))],
        out_specs=tile(D_MODEL),
        out_shape=jax.ShapeDtypeStruct((m, D_MODEL), f32),
        compiler_params=pltpu.CompilerParams(dimension_semantics=("arbitrary",),
                                             vmem_limit_bytes=VMEM_LIMIT),
        name="merge_final" if final else "merge",
    )(x, ya, yb, yc, yd, g, wt, wbr, wout, fg)


def _lspec(arr, layer):
    return pl.BlockSpec((None,) + arr.shape[1:], lambda *_: (layer,) + (0,) * (arr.ndim - 1))


def _stacked_params(norm_g, mu_shift, w0, w_decay_up, a0, a_icl_up, k_k, k_a, r_k, ln_x_g, ln_x_b,
                    gla_gate_up, gla_gate_b, gla_norm_g, swa_sinks, lru_conv_w, lru_conv_b, lru_wa, lru_ba,
                    lru_wx, lru_bx, lru_lambda):
    depth = norm_g.shape[0]
    row = lambda t: t.reshape(depth, 1, -1)
    pad_rows = lambda w, start: jnp.pad(w, ((0, 0), (start, LANE - start - w.shape[1]), (0, 0)))
    dense = lambda w: jnp.einsum("lnij,nm->lnimj", w, jnp.eye(w.shape[1], dtype=w.dtype)).reshape(
        depth, W_BR, W_BR).astype(bf16)
    return dict(
        g=row(norm_g),
        mu=row(jnp.pad(mu_shift, ((0, 0), (0, A_PAD - A_COLS)))),
        wup=pad_rows(w_decay_up, 0), aup=pad_rows(a_icl_up, R_DECAY),
        gup=pad_rows(gla_gate_up, 0), gbias=row(gla_gate_b),
        rows=jnp.stack([w0, a0, k_k, k_a, r_k.reshape(depth, W_BR), ln_x_g, ln_x_b, jnp.tile(gla_norm_g, (1, H_B)),
                        lru_conv_b, lru_ba, lru_bx, lru_lambda], axis=1),
        sinks=row(swa_sinks),
        sink8=jnp.pad(swa_sinks.reshape(depth, KV_C, G_C, 1), ((0, 0), (0, 0), (0, 8 - G_C), (0, 0))),
        cw=lru_conv_w, lwa=dense(lru_wa), lwx=dense(lru_wx),
    )


def kernel(x_prompt, x_sample, state_wkv, state_shift, state_gla, cache_swa_k, cache_swa_v, state_lru_conv, state_lru_h, norm_g, w_in, mu_shift, w0, w_decay_up, a0, a_icl_up, k_k, k_a, r_k, ln_x_g, ln_x_b, gla_gate_up, gla_gate_b, gla_norm_g, swa_sinks, lru_conv_w, lru_conv_b, lru_wa, lru_ba, lru_wx, lru_bx, lru_lambda, w_branch, w_out, final_norm_g):
    bp, lp, _ = x_prompt.shape
    bs = x_sample.shape[0]
    depth = w_in.shape[0]
    fg = final_norm_g.reshape(1, -1)
    xp = x_prompt.reshape(bp * lp, D_MODEL)
    xs = x_sample.reshape(bs, D_MODEL)
    outs_p = [[] for _ in range(7)]
    outs_s = [[] for _ in range(7)]
    wt_mix, w_gate = _pack_w_in(w_in)
    wbr_all = w_branch.astype(bf16)
    wout_all = w_out.astype(bf16)
    stacked_s = None
    prm = _stacked_params(norm_g, mu_shift, w0, w_decay_up, a0, a_icl_up, k_k, k_a, r_k, ln_x_g, ln_x_b,
                          gla_gate_up, gla_gate_b, gla_norm_g, swa_sinks, lru_conv_w, lru_conv_b, lru_wa,
                          lru_ba, lru_wx, lru_bx, lru_lambda)
    shift_pad = jnp.pad(state_shift, ((0, 0), (0, 0), (0, A_PAD - A_COLS)))
    wkv_t = jnp.transpose(state_wkv, (0, 2, 3, 4, 1))
    gla_t = jnp.transpose(state_gla, (0, 2, 3, 4, 1))
    stacked_states = None
    kbuf_t = jnp.transpose(cache_swa_k, (0, 1, 3, 4, 2))
    vbuf_t = jnp.transpose(cache_swa_v, (0, 1, 3, 4, 2))
    for l in range(depth):
        final = l == depth - 1
        pa, pb, pc, pd = _inproj(xp, prm["g"], wt_mix, l, tm=1024)
        pa3, pb3 = pa.reshape(bp, lp, A_PAD), pb.reshape(bp, lp, B_PAD)
        pc3, pd3 = pc.reshape(bp, lp, C_COLS), pd.reshape(bp, lp, D_COLS)
        ya, st_t = _rwkv_prompt(pa3, prm, l)
        yb, sgla = _gla_prompt(pb3, prm, l)
        yc = _swa_prompt(pc3, prm, l)
        yd = _lru_prompt(pd3, prm, l)
        flat = lambda t: t.reshape(bp * lp, W_BR)
        xp = _merge(xp, flat(ya), flat(yb), flat(yc), flat(yd), prm["g"], w_gate, wbr_all, wout_all, fg, l,
                    tm=512, final=final)
        kv = pc3[:, lp - WINDOW:, H_C * HD_C:]
        outs_p[0].append(jnp.swapaxes(st_t, -1, -2))
        outs_p[1].append(pa3[:, lp - 1, :A_COLS])
        outs_p[2].append(sgla)
        outs_p[3].append(kv[:, :, :KV_C * HD_C].reshape(bp, WINDOW, KV_C, HD_C))
        outs_p[4].append(kv[:, :, KV_C * HD_C:].reshape(bp, WINDOW, KV_C, HD_C))
        outs_p[5].append(pd3[:, lp - (CONV_W - 1):, :])
        outs_p[6].append(yd[:, lp - 1, :])
        sa, sb, sc, sd = _inproj(xs, prm["g"], wt_mix, l, tm=bs)
        conv0 = state_lru_conv[l]
        oa_s, ob_s, *stacked_states = _decode_states(sa, sb, shift_pad[l], wkv_t, gla_t, prm, l, stacked_states)
        ya_s, yb_s, yc_s, yd_s, *stacked_s = _decode(
            sa, oa_s, ob_s, sc, sd, shift_pad[l], kbuf_t, vbuf_t,
            conv0[:, 0], conv0[:, 1], conv0[:, 2], state_lru_h[l], prm, l, stacked_s, bt=32)
        xs = _merge(xs, ya_s, yb_s, yc_s, yd_s, prm["g"], w_gate, wbr_all, wout_all, fg, l, tm=bs, final=final)
        outs_s[1].append(sa[:, :A_COLS])
        outs_s[5].append(jnp.stack([conv0[:, 1], conv0[:, 2], sd], axis=1))
        outs_s[6].append(yd_s)
    y_prompt = xp.reshape(bp, lp, D_MODEL)
    y_sample = xs.reshape(bs, 1, D_MODEL)
    sp = [jnp.stack(t) for t in outs_p]
    wkv_s, gla_s = (jnp.transpose(t, (0, 4, 1, 2, 3)) for t in stacked_states)
    k_s, v_s = (jnp.transpose(t, (0, 1, 4, 2, 3)) for t in stacked_s)
    shift_s, conv_s, h_s = (jnp.stack(outs_s[i]) for i in (1, 5, 6))
    return (y_prompt, y_sample, sp[0], wkv_s, sp[1], shift_s, sp[2], gla_s, sp[3], k_s, sp[4], v_s,
            sp[5], conv_s, sp[6], h_s)
```
